```python
import jax, jax.numpy as jnp
from jax import lax
import numpy as np

D_MODEL = 1024
BATCH = 4
SEQ = 8192
DEPTH = 1

CHUNK = 64
D_MIX = D_MODEL
M_WIDTH = D_MIX // 2
M_HEADS = 4
M_HEAD_DIM = M_WIDTH // M_HEADS
SB_WIDTH = D_MIX - M_WIDTH
SB_HEADS = 8
SB_HEAD_DIM = SB_WIDTH // SB_HEADS
SB_Q_BLOCK = 128
CONV_WIDTH = 4
N_GROUPS = 4
EXPERTS_PER_GROUP = 8
N_EXPERTS = N_GROUPS * EXPERTS_PER_GROUP
TOP_K_IN_GROUP = 2
D_EXPERT = D_MODEL // 2
EXPERT_BLOCK = 128
PLE_DIM = 256
EPS = 1e-6
SPLITS = [M_WIDTH, 2 * M_WIDTH, 3 * M_WIDTH, 4 * M_WIDTH,
          4 * M_WIDTH + M_HEADS, 4 * M_WIDTH + 2 * M_HEADS,
          4 * M_WIDTH + 2 * M_HEADS + SB_WIDTH, 4 * M_WIDTH + 2 * M_HEADS + 2 * SB_WIDTH]
IN_COLS = 4 * M_WIDTH + 2 * M_HEADS + 3 * SB_WIDTH

kernel_name = 'hymba_mlstm_stickbreak_hiermoe_ple'


def rmsnorm(x, g):
    xf = x.astype(jnp.float32)
    y = xf * lax.rsqrt(jnp.mean(xf * xf, axis=-1, keepdims=True) + EPS)
    return (y * g.astype(jnp.float32)).astype(x.dtype)


def causal_conv(x, w):
    K = w.shape[0]
    S = x.shape[1]
    xp = jnp.pad(x, ((0, 0), (K - 1, 0), (0, 0)))
    out = xp[:, 0:S] * w[0]
    for j in range(1, K):
        out = out + xp[:, j:j + S] * w[j]
    return out


def mlstm_chunkwise(q, k, v, i_pre, f_pre):
    B, S, H, D = q.shape
    L = CHUNK
    NC = S // L
    f32 = jnp.float32

    def chunks(t):
        return t.astype(f32).reshape(B, NC, L, H, -1).transpose(0, 1, 3, 2, 4)

    q = chunks(q)
    k = chunks(k) * (D ** -0.5)
    v = chunks(v)
    log_i = i_pre.astype(f32).reshape(B, NC, L, H).transpose(0, 1, 3, 2)
    log_f = jax.nn.log_sigmoid(f_pre.astype(f32)).reshape(B, NC, L, H).transpose(0, 1, 3, 2)
    b = jnp.cumsum(log_f, axis=-1)
    b_last = b[..., -1]

    w_end = b_last[..., None] - b + log_i
    m_loc = jnp.max(w_end, axis=-1)
    e_end = jnp.exp(w_end - m_loc[..., None])
    dC = jnp.einsum('bchs,bchsd,bchse->bchde', e_end, k, v)
    dn = jnp.einsum('bchs,bchsd->bchd', e_end, k)

    def step(carry, xs):
        C, n, m = carry
        dC_c, dn_c, mloc_c, blast_c = xs
        m_new = jnp.maximum(blast_c + m, mloc_c)
        decay = jnp.exp(blast_c + m - m_new)
        scale = jnp.exp(mloc_c - m_new)
        C_new = decay[..., None, None] * C + scale[..., None, None] * dC_c
        n_new = decay[..., None] * n + scale[..., None] * dn_c
        return (C_new, n_new, m_new), (C, n, m)

    init = (jnp.zeros((B, H, D, D), f32), jnp.zeros((B, H, D), f32), jnp.zeros((B, H), f32))
    xs = (jnp.moveaxis(dC, 1, 0), jnp.moveaxis(dn, 1, 0),
          jnp.moveaxis(m_loc, 1, 0), jnp.moveaxis(b_last, 1, 0))
    _, (C_prev, n_prev, m_prev) = lax.scan(step, init, xs)
    C_prev = jnp.moveaxis(C_prev, 0, 1)
    n_prev = jnp.moveaxis(n_prev, 0, 1)
    m_prev = jnp.moveaxis(m_prev, 0, 1)

    causal = jnp.tril(jnp.ones((L, L), dtype=bool))
    log_intra = b[..., :, None] - b[..., None, :] + log_i[..., None, :]
    log_intra = jnp.where(causal, log_intra, -jnp.inf)
    log_inter = b + m_prev[..., None]
    m_t = jnp.maximum(log_inter, jnp.max(log_intra, axis=-1))
    w_qk = jnp.exp(log_intra - m_t[..., None]) * jnp.einsum('bchtd,bchsd->bchts', q, k)
    a_int = jnp.exp(log_inter - m_t)
    num = (jnp.einsum('bchts,bchse->bchte', w_qk, v)
           + a_int[..., None] * jnp.einsum('bchtd,bchde->bchte', q, C_prev))
    den = jnp.sum(w_qk, axis=-1) + a_int * jnp.einsum('bchtd,bchd->bcht', q, n_prev)
    h = num / jnp.maximum(jnp.abs(den), jnp.exp(-m_t))[..., None]
    return h.transpose(0, 1, 3, 2, 4).reshape(B, S, H, D)


def stick_breaking(q, k, v):
    B, S, H, D = q.shape
    f32 = jnp.float32
    q = q.astype(f32).transpose(0, 2, 1, 3) * (D ** -0.5)
    k = k.astype(f32).transpose(0, 2, 1, 3)
    v = v.astype(f32).transpose(0, 2, 1, 3)
    s_idx = jnp.arange(S)

    def block(qb):
        start = qb * SB_Q_BLOCK
        qblk = lax.dynamic_slice_in_dim(q, start, SB_Q_BLOCK, axis=2)
        z = jnp.einsum('bhtd,bhsd->bhts', qblk, k)
        t_idx = start + jnp.arange(SB_Q_BLOCK)
        mask = s_idx[None, :] < t_idx[:, None]
        log_keep = jnp.where(mask, jax.nn.log_sigmoid(-z), 0.0)
        later = lax.cumsum(log_keep, axis=3, reverse=True) - log_keep
        log_a = jnp.where(mask, jax.nn.log_sigmoid(z) + later, -jnp.inf)
        return jnp.einsum('bhts,bhsd->bhtd', jnp.exp(log_a), v)

    o = lax.map(block, jnp.arange(S // SB_Q_BLOCK))
    return o.transpose(1, 0, 3, 2, 4).reshape(B, S, H, D)


def hier_moe(c, w_rg, b_rg, w_re, b_re, w_gate, w_up, w_down):
    B, S, D = c.shape
    T = B * S
    xf = c.reshape(T, D)
    x32 = xf.astype(jnp.float32)
    g_logits = x32 @ w_rg.astype(jnp.float32) + b_rg.astype(jnp.float32)
    g_prob = jax.nn.softmax(g_logits, axis=-1)
    g_sel = jnp.argmax(g_logits, axis=-1).astype(jnp.int32)
    p_g = jnp.take_along_axis(g_prob, g_sel[:, None], axis=1)[:, 0]
    e_logits = (x32 @ w_re.astype(jnp.float32) + b_re.astype(jnp.float32)).reshape(T, N_GROUPS, EXPERTS_PER_GROUP)
    e_sel_logits = jnp.take_along_axis(e_logits, g_sel[:, None, None], axis=1)[:, 0]
    top_v, top_i = lax.top_k(e_sel_logits, TOP_K_IN_GROUP)
    weights = p_g[:, None] * jax.nn.softmax(top_v, axis=-1)
    eids = g_sel[:, None] * EXPERTS_PER_GROUP + top_i.astype(jnp.int32)

    flat_e = eids.reshape(-1)
    flat_w = weights.reshape(-1)
    flat_tok = jnp.repeat(jnp.arange(T, dtype=jnp.int32), TOP_K_IN_GROUP)
    order = jnp.argsort(flat_e, stable=True)
    sorted_e = flat_e[order]
    counts = jnp.bincount(flat_e, length=N_EXPERTS).astype(jnp.int32)
    padded = ((counts + EXPERT_BLOCK - 1) // EXPERT_BLOCK) * EXPERT_BLOCK
    offs = jnp.cumsum(counts) - counts
    pend = jnp.cumsum(padded)
    poffs = pend - padded
    n_assign = T * TOP_K_IN_GROUP
    rank = jnp.arange(n_assign, dtype=jnp.int32) - offs[sorted_e]
    dest = poffs[sorted_e] + rank
    R = n_assign + N_EXPERTS * EXPERT_BLOCK
    NB = R // EXPERT_BLOCK
    buf_tok = jnp.zeros((R,), jnp.int32).at[dest].set(flat_tok[order])
    buf_w = jnp.zeros((R,), jnp.float32).at[dest].set(flat_w[order])
    block_start = jnp.arange(NB, dtype=jnp.int32) * EXPERT_BLOCK
    block_e = jnp.minimum(jnp.searchsorted(pend, block_start, side='right'), N_EXPERTS - 1)
    xin = xf[buf_tok].reshape(NB, EXPERT_BLOCK, D)

    def expert_block(args):
        xb, e = args
        hid = jax.nn.silu(xb @ w_gate[e]) * (xb @ w_up[e])
        return hid @ w_down[e]

    y = lax.map(expert_block, (xin, block_e)).reshape(R, D)
    out = jax.ops.segment_sum(y.astype(jnp.float32) * buf_w[:, None], buf_tok, num_segments=T)
    return out.reshape(B, S, D).astype(c.dtype)


def setup_inputs(seed: int = 0) -> dict:
    key = jax.random.key(seed)
    ks = jax.random.split(key, 24)
    f32 = jnp.float32

    def nrm(k, shape, scale):
        return jax.random.normal(k, shape, f32) * scale

    def gain(k, shape):
        return 1.0 + 0.02 * jax.random.normal(k, shape, f32)

    b_gates = jnp.concatenate([
        0.1 * jax.random.normal(ks[3], (DEPTH, M_HEADS), f32),
        jnp.linspace(3.0, 6.0, M_HEADS, dtype=f32)[None, :] + 0.1 * jax.random.normal(ks[4], (DEPTH, M_HEADS), f32),
    ], axis=-1)
    return {
        'x': nrm(ks[0], (BATCH, SEQ, D_MODEL), 1.0),
        'p': nrm(ks[1], (DEPTH, BATCH, SEQ, PLE_DIM), 1.0),
        'g_mix': gain(ks[2], (DEPTH, D_MODEL)),
        'w_in': nrm(ks[5], (DEPTH, D_MODEL, IN_COLS), D_MODEL ** -0.5),
        'b_gates': b_gates,
        'conv_q': nrm(ks[6], (DEPTH, CONV_WIDTH, M_WIDTH), CONV_WIDTH ** -0.5),
        'conv_k': nrm(ks[7], (DEPTH, CONV_WIDTH, M_WIDTH), CONV_WIDTH ** -0.5),
        'g_mhead': gain(ks[8], (DEPTH, M_WIDTH)),
        'w_out': nrm(ks[9], (DEPTH, D_MIX, D_MODEL), D_MIX ** -0.5),
        'g_ffn': gain(ks[10], (DEPTH, D_MODEL)),
        'w_router_group': nrm(ks[11], (DEPTH, D_MODEL, N_GROUPS), D_MODEL ** -0.5),
        'b_router_group': nrm(ks[12], (DEPTH, N_GROUPS), 0.01),
        'w_router_expert': nrm(ks[13], (DEPTH, D_MODEL, N_EXPERTS), D_MODEL ** -0.5),
        'b_router_expert': nrm(ks[14], (DEPTH, N_EXPERTS), 0.01),
        'w_exp_gate': nrm(ks[15], (DEPTH, N_EXPERTS, D_MODEL, D_EXPERT), D_MODEL ** -0.5),
        'w_exp_up': nrm(ks[16], (DEPTH, N_EXPERTS, D_MODEL, D_EXPERT), D_MODEL ** -0.5),
        'w_exp_down': nrm(ks[17], (DEPTH, N_EXPERTS, D_EXPERT, D_MODEL), D_EXPERT ** -0.5),
        'g_ple': gain(ks[18], (DEPTH, D_MODEL)),
        'w_ple_gate': nrm(ks[19], (DEPTH, D_MODEL, D_MODEL), D_MODEL ** -0.5),
        'w_ple_proj': nrm(ks[20], (DEPTH, PLE_DIM, D_MODEL), PLE_DIM ** -0.5),
        'g_ple_post': gain(ks[21], (DEPTH, D_MODEL)),
        'g_final': gain(ks[22], (D_MODEL,)),
    }


def reference(x, p, g_mix, w_in, b_gates, conv_q, conv_k, g_mhead, w_out, g_ffn,
              w_router_group, b_router_group, w_router_expert, b_router_expert,
              w_exp_gate, w_exp_up, w_exp_down, g_ple, w_ple_gate, w_ple_proj,
              g_ple_post, g_final):
    h = x
    B, S, _ = x.shape
    for l in range(DEPTH):
        a = rmsnorm(h, g_mix[l])
        u = a @ w_in[l]
        mq, mk, mv, mo, mi, mf, sq, sk, sv = jnp.split(u, SPLITS, axis=-1)
        mq = jax.nn.silu(causal_conv(mq, conv_q[l]))
        mk = jax.nn.silu(causal_conv(mk, conv_k[l]))
        i_pre = mi + b_gates[l, :M_HEADS]
        f_pre = mf + b_gates[l, M_HEADS:]
        hm = mlstm_chunkwise(mq.reshape(B, S, M_HEADS, M_HEAD_DIM),
                             mk.reshape(B, S, M_HEADS, M_HEAD_DIM),
                             mv.reshape(B, S, M_HEADS, M_HEAD_DIM), i_pre, f_pre)
        hm = rmsnorm(hm, g_mhead[l].reshape(M_HEADS, M_HEAD_DIM))
        hm = jax.nn.sigmoid(mo.astype(jnp.float32)).reshape(B, S, M_HEADS, M_HEAD_DIM) * hm
        hs = stick_breaking(sq.reshape(B, S, SB_HEADS, SB_HEAD_DIM),
                            sk.reshape(B, S, SB_HEADS, SB_HEAD_DIM),
                            sv.reshape(B, S, SB_HEADS, SB_HEAD_DIM))
        mixed = jnp.concatenate([hm.reshape(B, S, M_WIDTH), hs.reshape(B, S, SB_WIDTH)], axis=-1)
        h = h + mixed.astype(h.dtype) @ w_out[l]
        c = rmsnorm(h, g_ffn[l])
        h = h + hier_moe(c, w_router_group[l], b_router_group[l], w_router_expert[l],
                         b_router_expert[l], w_exp_gate[l], w_exp_up[l], w_exp_down[l])
        gate = jax.nn.sigmoid((rmsnorm(h, g_ple[l]) @ w_ple_gate[l]).astype(jnp.float32))
        ple = rmsnorm(p[l].astype(h.dtype) @ w_ple_proj[l], g_ple_post[l])
        h = h + (gate * ple.astype(jnp.float32)).astype(h.dtype)
    return rmsnorm(h, g_final)
```

```python
import functools

import jax
import jax.numpy as jnp
from jax import lax
from jax.experimental import pallas as pl
from jax.experimental.pallas import tpu as pltpu

F32 = jnp.float32
BF16 = jnp.bfloat16
EPS = 1e-6

M_HEADS = 4
M_HEAD_DIM = 128
SB_HEAD_DIM = 64
CONV_WIDTH = 4
TOP_K = 2
LANES = 128
VMEM_LIMIT = 56 * 1024 * 1024

MLSTM_CHUNK = 128
SB_BLOCK = 256
EXPERT_ROWS = 256
ROUTE_LANE_E = 0
ROUTE_LANE_W = 2
ROUTE_LANE_R = 4
LOGIT_LANE_E = 4


def _rms(x, g):
    return x * lax.rsqrt(jnp.mean(x * x, axis=-1, keepdims=True) + EPS) * g


def _sigmoid(x):
    return 1.0 / (1.0 + jnp.exp(-x))


def _split3(a):
    a1 = a.astype(BF16)
    r1 = a - a1.astype(F32)
    a2 = r1.astype(BF16)
    a3 = (r1 - a2.astype(F32)).astype(BF16)
    return a1, a2, a3


def _dot(a, b):
    return jnp.dot(a, b, preferred_element_type=F32)


def _dot_nt(a, b):
    return lax.dot_general(a, b, (((1,), (1,)), ((), ())), preferred_element_type=F32)


def _dot_tn(a, b):
    return lax.dot_general(a, b, (((0,), (0,)), ((), ())), preferred_element_type=F32)


def _inproj_kernel(x_ref, g_ref, wqk_ref, wvo_ref, wg_ref, ws_ref, cq_ref, ck_ref,
                   mq_ref, mk_ref, mv_ref, mo_ref, gate_ref, sq_ref, sk_ref, sv_ref,
                   ext_ref, *, tiles_per_seq, k_scale):
    i = pl.program_id(0)
    tm = x_ref.shape[0]
    mw = mq_ref.shape[1]
    sw = sq_ref.shape[1]
    a = _rms(x_ref[...], g_ref[...]).astype(BF16)

    @pl.when(i % tiles_per_seq == 0)
    def _():
        ext_ref[0:8, :] = jnp.zeros((8, 2 * mw), F32)

    ext_ref[8:8 + tm, :] = _dot(a, wqk_ref[...])

    def conv_silu(w_ref, c0):
        acc = ext_ref[pl.ds(8 - (CONV_WIDTH - 1), tm), c0:c0 + mw] * w_ref[0:1, :]
        for j in range(1, CONV_WIDTH):
            acc = acc + ext_ref[pl.ds(8 - (CONV_WIDTH - 1) + j, tm), c0:c0 + mw] * w_ref[j:j + 1, :]
        return acc * _sigmoid(acc)

    mq_ref[...] = conv_silu(cq_ref, 0).astype(BF16)
    mk_ref[...] = (conv_silu(ck_ref, mw) * k_scale).astype(BF16)
    ext_ref[0:8, :] = ext_ref[tm:tm + 8, :]

    vo = _dot(a, wvo_ref[...])
    mv_ref[...] = vo[:, 0:mw].astype(BF16)
    mo_ref[...] = vo[:, mw:2 * mw].astype(BF16)
    gate_ref[...] = _dot(a, wg_ref[...])
    s = _dot(a, ws_ref[...])
    sq_ref[...] = s[:, 0:sw].astype(BF16)
    sk_ref[...] = s[:, sw:2 * sw].astype(BF16)
    sv_ref[...] = s[:, 2 * sw:3 * sw].astype(BF16)


def _inproj(x2, g_mix, w_in, conv_q, conv_k, seq_len, tm):
    t, d = x2.shape
    mw = conv_q.shape[1]
    h = M_HEADS
    sw = (w_in.shape[1] - 4 * mw - 2 * h) // 3
    wqk = w_in[:, 0:2 * mw].astype(BF16)
    wvo = w_in[:, 2 * mw:4 * mw].astype(BF16)
    wg = jnp.zeros((d, 2 * LANES), F32)
    wg = wg.at[:, 0:h].set(w_in[:, 4 * mw:4 * mw + h])
    wg = wg.at[:, LANES:LANES + h].set(w_in[:, 4 * mw + h:4 * mw + 2 * h]).astype(BF16)
    ws = w_in[:, 4 * mw + 2 * h:]
    ws = jnp.concatenate([ws[:, 0:sw] * (SB_HEAD_DIM ** -0.5), ws[:, sw:]], axis=1).astype(BF16)
    row = lambda i: (i, 0)
    const = lambda i: (0, 0)
    kern = functools.partial(_inproj_kernel, tiles_per_seq=seq_len // tm, k_scale=M_HEAD_DIM ** -0.5)
    bf = lambda w: jax.ShapeDtypeStruct((t, w), BF16)
    return pl.pallas_call(
        kern,
        grid=(t // tm,),
        in_specs=[
            pl.BlockSpec((tm, d), row),
            pl.BlockSpec((1, d), const),
            pl.BlockSpec((d, 2 * mw), const),
            pl.BlockSpec((d, 2 * mw), const),
            pl.BlockSpec((d, 2 * LANES), const),
            pl.BlockSpec((d, 3 * sw), const),
            pl.BlockSpec((CONV_WIDTH, mw), const),
            pl.BlockSpec((CONV_WIDTH, mw), const),
        ],
        out_specs=[
            pl.BlockSpec((tm, mw), row), pl.BlockSpec((tm, mw), row),
            pl.BlockSpec((tm, mw), row), pl.BlockSpec((tm, mw), row),
            pl.BlockSpec((tm, 2 * LANES), row),
            pl.BlockSpec((tm, sw), row), pl.BlockSpec((tm, sw), row), pl.BlockSpec((tm, sw), row),
        ],
        out_shape=[bf(mw), bf(mw), bf(mw), bf(mw),
                   jax.ShapeDtypeStruct((t, 2 * LANES), F32), bf(sw), bf(sw), bf(sw)],
        scratch_shapes=[pltpu.VMEM((tm + 8, 2 * mw), F32)],
        compiler_params=pltpu.CompilerParams(
            dimension_semantics=("arbitrary",), vmem_limit_bytes=VMEM_LIMIT),
        name="inproj",
    )(x2, g_mix.reshape(1, d), wqk, wvo, wg, ws, conv_q, conv_k)


def _mlstm_kernel(q_ref, k_ref, v_ref, o_ref, gate_ref, bias_ref, gh_ref, out_ref,
                  c_ref, m_ref, *, chunk):
    L = chunk
    hd = M_HEAD_DIM
    nchunks = q_ref.shape[0] // L

    @pl.when(pl.program_id(1) == 0)
    def _():
        c_ref[...] = jnp.zeros(c_ref.shape, F32)
        m_ref[...] = jnp.zeros(m_ref.shape, F32)

    rows = lax.broadcasted_iota(jnp.int32, (L, L), 0)
    cols = lax.broadcasted_iota(jnp.int32, (L, L), 1)
    causal = cols <= rows
    tri = causal.astype(BF16)
    lane2 = lax.broadcasted_iota(jnp.int32, (L, 2 * hd), 1)
    ones_col = (lane2 == hd).astype(F32)

    def chunk_body(c, _):
        r0 = pl.multiple_of(c * L, L)
        g = gate_ref[pl.ds(r0, L), :] + bias_ref[...]
        gi = g[:, 0:LANES]
        gf = g[:, LANES:2 * LANES]
        lf = jnp.minimum(gf, 0.0) - jnp.log(1.0 + jnp.exp(-jnp.abs(gf)))
        l1, l2, l3 = _split3(lf)
        b = _dot(tri, l1) + _dot(tri, l2) + _dot(tri, l3)
        b_last = b[L - 1:L, :]
        w_end = b_last - b + gi
        m_loc = jnp.max(w_end, axis=0, keepdims=True)
        e_end = jnp.exp(w_end - m_loc)
        m_prev = m_ref[...]
        m_new = jnp.maximum(b_last + m_prev, m_loc)
        decay = jnp.exp(b_last + m_prev - m_new)
        scale = jnp.exp(m_loc - m_new)
        b_t = b.T
        gi_t = gi.T
        for h in range(M_HEADS):
            hs = slice(h * hd, (h + 1) * hd)
            qh = q_ref[pl.ds(r0, L), hs]
            kh = k_ref[pl.ds(r0, L), hs]
            vh = v_ref[pl.ds(r0, L), hs].astype(F32)
            vext = jnp.concatenate([vh, jnp.zeros((L, hd), F32)], axis=1) + ones_col
            bc = b[:, h:h + 1]
            e = jnp.where(causal, bc - b_t[h:h + 1, :] + gi_t[h:h + 1, :], -jnp.inf)
            log_inter = bc + m_prev[:, h:h + 1]
            m_t = jnp.maximum(log_inter, jnp.max(e, axis=1, keepdims=True))
            w = (jnp.exp(e - m_t) * _dot_nt(qh, kh)).astype(BF16)
            a_int = jnp.exp(log_inter - m_t)
            cext = c_ref[h]
            num = _dot(w, vext.astype(BF16)) + a_int * _dot(qh, cext.astype(BF16))
            den = num[:, hd:hd + 1]
            hh = num[:, 0:hd] / jnp.maximum(jnp.abs(den), jnp.exp(-m_t))
            hh = _rms(hh, gh_ref[:, hs])
            og = _sigmoid(o_ref[pl.ds(r0, L), hs].astype(F32))
            out_ref[pl.ds(r0, L), hs] = (og * hh).astype(BF16)
            ev = (e_end[:, h:h + 1] * vext).astype(BF16)
            c_ref[h] = decay[:, h:h + 1] * cext + scale[:, h:h + 1] * _dot_tn(kh, ev)
        m_ref[...] = m_new
        return 0

    lax.fori_loop(0, nchunks, chunk_body, 0)


def _mlstm(mq, mk, mv, mo, gates, b_gates, g_mhead, batch, seq_len, rows):
    t, mw = mq.shape
    h = M_HEADS
    bias = jnp.zeros((1, 2 * LANES), F32)
    bias = bias.at[0, 0:h].set(b_gates[0:h]).at[0, LANES:LANES + h].set(b_gates[h:2 * h])
    nb = seq_len // rows
    row = lambda b, i: (b * nb + i, 0)
    const = lambda b, i: (0, 0)
    return pl.pallas_call(
        functools.partial(_mlstm_kernel, chunk=MLSTM_CHUNK),
        grid=(batch, nb),
        in_specs=[pl.BlockSpec((rows, mw), row)] * 4 + [
            pl.BlockSpec((rows, 2 * LANES), row),
            pl.BlockSpec((1, 2 * LANES), const),
            pl.BlockSpec((1, mw), const),
        ],
        out_specs=pl.BlockSpec((rows, mw), row),
        out_shape=jax.ShapeDtypeStruct((t, mw), BF16),
        scratch_shapes=[pltpu.VMEM((h, M_HEAD_DIM, 2 * M_HEAD_DIM), F32),
                        pltpu.VMEM((1, LANES), F32)],
        compiler_params=pltpu.CompilerParams(
            dimension_semantics=("arbitrary", "arbitrary"), vmem_limit_bytes=VMEM_LIMIT),
        name="mlstm",
    )(mq, mk, mv, mo, gates, bias, g_mhead.reshape(1, mw))


def _sb_kernel(q_ref, k_ref, v_ref, out_ref, acc_ref, carry_ref):
    blk = q_ref.shape[0]
    qi = pl.program_id(2)
    lane = lax.broadcasted_iota(jnp.int32, (blk, LANES), 1)
    head0 = lane < SB_HEAD_DIM
    q = q_ref[...]
    zero = jnp.zeros_like(q)
    qm = (jnp.where(head0, q, zero), jnp.where(head0, zero, q))
    rows = lax.broadcasted_iota(jnp.int32, (blk, blk), 0)
    cols = lax.broadcasted_iota(jnp.int32, (blk, blk), 1)
    strict = cols < rows
    neg_suffix = jnp.where(rows >= cols, -1.0, 0.0).astype(BF16)

    acc_ref[...] = jnp.zeros(acc_ref.shape, F32)
    carry_ref[...] = jnp.zeros(carry_ref.shape, F32)

    def block(j, masked):
        k0 = pl.multiple_of(j * blk, blk)
        kb = k_ref[pl.ds(k0, blk), :]
        vb = v_ref[pl.ds(k0, blk), :]
        vz = jnp.zeros_like(vb)
        vm = (jnp.where(head0, vb, vz), jnp.where(head0, vz, vb))
        upd = None
        for h in range(2):
            z = _dot_nt(qm[h], kb)
            sp = jnp.maximum(z, 0.0) + jnp.log(1.0 + jnp.exp(-jnp.abs(z)))
            if masked:
                sp = jnp.where(strict, sp, 0.0)
            rc = _dot(sp.astype(BF16), neg_suffix)
            carry = carry_ref[h]
            p = jnp.exp(z + rc + carry)
            if masked:
                p = jnp.where(strict, p, 0.0)
            carry_ref[h] = carry + rc[:, 0:1]
            d = _dot(p.astype(BF16), vm[h])
            upd = d if upd is None else upd + d
        acc_ref[...] += upd

    block(qi, True)

    def body(it, _):
        block(qi - 1 - it, False)
        return 0

    lax.fori_loop(0, qi, body, 0)
    out_ref[...] = acc_ref[...].astype(BF16)


def _stickbreak(sq, sk, sv, batch, seq_len):
    t, sw = sq.shape
    blk = SB_BLOCK
    nq = seq_len // blk
    npair = sw // LANES
    return pl.pallas_call(
        _sb_kernel,
        grid=(batch, npair, nq),
        in_specs=[
            pl.BlockSpec((blk, LANES), lambda b, hp, qi: (b * nq + qi, hp)),
            pl.BlockSpec((seq_len, LANES), lambda b, hp, qi: (b, hp)),
            pl.BlockSpec((seq_len, LANES), lambda b, hp, qi: (b, hp)),
        ],
        out_specs=pl.BlockSpec((blk, LANES), lambda b, hp, qi: (b * nq + qi, hp)),
        out_shape=jax.ShapeDtypeStruct((t, sw), BF16),
        scratch_shapes=[pltpu.VMEM((blk, LANES), F32), pltpu.VMEM((2, blk, 1), F32)],
        compiler_params=pltpu.CompilerParams(
            dimension_semantics=("arbitrary", "arbitrary", "arbitrary"),
            vmem_limit_bytes=VMEM_LIMIT),
        name="stickbrk",
    )(sq, sk, sv)


def _outroute_kernel(x_ref, hm_ref, hs_ref, wom_ref, wos_ref, g_ref, wr_ref, br_ref,
                     h1_ref, c_ref, route_ref, cnt_ref, run_ref, *, n_groups, per_group):
    i = pl.program_id(0)
    tm = x_ref.shape[0]

    @pl.when(i == 0)
    def _():
        run_ref[...] = jnp.zeros(run_ref.shape, F32)

    h1 = x_ref[...] + _dot(hm_ref[...], wom_ref[...]) + _dot(hs_ref[...], wos_ref[...])
    h1_ref[...] = h1
    c = _rms(h1, g_ref[...])
    c_ref[...] = c

    c1, c2, c3 = _split3(c)
    w1, w2, w3 = wr_ref[0], wr_ref[1], wr_ref[2]
    logits = (_dot(c1, w1) + (_dot(c1, w2) + _dot(c2, w1))
              + (_dot(c1, w3) + _dot(c2, w2) + _dot(c3, w1))) + br_ref[...]

    lane = lax.broadcasted_iota(jnp.int32, (tm, LANES), 1).astype(F32)
    ninf = -jnp.inf
    big = float(LANES)

    def first_max(v):
        mx = jnp.max(v, axis=1, keepdims=True)
        idx = jnp.min(jnp.where(v == mx, lane, big), axis=1, keepdims=True)
        return mx, idx

    gl = jnp.where(lane < n_groups, logits, ninf)
    gmax, gsel = first_max(gl)
    p_g = 1.0 / jnp.sum(jnp.exp(gl - gmax), axis=1, keepdims=True)
    lo = LOGIT_LANE_E + per_group * gsel
    el = jnp.where((lane >= lo) & (lane < lo + per_group), logits, ninf)
    v1, i1 = first_max(el)
    v2, i2 = first_max(jnp.where(lane == i1, ninf, el))
    tt = jnp.exp(v2 - v1)
    w0 = p_g / (1.0 + tt)
    w1_ = p_g * tt / (1.0 + tt)

    oh0 = lane == i1
    oh1 = lane == i2
    ohsum = oh0.astype(F32) + oh1.astype(F32)
    rows = lax.broadcasted_iota(jnp.int32, (tm, tm), 0)
    cols = lax.broadcasted_iota(jnp.int32, (tm, tm), 1)
    before = (cols < rows).astype(BF16)
    prefix = _dot(before, ohsum.astype(BF16)) + run_ref[...]
    r0 = jnp.sum(jnp.where(oh0, prefix, 0.0), axis=1, keepdims=True)
    r1 = jnp.sum(jnp.where(oh1, prefix, 0.0), axis=1, keepdims=True)
    run = run_ref[...] + jnp.sum(ohsum, axis=0, keepdims=True)
    run_ref[...] = run
    cnt_ref[...] = jnp.broadcast_to(run, cnt_ref.shape)

    e0 = i1 - LOGIT_LANE_E
    e1 = i2 - LOGIT_LANE_E
    route = jnp.zeros((tm, LANES), F32)
    for ln, val in ((ROUTE_LANE_E, e0), (ROUTE_LANE_E + 1, e1), (ROUTE_LANE_W, w0),
                    (ROUTE_LANE_W + 1, w1_), (ROUTE_LANE_R, r0), (ROUTE_LANE_R + 1, r1)):
        route = jnp.where(lane == ln, val, route)
    route_ref[...] = route


def _outroute(x2, hm, hs, w_out, g_ffn, w_rg, b_rg, w_re, b_re, tm):
    t, d = x2.shape
    mw = hm.shape[1]
    sw = hs.shape[1]
    n_groups = w_rg.shape[1]
    n_exp = w_re.shape[1]
    wr = jnp.zeros((d, LANES), F32)
    wr = wr.at[:, 0:n_groups].set(w_rg).at[:, LOGIT_LANE_E:LOGIT_LANE_E + n_exp].set(w_re)
    wr3 = jnp.stack(_split3(wr))
    br = jnp.zeros((1, LANES), F32)
    br = br.at[0, 0:n_groups].set(b_rg).at[0, LOGIT_LANE_E:LOGIT_LANE_E + n_exp].set(b_re)
    row = lambda i: (i, 0)
    const = lambda i: (0, 0)
    kern = functools.partial(_outroute_kernel, n_groups=n_groups, per_group=n_exp // n_groups)
    return pl.pallas_call(
        kern,
        grid=(t // tm,),
        in_specs=[
            pl.BlockSpec((tm, d), row),
            pl.BlockSpec((tm, mw), row),
            pl.BlockSpec((tm, sw), row),
            pl.BlockSpec((mw, d), const),
            pl.BlockSpec((sw, d), const),
            pl.BlockSpec((1, d), const),
            pl.BlockSpec((3, d, LANES), lambda i: (0, 0, 0)),
            pl.BlockSpec((1, LANES), const),
        ],
        out_specs=[
            pl.BlockSpec((tm, d), row),
            pl.BlockSpec((tm, d), row),
            pl.BlockSpec((tm, LANES), row),
            pl.BlockSpec((8, LANES), const),
        ],
        out_shape=[
            jax.ShapeDtypeStruct((t, d), F32),
            jax.ShapeDtypeStruct((t, d), F32),
            jax.ShapeDtypeStruct((t, LANES), F32),
            jax.ShapeDtypeStruct((8, LANES), F32),
        ],
        scratch_shapes=[pltpu.VMEM((1, LANES), F32)],
        compiler_params=pltpu.CompilerParams(
            dimension_semantics=("arbitrary",), vmem_limit_bytes=VMEM_LIMIT),
        name="outroute",
    )(x2, hm, hs, w_out[0:mw].astype(BF16), w_out[mw:].astype(BF16), g_ffn.reshape(1, d),
      wr3, br)


def _slotpos_kernel(route_ref, offs_ref, pos_ref):
    route = route_ref[...]
    tm = route.shape[0]
    lane = lax.broadcasted_iota(jnp.int32, (tm, LANES), 1)
    offs = offs_ref[...]
    out = jnp.zeros((tm, LANES), F32)
    for j in range(TOP_K):
        e = route[:, ROUTE_LANE_E + j:ROUTE_LANE_E + j + 1].astype(jnp.int32)
        base = jnp.sum(jnp.where(lane == e, offs, 0.0), axis=1, keepdims=True)
        out = jnp.where(lane == j, base + route[:, ROUTE_LANE_R + j:ROUTE_LANE_R + j + 1], out)
    pos_ref[...] = out.astype(jnp.int32)


def _slotpos(route, offs_row, tm):
    t = route.shape[0]
    return pl.pallas_call(
        _slotpos_kernel,
        grid=(t // tm,),
        in_specs=[pl.BlockSpec((tm, LANES), lambda i: (i, 0)),
                  pl.BlockSpec((1, LANES), lambda i: (0, 0))],
        out_specs=pl.BlockSpec((tm, LANES), lambda i: (i, 0)),
        out_shape=jax.ShapeDtypeStruct((t, LANES), jnp.int32),
        compiler_params=pltpu.CompilerParams(dimension_semantics=("arbitrary",)),
        name="slotpos",
    )(route, offs_row)


def _row_copy(src_ref, src_row, dst_ref, dst_row, sem):
    return pltpu.make_async_copy(src_ref.at[pl.ds(src_row, 1), :],
                                 dst_ref.at[pl.ds(dst_row, 1), :], sem)


def _dispatch_kernel(pos_ref, c_ref, init_ref, xin_ref, sem):
    del init_ref
    tm = c_ref.shape[0]

    def issue(t, _):
        for j in range(TOP_K):
            _row_copy(c_ref, t, xin_ref, pos_ref[0, 0, TOP_K * t + j], sem).start()
        return 0

    lax.fori_loop(0, tm, issue, 0)

    def drain(t, _):
        for j in range(TOP_K):
            _row_copy(c_ref, 0, xin_ref, 0, sem).wait()
        return 0

    lax.fori_loop(0, tm, drain, 0)


def _dispatch(pos3, c, n_rows, tm):
    t, d = c.shape
    init = jnp.zeros((n_rows, d), F32)
    return pl.pallas_call(
        _dispatch_kernel,
        grid=(t // tm,),
        in_specs=[
            pl.BlockSpec((1, 1, TOP_K * tm), lambda i: (i, 0, 0), memory_space=pltpu.SMEM),
            pl.BlockSpec((tm, d), lambda i: (i, 0)),
            pl.BlockSpec(memory_space=pl.ANY),
        ],
        out_specs=pl.BlockSpec(memory_space=pl.ANY),
        out_shape=jax.ShapeDtypeStruct((n_rows, d), F32),
        scratch_shapes=[pltpu.SemaphoreType.DMA(())],
        input_output_aliases={2: 0},
        compiler_params=pltpu.CompilerParams(dimension_semantics=("arbitrary",)),
        name="dispatch",
    )(pos3, c, init)


def _experts_kernel(be_ref, nv_ref, x_ref, wg_ref, wu_ref, wd_ref, y_ref, wgb, wub, wdb):
    i = pl.program_id(0)
    prev = be_ref[jnp.maximum(i - 1, 0)]
    active = i < nv_ref[0]

    @pl.when(active & ((i == 0) | (be_ref[i] != prev)))
    def _():
        wgb[...] = wg_ref[0].astype(BF16)
        wub[...] = wu_ref[0].astype(BF16)
        wdb[...] = wd_ref[0].astype(BF16)

    @pl.when(active)
    def _():
        x = x_ref[...].astype(BF16)
        gt = _dot(x, wgb[...])
        up = _dot(x, wub[...])
        hid = (gt * _sigmoid(gt) * up).astype(BF16)
        y_ref[...] = _dot(hid, wdb[...])

    @pl.when(jnp.logical_not(active))
    def _():
        y_ref[...] = jnp.zeros(y_ref.shape, F32)


def _experts(block_e, n_valid, xin, w_gate, w_up, w_down):
    r, d = xin.shape
    de = w_gate.shape[2]
    nblk = r // EXPERT_ROWS
    grid_spec = pltpu.PrefetchScalarGridSpec(
        num_scalar_prefetch=2,
        grid=(nblk,),
        in_specs=[
            pl.BlockSpec((EXPERT_ROWS, d), lambda i, be, nv: (i, 0)),
            pl.BlockSpec((1, d, de), lambda i, be, nv: (be[i], 0, 0)),
            pl.BlockSpec((1, d, de), lambda i, be, nv: (be[i], 0, 0)),
            pl.BlockSpec((1, de, d), lambda i, be, nv: (be[i], 0, 0)),
        ],
        out_specs=pl.BlockSpec((EXPERT_ROWS, d), lambda i, be, nv: (i, 0)),
        scratch_shapes=[pltpu.VMEM((d, de), BF16), pltpu.VMEM((d, de), BF16),
                        pltpu.VMEM((de, d), BF16)],
    )
    return pl.pallas_call(
        _experts_kernel,
        grid_spec=grid_spec,
        out_shape=jax.ShapeDtypeStruct((r, d), F32),
        compiler_params=pltpu.CompilerParams(
            dimension_semantics=("arbitrary",), vmem_limit_bytes=VMEM_LIMIT),
        name="experts",
    )(block_e, n_valid, xin, w_gate, w_up, w_down)


def _combine_kernel(pos_ref, h1_ref, route_ref, p_ref, y_ref, wpg_ref, wpp_ref,
                    gple_ref, gpost_ref, gfin_ref, out_ref, ybuf, sem):
    tm = h1_ref.shape[0]

    def issue(t, _):
        for j in range(TOP_K):
            _row_copy(y_ref, pos_ref[0, 0, TOP_K * t + j], ybuf.at[j], t, sem).start()
        return 0

    lax.fori_loop(0, tm, issue, 0)

    def drain(t, _):
        for j in range(TOP_K):
            _row_copy(y_ref, 0, ybuf.at[j], 0, sem).wait()
        return 0

    lax.fori_loop(0, tm, drain, 0)

    route = route_ref[...]
    w0 = route[:, ROUTE_LANE_W:ROUTE_LANE_W + 1]
    w1 = route[:, ROUTE_LANE_W + 1:ROUTE_LANE_W + 2]
    h2 = h1_ref[...] + (w0 * ybuf[0] + w1 * ybuf[1])
    gate = _sigmoid(_dot(_rms(h2, gple_ref[...]).astype(BF16), wpg_ref[...]))
    ple = _rms(_dot(p_ref[...].astype(BF16), wpp_ref[...]), gpost_ref[...])
    h3 = h2 + gate * ple
    out_ref[...] = _rms(h3, gfin_ref[...])


def _combine(pos3, h1, route, p2, y, w_pg, w_pp, g_ple, g_post, g_final, tm):
    t, d = h1.shape
    pd = p2.shape[1]
    row = lambda i: (i, 0)
    const = lambda i: (0, 0)
    return pl.pallas_call(
        _combine_kernel,
        grid=(t // tm,),
        in_specs=[
            pl.BlockSpec((1, 1, TOP_K * tm), lambda i: (i, 0, 0), memory_space=pltpu.SMEM),
            pl.BlockSpec((tm, d), row),
            pl.BlockSpec((tm, LANES), row),
            pl.BlockSpec((tm, pd), row),
            pl.BlockSpec(memory_space=pl.ANY),
            pl.BlockSpec((d, d), const),
            pl.BlockSpec((pd, d), const),
            pl.BlockSpec((1, d), const),
            pl.BlockSpec((1, d), const),
            pl.BlockSpec((1, d), const),
        ],
        out_specs=pl.BlockSpec((tm, d), row),
        out_shape=jax.ShapeDtypeStruct((t, d), F32),
        scratch_shapes=[pltpu.VMEM((TOP_K, tm, d), F32), pltpu.SemaphoreType.DMA(())],
        compiler_params=pltpu.CompilerParams(
            dimension_semantics=("arbitrary",), vmem_limit_bytes=VMEM_LIMIT),
        name="combine",
    )(pos3, h1, route, p2, y, w_pg.astype(BF16), w_pp.astype(BF16),
      g_ple.reshape(1, d), g_post.reshape(1, d), g_final.reshape(1, d))


def _largest_tile(n, cap):
    tile = cap
    while n % tile:
        tile //= 2
    return tile


def kernel(x, p, g_mix, w_in, b_gates, conv_q, conv_k, g_mhead, w_out, g_ffn, w_router_group,
           b_router_group, w_router_expert, b_router_expert, w_exp_gate, w_exp_up, w_exp_down,
           g_ple, w_ple_gate, w_ple_proj, g_ple_post, g_final):
    batch, seq_len, d = x.shape
    t = batch * seq_len
    tm = _largest_tile(seq_len, 512)
    tm_rows = _largest_tile(seq_len, 256)
    n_exp = w_router_expert.shape[-1]
    n_rows = t * TOP_K + n_exp * EXPERT_ROWS
    nblk = n_rows // EXPERT_ROWS

    assert w_in.shape[0] == 1, "single-layer block"
    l = 0
    h = x.reshape(t, d)
    mq, mk, mv, mo, gates, sq, sk, sv = _inproj(
        h, g_mix[l], w_in[l], conv_q[l], conv_k[l], seq_len, tm)
    hm = _mlstm(mq, mk, mv, mo, gates, b_gates[l], g_mhead[l], batch, seq_len,
                _largest_tile(seq_len, 1024))
    hs = _stickbreak(sq, sk, sv, batch, seq_len)
    h1, c, route, counts = _outroute(
        h, hm, hs, w_out[l], g_ffn[l], w_router_group[l], b_router_group[l],
        w_router_expert[l], b_router_expert[l], tm)

    cnt = counts[0, LOGIT_LANE_E:LOGIT_LANE_E + n_exp].astype(jnp.int32)
    padded = ((cnt + EXPERT_ROWS - 1) // EXPERT_ROWS) * EXPERT_ROWS
    pend = jnp.cumsum(padded)
    offs = pend - padded
    offs_row = jnp.zeros((1, LANES), F32).at[0, 0:n_exp].set(offs.astype(F32))
    block_start = jnp.arange(nblk, dtype=jnp.int32) * EXPERT_ROWS
    block_e = jnp.minimum(jnp.searchsorted(pend, block_start, side='right'),
                          n_exp - 1).astype(jnp.int32)
    n_valid = (pend[-1:] // EXPERT_ROWS).astype(jnp.int32)

    pos = _slotpos(route, offs_row, tm)
    pos3 = pos[:, 0:TOP_K].reshape(t // tm_rows, 1, TOP_K * tm_rows)
    xin = _dispatch(pos3, c, n_rows, tm_rows)
    y = _experts(block_e, n_valid, xin, w_exp_gate[l], w_exp_up[l], w_exp_down[l])
    out = _combine(pos3, h1, route, p[l].reshape(t, -1), y, w_ple_gate[l], w_ple_proj[l],
                   g_ple[l], g_ple_post[l], g_final, tm_rows)
    return out.reshape(batch, seq_len, d)
```

```python
import functools

import jax
import jax.numpy as jnp
from jax import lax
from jax.experimental import pallas as pl
from jax.experimental.pallas import tpu as pltpu

F32 = jnp.float32
BF16 = jnp.bfloat16
EPS = 1e-6

M_HEADS = 4
M_HEAD_DIM = 128
SB_HEAD_DIM = 64
CONV_WIDTH = 4
TOP_K = 2
LANES = 128
VMEM_LIMIT = 56 * 1024 * 1024

MLSTM_CHUNK = 128
SB_BLOCK = 256
SB_ZERO_LOG = -105.0
EXPERT_ROWS = 256
ROW_DMA_UNROLL = 8
ROUTE_LANE_E = 0
ROUTE_LANE_W = 2
ROUTE_LANE_R = 4
LOGIT_LANE_E = 4


def _rms(x, g):
    return x * lax.rsqrt(jnp.mean(x * x, axis=-1, keepdims=True) + EPS) * g


def _sigmoid(x):
    return 1.0 / (1.0 + jnp.exp(-x))


def _split3(a):
    a1 = a.astype(BF16)
    r1 = a - a1.astype(F32)
    a2 = r1.astype(BF16)
    a3 = (r1 - a2.astype(F32)).astype(BF16)
    return a1, a2, a3


def _dot(a, b):
    return jnp.dot(a, b, preferred_element_type=F32)


def _dot_nt(a, b):
    return lax.dot_general(a, b, (((1,), (1,)), ((), ())), preferred_element_type=F32)


def _dot_tn(a, b):
    return lax.dot_general(a, b, (((0,), (0,)), ((), ())), preferred_element_type=F32)


def _inproj_kernel(x_ref, g_ref, wqk_ref, wvo_ref, wg_ref, ws_ref, cq_ref, ck_ref,
                   mq_ref, mk_ref, mv_ref, mo_ref, gate_ref, sq_ref, sk_ref, sv_ref,
                   ext_ref, *, tiles_per_seq, k_scale):
    i = pl.program_id(0)
    tm = x_ref.shape[0]
    mw = mq_ref.shape[1]
    sw = sq_ref.shape[1]
    a = _rms(x_ref[...], g_ref[...]).astype(BF16)

    @pl.when(i % tiles_per_seq == 0)
    def _():
        ext_ref[0:8, :] = jnp.zeros((8, 2 * mw), F32)

    ext_ref[8:8 + tm, :] = _dot(a, wqk_ref[...])

    def conv_silu(w_ref, c0):
        acc = ext_ref[pl.ds(8 - (CONV_WIDTH - 1), tm), c0:c0 + mw] * w_ref[0:1, :]
        for j in range(1, CONV_WIDTH):
            acc = acc + ext_ref[pl.ds(8 - (CONV_WIDTH - 1) + j, tm), c0:c0 + mw] * w_ref[j:j + 1, :]
        return acc * _sigmoid(acc)

    mq_ref[...] = conv_silu(cq_ref, 0).astype(BF16)
    mk_ref[...] = (conv_silu(ck_ref, mw) * k_scale).astype(BF16)
    ext_ref[0:8, :] = ext_ref[tm:tm + 8, :]

    vo = _dot(a, wvo_ref[...])
    mv_ref[...] = vo[:, 0:mw].astype(BF16)
    mo_ref[...] = vo[:, mw:2 * mw].astype(BF16)
    gate_ref[...] = _dot(a, wg_ref[...])
    s = _dot(a, ws_ref[...])
    sq_ref[...] = s[:, 0:sw].astype(BF16)
    sk_ref[...] = s[:, sw:2 * sw].astype(BF16)
    sv_ref[...] = s[:, 2 * sw:3 * sw].astype(BF16)


def _inproj(x2, g_mix, w_in, conv_q, conv_k, seq_len, tm):
    t, d = x2.shape
    mw = conv_q.shape[1]
    h = M_HEADS
    sw = (w_in.shape[1] - 4 * mw - 2 * h) // 3
    wqk = w_in[:, 0:2 * mw].astype(BF16)
    wvo = w_in[:, 2 * mw:4 * mw].astype(BF16)
    wg = jnp.zeros((d, 2 * LANES), F32)
    wg = wg.at[:, 0:h].set(w_in[:, 4 * mw:4 * mw + h])
    wg = wg.at[:, LANES:LANES + h].set(w_in[:, 4 * mw + h:4 * mw + 2 * h]).astype(BF16)
    ws = w_in[:, 4 * mw + 2 * h:]
    ws = jnp.concatenate([ws[:, 0:sw] * (SB_HEAD_DIM ** -0.5), ws[:, sw:]], axis=1).astype(BF16)
    row = lambda i: (i, 0)
    const = lambda i: (0, 0)
    kern = functools.partial(_inproj_kernel, tiles_per_seq=seq_len // tm, k_scale=M_HEAD_DIM ** -0.5)
    bf = lambda w: jax.ShapeDtypeStruct((t, w), BF16)
    return pl.pallas_call(
        kern,
        grid=(t // tm,),
        in_specs=[
            pl.BlockSpec((tm, d), row),
            pl.BlockSpec((1, d), const),
            pl.BlockSpec((d, 2 * mw), const),
            pl.BlockSpec((d, 2 * mw), const),
            pl.BlockSpec((d, 2 * LANES), const),
            pl.BlockSpec((d, 3 * sw), const),
            pl.BlockSpec((CONV_WIDTH, mw), const),
            pl.BlockSpec((CONV_WIDTH, mw), const),
        ],
        out_specs=[
            pl.BlockSpec((tm, mw), row), pl.BlockSpec((tm, mw), row),
            pl.BlockSpec((tm, mw), row), pl.BlockSpec((tm, mw), row),
            pl.BlockSpec((tm, 2 * LANES), row),
            pl.BlockSpec((tm, sw), row), pl.BlockSpec((tm, sw), row), pl.BlockSpec((tm, sw), row),
        ],
        out_shape=[bf(mw), bf(mw), bf(mw), bf(mw),
                   jax.ShapeDtypeStruct((t, 2 * LANES), F32), bf(sw), bf(sw), bf(sw)],
        scratch_shapes=[pltpu.VMEM((tm + 8, 2 * mw), F32)],
        compiler_params=pltpu.CompilerParams(
            dimension_semantics=("arbitrary",), vmem_limit_bytes=VMEM_LIMIT),
        name="inproj",
    )(x2, g_mix.reshape(1, d), wqk, wvo, wg, ws, conv_q, conv_k)


def _mlstm_kernel(q_ref, k_ref, v_ref, o_ref, gate_ref, bias_ref, gh_ref, out_ref,
                  c_ref, m_ref, *, chunk):
    L = chunk
    hd = M_HEAD_DIM
    nchunks = q_ref.shape[0] // L

    @pl.when(pl.program_id(1) == 0)
    def _():
        c_ref[...] = jnp.zeros(c_ref.shape, F32)
        m_ref[...] = jnp.zeros(m_ref.shape, F32)

    rows = lax.broadcasted_iota(jnp.int32, (L, L), 0)
    cols = lax.broadcasted_iota(jnp.int32, (L, L), 1)
    causal = cols <= rows
    tri = causal.astype(BF16)
    lane2 = lax.broadcasted_iota(jnp.int32, (L, 2 * hd), 1)
    ones_col = (lane2 == hd).astype(F32)

    def chunk_body(c, _):
        r0 = pl.multiple_of(c * L, L)
        g = gate_ref[pl.ds(r0, L), :] + bias_ref[...]
        gi = g[:, 0:LANES]
        gf = g[:, LANES:2 * LANES]
        lf = jnp.minimum(gf, 0.0) - jnp.log(1.0 + jnp.exp(-jnp.abs(gf)))
        l1, l2, l3 = _split3(lf)
        b = _dot(tri, l1) + _dot(tri, l2) + _dot(tri, l3)
        b_last = b[L - 1:L, :]
        w_end = b_last - b + gi
        m_loc = jnp.max(w_end, axis=0, keepdims=True)
        e_end = jnp.exp(w_end - m_loc)
        m_prev = m_ref[...]
        m_new = jnp.maximum(b_last + m_prev, m_loc)
        decay = jnp.exp(b_last + m_prev - m_new)
        scale = jnp.exp(m_loc - m_new)
        b_t = b.T
        gi_t = gi.T
        for h in range(M_HEADS):
            hs = slice(h * hd, (h + 1) * hd)
            qh = q_ref[pl.ds(r0, L), hs]
            kh = k_ref[pl.ds(r0, L), hs]
            vh = v_ref[pl.ds(r0, L), hs].astype(F32)
            vext = jnp.concatenate([vh, jnp.zeros((L, hd), F32)], axis=1) + ones_col
            bc = b[:, h:h + 1]
            e = jnp.where(causal, bc - b_t[h:h + 1, :] + gi_t[h:h + 1, :], -jnp.inf)
            log_inter = bc + m_prev[:, h:h + 1]
            m_t = jnp.maximum(log_inter, jnp.max(e, axis=1, keepdims=True))
            w = (jnp.exp(e - m_t) * _dot_nt(qh, kh)).astype(BF16)
            a_int = jnp.exp(log_inter - m_t)
            cext = c_ref[h]
            num = _dot(w, vext.astype(BF16)) + a_int * _dot(qh, cext.astype(BF16))
            den = num[:, hd:hd + 1]
            hh = num[:, 0:hd] / jnp.maximum(jnp.abs(den), jnp.exp(-m_t))
            hh = _rms(hh, gh_ref[:, hs])
            og = _sigmoid(o_ref[pl.ds(r0, L), hs].astype(F32))
            out_ref[pl.ds(r0, L), hs] = (og * hh).astype(BF16)
            ev = (e_end[:, h:h + 1] * vext).astype(BF16)
            c_ref[h] = decay[:, h:h + 1] * cext + scale[:, h:h + 1] * _dot_tn(kh, ev)
        m_ref[...] = m_new
        return 0

    lax.fori_loop(0, nchunks, chunk_body, 0)


def _mlstm(mq, mk, mv, mo, gates, b_gates, g_mhead, batch, seq_len, rows):
    t, mw = mq.shape
    h = M_HEADS
    bias = jnp.zeros((1, 2 * LANES), F32)
    bias = bias.at[0, 0:h].set(b_gates[0:h]).at[0, LANES:LANES + h].set(b_gates[h:2 * h])
    nb = seq_len // rows
    row = lambda b, i: (b * nb + i, 0)
    const = lambda b, i: (0, 0)
    return pl.pallas_call(
        functools.partial(_mlstm_kernel, chunk=MLSTM_CHUNK),
        grid=(batch, nb),
        in_specs=[pl.BlockSpec((rows, mw), row)] * 4 + [
            pl.BlockSpec((rows, 2 * LANES), row),
            pl.BlockSpec((1, 2 * LANES), const),
            pl.BlockSpec((1, mw), const),
        ],
        out_specs=pl.BlockSpec((rows, mw), row),
        out_shape=jax.ShapeDtypeStruct((t, mw), BF16),
        scratch_shapes=[pltpu.VMEM((h, M_HEAD_DIM, 2 * M_HEAD_DIM), F32),
                        pltpu.VMEM((1, LANES), F32)],
        compiler_params=pltpu.CompilerParams(
            dimension_semantics=("arbitrary", "arbitrary"), vmem_limit_bytes=VMEM_LIMIT),
        name="mlstm",
    )(mq, mk, mv, mo, gates, bias, g_mhead.reshape(1, mw))


def _sb_kernel(q_ref, k_ref, v_ref, out_ref, acc_ref, carry_ref):
    blk = q_ref.shape[0]
    qi = pl.program_id(2)
    lane = lax.broadcasted_iota(jnp.int32, (blk, LANES), 1)
    head0 = lane < SB_HEAD_DIM
    q = q_ref[...]
    zero = jnp.zeros_like(q)
    qm = (jnp.where(head0, q, zero), jnp.where(head0, zero, q))
    rows = lax.broadcasted_iota(jnp.int32, (blk, blk), 0)
    cols = lax.broadcasted_iota(jnp.int32, (blk, blk), 1)
    strict = cols < rows
    neg_suffix = jnp.where(rows >= cols, -1.0, 0.0).astype(BF16)

    def block(j, carries, mask=None):
        k0 = pl.multiple_of(j * blk, blk)
        kb = k_ref[pl.ds(k0, blk), :]
        vb = v_ref[pl.ds(k0, blk), :]
        vz = jnp.zeros_like(vb)
        vm = (jnp.where(head0, vb, vz), jnp.where(head0, vz, vb))
        upd = None
        new = []
        for h in range(2):
            z = _dot_nt(qm[h], kb)
            sp = jnp.maximum(z, 0.0) + jnp.log(1.0 + jnp.exp(-jnp.abs(z)))
            if mask is not None:
                sp = jnp.where(mask, sp, 0.0)
            rc = _dot(sp.astype(BF16), neg_suffix)
            p = jnp.exp(z + rc + carries[h])
            if mask is not None:
                p = jnp.where(mask, p, 0.0)
            new.append(carries[h] + rc[:, 0:1])
            d = _dot(p.astype(BF16), vm[h])
            upd = d if upd is None else upd + d
        return upd, new

    zeros = jnp.zeros((blk, 1), F32)
    u_diag, carries = block(qi, [zeros, zeros], strict)
    has_prev = jnp.broadcast_to(qi > 0, (blk, blk))
    u_prev, carries = block(jnp.maximum(qi - 1, 0), carries, has_prev)
    acc_ref[...] = u_diag + u_prev
    carry_ref[0] = carries[0]
    carry_ref[1] = carries[1]

    def cond(state):
        it, top = state
        return (it < qi) & (top > SB_ZERO_LOG)

    def body(state):
        it, _ = state
        upd, new = block(qi - 1 - it, [carry_ref[0], carry_ref[1]])
        acc_ref[...] += upd
        carry_ref[0] = new[0]
        carry_ref[1] = new[1]
        return it + 1, jnp.maximum(jnp.max(new[0]), jnp.max(new[1]))

    lax.while_loop(cond, body, (jnp.int32(1), jnp.maximum(jnp.max(carries[0]), jnp.max(carries[1]))))
    out_ref[...] = acc_ref[...].astype(BF16)


def _stickbreak(sq, sk, sv, batch, seq_len):
    t, sw = sq.shape
    blk = SB_BLOCK
    nq = seq_len // blk
    npair = sw // LANES
    return pl.pallas_call(
        _sb_kernel,
        grid=(batch, npair, nq),
        in_specs=[
            pl.BlockSpec((blk, LANES), lambda b, hp, qi: (b * nq + qi, hp)),
            pl.BlockSpec((seq_len, LANES), lambda b, hp, qi: (b, hp)),
            pl.BlockSpec((seq_len, LANES), lambda b, hp, qi: (b, hp)),
        ],
        out_specs=pl.BlockSpec((blk, LANES), lambda b, hp, qi: (b * nq + qi, hp)),
        out_shape=jax.ShapeDtypeStruct((t, sw), BF16),
        scratch_shapes=[pltpu.VMEM((blk, LANES), F32), pltpu.VMEM((2, blk, 1), F32)],
        compiler_params=pltpu.CompilerParams(
            dimension_semantics=("arbitrary", "arbitrary", "arbitrary"),
            vmem_limit_bytes=VMEM_LIMIT),
        name="stickbrk",
    )(sq, sk, sv)


def _outroute_kernel(x_ref, hm_ref, hs_ref, wom_ref, wos_ref, g_ref, wr_ref, br_ref,
                     h1_ref, c_ref, route_ref, cnt_ref, run_ref, *, n_groups, per_group):
    i = pl.program_id(0)
    tm = x_ref.shape[0]

    @pl.when(i == 0)
    def _():
        run_ref[...] = jnp.zeros(run_ref.shape, F32)

    h1 = x_ref[...] + _dot(hm_ref[...], wom_ref[...]) + _dot(hs_ref[...], wos_ref[...])
    h1_ref[...] = h1
    c = _rms(h1, g_ref[...])
    c_ref[...] = c

    c1, c2, c3 = _split3(c)
    w1, w2, w3 = wr_ref[0], wr_ref[1], wr_ref[2]
    logits = (_dot(c1, w1) + (_dot(c1, w2) + _dot(c2, w1))
              + (_dot(c1, w3) + _dot(c2, w2) + _dot(c3, w1))) + br_ref[...]

    lane = lax.broadcasted_iota(jnp.int32, (tm, LANES), 1).astype(F32)
    ninf = -jnp.inf
    big = float(LANES)

    def first_max(v):
        mx = jnp.max(v, axis=1, keepdims=True)
        idx = jnp.min(jnp.where(v == mx, lane, big), axis=1, keepdims=True)
        return mx, idx

    gl = jnp.where(lane < n_groups, logits, ninf)
    gmax, gsel = first_max(gl)
    p_g = 1.0 / jnp.sum(jnp.exp(gl - gmax), axis=1, keepdims=True)
    lo = LOGIT_LANE_E + per_group * gsel
    el = jnp.where((lane >= lo) & (lane < lo + per_group), logits, ninf)
    v1, i1 = first_max(el)
    v2, i2 = first_max(jnp.where(lane == i1, ninf, el))
    tt = jnp.exp(v2 - v1)
    w0 = p_g / (1.0 + tt)
    w1_ = p_g * tt / (1.0 + tt)

    oh0 = lane == i1
    oh1 = lane == i2
    ohsum = oh0.astype(F32) + oh1.astype(F32)
    rows = lax.broadcasted_iota(jnp.int32, (tm, tm), 0)
    cols = lax.broadcasted_iota(jnp.int32, (tm, tm), 1)
    before = (cols < rows).astype(BF16)
    prefix = _dot(before, ohsum.astype(BF16)) + run_ref[...]
    r0 = jnp.sum(jnp.where(oh0, prefix, 0.0), axis=1, keepdims=True)
    r1 = jnp.sum(jnp.where(oh1, prefix, 0.0), axis=1, keepdims=True)
    run = run_ref[...] + jnp.sum(ohsum, axis=0, keepdims=True)
    run_ref[...] = run
    cnt_ref[...] = jnp.broadcast_to(run, cnt_ref.shape)

    e0 = i1 - LOGIT_LANE_E
    e1 = i2 - LOGIT_LANE_E
    route = jnp.zeros((tm, LANES), F32)
    for ln, val in ((ROUTE_LANE_E, e0), (ROUTE_LANE_E + 1, e1), (ROUTE_LANE_W, w0),
                    (ROUTE_LANE_W + 1, w1_), (ROUTE_LANE_R, r0), (ROUTE_LANE_R + 1, r1)):
        route = jnp.where(lane == ln, val, route)
    route_ref[...] = route


def _outroute(x2, hm, hs, w_out, g_ffn, w_rg, b_rg, w_re, b_re, tm):
    t, d = x2.shape
    mw = hm.shape[1]
    sw = hs.shape[1]
    n_groups = w_rg.shape[1]
    n_exp = w_re.shape[1]
    wr = jnp.zeros((d, LANES), F32)
    wr = wr.at[:, 0:n_groups].set(w_rg).at[:, LOGIT_LANE_E:LOGIT_LANE_E + n_exp].set(w_re)
    wr3 = jnp.stack(_split3(wr))
    br = jnp.zeros((1, LANES), F32)
    br = br.at[0, 0:n_groups].set(b_rg).at[0, LOGIT_LANE_E:LOGIT_LANE_E + n_exp].set(b_re)
    row = lambda i: (i, 0)
    const = lambda i: (0, 0)
    kern = functools.partial(_outroute_kernel, n_groups=n_groups, per_group=n_exp // n_groups)
    return pl.pallas_call(
        kern,
        grid=(t // tm,),
        in_specs=[
            pl.BlockSpec((tm, d), row),
            pl.BlockSpec((tm, mw), row),
            pl.BlockSpec((tm, sw), row),
            pl.BlockSpec((mw, d), const),
            pl.BlockSpec((sw, d), const),
            pl.BlockSpec((1, d), const),
            pl.BlockSpec((3, d, LANES), lambda i: (0, 0, 0)),
            pl.BlockSpec((1, LANES), const),
        ],
        out_specs=[
            pl.BlockSpec((tm, d), row),
            pl.BlockSpec((tm, d), row),
            pl.BlockSpec((tm, LANES), row),
            pl.BlockSpec((8, LANES), const),
        ],
        out_shape=[
            jax.ShapeDtypeStruct((t, d), F32),
            jax.ShapeDtypeStruct((t, d), F32),
            jax.ShapeDtypeStruct((t, LANES), F32),
            jax.ShapeDtypeStruct((8, LANES), F32),
        ],
        scratch_shapes=[pltpu.VMEM((1, LANES), F32)],
        compiler_params=pltpu.CompilerParams(
            dimension_semantics=("arbitrary",), vmem_limit_bytes=VMEM_LIMIT),
        name="outroute",
    )(x2, hm, hs, w_out[0:mw].astype(BF16), w_out[mw:].astype(BF16), g_ffn.reshape(1, d),
      wr3, br)


def _slotpos_kernel(route_ref, offs_ref, pos_ref):
    route = route_ref[...]
    tm = route.shape[0]
    lane = lax.broadcasted_iota(jnp.int32, (tm, LANES), 1)
    offs = offs_ref[...]
    out = jnp.zeros((tm, LANES), F32)
    for j in range(TOP_K):
        e = route[:, ROUTE_LANE_E + j:ROUTE_LANE_E + j + 1].astype(jnp.int32)
        base = jnp.sum(jnp.where(lane == e, offs, 0.0), axis=1, keepdims=True)
        out = jnp.where(lane == j, base + route[:, ROUTE_LANE_R + j:ROUTE_LANE_R + j + 1], out)
    pos_ref[...] = out.astype(jnp.int32)


def _slotpos(route, offs_row, tm):
    t = route.shape[0]
    return pl.pallas_call(
        _slotpos_kernel,
        grid=(t // tm,),
        in_specs=[pl.BlockSpec((tm, LANES), lambda i: (i, 0)),
                  pl.BlockSpec((1, LANES), lambda i: (0, 0))],
        out_specs=pl.BlockSpec((tm, LANES), lambda i: (i, 0)),
        out_shape=jax.ShapeDtypeStruct((t, LANES), jnp.int32),
        compiler_params=pltpu.CompilerParams(dimension_semantics=("arbitrary",)),
        name="slotpos",
    )(route, offs_row)


def _row_copy(src_ref, src_row, dst_ref, dst_row, sem):
    return pltpu.make_async_copy(src_ref.at[pl.ds(src_row, 1), :],
                                 dst_ref.at[pl.ds(dst_row, 1), :], sem)


def _dispatch_kernel(pos_ref, c_ref, init_ref, xin_ref, sem):
    del init_ref
    tm = c_ref.shape[0]

    def issue(t, _):
        for j in range(TOP_K):
            _row_copy(c_ref, t, xin_ref, pos_ref[0, 0, TOP_K * t + j], sem).start(priority=j)
        return 0

    lax.fori_loop(0, tm, issue, 0, unroll=ROW_DMA_UNROLL)
    for j in range(TOP_K):
        pltpu.make_async_copy(c_ref, c_ref, sem).wait()


def _dispatch(pos3, c, n_rows, tm):
    t, d = c.shape
    init = jnp.zeros((n_rows, d), F32)
    return pl.pallas_call(
        _dispatch_kernel,
        grid=(t // tm,),
        in_specs=[
            pl.BlockSpec((1, 1, TOP_K * tm), lambda i: (i, 0, 0), memory_space=pltpu.SMEM),
            pl.BlockSpec((tm, d), lambda i: (i, 0)),
            pl.BlockSpec(memory_space=pl.ANY),
        ],
        out_specs=pl.BlockSpec(memory_space=pl.ANY),
        out_shape=jax.ShapeDtypeStruct((n_rows, d), F32),
        scratch_shapes=[pltpu.SemaphoreType.DMA(())],
        input_output_aliases={2: 0},
        compiler_params=pltpu.CompilerParams(dimension_semantics=("arbitrary",)),
        name="dispatch",
    )(pos3, c, init)


def _experts_kernel(be_ref, nv_ref, x_ref, wg_ref, wu_ref, wd_ref, y_ref, wgb, wub, wdb):
    i = pl.program_id(0)
    prev = be_ref[jnp.maximum(i - 1, 0)]
    active = i < nv_ref[0]

    @pl.when(active & ((i == 0) | (be_ref[i] != prev)))
    def _():
        wgb[...] = wg_ref[0].astype(BF16)
        wub[...] = wu_ref[0].astype(BF16)
        wdb[...] = wd_ref[0].astype(BF16)

    @pl.when(active)
    def _():
        x = x_ref[...].astype(BF16)
        gt = _dot(x, wgb[...])
        up = _dot(x, wub[...])
        hid = (gt * _sigmoid(gt) * up).astype(BF16)
        y_ref[...] = _dot(hid, wdb[...])

    @pl.when(jnp.logical_not(active))
    def _():
        y_ref[...] = jnp.zeros(y_ref.shape, F32)


def _experts(block_e, n_valid, xin, w_gate, w_up, w_down):
    r, d = xin.shape
    de = w_gate.shape[2]
    nblk = r // EXPERT_ROWS
    grid_spec = pltpu.PrefetchScalarGridSpec(
        num_scalar_prefetch=2,
        grid=(nblk,),
        in_specs=[
            pl.BlockSpec((EXPERT_ROWS, d), lambda i, be, nv: (i, 0)),
            pl.BlockSpec((1, d, de), lambda i, be, nv: (be[i], 0, 0)),
            pl.BlockSpec((1, d, de), lambda i, be, nv: (be[i], 0, 0)),
            pl.BlockSpec((1, de, d), lambda i, be, nv: (be[i], 0, 0)),
        ],
        out_specs=pl.BlockSpec((EXPERT_ROWS, d), lambda i, be, nv: (i, 0)),
        scratch_shapes=[pltpu.VMEM((d, de), BF16), pltpu.VMEM((d, de), BF16),
                        pltpu.VMEM((de, d), BF16)],
    )
    return pl.pallas_call(
        _experts_kernel,
        grid_spec=grid_spec,
        out_shape=jax.ShapeDtypeStruct((r, d), F32),
        compiler_params=pltpu.CompilerParams(
            dimension_semantics=("arbitrary",), vmem_limit_bytes=VMEM_LIMIT),
        name="experts",
    )(block_e, n_valid, xin, w_gate, w_up, w_down)


def _combine_kernel(pos_ref, posn_ref, h1_ref, route_ref, p_ref, y_ref, wpg_ref, wpp_ref,
                    gple_ref, gpost_ref, gfin_ref, out_ref, ybuf, sems):
    i = pl.program_id(0)
    tm = h1_ref.shape[0]

    def gather(pref, slot):
        def issue(t, _):
            for j in range(TOP_K):
                _row_copy(y_ref, pref[0, 0, TOP_K * t + j], ybuf.at[slot, j], t,
                          sems.at[slot]).start(priority=j)
            return 0

        lax.fori_loop(0, tm, issue, 0, unroll=ROW_DMA_UNROLL)

    @pl.when(i == 0)
    def _():
        gather(pos_ref, 0)

    @pl.when(i + 1 < pl.num_programs(0))
    def _():
        gather(posn_ref, (i + 1) % 2)

    slot = i % 2
    for j in range(TOP_K):
        pltpu.make_async_copy(ybuf.at[slot, j], ybuf.at[slot, j], sems.at[slot]).wait()

    route = route_ref[...]
    w0 = route[:, ROUTE_LANE_W:ROUTE_LANE_W + 1]
    w1 = route[:, ROUTE_LANE_W + 1:ROUTE_LANE_W + 2]
    h2 = h1_ref[...] + (w0 * ybuf[slot, 0] + w1 * ybuf[slot, 1])
    gate = _sigmoid(_dot(_rms(h2, gple_ref[...]).astype(BF16), wpg_ref[...]))
    ple = _rms(_dot(p_ref[...].astype(BF16), wpp_ref[...]), gpost_ref[...])
    h3 = h2 + gate * ple
    out_ref[...] = _rms(h3, gfin_ref[...])


def _combine(pos3, h1, route, p2, y, w_pg, w_pp, g_ple, g_post, g_final, tm):
    t, d = h1.shape
    pd = p2.shape[1]
    row = lambda i: (i, 0)
    const = lambda i: (0, 0)
    last = t // tm - 1
    return pl.pallas_call(
        _combine_kernel,
        grid=(t // tm,),
        in_specs=[
            pl.BlockSpec((1, 1, TOP_K * tm), lambda i: (i, 0, 0), memory_space=pltpu.SMEM),
            pl.BlockSpec((1, 1, TOP_K * tm), lambda i: (jnp.minimum(i + 1, last), 0, 0),
                         memory_space=pltpu.SMEM),
            pl.BlockSpec((tm, d), row),
            pl.BlockSpec((tm, LANES), row),
            pl.BlockSpec((tm, pd), row),
            pl.BlockSpec(memory_space=pl.ANY),
            pl.BlockSpec((d, d), const),
            pl.BlockSpec((pd, d), const),
            pl.BlockSpec((1, d), const),
            pl.BlockSpec((1, d), const),
            pl.BlockSpec((1, d), const),
        ],
        out_specs=pl.BlockSpec((tm, d), row),
        out_shape=jax.ShapeDtypeStruct((t, d), F32),
        scratch_shapes=[pltpu.VMEM((2, TOP_K, tm, d), F32), pltpu.SemaphoreType.DMA((2,))],
        compiler_params=pltpu.CompilerParams(
            dimension_semantics=("arbitrary",), vmem_limit_bytes=VMEM_LIMIT),
        name="combine",
    )(pos3, pos3, h1, route, p2, y, w_pg.astype(BF16), w_pp.astype(BF16),
      g_ple.reshape(1, d), g_post.reshape(1, d), g_final.reshape(1, d))


def _largest_tile(n, cap):
    tile = cap
    while n % tile:
        tile //= 2
    return tile


def kernel(x, p, g_mix, w_in, b_gates, conv_q, conv_k, g_mhead, w_out, g_ffn, w_router_group,
           b_router_group, w_router_expert, b_router_expert, w_exp_gate, w_exp_up, w_exp_down,
           g_ple, w_ple_gate, w_ple_proj, g_ple_post, g_final):
    batch, seq_len, d = x.shape
    t = batch * seq_len
    tm = _largest_tile(seq_len, 512)
    tm_rows = _largest_tile(seq_len, 256)
    n_exp = w_router_expert.shape[-1]
    n_rows = t * TOP_K + n_exp * EXPERT_ROWS
    nblk = n_rows // EXPERT_ROWS

    assert w_in.shape[0] == 1, "single-layer block"
    l = 0
    h = x.reshape(t, d)
    mq, mk, mv, mo, gates, sq, sk, sv = _inproj(
        h, g_mix[l], w_in[l], conv_q[l], conv_k[l], seq_len, tm)
    hm = _mlstm(mq, mk, mv, mo, gates, b_gates[l], g_mhead[l], batch, seq_len,
                _largest_tile(seq_len, 1024))
    hs = _stickbreak(sq, sk, sv, batch, seq_len)
    h1, c, route, counts = _outroute(
        h, hm, hs, w_out[l], g_ffn[l], w_router_group[l], b_router_group[l],
        w_router_expert[l], b_router_expert[l], tm)

    cnt = counts[0, LOGIT_LANE_E:LOGIT_LANE_E + n_exp].astype(jnp.int32)
    padded = ((cnt + EXPERT_ROWS - 1) // EXPERT_ROWS) * EXPERT_ROWS
    pend = jnp.cumsum(padded)
    offs = pend - padded
    offs_row = jnp.zeros((1, LANES), F32).at[0, 0:n_exp].set(offs.astype(F32))
    block_start = jnp.arange(nblk, dtype=jnp.int32) * EXPERT_ROWS
    block_e = jnp.minimum(jnp.sum(pend[None, :] <= block_start[:, None], axis=1),
                          n_exp - 1).astype(jnp.int32)
    n_valid = (pend[-1:] // EXPERT_ROWS).astype(jnp.int32)

    pos = _slotpos(route, offs_row, tm)
    pos3 = pos[:, 0:TOP_K].reshape(t // tm_rows, 1, TOP_K * tm_rows)
    xin = _dispatch(pos3, c, n_rows, tm_rows)
    y = _experts(block_e, n_valid, xin, w_exp_gate[l], w_exp_up[l], w_exp_down[l])
    out = _combine(pos3, h1, route, p[l].reshape(t, -1), y, w_ple_gate[l], w_ple_proj[l],
                   g_ple[l], g_ple_post[l], g_final, tm_rows)
    return out.reshape(batch, seq_len, d)
```

```python
import functools

import jax
import jax.numpy as jnp
from jax import lax
from jax.experimental import pallas as pl
from jax.experimental.pallas import tpu as pltpu

F32 = jnp.float32
BF16 = jnp.bfloat16
EPS = 1e-6

M_HEADS = 4
M_HEAD_DIM = 128
SB_HEAD_DIM = 64
CONV_WIDTH = 4
TOP_K = 2
LANES = 128
VMEM_LIMIT = 56 * 1024 * 1024

MLSTM_CHUNK = 128
SB_BLOCK = 256
SB_ZERO_LOG = -105.0
EXPERT_ROWS = 256
ROW_DMA_UNROLL = 8
ROUTE_LANE_E = 0
ROUTE_LANE_W = 2
ROUTE_LANE_R = 4
LOGIT_LANE_E = 4


def _rms(x, g):
    return x * lax.rsqrt(jnp.mean(x * x, axis=-1, keepdims=True) + EPS) * g


def _sigmoid(x):
    return 1.0 / (1.0 + jnp.exp(-x))


def _split3(a):
    a1 = a.astype(BF16)
    r1 = a - a1.astype(F32)
    a2 = r1.astype(BF16)
    a3 = (r1 - a2.astype(F32)).astype(BF16)
    return a1, a2, a3


def _dot(a, b):
    return jnp.dot(a, b, preferred_element_type=F32)


def _dot_nt(a, b):
    return lax.dot_general(a, b, (((1,), (1,)), ((), ())), preferred_element_type=F32)


def _dot_tn(a, b):
    return lax.dot_general(a, b, (((0,), (0,)), ((), ())), preferred_element_type=F32)


def _inproj_kernel(x_ref, g_ref, wqk_ref, wvo_ref, wg_ref, ws_ref, cq_ref, ck_ref,
                   mq_ref, mk_ref, mv_ref, mo_ref, gate_ref, sq_ref, sk_ref, sv_ref,
                   ext_ref, *, tiles_per_seq, k_scale):
    i = pl.program_id(0)
    tm = x_ref.shape[0]
    mw = mq_ref.shape[1]
    sw = sq_ref.shape[1]
    a = _rms(x_ref[...], g_ref[...]).astype(BF16)

    @pl.when(i % tiles_per_seq == 0)
    def _():
        ext_ref[0:8, :] = jnp.zeros((8, 2 * mw), F32)

    ext_ref[8:8 + tm, :] = _dot(a, wqk_ref[...])

    def conv_silu(w_ref, c0):
        acc = ext_ref[pl.ds(8 - (CONV_WIDTH - 1), tm), c0:c0 + mw] * w_ref[0:1, :]
        for j in range(1, CONV_WIDTH):
            acc = acc + ext_ref[pl.ds(8 - (CONV_WIDTH - 1) + j, tm), c0:c0 + mw] * w_ref[j:j + 1, :]
        return acc * _sigmoid(acc)

    mq_ref[...] = conv_silu(cq_ref, 0).astype(BF16)
    mk_ref[...] = (conv_silu(ck_ref, mw) * k_scale).astype(BF16)
    ext_ref[0:8, :] = ext_ref[tm:tm + 8, :]

    vo = _dot(a, wvo_ref[...])
    mv_ref[...] = vo[:, 0:mw].astype(BF16)
    mo_ref[...] = vo[:, mw:2 * mw].astype(BF16)
    gate_ref[...] = _dot(a, wg_ref[...])
    s = _dot(a, ws_ref[...])
    sq_ref[...] = s[:, 0:sw].astype(BF16)
    sk_ref[...] = s[:, sw:2 * sw].astype(BF16)
    sv_ref[...] = s[:, 2 * sw:3 * sw].astype(BF16)


def _inproj(x2, g_mix, w_in, conv_q, conv_k, seq_len, tm):
    t, d = x2.shape
    mw = conv_q.shape[1]
    h = M_HEADS
    sw = (w_in.shape[1] - 4 * mw - 2 * h) // 3
    wqk = w_in[:, 0:2 * mw].astype(BF16)
    wvo = w_in[:, 2 * mw:4 * mw].astype(BF16)
    wg = jnp.zeros((d, 2 * LANES), F32)
    wg = wg.at[:, 0:h].set(w_in[:, 4 * mw:4 * mw + h])
    wg = wg.at[:, LANES:LANES + h].set(w_in[:, 4 * mw + h:4 * mw + 2 * h]).astype(BF16)
    ws = w_in[:, 4 * mw + 2 * h:]
    ws = jnp.concatenate([ws[:, 0:sw] * (SB_HEAD_DIM ** -0.5), ws[:, sw:]], axis=1).astype(BF16)
    row = lambda i: (i, 0)
    const = lambda i: (0, 0)
    kern = functools.partial(_inproj_kernel, tiles_per_seq=seq_len // tm, k_scale=M_HEAD_DIM ** -0.5)
    bf = lambda w: jax.ShapeDtypeStruct((t, w), BF16)
    return pl.pallas_call(
        kern,
        grid=(t // tm,),
        in_specs=[
            pl.BlockSpec((tm, d), row),
            pl.BlockSpec((1, d), const),
            pl.BlockSpec((d, 2 * mw), const),
            pl.BlockSpec((d, 2 * mw), const),
            pl.BlockSpec((d, 2 * LANES), const),
            pl.BlockSpec((d, 3 * sw), const),
            pl.BlockSpec((CONV_WIDTH, mw), const),
            pl.BlockSpec((CONV_WIDTH, mw), const),
        ],
        out_specs=[
            pl.BlockSpec((tm, mw), row), pl.BlockSpec((tm, mw), row),
            pl.BlockSpec((tm, mw), row), pl.BlockSpec((tm, mw), row),
            pl.BlockSpec((tm, 2 * LANES), row),
            pl.BlockSpec((tm, sw), row), pl.BlockSpec((tm, sw), row), pl.BlockSpec((tm, sw), row),
        ],
        out_shape=[bf(mw), bf(mw), bf(mw), bf(mw),
                   jax.ShapeDtypeStruct((t, 2 * LANES), F32), bf(sw), bf(sw), bf(sw)],
        scratch_shapes=[pltpu.VMEM((tm + 8, 2 * mw), F32)],
        compiler_params=pltpu.CompilerParams(
            dimension_semantics=("arbitrary",), vmem_limit_bytes=VMEM_LIMIT),
        name="inproj",
    )(x2, g_mix.reshape(1, d), wqk, wvo, wg, ws, conv_q, conv_k)


def _mlstm_kernel(q_ref, k_ref, v_ref, o_ref, gate_ref, bias_ref, gh_ref, out_ref,
                  c_ref, m_ref, *, chunk):
    L = chunk
    hd = M_HEAD_DIM
    nchunks = q_ref.shape[0] // L

    @pl.when(pl.program_id(1) == 0)
    def _():
        c_ref[...] = jnp.zeros(c_ref.shape, F32)
        m_ref[...] = jnp.zeros(m_ref.shape, F32)

    rows = lax.broadcasted_iota(jnp.int32, (L, L), 0)
    cols = lax.broadcasted_iota(jnp.int32, (L, L), 1)
    causal = cols <= rows
    tri = causal.astype(BF16)
    lane2 = lax.broadcasted_iota(jnp.int32, (L, 2 * hd), 1)
    ones_col = (lane2 == hd).astype(F32)

    def chunk_body(c, _):
        r0 = pl.multiple_of(c * L, L)
        g = gate_ref[pl.ds(r0, L), :] + bias_ref[...]
        gi = g[:, 0:LANES]
        gf = g[:, LANES:2 * LANES]
        lf = jnp.minimum(gf, 0.0) - jnp.log(1.0 + jnp.exp(-jnp.abs(gf)))
        l1, l2, l3 = _split3(lf)
        b = _dot(tri, l1) + _dot(tri, l2) + _dot(tri, l3)
        b_last = b[L - 1:L, :]
        w_end = b_last - b + gi
        m_loc = jnp.max(w_end, axis=0, keepdims=True)
        e_end = jnp.exp(w_end - m_loc)
        m_prev = m_ref[...]
        m_new = jnp.maximum(b_last + m_prev, m_loc)
        decay = jnp.exp(b_last + m_prev - m_new)
        scale = jnp.exp(m_loc - m_new)
        b_t = b.T
        gi_t = gi.T
        for h in range(M_HEADS):
            hs = slice(h * hd, (h + 1) * hd)
            qh = q_ref[pl.ds(r0, L), hs]
            kh = k_ref[pl.ds(r0, L), hs]
            vh = v_ref[pl.ds(r0, L), hs].astype(F32)
            vext = jnp.concatenate([vh, jnp.zeros((L, hd), F32)], axis=1) + ones_col
            bc = b[:, h:h + 1]
            e = jnp.where(causal, bc - b_t[h:h + 1, :] + gi_t[h:h + 1, :], -jnp.inf)
            log_inter = bc + m_prev[:, h:h + 1]
            m_t = jnp.maximum(log_inter, jnp.max(e, axis=1, keepdims=True))
            w = (jnp.exp(e - m_t) * _dot_nt(qh, kh)).astype(BF16)
            a_int = jnp.exp(log_inter - m_t)
            cext = c_ref[h]
            num = _dot(w, vext.astype(BF16)) + a_int * _dot(qh, cext.astype(BF16))
            den = num[:, hd:hd + 1]
            hh = num[:, 0:hd] / jnp.maximum(jnp.abs(den), jnp.exp(-m_t))
            hh = _rms(hh, gh_ref[:, hs])
            og = _sigmoid(o_ref[pl.ds(r0, L), hs].astype(F32))
            out_ref[pl.ds(r0, L), hs] = (og * hh).astype(BF16)
            ev = (e_end[:, h:h + 1] * vext).astype(BF16)
            c_ref[h] = decay[:, h:h + 1] * cext + scale[:, h:h + 1] * _dot_tn(kh, ev)
        m_ref[...] = m_new
        return 0

    lax.fori_loop(0, nchunks, chunk_body, 0)


def _mlstm(mq, mk, mv, mo, gates, b_gates, g_mhead, batch, seq_len, rows):
    t, mw = mq.shape
    h = M_HEADS
    bias = jnp.zeros((1, 2 * LANES), F32)
    bias = bias.at[0, 0:h].set(b_gates[0:h]).at[0, LANES:LANES + h].set(b_gates[h:2 * h])
    nb = seq_len // rows
    row = lambda b, i: (b * nb + i, 0)
    const = lambda b, i: (0, 0)
    return pl.pallas_call(
        functools.partial(_mlstm_kernel, chunk=MLSTM_CHUNK),
        grid=(batch, nb),
        in_specs=[pl.BlockSpec((rows, mw), row)] * 4 + [
            pl.BlockSpec((rows, 2 * LANES), row),
            pl.BlockSpec((1, 2 * LANES), const),
            pl.BlockSpec((1, mw), const),
        ],
        out_specs=pl.BlockSpec((rows, mw), row),
        out_shape=jax.ShapeDtypeStruct((t, mw), BF16),
        scratch_shapes=[pltpu.VMEM((h, M_HEAD_DIM, 2 * M_HEAD_DIM), F32),
                        pltpu.VMEM((1, LANES), F32)],
        compiler_params=pltpu.CompilerParams(
            dimension_semantics=("arbitrary", "arbitrary"), vmem_limit_bytes=VMEM_LIMIT),
        name="mlstm",
    )(mq, mk, mv, mo, gates, bias, g_mhead.reshape(1, mw))


def _sb_kernel(q_ref, k_ref, v_ref, out_ref, acc_ref, carry_ref):
    lax.fori_loop(0, q_ref.shape[0] // SB_BLOCK,
                  functools.partial(_sb_query_block, q_ref, k_ref, v_ref, out_ref, acc_ref, carry_ref),
                  0)


def _sb_query_block(q_ref, k_ref, v_ref, out_ref, acc_ref, carry_ref, qi, _):
    blk = SB_BLOCK
    q0 = pl.multiple_of(qi * blk, blk)
    lane = lax.broadcasted_iota(jnp.int32, (blk, LANES), 1)
    head0 = lane < SB_HEAD_DIM
    q = q_ref[pl.ds(q0, blk), :]
    zero = jnp.zeros_like(q)
    qm = (jnp.where(head0, q, zero), jnp.where(head0, zero, q))
    rows = lax.broadcasted_iota(jnp.int32, (blk, blk), 0)
    cols = lax.broadcasted_iota(jnp.int32, (blk, blk), 1)
    strict = cols < rows
    neg_suffix = jnp.where(rows >= cols, -1.0, 0.0).astype(BF16)

    def block(j, carries, mask=None):
        k0 = pl.multiple_of(j * blk, blk)
        kb = k_ref[pl.ds(k0, blk), :]
        vb = v_ref[pl.ds(k0, blk), :]
        vz = jnp.zeros_like(vb)
        vm = (jnp.where(head0, vb, vz), jnp.where(head0, vz, vb))
        upd = None
        new = []
        for h in range(2):
            z = _dot_nt(qm[h], kb)
            sp = jnp.maximum(z, 0.0) + jnp.log(1.0 + jnp.exp(-jnp.abs(z)))
            if mask is not None:
                sp = jnp.where(mask, sp, 0.0)
            rc = _dot(sp.astype(BF16), neg_suffix)
            p = jnp.exp(z + rc + carries[h])
            if mask is not None:
                p = jnp.where(mask, p, 0.0)
            new.append(carries[h] + rc[:, 0:1])
            d = _dot(p.astype(BF16), vm[h])
            upd = d if upd is None else upd + d
        return upd, new

    zeros = jnp.zeros((blk, 1), F32)
    u_diag, carries = block(qi, [zeros, zeros], strict)
    has_prev = jnp.broadcast_to(qi > 0, (blk, blk))
    u_prev, carries = block(jnp.maximum(qi - 1, 0), carries, has_prev)
    acc_ref[...] = u_diag + u_prev
    carry_ref[0] = carries[0]
    carry_ref[1] = carries[1]

    def cond(state):
        it, top = state
        return (it < qi) & (top > SB_ZERO_LOG)

    def body(state):
        it, _ = state
        upd, new = block(qi - 1 - it, [carry_ref[0], carry_ref[1]])
        acc_ref[...] += upd
        carry_ref[0] = new[0]
        carry_ref[1] = new[1]
        return it + 1, jnp.maximum(jnp.max(new[0]), jnp.max(new[1]))

    lax.while_loop(cond, body, (jnp.int32(1), jnp.maximum(jnp.max(carries[0]), jnp.max(carries[1]))))
    out_ref[pl.ds(q0, blk), :] = acc_ref[...].astype(BF16)
    return 0


def _stickbreak(sq, sk, sv, batch, seq_len):
    t, sw = sq.shape
    npair = sw // LANES
    seq = pl.BlockSpec((seq_len, LANES), lambda b, hp: (b, hp))
    return pl.pallas_call(
        _sb_kernel,
        grid=(batch, npair),
        in_specs=[seq, seq, seq],
        out_specs=seq,
        out_shape=jax.ShapeDtypeStruct((t, sw), BF16),
        scratch_shapes=[pltpu.VMEM((SB_BLOCK, LANES), F32), pltpu.VMEM((2, SB_BLOCK, 1), F32)],
        compiler_params=pltpu.CompilerParams(
            dimension_semantics=("arbitrary", "arbitrary"), vmem_limit_bytes=VMEM_LIMIT),
        name="stickbrk",
    )(sq, sk, sv)


def _outroute_kernel(x_ref, hm_ref, hs_ref, wom_ref, wos_ref, g_ref, wr_ref, br_ref,
                     h1_ref, c_ref, route_ref, cnt_ref, run_ref, *, n_groups, per_group):
    i = pl.program_id(0)
    tm = x_ref.shape[0]

    @pl.when(i == 0)
    def _():
        run_ref[...] = jnp.zeros(run_ref.shape, F32)

    h1 = x_ref[...] + _dot(hm_ref[...], wom_ref[...]) + _dot(hs_ref[...], wos_ref[...])
    h1_ref[...] = h1
    c = _rms(h1, g_ref[...])
    c_ref[...] = c

    c1, c2, _ = _split3(c)
    pa = _dot(c1, wr_ref[...])
    pb = _dot(c2, wr_ref[...])
    logits = (pa[:, 0:LANES] + (pa[:, LANES:] + pb[:, 0:LANES]) + pb[:, LANES:]) + br_ref[...]

    lane = lax.broadcasted_iota(jnp.int32, (tm, LANES), 1).astype(F32)
    ninf = -jnp.inf
    big = float(LANES)

    def first_max(v):
        mx = jnp.max(v, axis=1, keepdims=True)
        idx = jnp.min(jnp.where(v == mx, lane, big), axis=1, keepdims=True)
        return mx, idx

    gl = jnp.where(lane < n_groups, logits, ninf)
    gmax, gsel = first_max(gl)
    p_g = 1.0 / jnp.sum(jnp.exp(gl - gmax), axis=1, keepdims=True)
    lo = LOGIT_LANE_E + per_group * gsel
    el = jnp.where((lane >= lo) & (lane < lo + per_group), logits, ninf)
    v1, i1 = first_max(el)
    v2, i2 = first_max(jnp.where(lane == i1, ninf, el))
    tt = jnp.exp(v2 - v1)
    w0 = p_g / (1.0 + tt)
    w1_ = p_g * tt / (1.0 + tt)

    oh0 = lane == i1
    oh1 = lane == i2
    ohsum = oh0.astype(F32) + oh1.astype(F32)
    rows = lax.broadcasted_iota(jnp.int32, (tm, tm), 0)
    cols = lax.broadcasted_iota(jnp.int32, (tm, tm), 1)
    before = (cols < rows).astype(BF16)
    prefix = _dot(before, ohsum.astype(BF16)) + run_ref[...]
    r0 = jnp.sum(jnp.where(oh0, prefix, 0.0), axis=1, keepdims=True)
    r1 = jnp.sum(jnp.where(oh1, prefix, 0.0), axis=1, keepdims=True)
    run = run_ref[...] + jnp.sum(ohsum, axis=0, keepdims=True)
    run_ref[...] = run
    cnt_ref[...] = jnp.broadcast_to(run, cnt_ref.shape)

    e0 = i1 - LOGIT_LANE_E
    e1 = i2 - LOGIT_LANE_E
    route = jnp.zeros((tm, LANES), F32)
    for ln, val in ((ROUTE_LANE_E, e0), (ROUTE_LANE_E + 1, e1), (ROUTE_LANE_W, w0),
                    (ROUTE_LANE_W + 1, w1_), (ROUTE_LANE_R, r0), (ROUTE_LANE_R + 1, r1)):
        route = jnp.where(lane == ln, val, route)
    route_ref[...] = route


def _outroute(x2, hm, hs, w_out, g_ffn, w_rg, b_rg, w_re, b_re, tm):
    t, d = x2.shape
    mw = hm.shape[1]
    sw = hs.shape[1]
    n_groups = w_rg.shape[1]
    n_exp = w_re.shape[1]
    wr = jnp.zeros((d, LANES), F32)
    wr = wr.at[:, 0:n_groups].set(w_rg).at[:, LOGIT_LANE_E:LOGIT_LANE_E + n_exp].set(w_re)
    wr_hi, wr_lo, _ = _split3(wr)
    wr2 = jnp.concatenate([wr_hi, wr_lo], axis=1)
    br = jnp.zeros((1, LANES), F32)
    br = br.at[0, 0:n_groups].set(b_rg).at[0, LOGIT_LANE_E:LOGIT_LANE_E + n_exp].set(b_re)
    row = lambda i: (i, 0)
    const = lambda i: (0, 0)
    kern = functools.partial(_outroute_kernel, n_groups=n_groups, per_group=n_exp // n_groups)
    return pl.pallas_call(
        kern,
        grid=(t // tm,),
        in_specs=[
            pl.BlockSpec((tm, d), row),
            pl.BlockSpec((tm, mw), row),
            pl.BlockSpec((tm, sw), row),
            pl.BlockSpec((mw, d), const),
            pl.BlockSpec((sw, d), const),
            pl.BlockSpec((1, d), const),
            pl.BlockSpec((d, 2 * LANES), const),
            pl.BlockSpec((1, LANES), const),
        ],
        out_specs=[
            pl.BlockSpec((tm, d), row),
            pl.BlockSpec((tm, d), row),
            pl.BlockSpec((tm, LANES), row),
            pl.BlockSpec((8, LANES), const),
        ],
        out_shape=[
            jax.ShapeDtypeStruct((t, d), F32),
            jax.ShapeDtypeStruct((t, d), F32),
            jax.ShapeDtypeStruct((t, LANES), F32),
            jax.ShapeDtypeStruct((8, LANES), F32),
        ],
        scratch_shapes=[pltpu.VMEM((1, LANES), F32)],
        compiler_params=pltpu.CompilerParams(
            dimension_semantics=("arbitrary",), vmem_limit_bytes=VMEM_LIMIT),
        name="outroute",
    )(x2, hm, hs, w_out[0:mw].astype(BF16), w_out[mw:].astype(BF16), g_ffn.reshape(1, d),
      wr2, br)


def _slotpos_kernel(route_ref, offs_ref, pos_ref):
    route = route_ref[...]
    tm = route.shape[0]
    lane = lax.broadcasted_iota(jnp.int32, (tm, LANES), 1)
    offs = offs_ref[...]
    out = jnp.zeros((tm, LANES), F32)
    for j in range(TOP_K):
        e = route[:, ROUTE_LANE_E + j:ROUTE_LANE_E + j + 1].astype(jnp.int32)
        base = jnp.sum(jnp.where(lane == e, offs, 0.0), axis=1, keepdims=True)
        out = jnp.where(lane == j, base + route[:, ROUTE_LANE_R + j:ROUTE_LANE_R + j + 1], out)
    pos_ref[...] = out.astype(jnp.int32)


def _slotpos(route, offs_row, tm):
    t = route.shape[0]
    return pl.pallas_call(
        _slotpos_kernel,
        grid=(t // tm,),
        in_specs=[pl.BlockSpec((tm, LANES), lambda i: (i, 0)),
                  pl.BlockSpec((1, LANES), lambda i: (0, 0))],
        out_specs=pl.BlockSpec((tm, LANES), lambda i: (i, 0)),
        out_shape=jax.ShapeDtypeStruct((t, LANES), jnp.int32),
        compiler_params=pltpu.CompilerParams(dimension_semantics=("arbitrary",)),
        name="slotpos",
    )(route, offs_row)


def _row_copy(src_ref, src_row, dst_ref, dst_row, sem):
    return pltpu.make_async_copy(src_ref.at[pl.ds(src_row, 1), :],
                                 dst_ref.at[pl.ds(dst_row, 1), :], sem)


def _dispatch_kernel(pos_ref, c_ref, init_ref, xin_ref, sem):
    del init_ref
    tm = c_ref.shape[0]

    def issue(t, _):
        for j in range(TOP_K):
            _row_copy(c_ref, t, xin_ref, pos_ref[0, 0, TOP_K * t + j], sem).start(priority=j)
        return 0

    lax.fori_loop(0, tm, issue, 0, unroll=ROW_DMA_UNROLL)
    for j in range(TOP_K):
        pltpu.make_async_copy(c_ref, c_ref, sem).wait()


def _dispatch(pos3, c, n_rows, tm):
    t, d = c.shape
    init = jnp.zeros((n_rows, d), F32)
    return pl.pallas_call(
        _dispatch_kernel,
        grid=(t // tm,),
        in_specs=[
            pl.BlockSpec((1, 1, TOP_K * tm), lambda i: (i, 0, 0), memory_space=pltpu.SMEM),
            pl.BlockSpec((tm, d), lambda i: (i, 0)),
            pl.BlockSpec(memory_space=pl.ANY),
        ],
        out_specs=pl.BlockSpec(memory_space=pl.ANY),
        out_shape=jax.ShapeDtypeStruct((n_rows, d), F32),
        scratch_shapes=[pltpu.SemaphoreType.DMA(())],
        input_output_aliases={2: 0},
        compiler_params=pltpu.CompilerParams(dimension_semantics=("arbitrary",)),
        name="dispatch",
    )(pos3, c, init)


def _experts_kernel(be_ref, nv_ref, x_ref, wg_ref, wu_ref, wd_ref, y_ref, wgb, wub, wdb):
    i = pl.program_id(0)
    prev = be_ref[jnp.maximum(i - 1, 0)]
    active = i < nv_ref[0]

    @pl.when(active & ((i == 0) | (be_ref[i] != prev)))
    def _():
        wgb[...] = wg_ref[0].astype(BF16)
        wub[...] = wu_ref[0].astype(BF16)
        wdb[...] = wd_ref[0].astype(BF16)

    @pl.when(active)
    def _():
        x = x_ref[...].astype(BF16)
        gt = _dot(x, wgb[...])
        up = _dot(x, wub[...])
        hid = (gt * _sigmoid(gt) * up).astype(BF16)
        y_ref[...] = _dot(hid, wdb[...])

    @pl.when(jnp.logical_not(active))
    def _():
        y_ref[...] = jnp.zeros(y_ref.shape, F32)


def _experts(block_e, n_valid, xin, w_gate, w_up, w_down):
    r, d = xin.shape
    de = w_gate.shape[2]
    nblk = r // EXPERT_ROWS
    grid_spec = pltpu.PrefetchScalarGridSpec(
        num_scalar_prefetch=2,
        grid=(nblk,),
        in_specs=[
            pl.BlockSpec((EXPERT_ROWS, d), lambda i, be, nv: (i, 0)),
            pl.BlockSpec((1, d, de), lambda i, be, nv: (be[i], 0, 0)),
            pl.BlockSpec((1, d, de), lambda i, be, nv: (be[i], 0, 0)),
            pl.BlockSpec((1, de, d), lambda i, be, nv: (be[i], 0, 0)),
        ],
        out_specs=pl.BlockSpec((EXPERT_ROWS, d), lambda i, be, nv: (i, 0)),
        scratch_shapes=[pltpu.VMEM((d, de), BF16), pltpu.VMEM((d, de), BF16),
                        pltpu.VMEM((de, d), BF16)],
    )
    return pl.pallas_call(
        _experts_kernel,
        grid_spec=grid_spec,
        out_shape=jax.ShapeDtypeStruct((r, d), F32),
        compiler_params=pltpu.CompilerParams(
            dimension_semantics=("arbitrary",), vmem_limit_bytes=VMEM_LIMIT),
        name="experts",
    )(block_e, n_valid, xin, w_gate, w_up, w_down)


def _combine_kernel(pos_ref, posn_ref, h1_ref, route_ref, p_ref, y_ref, wpg_ref, wpp_ref,
                    gple_ref, gpost_ref, gfin_ref, out_ref, ybuf, sems):
    i = pl.program_id(0)
    tm = h1_ref.shape[0]

    def issue(pref, slot, t):
        for j in range(TOP_K):
            _row_copy(y_ref, pref[0, 0, TOP_K * t + j], ybuf.at[slot, j], t,
                      sems.at[slot]).start(priority=j)

    def wait(slot):
        for j in range(TOP_K):
            pltpu.make_async_copy(ybuf.at[slot, j], ybuf.at[slot, j], sems.at[slot]).wait()

    @pl.when(i == 0)
    def _():
        lax.fori_loop(0, tm, lambda t, _: issue(pos_ref, 0, t) or 0, 0, unroll=ROW_DMA_UNROLL)

    slot = i % 2
    wait(slot)
    route = route_ref[...]
    w0 = route[:, ROUTE_LANE_W:ROUTE_LANE_W + 1]
    w1 = route[:, ROUTE_LANE_W + 1:ROUTE_LANE_W + 2]
    h2 = h1_ref[...] + (w0 * ybuf[slot, 0] + w1 * ybuf[slot, 1])

    for t in range(tm):
        issue(posn_ref, 1 - slot, t)
    gate = _sigmoid(_dot(_rms(h2, gple_ref[...]).astype(BF16), wpg_ref[...]))
    ple = _rms(_dot(p_ref[...].astype(BF16), wpp_ref[...]), gpost_ref[...])
    h3 = h2 + gate * ple
    out_ref[...] = _rms(h3, gfin_ref[...])

    @pl.when(i == pl.num_programs(0) - 1)
    def _():
        wait(1 - slot)


def _combine(pos3, h1, route, p2, y, w_pg, w_pp, g_ple, g_post, g_final, tm):
    t, d = h1.shape
    pd = p2.shape[1]
    row = lambda i: (i, 0)
    const = lambda i: (0, 0)
    last = t // tm - 1
    return pl.pallas_call(
        _combine_kernel,
        grid=(t // tm,),
        in_specs=[
            pl.BlockSpec((1, 1, TOP_K * tm), lambda i: (i, 0, 0), memory_space=pltpu.SMEM),
            pl.BlockSpec((1, 1, TOP_K * tm), lambda i: (jnp.minimum(i + 1, last), 0, 0),
                         memory_space=pltpu.SMEM),
            pl.BlockSpec((tm, d), row),
            pl.BlockSpec((tm, LANES), row),
            pl.BlockSpec((tm, pd), row),
            pl.BlockSpec(memory_space=pl.ANY),
            pl.BlockSpec((d, d), const),
            pl.BlockSpec((pd, d), const),
            pl.BlockSpec((1, d), const),
            pl.BlockSpec((1, d), const),
            pl.BlockSpec((1, d), const),
        ],
        out_specs=pl.BlockSpec((tm, d), row),
        out_shape=jax.ShapeDtypeStruct((t, d), F32),
        scratch_shapes=[pltpu.VMEM((2, TOP_K, tm, d), F32), pltpu.SemaphoreType.DMA((2,))],
        compiler_params=pltpu.CompilerParams(
            dimension_semantics=("arbitrary",), vmem_limit_bytes=VMEM_LIMIT),
        name="combine",
    )(pos3, pos3, h1, route, p2, y, w_pg.astype(BF16), w_pp.astype(BF16),
      g_ple.reshape(1, d), g_post.reshape(1, d), g_final.reshape(1, d))


def _largest_tile(n, cap):
    tile = cap
    while n % tile:
        tile //= 2
    return tile


def kernel(x, p, g_mix, w_in, b_gates, conv_q, conv_k, g_mhead, w_out, g_ffn, w_router_group,
           b_router_group, w_router_expert, b_router_expert, w_exp_gate, w_exp_up, w_exp_down,
           g_ple, w_ple_gate, w_ple_proj, g_ple_post, g_final):
    batch, seq_len, d = x.shape
    t = batch * seq_len
    tm = _largest_tile(seq_len, 512)
    tm_rows = _largest_tile(seq_len, 256)
    n_exp = w_router_expert.shape[-1]
    n_rows = t * TOP_K + n_exp * EXPERT_ROWS
    nblk = n_rows // EXPERT_ROWS

    assert w_in.shape[0] == 1, "single-layer block"
    l = 0
    h = x.reshape(t, d)
    mq, mk, mv, mo, gates, sq, sk, sv = _inproj(
        h, g_mix[l], w_in[l], conv_q[l], conv_k[l], seq_len, tm)
    hm = _mlstm(mq, mk, mv, mo, gates, b_gates[l], g_mhead[l], batch, seq_len,
                _largest_tile(seq_len, 1024))
    hs = _stickbreak(sq, sk, sv, batch, seq_len)
    h1, c, route, counts = _outroute(
        h, hm, hs, w_out[l], g_ffn[l], w_router_group[l], b_router_group[l],
        w_router_expert[l], b_router_expert[l], tm)

    cnt = counts[0, LOGIT_LANE_E:LOGIT_LANE_E + n_exp].astype(jnp.int32)
    padded = ((cnt + EXPERT_ROWS - 1) // EXPERT_ROWS) * EXPERT_ROWS
    pend = jnp.cumsum(padded)
    offs = pend - padded
    offs_row = jnp.zeros((1, LANES), F32).at[0, 0:n_exp].set(offs.astype(F32))
    block_start = jnp.arange(nblk, dtype=jnp.int32) * EXPERT_ROWS
    block_e = jnp.minimum(jnp.sum(pend[None, :] <= block_start[:, None], axis=1),
                          n_exp - 1).astype(jnp.int32)
    n_valid = (pend[-1:] // EXPERT_ROWS).astype(jnp.int32)

    pos = _slotpos(route, offs_row, _largest_tile(t, 2048))
    pos3 = pos[:, 0:TOP_K].reshape(t // tm_rows, 1, TOP_K * tm_rows)
    xin = _dispatch(pos3, c, n_rows, tm_rows)
    y = _experts(block_e, n_valid, xin, w_exp_gate[l], w_exp_up[l], w_exp_down[l])
    out = _combine(pos3, h1, route, p[l].reshape(t, -1), y, w_ple_gate[l], w_ple_proj[l],
                   g_ple[l], g_ple_post[l], g_final, tm_rows)
    return out.reshape(batch, seq_len, d)
```

```python
import functools

import jax
import jax.numpy as jnp
from jax import lax
from jax.experimental import pallas as pl
from jax.experimental.pallas import tpu as pltpu

F32 = jnp.float32
BF16 = jnp.bfloat16
EPS = 1e-6

M_HEADS = 4
M_HEAD_DIM = 128
SB_HEAD_DIM = 64
CONV_WIDTH = 4
TOP_K = 2
LANES = 128
VMEM_LIMIT = 56 * 1024 * 1024

MLSTM_CHUNK = 128
SB_BLOCK = 256
SB_ZERO_LOG = -105.0
EXPERT_ROWS = 256
ROW_DMA_UNROLL = 8
ROUTE_LANE_E = 0
ROUTE_LANE_W = 2
ROUTE_LANE_R = 4
META_DEST = 0
META_W = 1
LOGIT_LANE_E = 4


def _rms(x, g):
    return x * lax.rsqrt(jnp.mean(x * x, axis=-1, keepdims=True) + EPS) * g


def _sigmoid(x):
    return 1.0 / (1.0 + jnp.exp(-x))


def _split3(a):
    a1 = a.astype(BF16)
    r1 = a - a1.astype(F32)
    a2 = r1.astype(BF16)
    a3 = (r1 - a2.astype(F32)).astype(BF16)
    return a1, a2, a3


def _dot(a, b):
    return jnp.dot(a, b, preferred_element_type=F32)


def _dot_nt(a, b):
    return lax.dot_general(a, b, (((1,), (1,)), ((), ())), preferred_element_type=F32)


def _dot_tn(a, b):
    return lax.dot_general(a, b, (((0,), (0,)), ((), ())), preferred_element_type=F32)


def _inproj_kernel(x_ref, g_ref, wqk_ref, wvo_ref, wg_ref, ws_ref, cq_ref, ck_ref,
                   mq_ref, mk_ref, mv_ref, mo_ref, gate_ref, sq_ref, sk_ref, sv_ref,
                   ext_ref, *, tiles_per_seq, k_scale):
    i = pl.program_id(0)
    tm = x_ref.shape[0]
    mw = mq_ref.shape[1]
    sw = sq_ref.shape[1]
    a = _rms(x_ref[...], g_ref[...]).astype(BF16)

    @pl.when(i % tiles_per_seq == 0)
    def _():
        ext_ref[0:8, :] = jnp.zeros((8, 2 * mw), F32)

    ext_ref[8:8 + tm, 0:mw] = _dot(a, wqk_ref[:, 0:mw])
    ext_ref[8:8 + tm, mw:2 * mw] = _dot(a, wqk_ref[:, mw:2 * mw])

    def conv_silu(w_ref, c0):
        acc = ext_ref[pl.ds(8 - (CONV_WIDTH - 1), tm), c0:c0 + mw] * w_ref[0:1, :]
        for j in range(1, CONV_WIDTH):
            acc = acc + ext_ref[pl.ds(8 - (CONV_WIDTH - 1) + j, tm), c0:c0 + mw] * w_ref[j:j + 1, :]
        return acc * _sigmoid(acc)

    mq_ref[...] = conv_silu(cq_ref, 0).astype(BF16)
    mk_ref[...] = (conv_silu(ck_ref, mw) * k_scale).astype(BF16)
    ext_ref[0:8, :] = ext_ref[tm:tm + 8, :]

    mv_ref[...] = _dot(a, wvo_ref[:, 0:mw]).astype(BF16)
    mo_ref[...] = _dot(a, wvo_ref[:, mw:2 * mw]).astype(BF16)
    gate_ref[...] = _dot(a, wg_ref[...])
    sq_ref[...] = _dot(a, ws_ref[:, 0:sw]).astype(BF16)
    sk_ref[...] = _dot(a, ws_ref[:, sw:2 * sw]).astype(BF16)
    sv_ref[...] = _dot(a, ws_ref[:, 2 * sw:3 * sw]).astype(BF16)


def _inproj(x2, g_mix, w_in, conv_q, conv_k, seq_len, tm):
    t, d = x2.shape
    mw = conv_q.shape[1]
    h = M_HEADS
    sw = (w_in.shape[1] - 4 * mw - 2 * h) // 3
    wqk = w_in[:, 0:2 * mw].astype(BF16)
    wvo = w_in[:, 2 * mw:4 * mw].astype(BF16)
    wg = jnp.zeros((d, 2 * LANES), F32)
    wg = wg.at[:, 0:h].set(w_in[:, 4 * mw:4 * mw + h])
    wg = wg.at[:, LANES:LANES + h].set(w_in[:, 4 * mw + h:4 * mw + 2 * h]).astype(BF16)
    ws = w_in[:, 4 * mw + 2 * h:]
    ws = jnp.concatenate([ws[:, 0:sw] * (SB_HEAD_DIM ** -0.5), ws[:, sw:]], axis=1).astype(BF16)
    row = lambda i: (i, 0)
    const = lambda i: (0, 0)
    kern = functools.partial(_inproj_kernel, tiles_per_seq=seq_len // tm, k_scale=M_HEAD_DIM ** -0.5)
    bf = lambda w: jax.ShapeDtypeStruct((t, w), BF16)
    return pl.pallas_call(
        kern,
        grid=(t // tm,),
        in_specs=[
            pl.BlockSpec((tm, d), row),
            pl.BlockSpec((1, d), const),
            pl.BlockSpec((d, 2 * mw), const),
            pl.BlockSpec((d, 2 * mw), const),
            pl.BlockSpec((d, 2 * LANES), const),
            pl.BlockSpec((d, 3 * sw), const),
            pl.BlockSpec((CONV_WIDTH, mw), const),
            pl.BlockSpec((CONV_WIDTH, mw), const),
        ],
        out_specs=[
            pl.BlockSpec((tm, mw), row), pl.BlockSpec((tm, mw), row),
            pl.BlockSpec((tm, mw), row), pl.BlockSpec((tm, mw), row),
            pl.BlockSpec((tm, 2 * LANES), row),
            pl.BlockSpec((tm, sw), row), pl.BlockSpec((tm, sw), row), pl.BlockSpec((tm, sw), row),
        ],
        out_shape=[bf(mw), bf(mw), bf(mw), bf(mw),
                   jax.ShapeDtypeStruct((t, 2 * LANES), F32), bf(sw), bf(sw), bf(sw)],
        scratch_shapes=[pltpu.VMEM((tm + 8, 2 * mw), F32)],
        compiler_params=pltpu.CompilerParams(
            dimension_semantics=("arbitrary",), vmem_limit_bytes=VMEM_LIMIT),
        name="inproj",
    )(x2, g_mix.reshape(1, d), wqk, wvo, wg, ws, conv_q, conv_k)


def _mlstm_kernel(q_ref, k_ref, v_ref, o_ref, gate_ref, bias_ref, gh_ref, out_ref,
                  c_ref, m_ref, *, chunk):
    L = chunk
    hd = M_HEAD_DIM
    nchunks = q_ref.shape[0] // L

    @pl.when(pl.program_id(1) == 0)
    def _():
        c_ref[...] = jnp.zeros(c_ref.shape, F32)
        m_ref[...] = jnp.zeros(m_ref.shape, F32)

    rows = lax.broadcasted_iota(jnp.int32, (L, L), 0)
    cols = lax.broadcasted_iota(jnp.int32, (L, L), 1)
    causal = cols <= rows
    tri = causal.astype(BF16)
    lane2 = lax.broadcasted_iota(jnp.int32, (L, 2 * hd), 1)
    ones_col = (lane2 == hd).astype(F32)

    def chunk_body(c, _):
        r0 = pl.multiple_of(c * L, L)
        g = gate_ref[pl.ds(r0, L), :] + bias_ref[...]
        gi = g[:, 0:LANES]
        gf = g[:, LANES:2 * LANES]
        lf = jnp.minimum(gf, 0.0) - jnp.log(1.0 + jnp.exp(-jnp.abs(gf)))
        l1, l2, l3 = _split3(lf)
        b = _dot(tri, l1) + _dot(tri, l2) + _dot(tri, l3)
        b_last = b[L - 1:L, :]
        w_end = b_last - b + gi
        m_loc = jnp.max(w_end, axis=0, keepdims=True)
        e_end = jnp.exp(w_end - m_loc)
        m_prev = m_ref[...]
        m_new = jnp.maximum(b_last + m_prev, m_loc)
        decay = jnp.exp(b_last + m_prev - m_new)
        scale = jnp.exp(m_loc - m_new)
        b_t = b.T
        gi_t = gi.T
        for h in range(M_HEADS):
            hs = slice(h * hd, (h + 1) * hd)
            qh = q_ref[pl.ds(r0, L), hs]
            kh = k_ref[pl.ds(r0, L), hs]
            vh = v_ref[pl.ds(r0, L), hs].astype(F32)
            vext = jnp.concatenate([vh, jnp.zeros((L, hd), F32)], axis=1) + ones_col
            bc = b[:, h:h + 1]
            e = jnp.where(causal, bc - b_t[h:h + 1, :] + gi_t[h:h + 1, :], -jnp.inf)
            log_inter = bc + m_prev[:, h:h + 1]
            m_t = jnp.maximum(log_inter, jnp.max(e, axis=1, keepdims=True))
            w = (jnp.exp(e - m_t) * _dot_nt(qh, kh)).astype(BF16)
            a_int = jnp.exp(log_inter - m_t)
            cext = c_ref[h]
            num = _dot(w, vext.astype(BF16)) + a_int * _dot(qh, cext.astype(BF16))
            den = num[:, hd:hd + 1]
            hh = num[:, 0:hd] / jnp.maximum(jnp.abs(den), jnp.exp(-m_t))
            hh = _rms(hh, gh_ref[:, hs])
            og = _sigmoid(o_ref[pl.ds(r0, L), hs].astype(F32))
            out_ref[pl.ds(r0, L), hs] = (og * hh).astype(BF16)
            ev = (e_end[:, h:h + 1] * vext).astype(BF16)
            c_ref[h] = decay[:, h:h + 1] * cext + scale[:, h:h + 1] * _dot_tn(kh, ev)
        m_ref[...] = m_new
        return 0

    lax.fori_loop(0, nchunks, chunk_body, 0)


def _mlstm(mq, mk, mv, mo, gates, b_gates, g_mhead, batch, seq_len, rows):
    t, mw = mq.shape
    h = M_HEADS
    bias = jnp.zeros((1, 2 * LANES), F32)
    bias = bias.at[0, 0:h].set(b_gates[0:h]).at[0, LANES:LANES + h].set(b_gates[h:2 * h])
    nb = seq_len // rows
    row = lambda b, i: (b * nb + i, 0)
    const = lambda b, i: (0, 0)
    return pl.pallas_call(
        functools.partial(_mlstm_kernel, chunk=MLSTM_CHUNK),
        grid=(batch, nb),
        in_specs=[pl.BlockSpec((rows, mw), row)] * 4 + [
            pl.BlockSpec((rows, 2 * LANES), row),
            pl.BlockSpec((1, 2 * LANES), const),
            pl.BlockSpec((1, mw), const),
        ],
        out_specs=pl.BlockSpec((rows, mw), row),
        out_shape=jax.ShapeDtypeStruct((t, mw), BF16),
        scratch_shapes=[pltpu.VMEM((h, M_HEAD_DIM, 2 * M_HEAD_DIM), F32),
                        pltpu.VMEM((1, LANES), F32)],
        compiler_params=pltpu.CompilerParams(
            dimension_semantics=("arbitrary", "arbitrary"), vmem_limit_bytes=VMEM_LIMIT),
        name="mlstm",
    )(mq, mk, mv, mo, gates, bias, g_mhead.reshape(1, mw))


def _sb_kernel(q_ref, k_ref, v_ref, out_ref, acc_ref, carry_ref):
    lax.fori_loop(0, q_ref.shape[0] // SB_BLOCK,
                  functools.partial(_sb_query_block, q_ref, k_ref, v_ref, out_ref, acc_ref, carry_ref),
                  0)


def _sb_query_block(q_ref, k_ref, v_ref, out_ref, acc_ref, carry_ref, qi, _):
    blk = SB_BLOCK
    q0 = pl.multiple_of(qi * blk, blk)
    lane = lax.broadcasted_iota(jnp.int32, (blk, LANES), 1)
    head0 = lane < SB_HEAD_DIM
    q = q_ref[pl.ds(q0, blk), :]
    zero = jnp.zeros_like(q)
    qm = (jnp.where(head0, q, zero), jnp.where(head0, zero, q))
    rows = lax.broadcasted_iota(jnp.int32, (blk, blk), 0)
    cols = lax.broadcasted_iota(jnp.int32, (blk, blk), 1)
    strict = cols < rows
    neg_suffix = jnp.where(rows >= cols, -1.0, 0.0).astype(BF16)

    def block(j, carries, mask=None):
        k0 = pl.multiple_of(j * blk, blk)
        kb = k_ref[pl.ds(k0, blk), :]
        vb = v_ref[pl.ds(k0, blk), :]
        vz = jnp.zeros_like(vb)
        vm = (jnp.where(head0, vb, vz), jnp.where(head0, vz, vb))
        upd = None
        new = []
        for h in range(2):
            z = _dot_nt(qm[h], kb)
            sp = jnp.maximum(z, 0.0) + jnp.log(1.0 + jnp.exp(-jnp.abs(z)))
            if mask is not None:
                sp = jnp.where(mask, sp, 0.0)
            rc = _dot(sp.astype(BF16), neg_suffix)
            p = jnp.exp(z + rc + carries[h])
            if mask is not None:
                p = jnp.where(mask, p, 0.0)
            new.append(carries[h] + rc[:, 0:1])
            d = _dot(p.astype(BF16), vm[h])
            upd = d if upd is None else upd + d
        return upd, new

    zeros = jnp.zeros((blk, 1), F32)
    u_diag, carries = block(qi, [zeros, zeros], strict)
    has_prev = jnp.broadcast_to(qi > 0, (blk, blk))
    u_prev, carries = block(jnp.maximum(qi - 1, 0), carries, has_prev)
    acc_ref[...] = u_diag + u_prev
    carry_ref[0] = carries[0]
    carry_ref[1] = carries[1]

    def cond(state):
        it, top = state
        return (it < qi) & (top > SB_ZERO_LOG)

    def body(state):
        it, _ = state
        upd, new = block(qi - 1 - it, [carry_ref[0], carry_ref[1]])
        acc_ref[...] += upd
        carry_ref[0] = new[0]
        carry_ref[1] = new[1]
        return it + 1, jnp.maximum(jnp.max(new[0]), jnp.max(new[1]))

    lax.while_loop(cond, body, (jnp.int32(1), jnp.maximum(jnp.max(carries[0]), jnp.max(carries[1]))))
    out_ref[pl.ds(q0, blk), :] = acc_ref[...].astype(BF16)
    return 0


def _stickbreak(sq, sk, sv, batch, seq_len):
    t, sw = sq.shape
    npair = sw // LANES
    seq = pl.BlockSpec((seq_len, LANES), lambda b, hp: (b, hp))
    return pl.pallas_call(
        _sb_kernel,
        grid=(batch, npair),
        in_specs=[seq, seq, seq],
        out_specs=seq,
        out_shape=jax.ShapeDtypeStruct((t, sw), BF16),
        scratch_shapes=[pltpu.VMEM((SB_BLOCK, LANES), F32), pltpu.VMEM((2, SB_BLOCK, 1), F32)],
        compiler_params=pltpu.CompilerParams(
            dimension_semantics=("arbitrary", "arbitrary"), vmem_limit_bytes=VMEM_LIMIT),
        name="stickbrk",
    )(sq, sk, sv)


def _outroute_kernel(x_ref, hm_ref, hs_ref, wom_ref, wos_ref, g_ref, wr_ref, br_ref,
                     h1_ref, c_ref, route_ref, cnt_ref, run_ref, *, n_groups, per_group):
    i = pl.program_id(0)
    tm = x_ref.shape[0]

    @pl.when(i == 0)
    def _():
        run_ref[...] = jnp.zeros(run_ref.shape, F32)

    h1 = x_ref[...] + _dot(hm_ref[...], wom_ref[...]) + _dot(hs_ref[...], wos_ref[...])
    h1_ref[...] = h1
    c = _rms(h1, g_ref[...])
    c_ref[...] = c

    c1, c2, _ = _split3(c)
    pa = _dot(c1, wr_ref[...])
    pb = _dot(c2, wr_ref[...])
    logits = (pa[:, 0:LANES] + (pa[:, LANES:] + pb[:, 0:LANES]) + pb[:, LANES:]) + br_ref[...]

    lane = lax.broadcasted_iota(jnp.int32, (tm, LANES), 1).astype(F32)
    ninf = -jnp.inf
    big = float(LANES)

    def first_max(v):
        mx = jnp.max(v, axis=1, keepdims=True)
        idx = jnp.min(jnp.where(v == mx, lane, big), axis=1, keepdims=True)
        return mx, idx

    gl = jnp.where(lane < n_groups, logits, ninf)
    gmax, gsel = first_max(gl)
    p_g = 1.0 / jnp.sum(jnp.exp(gl - gmax), axis=1, keepdims=True)
    lo = LOGIT_LANE_E + per_group * gsel
    el = jnp.where((lane >= lo) & (lane < lo + per_group), logits, ninf)
    v1, i1 = first_max(el)
    v2, i2 = first_max(jnp.where(lane == i1, ninf, el))
    tt = jnp.exp(v2 - v1)
    w0 = p_g / (1.0 + tt)
    w1_ = p_g * tt / (1.0 + tt)

    oh0 = lane == i1
    oh1 = lane == i2
    ohsum = oh0.astype(F32) + oh1.astype(F32)
    rows = lax.broadcasted_iota(jnp.int32, (tm, tm), 0)
    cols = lax.broadcasted_iota(jnp.int32, (tm, tm), 1)
    before = (cols < rows).astype(BF16)
    prefix = _dot(before, ohsum.astype(BF16)) + run_ref[...]
    r0 = jnp.sum(jnp.where(oh0, prefix, 0.0), axis=1, keepdims=True)
    r1 = jnp.sum(jnp.where(oh1, prefix, 0.0), axis=1, keepdims=True)
    run = run_ref[...] + jnp.sum(ohsum, axis=0, keepdims=True)
    run_ref[...] = run
    cnt_ref[...] = jnp.broadcast_to(run, cnt_ref.shape)

    e0 = i1 - LOGIT_LANE_E
    e1 = i2 - LOGIT_LANE_E
    route = jnp.zeros((tm, LANES), F32)
    for ln, val in ((ROUTE_LANE_E, e0), (ROUTE_LANE_E + 1, e1), (ROUTE_LANE_W, w0),
                    (ROUTE_LANE_W + 1, w1_), (ROUTE_LANE_R, r0), (ROUTE_LANE_R + 1, r1)):
        route = jnp.where(lane == ln, val, route)
    route_ref[...] = route


def _row_copy(src_ref, src_row, dst_ref, dst_row, sem):
    return pltpu.make_async_copy(src_ref.at[pl.ds(src_row, 1), :],
                                 dst_ref.at[pl.ds(dst_row, 1), :], sem)


def _lanes_to_smem(vals, vm_ref, sm_ref, sem):
    vm_ref[...] = vals.T[0:8, :].astype(jnp.int32)
    cp = pltpu.make_async_copy(vm_ref, sm_ref, sem)
    cp.start()
    cp.wait()


def _outroute(x2, hm, hs, w_out, g_ffn, w_rg, b_rg, w_re, b_re, tm):
    t, d = x2.shape
    mw = hm.shape[1]
    sw = hs.shape[1]
    n_groups = w_rg.shape[1]
    n_exp = w_re.shape[1]
    wr = jnp.zeros((d, LANES), F32)
    wr = wr.at[:, 0:n_groups].set(w_rg).at[:, LOGIT_LANE_E:LOGIT_LANE_E + n_exp].set(w_re)
    wr_hi, wr_lo, _ = _split3(wr)
    wr2 = jnp.concatenate([wr_hi, wr_lo], axis=1)
    br = jnp.zeros((1, LANES), F32)
    br = br.at[0, 0:n_groups].set(b_rg).at[0, LOGIT_LANE_E:LOGIT_LANE_E + n_exp].set(b_re)
    row = lambda i: (i, 0)
    const = lambda i: (0, 0)
    kern = functools.partial(_outroute_kernel, n_groups=n_groups, per_group=n_exp // n_groups)
    return pl.pallas_call(
        kern,
        grid=(t // tm,),
        in_specs=[
            pl.BlockSpec((tm, d), row),
            pl.BlockSpec((tm, mw), row),
            pl.BlockSpec((tm, sw), row),
            pl.BlockSpec((mw, d), const),
            pl.BlockSpec((sw, d), const),
            pl.BlockSpec((1, d), const),
            pl.BlockSpec((d, 2 * LANES), const),
            pl.BlockSpec((1, LANES), const),
        ],
        out_specs=[
            pl.BlockSpec((tm, d), row),
            pl.BlockSpec((tm, d), row),
            pl.BlockSpec((tm, LANES), row),
            pl.BlockSpec((8, LANES), const),
        ],
        out_shape=[
            jax.ShapeDtypeStruct((t, d), F32),
            jax.ShapeDtypeStruct((t, d), F32),
            jax.ShapeDtypeStruct((t, LANES), F32),
            jax.ShapeDtypeStruct((8, LANES), F32),
        ],
        scratch_shapes=[pltpu.VMEM((1, LANES), F32)],
        compiler_params=pltpu.CompilerParams(
            dimension_semantics=("arbitrary",), vmem_limit_bytes=VMEM_LIMIT),
        name="outroute",
    )(x2, hm, hs, w_out[0:mw].astype(BF16), w_out[mw:].astype(BF16), g_ffn.reshape(1, d),
      wr2, br)


def _slotpos_kernel(route_ref, offs_ref, pos_ref):
    route = route_ref[...]
    tm = route.shape[0]
    lane = lax.broadcasted_iota(jnp.int32, (tm, LANES), 1)
    offs = offs_ref[...]
    out = jnp.zeros((tm, LANES), F32)
    for j in range(TOP_K):
        e = route[:, ROUTE_LANE_E + j:ROUTE_LANE_E + j + 1].astype(jnp.int32)
        base = jnp.sum(jnp.where(lane == e, offs, 0.0), axis=1, keepdims=True)
        out = jnp.where(lane == j, base + route[:, ROUTE_LANE_R + j:ROUTE_LANE_R + j + 1], out)
    pos_ref[...] = out.astype(jnp.int32)


def _slotpos(route, offs_row, tm):
    t = route.shape[0]
    return pl.pallas_call(
        _slotpos_kernel,
        grid=(t // tm,),
        in_specs=[pl.BlockSpec((tm, LANES), lambda i: (i, 0)),
                  pl.BlockSpec((1, LANES), lambda i: (0, 0))],
        out_specs=pl.BlockSpec((tm, LANES), lambda i: (i, 0)),
        out_shape=jax.ShapeDtypeStruct((t, LANES), jnp.int32),
        compiler_params=pltpu.CompilerParams(dimension_semantics=("arbitrary",)),
        name="slotpos",
    )(route, offs_row)


def _dispatch_kernel(pos_ref, c_ref, route_ref, init_ref, xin_ref, rows_ref, sem, *, n_tokens):
    del init_ref
    i = pl.program_id(0)
    tm, d = c_ref.shape
    lane = lax.broadcasted_iota(jnp.int32, (tm, LANES), 1)
    row_id = (i * tm + lax.broadcasted_iota(jnp.int32, (tm, 1), 0)).astype(F32)
    route = route_ref[...]
    c = c_ref[...]
    for j in range(TOP_K):
        w = route[:, ROUTE_LANE_W + j:ROUTE_LANE_W + j + 1]
        rows_ref[j, :, 0:d] = c
        rows_ref[j, :, d:d + LANES] = jnp.where(
            lane == META_DEST, row_id + j * n_tokens, jnp.where(lane == META_W, w, 0.0))

    def issue(t, _):
        for j in range(TOP_K):
            _row_copy(rows_ref.at[j], t, xin_ref, pos_ref[0, 0, TOP_K * t + j], sem).start(priority=j)
        return 0

    lax.fori_loop(0, tm, issue, 0, unroll=ROW_DMA_UNROLL)
    for j in range(TOP_K):
        pltpu.make_async_copy(rows_ref.at[j], rows_ref.at[j], sem).wait()


def _dispatch(pos3, c, route, n_rows, tm):
    t, d = c.shape
    init = jnp.zeros((n_rows, d + LANES), F32)
    return pl.pallas_call(
        functools.partial(_dispatch_kernel, n_tokens=t),
        grid=(t // tm,),
        in_specs=[
            pl.BlockSpec((1, 1, TOP_K * tm), lambda i: (i, 0, 0), memory_space=pltpu.SMEM),
            pl.BlockSpec((tm, d), lambda i: (i, 0)),
            pl.BlockSpec((tm, LANES), lambda i: (i, 0)),
            pl.BlockSpec(memory_space=pl.ANY),
        ],
        out_specs=pl.BlockSpec(memory_space=pl.ANY),
        out_shape=jax.ShapeDtypeStruct((n_rows, d + LANES), F32),
        scratch_shapes=[pltpu.VMEM((TOP_K, tm, d + LANES), F32), pltpu.SemaphoreType.DMA(())],
        input_output_aliases={3: 0},
        compiler_params=pltpu.CompilerParams(
            dimension_semantics=("arbitrary",), vmem_limit_bytes=VMEM_LIMIT),
        name="dispatch",
    )(pos3, c, route, init)


def _experts_kernel(be_ref, bn_ref, nv_ref, x_ref, wg_ref, wu_ref, wd_ref, yout_ref,
                    wgb, wub, wdb, ybuf, dest_vm, dest_sm, sems, dsem, *, dump_row):
    i = pl.program_id(0)
    s = i % 2
    rows, d = ybuf.shape[1], ybuf.shape[2]
    prev = be_ref[jnp.maximum(i - 1, 0)]
    active = i < nv_ref[0]

    def to_dump(slot):
        def body(r, _):
            dest_sm[slot, 0, r] = dump_row + r
            return 0
        lax.fori_loop(0, rows, body, 0)

    def send(slot, r, queue=0):
        _row_copy(ybuf.at[slot], r, yout_ref, dest_sm[slot, 0, r], sems.at[slot]).start(priority=queue)

    def wait(slot):
        pltpu.make_async_copy(ybuf.at[slot], ybuf.at[slot], sems.at[slot]).wait()

    @pl.when(i == 0)
    def _():
        ybuf[1] = jnp.zeros((rows, d), F32)
        to_dump(1)

    @pl.when(i > 0)
    def _():
        wait(s)

    @pl.when(active & ((i == 0) | (be_ref[i] != prev)))
    def _():
        wgb[...] = wg_ref[0].astype(BF16)
        wub[...] = wu_ref[0].astype(BF16)
        wdb[...] = wd_ref[0].astype(BF16)

    @pl.when(active)
    def _():
        for r in range(rows):
            send(1 - s, r, r % 2)
        blk = x_ref[...]
        x = blk[:, 0:d].astype(BF16)
        meta = blk[:, d:d + LANES]
        gt = _dot(x, wgb[...])
        up = _dot(x, wub[...])
        hid = (gt * _sigmoid(gt) * up).astype(BF16)
        ybuf[s] = _dot(hid, wdb[...]) * meta[:, META_W:META_W + 1]
        row = lax.broadcasted_iota(jnp.int32, (rows, LANES), 0)
        dest = jnp.where(row < bn_ref[i], meta, (dump_row + row).astype(F32))
        _lanes_to_smem(dest, dest_vm, dest_sm.at[s], dsem)

    @pl.when(jnp.logical_not(active))
    def _():
        lax.fori_loop(0, rows, lambda r, _: send(1 - s, r) or 0, 0, unroll=ROW_DMA_UNROLL)
        to_dump(s)

    @pl.when(i == pl.num_programs(0) - 1)
    def _():
        wait(1 - s)


def _experts(blk_e, blk_n, n_valid, xin, w_gate, w_up, w_down, n_out_rows):
    r, xw = xin.shape
    d = xw - LANES
    de = w_gate.shape[2]
    rows = EXPERT_ROWS
    last = r // rows - 1
    grid_spec = pltpu.PrefetchScalarGridSpec(
        num_scalar_prefetch=3,
        grid=(r // rows + 1,),
        in_specs=[
            pl.BlockSpec((rows, xw), lambda i, be, bn, nv: (jnp.minimum(i, last), 0)),
            pl.BlockSpec((1, d, de), lambda i, be, bn, nv: (be[i], 0, 0)),
            pl.BlockSpec((1, d, de), lambda i, be, bn, nv: (be[i], 0, 0)),
            pl.BlockSpec((1, de, d), lambda i, be, bn, nv: (be[i], 0, 0)),
        ],
        out_specs=pl.BlockSpec(memory_space=pl.ANY),
        scratch_shapes=[pltpu.VMEM((d, de), BF16), pltpu.VMEM((d, de), BF16),
                        pltpu.VMEM((de, d), BF16),
                        pltpu.VMEM((2, rows, d), F32),
                        pltpu.VMEM((8, rows), jnp.int32),
                        pltpu.SMEM((2, 8, rows), jnp.int32),
                        pltpu.SemaphoreType.DMA((2,)),
                        pltpu.SemaphoreType.DMA(())],
    )
    return pl.pallas_call(
        functools.partial(_experts_kernel, dump_row=n_out_rows),
        grid_spec=grid_spec,
        out_shape=jax.ShapeDtypeStruct((n_out_rows + rows, d), F32),
        compiler_params=pltpu.CompilerParams(
            dimension_semantics=("arbitrary",), vmem_limit_bytes=VMEM_LIMIT),
        name="experts",
    )(blk_e, blk_n, n_valid, xin, w_gate, w_up, w_down)


def _combine_kernel(h1_ref, y0_ref, y1_ref, p_ref, wpg_ref, wpp_ref,
                    gple_ref, gpost_ref, gfin_ref, out_ref):
    h2 = h1_ref[...] + (y0_ref[...] + y1_ref[...])
    gate = _sigmoid(_dot(_rms(h2, gple_ref[...]).astype(BF16), wpg_ref[...]))
    ple = _rms(_dot(p_ref[...].astype(BF16), wpp_ref[...]), gpost_ref[...])
    h3 = h2 + gate * ple
    out_ref[...] = _rms(h3, gfin_ref[...])


def _combine(h1, p2, y, w_pg, w_pp, g_ple, g_post, g_final, tm):
    t, d = h1.shape
    pd = p2.shape[1]
    row = lambda i: (i, 0)
    const = lambda i: (0, 0)
    return pl.pallas_call(
        _combine_kernel,
        grid=(t // tm,),
        in_specs=[
            pl.BlockSpec((tm, d), row),
            pl.BlockSpec((tm, d), row),
            pl.BlockSpec((tm, d), lambda i: (t // tm + i, 0)),
            pl.BlockSpec((tm, pd), row),
            pl.BlockSpec((d, d), const),
            pl.BlockSpec((pd, d), const),
            pl.BlockSpec((1, d), const),
            pl.BlockSpec((1, d), const),
            pl.BlockSpec((1, d), const),
        ],
        out_specs=pl.BlockSpec((tm, d), row),
        out_shape=jax.ShapeDtypeStruct((t, d), F32),
        compiler_params=pltpu.CompilerParams(
            dimension_semantics=("arbitrary",), vmem_limit_bytes=VMEM_LIMIT),
        name="combine",
    )(h1, y, y, p2, w_pg.astype(BF16), w_pp.astype(BF16),
      g_ple.reshape(1, d), g_post.reshape(1, d), g_final.reshape(1, d))


def _largest_tile(n, cap):
    tile = cap
    while n % tile:
        tile //= 2
    return tile


def kernel(x, p, g_mix, w_in, b_gates, conv_q, conv_k, g_mhead, w_out, g_ffn, w_router_group,
           b_router_group, w_router_expert, b_router_expert, w_exp_gate, w_exp_up, w_exp_down,
           g_ple, w_ple_gate, w_ple_proj, g_ple_post, g_final):
    batch, seq_len, d = x.shape
    t = batch * seq_len
    tm = _largest_tile(seq_len, 512)
    n_exp = w_router_expert.shape[-1]
    rows = EXPERT_ROWS
    nblk = t * TOP_K // rows + n_exp

    assert w_in.shape[0] == 1, "single-layer block"
    l = 0
    h = x.reshape(t, d)
    mq, mk, mv, mo, gates, sq, sk, sv = _inproj(
        h, g_mix[l], w_in[l], conv_q[l], conv_k[l], seq_len, tm)
    hm = _mlstm(mq, mk, mv, mo, gates, b_gates[l], g_mhead[l], batch, seq_len,
                _largest_tile(seq_len, 1024))
    hs = _stickbreak(sq, sk, sv, batch, seq_len)
    h1, c, route, counts = _outroute(
        h, hm, hs, w_out[l], g_ffn[l], w_router_group[l], b_router_group[l],
        w_router_expert[l], b_router_expert[l], tm)

    cnt = counts[0, LOGIT_LANE_E:LOGIT_LANE_E + n_exp].astype(jnp.int32)
    nb_e = (cnt + rows - 1) // rows
    cum = jnp.cumsum(nb_e)
    offs_row = jnp.zeros((1, LANES), F32).at[0, 0:n_exp].set(((cum - nb_e) * rows).astype(F32))
    n_valid = cum[-1:]
    step = jnp.arange(nblk + 1, dtype=jnp.int32)
    blk_e = jnp.minimum(jnp.sum(cum[None, :] <= step[:, None], axis=1), n_exp - 1).astype(jnp.int32)
    blk_n = jnp.clip(cnt[blk_e] - rows * (step - (cum - nb_e)[blk_e]), 0, rows)

    pos = _slotpos(route, offs_row, _largest_tile(t, 2048))
    tm_rows = _largest_tile(seq_len, 256)
    pos3 = pos[:, 0:TOP_K].reshape(t // tm_rows, 1, TOP_K * tm_rows)
    xin = _dispatch(pos3, c, route, nblk * rows, tm_rows)
    y = _experts(blk_e, blk_n, n_valid, xin, w_exp_gate[l], w_exp_up[l], w_exp_down[l], TOP_K * t)
    out = _combine(h1, p[l].reshape(t, -1), y, w_ple_gate[l], w_ple_proj[l],
                   g_ple[l], g_ple_post[l], g_final, tm)
    return out.reshape(batch, seq_len, d)
```

```python
import functools

import jax
import jax.numpy as jnp
from jax import lax
from jax.experimental import pallas as pl
from jax.experimental.pallas import tpu as pltpu

F32 = jnp.float32
BF16 = jnp.bfloat16
EPS = 1e-6

M_HEADS = 4
M_HEAD_DIM = 128
SB_HEAD_DIM = 64
CONV_WIDTH = 4
TOP_K = 2
LANES = 128
VMEM_LIMIT = 56 * 1024 * 1024

MLSTM_CHUNK = 128
SB_BLOCK = 256
SB_ZERO_LOG = -105.0
EXPERT_ROWS = 256
ROW_DMA_UNROLL = 8
ROUTE_LANE_E = 0
ROUTE_LANE_W = 2
ROUTE_LANE_R = 4
SUBLANES = 8
META_SUBLANE = 4
META_DEST = 0
META_W = 1
LOGIT_LANE_E = 4


def _rms(x, g):
    return x * lax.rsqrt(jnp.mean(x * x, axis=-1, keepdims=True) + EPS) * g


def _sigmoid(x):
    return 1.0 / (1.0 + jnp.exp(-x))


def _split3(a):
    a1 = a.astype(BF16)
    r1 = a - a1.astype(F32)
    a2 = r1.astype(BF16)
    a3 = (r1 - a2.astype(F32)).astype(BF16)
    return a1, a2, a3


def _dot(a, b):
    return jnp.dot(a, b, preferred_element_type=F32)


def _dot_nt(a, b):
    return lax.dot_general(a, b, (((1,), (1,)), ((), ())), preferred_element_type=F32)


def _dot_tn(a, b):
    return lax.dot_general(a, b, (((0,), (0,)), ((), ())), preferred_element_type=F32)


def _inproj_kernel(x_ref, g_ref, wqk_ref, wvo_ref, wg_ref, ws_ref, cq_ref, ck_ref,
                   mq_ref, mk_ref, mv_ref, mo_ref, gate_ref, sq_ref, sk_ref, sv_ref,
                   ext_ref, *, tiles_per_seq, k_scale):
    i = pl.program_id(0)
    tm = x_ref.shape[0]
    mw = mq_ref.shape[1]
    sw = sq_ref.shape[1]
    a = _rms(x_ref[...], g_ref[...]).astype(BF16)

    @pl.when(i % tiles_per_seq == 0)
    def _():
        ext_ref[0:8, :] = jnp.zeros((8, 2 * mw), F32)

    ext_ref[8:8 + tm, 0:mw] = _dot(a, wqk_ref[:, 0:mw])
    ext_ref[8:8 + tm, mw:2 * mw] = _dot(a, wqk_ref[:, mw:2 * mw])

    def conv_silu(w_ref, c0):
        acc = ext_ref[pl.ds(8 - (CONV_WIDTH - 1), tm), c0:c0 + mw] * w_ref[0:1, :]
        for j in range(1, CONV_WIDTH):
            acc = acc + ext_ref[pl.ds(8 - (CONV_WIDTH - 1) + j, tm), c0:c0 + mw] * w_ref[j:j + 1, :]
        return acc * _sigmoid(acc)

    mq_ref[...] = conv_silu(cq_ref, 0).astype(BF16)
    mk_ref[...] = (conv_silu(ck_ref, mw) * k_scale).astype(BF16)
    ext_ref[0:8, :] = ext_ref[tm:tm + 8, :]

    mv_ref[...] = _dot(a, wvo_ref[:, 0:mw]).astype(BF16)
    mo_ref[...] = _dot(a, wvo_ref[:, mw:2 * mw]).astype(BF16)
    gate_ref[...] = _dot(a, wg_ref[...])
    sq_ref[...] = _dot(a, ws_ref[:, 0:sw]).astype(BF16)
    sk_ref[...] = _dot(a, ws_ref[:, sw:2 * sw]).astype(BF16)
    sv_ref[...] = _dot(a, ws_ref[:, 2 * sw:3 * sw]).astype(BF16)


def _inproj(x2, g_mix, w_in, conv_q, conv_k, seq_len, tm):
    t, d = x2.shape
    mw = conv_q.shape[1]
    h = M_HEADS
    sw = (w_in.shape[1] - 4 * mw - 2 * h) // 3
    wqk = w_in[:, 0:2 * mw].astype(BF16)
    wvo = w_in[:, 2 * mw:4 * mw].astype(BF16)
    wg = jnp.zeros((d, 2 * LANES), F32)
    wg = wg.at[:, 0:h].set(w_in[:, 4 * mw:4 * mw + h])
    wg = wg.at[:, LANES:LANES + h].set(w_in[:, 4 * mw + h:4 * mw + 2 * h]).astype(BF16)
    ws = w_in[:, 4 * mw + 2 * h:]
    ws = jnp.concatenate([ws[:, 0:sw] * (SB_HEAD_DIM ** -0.5), ws[:, sw:]], axis=1).astype(BF16)
    row = lambda i: (i, 0)
    const = lambda i: (0, 0)
    kern = functools.partial(_inproj_kernel, tiles_per_seq=seq_len // tm, k_scale=M_HEAD_DIM ** -0.5)
    bf = lambda w: jax.ShapeDtypeStruct((t, w), BF16)
    return pl.pallas_call(
        kern,
        grid=(t // tm,),
        in_specs=[
            pl.BlockSpec((tm, d), row),
            pl.BlockSpec((1, d), const),
            pl.BlockSpec((d, 2 * mw), const),
            pl.BlockSpec((d, 2 * mw), const),
            pl.BlockSpec((d, 2 * LANES), const),
            pl.BlockSpec((d, 3 * sw), const),
            pl.BlockSpec((CONV_WIDTH, mw), const),
            pl.BlockSpec((CONV_WIDTH, mw), const),
        ],
        out_specs=[
            pl.BlockSpec((tm, mw), row), pl.BlockSpec((tm, mw), row),
            pl.BlockSpec((tm, mw), row), pl.BlockSpec((tm, mw), row),
            pl.BlockSpec((tm, 2 * LANES), row),
            pl.BlockSpec((tm, sw), row), pl.BlockSpec((tm, sw), row), pl.BlockSpec((tm, sw), row),
        ],
        out_shape=[bf(mw), bf(mw), bf(mw), bf(mw),
                   jax.ShapeDtypeStruct((t, 2 * LANES), F32), bf(sw), bf(sw), bf(sw)],
        scratch_shapes=[pltpu.VMEM((tm + 8, 2 * mw), F32)],
        compiler_params=pltpu.CompilerParams(
            dimension_semantics=("arbitrary",), vmem_limit_bytes=VMEM_LIMIT),
        name="inproj",
    )(x2, g_mix.reshape(1, d), wqk, wvo, wg, ws, conv_q, conv_k)


def _mlstm_kernel(q_ref, k_ref, v_ref, o_ref, gate_ref, bias_ref, gh_ref, out_ref,
                  c_ref, m_ref, *, chunk):
    L = chunk
    hd = M_HEAD_DIM
    nchunks = q_ref.shape[0] // L

    @pl.when(pl.program_id(1) == 0)
    def _():
        c_ref[...] = jnp.zeros(c_ref.shape, F32)
        m_ref[...] = jnp.zeros(m_ref.shape, F32)

    rows = lax.broadcasted_iota(jnp.int32, (L, L), 0)
    cols = lax.broadcasted_iota(jnp.int32, (L, L), 1)
    causal = cols <= rows
    tri = causal.astype(BF16)
    lane2 = lax.broadcasted_iota(jnp.int32, (L, 2 * hd), 1)
    ones_col = (lane2 == hd).astype(F32)

    def chunk_body(c, _):
        r0 = pl.multiple_of(c * L, L)
        g = gate_ref[pl.ds(r0, L), :] + bias_ref[...]
        gi = g[:, 0:LANES]
        gf = g[:, LANES:2 * LANES]
        lf = jnp.minimum(gf, 0.0) - jnp.log(1.0 + jnp.exp(-jnp.abs(gf)))
        l1, l2, l3 = _split3(lf)
        b = _dot(tri, l1) + _dot(tri, l2) + _dot(tri, l3)
        b_last = b[L - 1:L, :]
        w_end = b_last - b + gi
        m_loc = jnp.max(w_end, axis=0, keepdims=True)
        e_end = jnp.exp(w_end - m_loc)
        m_prev = m_ref[...]
        m_new = jnp.maximum(b_last + m_prev, m_loc)
        decay = jnp.exp(b_last + m_prev - m_new)
        scale = jnp.exp(m_loc - m_new)
        b_t = b.T
        gi_t = gi.T
        for h in range(M_HEADS):
            hs = slice(h * hd, (h + 1) * hd)
            qh = q_ref[pl.ds(r0, L), hs]
            kh = k_ref[pl.ds(r0, L), hs]
            vh = v_ref[pl.ds(r0, L), hs].astype(F32)
            vext = jnp.concatenate([vh, jnp.zeros((L, hd), F32)], axis=1) + ones_col
            bc = b[:, h:h + 1]
            e = jnp.where(causal, bc - b_t[h:h + 1, :] + gi_t[h:h + 1, :], -jnp.inf)
            log_inter = bc + m_prev[:, h:h + 1]
            m_t = jnp.maximum(log_inter, jnp.max(e, axis=1, keepdims=True))
            w = (jnp.exp(e - m_t) * _dot_nt(qh, kh)).astype(BF16)
            a_int = jnp.exp(log_inter - m_t)
            cext = c_ref[h]
            num = _dot(w, vext.astype(BF16)) + a_int * _dot(qh, cext.astype(BF16))
            den = num[:, hd:hd + 1]
            hh = num[:, 0:hd] / jnp.maximum(jnp.abs(den), jnp.exp(-m_t))
            hh = _rms(hh, gh_ref[:, hs])
            og = _sigmoid(o_ref[pl.ds(r0, L), hs].astype(F32))
            out_ref[pl.ds(r0, L), hs] = (og * hh).astype(BF16)
            ev = (e_end[:, h:h + 1] * vext).astype(BF16)
            c_ref[h] = decay[:, h:h + 1] * cext + scale[:, h:h + 1] * _dot_tn(kh, ev)
        m_ref[...] = m_new
        return 0

    lax.fori_loop(0, nchunks, chunk_body, 0)


def _mlstm(mq, mk, mv, mo, gates, b_gates, g_mhead, batch, seq_len, rows):
    t, mw = mq.shape
    h = M_HEADS
    bias = jnp.zeros((1, 2 * LANES), F32)
    bias = bias.at[0, 0:h].set(b_gates[0:h]).at[0, LANES:LANES + h].set(b_gates[h:2 * h])
    nb = seq_len // rows
    row = lambda b, i: (b * nb + i, 0)
    const = lambda b, i: (0, 0)
    return pl.pallas_call(
        functools.partial(_mlstm_kernel, chunk=MLSTM_CHUNK),
        grid=(batch, nb),
        in_specs=[pl.BlockSpec((rows, mw), row)] * 4 + [
            pl.BlockSpec((rows, 2 * LANES), row),
            pl.BlockSpec((1, 2 * LANES), const),
            pl.BlockSpec((1, mw), const),
        ],
        out_specs=pl.BlockSpec((rows, mw), row),
        out_shape=jax.ShapeDtypeStruct((t, mw), BF16),
        scratch_shapes=[pltpu.VMEM((h, M_HEAD_DIM, 2 * M_HEAD_DIM), F32),
                        pltpu.VMEM((1, LANES), F32)],
        compiler_params=pltpu.CompilerParams(
            dimension_semantics=("arbitrary", "arbitrary"), vmem_limit_bytes=VMEM_LIMIT),
        name="mlstm",
    )(mq, mk, mv, mo, gates, bias, g_mhead.reshape(1, mw))


def _sb_kernel(q_ref, k_ref, v_ref, out_ref, acc_ref, carry_ref):
    lax.fori_loop(0, q_ref.shape[0] // SB_BLOCK,
                  functools.partial(_sb_query_block, q_ref, k_ref, v_ref, out_ref, acc_ref, carry_ref),
                  0)


def _sb_query_block(q_ref, k_ref, v_ref, out_ref, acc_ref, carry_ref, qi, _):
    blk = SB_BLOCK
    q0 = pl.multiple_of(qi * blk, blk)
    lane = lax.broadcasted_iota(jnp.int32, (blk, LANES), 1)
    head0 = lane < SB_HEAD_DIM
    q = q_ref[pl.ds(q0, blk), :]
    zero = jnp.zeros_like(q)
    qm = (jnp.where(head0, q, zero), jnp.where(head0, zero, q))
    rows = lax.broadcasted_iota(jnp.int32, (blk, blk), 0)
    cols = lax.broadcasted_iota(jnp.int32, (blk, blk), 1)
    strict = cols < rows
    neg_suffix = jnp.where(rows >= cols, -1.0, 0.0).astype(BF16)

    def block(j, carries, mask=None):
        k0 = pl.multiple_of(j * blk, blk)
        kb = k_ref[pl.ds(k0, blk), :]
        vb = v_ref[pl.ds(k0, blk), :]
        vz = jnp.zeros_like(vb)
        vm = (jnp.where(head0, vb, vz), jnp.where(head0, vz, vb))
        upd = None
        new = []
        for h in range(2):
            z = _dot_nt(qm[h], kb)
            sp = jnp.maximum(z, 0.0) + jnp.log(1.0 + jnp.exp(-jnp.abs(z)))
            if mask is not None:
                sp = jnp.where(mask, sp, 0.0)
            rc = _dot(sp.astype(BF16), neg_suffix)
            p = jnp.exp(z + rc + carries[h])
            if mask is not None:
                p = jnp.where(mask, p, 0.0)
            new.append(carries[h] + rc[:, 0:1])
            d = _dot(p.astype(BF16), vm[h])
            upd = d if upd is None else upd + d
        return upd, new

    zeros = jnp.zeros((blk, 1), F32)
    u_diag, carries = block(qi, [zeros, zeros], strict)
    has_prev = jnp.broadcast_to(qi > 0, (blk, blk))
    u_prev, carries = block(jnp.maximum(qi - 1, 0), carries, has_prev)
    acc_ref[...] = u_diag + u_prev
    carry_ref[0] = carries[0]
    carry_ref[1] = carries[1]

    def cond(state):
        it, top = state
        return (it < qi) & (top > SB_ZERO_LOG)

    def body(state):
        it, _ = state
        upd, new = block(qi - 1 - it, [carry_ref[0], carry_ref[1]])
        acc_ref[...] += upd
        carry_ref[0] = new[0]
        carry_ref[1] = new[1]
        return it + 1, jnp.maximum(jnp.max(new[0]), jnp.max(new[1]))

    lax.while_loop(cond, body, (jnp.int32(1), jnp.maximum(jnp.max(carries[0]), jnp.max(carries[1]))))
    out_ref[pl.ds(q0, blk), :] = acc_ref[...].astype(BF16)
    return 0


def _stickbreak(sq, sk, sv, batch, seq_len):
    t, sw = sq.shape
    npair = sw // LANES
    seq = pl.BlockSpec((seq_len, LANES), lambda b, hp: (b, hp))
    return pl.pallas_call(
        _sb_kernel,
        grid=(batch, npair),
        in_specs=[seq, seq, seq],
        out_specs=seq,
        out_shape=jax.ShapeDtypeStruct((t, sw), BF16),
        scratch_shapes=[pltpu.VMEM((SB_BLOCK, LANES), F32), pltpu.VMEM((2, SB_BLOCK, 1), F32)],
        compiler_params=pltpu.CompilerParams(
            dimension_semantics=("arbitrary", "arbitrary"), vmem_limit_bytes=VMEM_LIMIT),
        name="stickbrk",
    )(sq, sk, sv)


def _outroute_kernel(x_ref, hm_ref, hs_ref, wom_ref, wos_ref, g_ref, wr_ref, br_ref,
                     h1_ref, c_ref, route_ref, cnt_ref, run_ref, *, n_groups, per_group):
    i = pl.program_id(0)
    tm = x_ref.shape[0]

    @pl.when(i == 0)
    def _():
        run_ref[...] = jnp.zeros(run_ref.shape, F32)

    h1 = x_ref[...] + _dot(hm_ref[...], wom_ref[...]) + _dot(hs_ref[...], wos_ref[...])
    h1_ref[...] = h1
    c = _rms(h1, g_ref[...])
    c_ref[...] = c

    c1, c2, _ = _split3(c)
    pa = _dot(c1, wr_ref[...])
    pb = _dot(c2, wr_ref[...])
    logits = (pa[:, 0:LANES] + (pa[:, LANES:] + pb[:, 0:LANES]) + pb[:, LANES:]) + br_ref[...]

    lane = lax.broadcasted_iota(jnp.int32, (tm, LANES), 1).astype(F32)
    ninf = -jnp.inf
    big = float(LANES)

    def first_max(v):
        mx = jnp.max(v, axis=1, keepdims=True)
        idx = jnp.min(jnp.where(v == mx, lane, big), axis=1, keepdims=True)
        return mx, idx

    gl = jnp.where(lane < n_groups, logits, ninf)
    gmax, gsel = first_max(gl)
    p_g = 1.0 / jnp.sum(jnp.exp(gl - gmax), axis=1, keepdims=True)
    lo = LOGIT_LANE_E + per_group * gsel
    el = jnp.where((lane >= lo) & (lane < lo + per_group), logits, ninf)
    v1, i1 = first_max(el)
    v2, i2 = first_max(jnp.where(lane == i1, ninf, el))
    tt = jnp.exp(v2 - v1)
    w0 = p_g / (1.0 + tt)
    w1_ = p_g * tt / (1.0 + tt)

    oh0 = lane == i1
    oh1 = lane == i2
    ohsum = oh0.astype(F32) + oh1.astype(F32)
    rows = lax.broadcasted_iota(jnp.int32, (tm, tm), 0)
    cols = lax.broadcasted_iota(jnp.int32, (tm, tm), 1)
    before = (cols < rows).astype(BF16)
    prefix = _dot(before, ohsum.astype(BF16)) + run_ref[...]
    r0 = jnp.sum(jnp.where(oh0, prefix, 0.0), axis=1, keepdims=True)
    r1 = jnp.sum(jnp.where(oh1, prefix, 0.0), axis=1, keepdims=True)
    run = run_ref[...] + jnp.sum(ohsum, axis=0, keepdims=True)
    run_ref[...] = run
    cnt_ref[...] = jnp.broadcast_to(run, cnt_ref.shape)

    e0 = i1 - LOGIT_LANE_E
    e1 = i2 - LOGIT_LANE_E
    route = jnp.zeros((tm, LANES), F32)
    for ln, val in ((ROUTE_LANE_E, e0), (ROUTE_LANE_E + 1, e1), (ROUTE_LANE_W, w0),
                    (ROUTE_LANE_W + 1, w1_), (ROUTE_LANE_R, r0), (ROUTE_LANE_R + 1, r1)):
        route = jnp.where(lane == ln, val, route)
    route_ref[...] = route


def _tile_rows(ref, s, n):
    return ref.at[pl.ds(s, n, stride=SUBLANES), :]


def _tile_copy(src_ref, src_row, dst_ref, dst_row, sem):
    return pltpu.make_async_copy(
        src_ref.at[pl.ds(pl.multiple_of(src_row * SUBLANES, SUBLANES), SUBLANES), :],
        dst_ref.at[pl.ds(pl.multiple_of(dst_row * SUBLANES, SUBLANES), SUBLANES), :], sem)


def _lanes_to_smem(vals, vm_ref, sm_ref, sem):
    vm_ref[...] = vals.T[0:8, :].astype(jnp.int32)
    cp = pltpu.make_async_copy(vm_ref, sm_ref, sem)
    cp.start()
    cp.wait()


def _outroute(x2, hm, hs, w_out, g_ffn, w_rg, b_rg, w_re, b_re, tm):
    t, d = x2.shape
    mw = hm.shape[1]
    sw = hs.shape[1]
    n_groups = w_rg.shape[1]
    n_exp = w_re.shape[1]
    wr = jnp.zeros((d, LANES), F32)
    wr = wr.at[:, 0:n_groups].set(w_rg).at[:, LOGIT_LANE_E:LOGIT_LANE_E + n_exp].set(w_re)
    wr_hi, wr_lo, _ = _split3(wr)
    wr2 = jnp.concatenate([wr_hi, wr_lo], axis=1)
    br = jnp.zeros((1, LANES), F32)
    br = br.at[0, 0:n_groups].set(b_rg).at[0, LOGIT_LANE_E:LOGIT_LANE_E + n_exp].set(b_re)
    row = lambda i: (i, 0)
    const = lambda i: (0, 0)
    kern = functools.partial(_outroute_kernel, n_groups=n_groups, per_group=n_exp // n_groups)
    return pl.pallas_call(
        kern,
        grid=(t // tm,),
        in_specs=[
            pl.BlockSpec((tm, d), row),
            pl.BlockSpec((tm, mw), row),
            pl.BlockSpec((tm, sw), row),
            pl.BlockSpec((mw, d), const),
            pl.BlockSpec((sw, d), const),
            pl.BlockSpec((1, d), const),
            pl.BlockSpec((d, 2 * LANES), const),
            pl.BlockSpec((1, LANES), const),
        ],
        out_specs=[
            pl.BlockSpec((tm, d), row),
            pl.BlockSpec((tm, d), row),
            pl.BlockSpec((tm, LANES), row),
            pl.BlockSpec((8, LANES), const),
        ],
        out_shape=[
            jax.ShapeDtypeStruct((t, d), F32),
            jax.ShapeDtypeStruct((t, d), F32),
            jax.ShapeDtypeStruct((t, LANES), F32),
            jax.ShapeDtypeStruct((8, LANES), F32),
        ],
        scratch_shapes=[pltpu.VMEM((1, LANES), F32)],
        compiler_params=pltpu.CompilerParams(
            dimension_semantics=("arbitrary",), vmem_limit_bytes=VMEM_LIMIT),
        name="outroute",
    )(x2, hm, hs, w_out[0:mw].astype(BF16), w_out[mw:].astype(BF16), g_ffn.reshape(1, d),
      wr2, br)


def _slotpos_kernel(route_ref, offs_ref, pos_ref):
    route = route_ref[...]
    tm = route.shape[0]
    lane = lax.broadcasted_iota(jnp.int32, (tm, LANES), 1)
    offs = offs_ref[...]
    out = jnp.zeros((tm, LANES), F32)
    for j in range(TOP_K):
        e = route[:, ROUTE_LANE_E + j:ROUTE_LANE_E + j + 1].astype(jnp.int32)
        base = jnp.sum(jnp.where(lane == e, offs, 0.0), axis=1, keepdims=True)
        out = jnp.where(lane == j, base + route[:, ROUTE_LANE_R + j:ROUTE_LANE_R + j + 1], out)
    pos_ref[...] = out.astype(jnp.int32)


def _slotpos(route, offs_row, tm):
    t = route.shape[0]
    return pl.pallas_call(
        _slotpos_kernel,
        grid=(t // tm,),
        in_specs=[pl.BlockSpec((tm, LANES), lambda i: (i, 0)),
                  pl.BlockSpec((1, LANES), lambda i: (0, 0))],
        out_specs=pl.BlockSpec((tm, LANES), lambda i: (i, 0)),
        out_shape=jax.ShapeDtypeStruct((t, LANES), jnp.int32),
        compiler_params=pltpu.CompilerParams(dimension_semantics=("arbitrary",)),
        name="slotpos",
    )(route, offs_row)


def _dispatch_kernel(pos_ref, c_ref, route_ref, init_ref, xin_ref, rows_ref, sem, *, n_tokens):
    del init_ref
    i = pl.program_id(0)
    tm, d = c_ref.shape
    half = d // 2
    lane = lax.broadcasted_iota(jnp.int32, (tm, LANES), 1)
    row_id = (i * tm + lax.broadcasted_iota(jnp.int32, (tm, 1), 0)).astype(F32)
    route = route_ref[...]
    bits = lax.bitcast_convert_type(c_ref[...].astype(BF16).astype(F32), jnp.uint32)
    packed = (bits[:, 0:half] >> 16) | bits[:, half:d]
    for j in range(TOP_K):
        w = route[:, ROUTE_LANE_W + j:ROUTE_LANE_W + j + 1]
        meta = jnp.where(lane == META_DEST, row_id + j * n_tokens, jnp.where(lane == META_W, w, 0.0))
        for s in range(SUBLANES):
            if s < half // LANES:
                sub = packed[:, s * LANES:(s + 1) * LANES]
            elif s == META_SUBLANE:
                sub = lax.bitcast_convert_type(meta, jnp.uint32)
            else:
                sub = jnp.zeros((tm, LANES), jnp.uint32)
            _tile_rows(rows_ref.at[j], s, tm)[...] = sub

    def issue(t, _):
        for j in range(TOP_K):
            _tile_copy(rows_ref.at[j], t, xin_ref, pos_ref[0, 0, TOP_K * t + j],
                       sem).start(priority=j)
        return 0

    lax.fori_loop(0, tm, issue, 0, unroll=ROW_DMA_UNROLL)
    for j in range(TOP_K):
        pltpu.make_async_copy(rows_ref.at[j], rows_ref.at[j], sem).wait()


def _dispatch(pos3, c, route, n_rows, tm):
    t, d = c.shape
    assert d // 2 // LANES <= META_SUBLANE < SUBLANES
    init = jnp.zeros((n_rows * SUBLANES, LANES), jnp.uint32)
    return pl.pallas_call(
        functools.partial(_dispatch_kernel, n_tokens=t),
        grid=(t // tm,),
        in_specs=[
            pl.BlockSpec((1, 1, TOP_K * tm), lambda i: (i, 0, 0), memory_space=pltpu.SMEM),
            pl.BlockSpec((tm, d), lambda i: (i, 0)),
            pl.BlockSpec((tm, LANES), lambda i: (i, 0)),
            pl.BlockSpec(memory_space=pl.ANY),
        ],
        out_specs=pl.BlockSpec(memory_space=pl.ANY),
        out_shape=jax.ShapeDtypeStruct((n_rows * SUBLANES, LANES), jnp.uint32),
        scratch_shapes=[pltpu.VMEM((TOP_K, tm * SUBLANES, LANES), jnp.uint32),
                        pltpu.SemaphoreType.DMA(())],
        input_output_aliases={3: 0},
        compiler_params=pltpu.CompilerParams(
            dimension_semantics=("arbitrary",), vmem_limit_bytes=VMEM_LIMIT),
        name="dispatch",
    )(pos3, c, route, init)


def _experts_kernel(be_ref, bn_ref, nv_ref, x_ref, wg_ref, wu_ref, wd_ref, yout_ref,
                    wgb, wub, wdb, ybuf, dest_vm, dest_sm, sems, dsem, *, dump_row):
    i = pl.program_id(0)
    s = i % 2
    rows = ybuf.shape[1] // SUBLANES
    half_tiles = wgb.shape[0] // 2 // LANES
    prev = be_ref[jnp.maximum(i - 1, 0)]
    active = i < nv_ref[0]

    def to_dump(slot):
        def body(r, _):
            dest_sm[slot, 0, r] = dump_row + r
            return 0
        lax.fori_loop(0, rows, body, 0)

    def send(slot, r, queue=0):
        _tile_copy(ybuf.at[slot], r, yout_ref, dest_sm[slot, 0, r], sems.at[slot]).start(priority=queue)

    def wait(slot):
        pltpu.make_async_copy(ybuf.at[slot], ybuf.at[slot], sems.at[slot]).wait()

    @pl.when(i == 0)
    def _():
        ybuf[1] = jnp.zeros(ybuf.shape[1:], F32)
        to_dump(1)

    @pl.when(i > 0)
    def _():
        wait(s)

    @pl.when(active & ((i == 0) | (be_ref[i] != prev)))
    def _():
        wgb[...] = wg_ref[0].astype(BF16)
        wub[...] = wu_ref[0].astype(BF16)
        wdb[...] = wd_ref[0].astype(BF16)

    @pl.when(active)
    def _():
        for r in range(rows):
            send(1 - s, r, r % 2)
        words = [_tile_rows(x_ref, t, rows)[...] for t in range(half_tiles)]
        lo = [lax.bitcast_convert_type(w << 16, F32) for w in words]
        hi = [lax.bitcast_convert_type(w & jnp.uint32(0xFFFF0000), F32) for w in words]
        x = jnp.concatenate(lo + hi, axis=1).astype(BF16)
        meta = lax.bitcast_convert_type(_tile_rows(x_ref, META_SUBLANE, rows)[...], F32)
        gt = _dot(x, wgb[...])
        up = _dot(x, wub[...])
        hid = (gt * _sigmoid(gt) * up).astype(BF16)
        y = _dot(hid, wdb[...]) * meta[:, META_W:META_W + 1]
        for t in range(SUBLANES):
            _tile_rows(ybuf.at[s], t, rows)[...] = y[:, t * LANES:(t + 1) * LANES]
        row = lax.broadcasted_iota(jnp.int32, (rows, LANES), 0)
        dest = jnp.where(row < bn_ref[i], meta, (dump_row + row).astype(F32))
        _lanes_to_smem(dest, dest_vm, dest_sm.at[s], dsem)

    @pl.when(jnp.logical_not(active))
    def _():
        lax.fori_loop(0, rows, lambda r, _: send(1 - s, r) or 0, 0, unroll=ROW_DMA_UNROLL)
        to_dump(s)

    @pl.when(i == pl.num_programs(0) - 1)
    def _():
        wait(1 - s)


def _experts(blk_e, blk_n, n_valid, xin, w_gate, w_up, w_down, n_out_rows):
    r = xin.shape[0] // SUBLANES
    d, de = w_gate.shape[1], w_gate.shape[2]
    assert d == SUBLANES * LANES, "an output row is one (8, 128) f32 tile"
    rows = EXPERT_ROWS
    last = r // rows - 1
    grid_spec = pltpu.PrefetchScalarGridSpec(
        num_scalar_prefetch=3,
        grid=(r // rows + 1,),
        in_specs=[
            pl.BlockSpec((rows * SUBLANES, LANES), lambda i, be, bn, nv: (jnp.minimum(i, last), 0)),
            pl.BlockSpec((1, d, de), lambda i, be, bn, nv: (be[i], 0, 0)),
            pl.BlockSpec((1, d, de), lambda i, be, bn, nv: (be[i], 0, 0)),
            pl.BlockSpec((1, de, d), lambda i, be, bn, nv: (be[i], 0, 0)),
        ],
        out_specs=pl.BlockSpec(memory_space=pl.ANY),
        scratch_shapes=[pltpu.VMEM((d, de), BF16), pltpu.VMEM((d, de), BF16),
                        pltpu.VMEM((de, d), BF16),
                        pltpu.VMEM((2, rows * SUBLANES, LANES), F32),
                        pltpu.VMEM((8, rows), jnp.int32),
                        pltpu.SMEM((2, 8, rows), jnp.int32),
                        pltpu.SemaphoreType.DMA((2,)),
                        pltpu.SemaphoreType.DMA(())],
    )
    return pl.pallas_call(
        functools.partial(_experts_kernel, dump_row=n_out_rows),
        grid_spec=grid_spec,
        out_shape=jax.ShapeDtypeStruct(((n_out_rows + rows) * SUBLANES, LANES), F32),
        compiler_params=pltpu.CompilerParams(
            dimension_semantics=("arbitrary",), vmem_limit_bytes=VMEM_LIMIT),
        name="experts",
    )(blk_e, blk_n, n_valid, xin, w_gate, w_up, w_down)


def _combine_kernel(h1_ref, y0_ref, y1_ref, p_ref, wpg_ref, wpp_ref,
                    gple_ref, gpost_ref, gfin_ref, out_ref):
    tm = h1_ref.shape[0]
    y = jnp.concatenate([_tile_rows(y0_ref, s, tm)[...] + _tile_rows(y1_ref, s, tm)[...]
                         for s in range(SUBLANES)], axis=1)
    h2 = h1_ref[...] + y
    gate = _sigmoid(_dot(_rms(h2, gple_ref[...]).astype(BF16), wpg_ref[...]))
    ple = _rms(_dot(p_ref[...].astype(BF16), wpp_ref[...]), gpost_ref[...])
    h3 = h2 + gate * ple
    out_ref[...] = _rms(h3, gfin_ref[...])


def _combine(h1, p2, y, w_pg, w_pp, g_ple, g_post, g_final, tm):
    t, d = h1.shape
    pd = p2.shape[1]
    row = lambda i: (i, 0)
    const = lambda i: (0, 0)
    return pl.pallas_call(
        _combine_kernel,
        grid=(t // tm,),
        in_specs=[
            pl.BlockSpec((tm, d), row),
            pl.BlockSpec((tm * SUBLANES, LANES), lambda i: (i, 0)),
            pl.BlockSpec((tm * SUBLANES, LANES), lambda i: (t // tm + i, 0)),
            pl.BlockSpec((tm, pd), row),
            pl.BlockSpec((d, d), const),
            pl.BlockSpec((pd, d), const),
            pl.BlockSpec((1, d), const),
            pl.BlockSpec((1, d), const),
            pl.BlockSpec((1, d), const),
        ],
        out_specs=pl.BlockSpec((tm, d), row),
        out_shape=jax.ShapeDtypeStruct((t, d), F32),
        compiler_params=pltpu.CompilerParams(
            dimension_semantics=("arbitrary",), vmem_limit_bytes=VMEM_LIMIT),
        name="combine",
    )(h1, y, y, p2, w_pg.astype(BF16), w_pp.astype(BF16),
      g_ple.reshape(1, d), g_post.reshape(1, d), g_final.reshape(1, d))


def _largest_tile(n, cap):
    tile = cap
    while n % tile:
        tile //= 2
    return tile


def kernel(x, p, g_mix, w_in, b_gates, conv_q, conv_k, g_mhead, w_out, g_ffn, w_router_group,
           b_router_group, w_router_expert, b_router_expert, w_exp_gate, w_exp_up, w_exp_down,
           g_ple, w_ple_gate, w_ple_proj, g_ple_post, g_final):
    batch, seq_len, d = x.shape
    t = batch * seq_len
    tm = _largest_tile(seq_len, 512)
    n_exp = w_router_expert.shape[-1]
    rows = EXPERT_ROWS
    nblk = t * TOP_K // rows + n_exp

    assert w_in.shape[0] == 1, "single-layer block"
    l = 0
    h = x.reshape(t, d)
    mq, mk, mv, mo, gates, sq, sk, sv = _inproj(
        h, g_mix[l], w_in[l], conv_q[l], conv_k[l], seq_len, tm)
    hm = _mlstm(mq, mk, mv, mo, gates, b_gates[l], g_mhead[l], batch, seq_len,
                _largest_tile(seq_len, 1024))
    hs = _stickbreak(sq, sk, sv, batch, seq_len)
    h1, c, route, counts = _outroute(
        h, hm, hs, w_out[l], g_ffn[l], w_router_group[l], b_router_group[l],
        w_router_expert[l], b_router_expert[l], tm)

    cnt = counts[0, LOGIT_LANE_E:LOGIT_LANE_E + n_exp].astype(jnp.int32)
    nb_e = (cnt + rows - 1) // rows
    cum = jnp.cumsum(nb_e)
    offs_row = jnp.zeros((1, LANES), F32).at[0, 0:n_exp].set(((cum - nb_e) * rows).astype(F32))
    n_valid = cum[-1:]
    step = jnp.arange(nblk + 1, dtype=jnp.int32)
    blk_e = jnp.minimum(jnp.sum(cum[None, :] <= step[:, None], axis=1), n_exp - 1).astype(jnp.int32)
    blk_n = jnp.clip(cnt[blk_e] - rows * (step - (cum - nb_e)[blk_e]), 0, rows)

    pos = _slotpos(route, offs_row, _largest_tile(t, 2048))
    tm_rows = _largest_tile(seq_len, 256)
    pos3 = pos[:, 0:TOP_K].reshape(t // tm_rows, 1, TOP_K * tm_rows)
    xin = _dispatch(pos3, c, route, nblk * rows, tm_rows)
    y = _experts(blk_e, blk_n, n_valid, xin, w_exp_gate[l], w_exp_up[l], w_exp_down[l], TOP_K * t)
    out = _combine(h1, p[l].reshape(t, -1), y, w_ple_gate[l], w_ple_proj[l],
                   g_ple[l], g_ple_post[l], g_final, tm)
    return out.reshape(batch, seq_len, d)
```

```python
import functools

import jax
import jax.numpy as jnp
from jax import lax
from jax.experimental import pallas as pl
from jax.experimental.pallas import tpu as pltpu

F32 = jnp.float32
BF16 = jnp.bfloat16
EPS = 1e-6

M_HEADS = 4
M_HEAD_DIM = 128
SB_HEAD_DIM = 64
CONV_WIDTH = 4
TOP_K = 2
LANES = 128
VMEM_LIMIT = 56 * 1024 * 1024

MLSTM_CHUNK = 128
SB_BLOCK = 256
SB_ZERO_LOG = -105.0
EXPERT_ROWS = 256
ROW_DMA_UNROLL = 8
SEND_PACE_SHIFTS = 4
ROUTE_LANE_E = 0
ROUTE_LANE_W = 2
ROUTE_LANE_R = 4
SUBLANES = 8
META_SUBLANE = 4
META_DEST = 0
META_W = 1
LOGIT_LANE_E = 4


def _rms(x, g):
    return x * lax.rsqrt(jnp.mean(x * x, axis=-1, keepdims=True) + EPS) * g


def _sigmoid(x):
    return 1.0 / (1.0 + jnp.exp(-x))


def _split3(a):
    a1 = a.astype(BF16)
    r1 = a - a1.astype(F32)
    a2 = r1.astype(BF16)
    a3 = (r1 - a2.astype(F32)).astype(BF16)
    return a1, a2, a3


def _dot(a, b):
    return jnp.dot(a, b, preferred_element_type=F32)


def _dot_nt(a, b):
    return lax.dot_general(a, b, (((1,), (1,)), ((), ())), preferred_element_type=F32)


def _dot_tn(a, b):
    return lax.dot_general(a, b, (((0,), (0,)), ((), ())), preferred_element_type=F32)


def _inproj_kernel(x_ref, g_ref, wqk_ref, wvo_ref, wg_ref, ws_ref, cq_ref, ck_ref,
                   mq_ref, mk_ref, mv_ref, mo_ref, gate_ref, sq_ref, sk_ref, sv_ref,
                   ext_ref, *, tiles_per_seq, k_scale):
    i = pl.program_id(0)
    tm = x_ref.shape[0]
    mw = mq_ref.shape[1]
    sw = sq_ref.shape[1]
    a = _rms(x_ref[...], g_ref[...]).astype(BF16)

    @pl.when(i % tiles_per_seq == 0)
    def _():
        ext_ref[0:8, :] = jnp.zeros((8, 2 * mw), F32)

    ext_ref[8:8 + tm, 0:mw] = _dot(a, wqk_ref[:, 0:mw])
    ext_ref[8:8 + tm, mw:2 * mw] = _dot(a, wqk_ref[:, mw:2 * mw])

    def conv_silu(w_ref, c0):
        acc = ext_ref[pl.ds(8 - (CONV_WIDTH - 1), tm), c0:c0 + mw] * w_ref[0:1, :]
        for j in range(1, CONV_WIDTH):
            acc = acc + ext_ref[pl.ds(8 - (CONV_WIDTH - 1) + j, tm), c0:c0 + mw] * w_ref[j:j + 1, :]
        return acc * _sigmoid(acc)

    mq_ref[...] = conv_silu(cq_ref, 0).astype(BF16)
    mk_ref[...] = (conv_silu(ck_ref, mw) * k_scale).astype(BF16)
    ext_ref[0:8, :] = ext_ref[tm:tm + 8, :]

    mv_ref[...] = _dot(a, wvo_ref[:, 0:mw]).astype(BF16)
    mo_ref[...] = _dot(a, wvo_ref[:, mw:2 * mw]).astype(BF16)
    gate_ref[...] = _dot(a, wg_ref[...])
    sq_ref[...] = _dot(a, ws_ref[:, 0:sw]).astype(BF16)
    sk_ref[...] = _dot(a, ws_ref[:, sw:2 * sw]).astype(BF16)
    sv_ref[...] = _dot(a, ws_ref[:, 2 * sw:3 * sw]).astype(BF16)


def _inproj(x2, g_mix, w_in, conv_q, conv_k, seq_len, tm):
    t, d = x2.shape
    mw = conv_q.shape[1]
    h = M_HEADS
    sw = (w_in.shape[1] - 4 * mw - 2 * h) // 3
    wqk = w_in[:, 0:2 * mw].astype(BF16)
    wvo = w_in[:, 2 * mw:4 * mw].astype(BF16)
    wg = jnp.zeros((d, 2 * LANES), F32)
    wg = wg.at[:, 0:h].set(w_in[:, 4 * mw:4 * mw + h])
    wg = wg.at[:, LANES:LANES + h].set(w_in[:, 4 * mw + h:4 * mw + 2 * h]).astype(BF16)
    ws = w_in[:, 4 * mw + 2 * h:]
    ws = jnp.concatenate([ws[:, 0:sw] * (SB_HEAD_DIM ** -0.5), ws[:, sw:]], axis=1).astype(BF16)
    row = lambda i: (i, 0)
    const = lambda i: (0, 0)
    kern = functools.partial(_inproj_kernel, tiles_per_seq=seq_len // tm, k_scale=M_HEAD_DIM ** -0.5)
    bf = lambda w: jax.ShapeDtypeStruct((t, w), BF16)
    return pl.pallas_call(
        kern,
        grid=(t // tm,),
        in_specs=[
            pl.BlockSpec((tm, d), row),
            pl.BlockSpec((1, d), const),
            pl.BlockSpec((d, 2 * mw), const),
            pl.BlockSpec((d, 2 * mw), const),
            pl.BlockSpec((d, 2 * LANES), const),
            pl.BlockSpec((d, 3 * sw), const),
            pl.BlockSpec((CONV_WIDTH, mw), const),
            pl.BlockSpec((CONV_WIDTH, mw), const),
        ],
        out_specs=[
            pl.BlockSpec((tm, mw), row), pl.BlockSpec((tm, mw), row),
            pl.BlockSpec((tm, mw), row), pl.BlockSpec((tm, mw), row),
            pl.BlockSpec((tm, 2 * LANES), row),
            pl.BlockSpec((tm, sw), row), pl.BlockSpec((tm, sw), row), pl.BlockSpec((tm, sw), row),
        ],
        out_shape=[bf(mw), bf(mw), bf(mw), bf(mw),
                   jax.ShapeDtypeStruct((t, 2 * LANES), F32), bf(sw), bf(sw), bf(sw)],
        scratch_shapes=[pltpu.VMEM((tm + 8, 2 * mw), F32)],
        compiler_params=pltpu.CompilerParams(
            dimension_semantics=("arbitrary",), vmem_limit_bytes=VMEM_LIMIT),
        name="inproj",
    )(x2, g_mix.reshape(1, d), wqk, wvo, wg, ws, conv_q, conv_k)


def _mlstm_kernel(q_ref, k_ref, v_ref, o_ref, gate_ref, bias_ref, gh_ref, out_ref,
                  c_ref, m_ref, *, chunk):
    L = chunk
    hd = M_HEAD_DIM
    nchunks = q_ref.shape[0] // L

    @pl.when(pl.program_id(1) == 0)
    def _():
        c_ref[...] = jnp.zeros(c_ref.shape, F32)
        m_ref[...] = jnp.zeros(m_ref.shape, F32)

    rows = lax.broadcasted_iota(jnp.int32, (L, L), 0)
    cols = lax.broadcasted_iota(jnp.int32, (L, L), 1)
    causal = cols <= rows
    tri = causal.astype(BF16)
    lane2 = lax.broadcasted_iota(jnp.int32, (L, 2 * hd), 1)
    ones_col = (lane2 == hd).astype(F32)

    def chunk_body(c, _):
        r0 = pl.multiple_of(c * L, L)
        g = gate_ref[pl.ds(r0, L), :] + bias_ref[...]
        gi = g[:, 0:LANES]
        gf = g[:, LANES:2 * LANES]
        lf = jnp.minimum(gf, 0.0) - jnp.log(1.0 + jnp.exp(-jnp.abs(gf)))
        l1, l2, l3 = _split3(lf)
        b = _dot(tri, l1) + _dot(tri, l2) + _dot(tri, l3)
        b_last = b[L - 1:L, :]
        w_end = b_last - b + gi
        m_loc = jnp.max(w_end, axis=0, keepdims=True)
        e_end = jnp.exp(w_end - m_loc)
        m_prev = m_ref[...]
        m_new = jnp.maximum(b_last + m_prev, m_loc)
        decay = jnp.exp(b_last + m_prev - m_new)
        scale = jnp.exp(m_loc - m_new)
        b_t = b.T
        gi_t = gi.T
        for h in range(M_HEADS):
            hs = slice(h * hd, (h + 1) * hd)
            qh = q_ref[pl.ds(r0, L), hs]
            kh = k_ref[pl.ds(r0, L), hs]
            vh = v_ref[pl.ds(r0, L), hs].astype(F32)
            vext = jnp.concatenate([vh, jnp.zeros((L, hd), F32)], axis=1) + ones_col
            bc = b[:, h:h + 1]
            e = jnp.where(causal, bc - b_t[h:h + 1, :] + gi_t[h:h + 1, :], -jnp.inf)
            log_inter = bc + m_prev[:, h:h + 1]
            m_t = jnp.maximum(log_inter, jnp.max(e, axis=1, keepdims=True))
            w = (jnp.exp(e - m_t) * _dot_nt(qh, kh)).astype(BF16)
            a_int = jnp.exp(log_inter - m_t)
            cext = c_ref[h]
            num = _dot(w, vext.astype(BF16)) + a_int * _dot(qh, cext.astype(BF16))
            den = num[:, hd:hd + 1]
            hh = num[:, 0:hd] / jnp.maximum(jnp.abs(den), jnp.exp(-m_t))
            hh = _rms(hh, gh_ref[:, hs])
            og = _sigmoid(o_ref[pl.ds(r0, L), hs].astype(F32))
            out_ref[pl.ds(r0, L), hs] = (og * hh).astype(BF16)
            ev = (e_end[:, h:h + 1] * vext).astype(BF16)
            c_ref[h] = decay[:, h:h + 1] * cext + scale[:, h:h + 1] * _dot_tn(kh, ev)
        m_ref[...] = m_new
        return 0

    lax.fori_loop(0, nchunks, chunk_body, 0)


def _mlstm(mq, mk, mv, mo, gates, b_gates, g_mhead, batch, seq_len, rows):
    t, mw = mq.shape
    h = M_HEADS
    bias = jnp.zeros((1, 2 * LANES), F32)
    bias = bias.at[0, 0:h].set(b_gates[0:h]).at[0, LANES:LANES + h].set(b_gates[h:2 * h])
    nb = seq_len // rows
    row = lambda b, i: (b * nb + i, 0)
    const = lambda b, i: (0, 0)
    return pl.pallas_call(
        functools.partial(_mlstm_kernel, chunk=MLSTM_CHUNK),
        grid=(batch, nb),
        in_specs=[pl.BlockSpec((rows, mw), row)] * 4 + [
            pl.BlockSpec((rows, 2 * LANES), row),
            pl.BlockSpec((1, 2 * LANES), const),
            pl.BlockSpec((1, mw), const),
        ],
        out_specs=pl.BlockSpec((rows, mw), row),
        out_shape=jax.ShapeDtypeStruct((t, mw), BF16),
        scratch_shapes=[pltpu.VMEM((h, M_HEAD_DIM, 2 * M_HEAD_DIM), F32),
                        pltpu.VMEM((1, LANES), F32)],
        compiler_params=pltpu.CompilerParams(
            dimension_semantics=("arbitrary", "arbitrary"), vmem_limit_bytes=VMEM_LIMIT),
        name="mlstm",
    )(mq, mk, mv, mo, gates, bias, g_mhead.reshape(1, mw))


def _sb_kernel(q_ref, k_ref, v_ref, out_ref, acc_ref, carry_ref):
    lax.fori_loop(0, q_ref.shape[0] // SB_BLOCK,
                  functools.partial(_sb_query_block, q_ref, k_ref, v_ref, out_ref, acc_ref, carry_ref),
                  0)


def _sb_query_block(q_ref, k_ref, v_ref, out_ref, acc_ref, carry_ref, qi, _):
    blk = SB_BLOCK
    q0 = pl.multiple_of(qi * blk, blk)
    lane = lax.broadcasted_iota(jnp.int32, (blk, LANES), 1)
    head0 = lane < SB_HEAD_DIM
    q = q_ref[pl.ds(q0, blk), :]
    zero = jnp.zeros_like(q)
    qm = (jnp.where(head0, q, zero), jnp.where(head0, zero, q))
    rows = lax.broadcasted_iota(jnp.int32, (blk, blk), 0)
    cols = lax.broadcasted_iota(jnp.int32, (blk, blk), 1)
    strict = cols < rows
    neg_suffix = jnp.where(rows >= cols, -1.0, 0.0).astype(BF16)

    def block(j, carries, mask=None):
        k0 = pl.multiple_of(j * blk, blk)
        kb = k_ref[pl.ds(k0, blk), :]
        vb = v_ref[pl.ds(k0, blk), :]
        vz = jnp.zeros_like(vb)
        vm = (jnp.where(head0, vb, vz), jnp.where(head0, vz, vb))
        upd = None
        new = []
        for h in range(2):
            z = _dot_nt(qm[h], kb)
            sp = jnp.maximum(z, 0.0) + jnp.log(1.0 + jnp.exp(-jnp.abs(z)))
            if mask is not None:
                sp = jnp.where(mask, sp, 0.0)
            rc = _dot(sp.astype(BF16), neg_suffix)
            p = jnp.exp(z + rc + carries[h])
            if mask is not None:
                p = jnp.where(mask, p, 0.0)
            new.append(carries[h] + rc[:, 0:1])
            d = _dot(p.astype(BF16), vm[h])
            upd = d if upd is None else upd + d
        return upd, new

    zeros = jnp.zeros((blk, 1), F32)
    u_diag, carries = block(qi, [zeros, zeros], strict)
    has_prev = jnp.broadcast_to(qi > 0, (blk, blk))
    u_prev, carries = block(jnp.maximum(qi - 1, 0), carries, has_prev)
    acc_ref[...] = u_diag + u_prev
    carry_ref[0] = carries[0]
    carry_ref[1] = carries[1]

    def cond(state):
        it, top = state
        return (it < qi) & (top > SB_ZERO_LOG)

    def body(state):
        it, _ = state
        upd, new = block(qi - 1 - it, [carry_ref[0], carry_ref[1]])
        acc_ref[...] += upd
        carry_ref[0] = new[0]
        carry_ref[1] = new[1]
        return it + 1, jnp.maximum(jnp.max(new[0]), jnp.max(new[1]))

    lax.while_loop(cond, body, (jnp.int32(1), jnp.maximum(jnp.max(carries[0]), jnp.max(carries[1]))))
    out_ref[pl.ds(q0, blk), :] = acc_ref[...].astype(BF16)
    return 0


def _stickbreak(sq, sk, sv, batch, seq_len):
    t, sw = sq.shape
    npair = sw // LANES
    seq = pl.BlockSpec((seq_len, LANES), lambda b, hp: (b, hp))
    return pl.pallas_call(
        _sb_kernel,
        grid=(batch, npair),
        in_specs=[seq, seq, seq],
        out_specs=seq,
        out_shape=jax.ShapeDtypeStruct((t, sw), BF16),
        scratch_shapes=[pltpu.VMEM((SB_BLOCK, LANES), F32), pltpu.VMEM((2, SB_BLOCK, 1), F32)],
        compiler_params=pltpu.CompilerParams(
            dimension_semantics=("arbitrary", "arbitrary"), vmem_limit_bytes=VMEM_LIMIT),
        name="stickbrk",
    )(sq, sk, sv)


def _outroute_kernel(x_ref, hm_ref, hs_ref, wom_ref, wos_ref, g_ref, wr_ref, br_ref,
                     h1_ref, c_ref, route_ref, cnt_ref, run_ref, *, n_groups, per_group):
    i = pl.program_id(0)
    tm = x_ref.shape[0]

    @pl.when(i == 0)
    def _():
        run_ref[...] = jnp.zeros(run_ref.shape, F32)

    h1 = x_ref[...] + _dot(hm_ref[...], wom_ref[...]) + _dot(hs_ref[...], wos_ref[...])
    h1_ref[...] = h1
    c = _rms(h1, g_ref[...])
    c_ref[...] = c

    c1, c2, _ = _split3(c)
    pa = _dot(c1, wr_ref[...])
    pb = _dot(c2, wr_ref[...])
    logits = (pa[:, 0:LANES] + (pa[:, LANES:] + pb[:, 0:LANES]) + pb[:, LANES:]) + br_ref[...]

    lane = lax.broadcasted_iota(jnp.int32, (tm, LANES), 1).astype(F32)
    ninf = -jnp.inf
    big = float(LANES)

    def first_max(v):
        mx = jnp.max(v, axis=1, keepdims=True)
        idx = jnp.min(jnp.where(v == mx, lane, big), axis=1, keepdims=True)
        return mx, idx

    gl = jnp.where(lane < n_groups, logits, ninf)
    gmax, gsel = first_max(gl)
    p_g = 1.0 / jnp.sum(jnp.exp(gl - gmax), axis=1, keepdims=True)
    lo = LOGIT_LANE_E + per_group * gsel
    el = jnp.where((lane >= lo) & (lane < lo + per_group), logits, ninf)
    v1, i1 = first_max(el)
    v2, i2 = first_max(jnp.where(lane == i1, ninf, el))
    tt = jnp.exp(v2 - v1)
    w0 = p_g / (1.0 + tt)
    w1_ = p_g * tt / (1.0 + tt)

    oh0 = lane == i1
    oh1 = lane == i2
    ohsum = oh0.astype(F32) + oh1.astype(F32)
    rows = lax.broadcasted_iota(jnp.int32, (tm, tm), 0)
    cols = lax.broadcasted_iota(jnp.int32, (tm, tm), 1)
    before = (cols < rows).astype(BF16)
    prefix = _dot(before, ohsum.astype(BF16)) + run_ref[...]
    r0 = jnp.sum(jnp.where(oh0, prefix, 0.0), axis=1, keepdims=True)
    r1 = jnp.sum(jnp.where(oh1, prefix, 0.0), axis=1, keepdims=True)
    run = run_ref[...] + jnp.sum(ohsum, axis=0, keepdims=True)
    run_ref[...] = run
    cnt_ref[...] = jnp.broadcast_to(run, cnt_ref.shape)

    e0 = i1 - LOGIT_LANE_E
    e1 = i2 - LOGIT_LANE_E
    route = jnp.zeros((tm, LANES), F32)
    for ln, val in ((ROUTE_LANE_E, e0), (ROUTE_LANE_E + 1, e1), (ROUTE_LANE_W, w0),
                    (ROUTE_LANE_W + 1, w1_), (ROUTE_LANE_R, r0), (ROUTE_LANE_R + 1, r1)):
        route = jnp.where(lane == ln, val, route)
    route_ref[...] = route


def _tile_rows(ref, s, n):
    return ref.at[pl.ds(s, n, stride=SUBLANES), :]


def _tile_copy(src_ref, src_row, dst_ref, dst_row, sem):
    return pltpu.make_async_copy(
        src_ref.at[pl.ds(pl.multiple_of(src_row * SUBLANES, SUBLANES), SUBLANES), :],
        dst_ref.at[pl.ds(pl.multiple_of(dst_row * SUBLANES, SUBLANES), SUBLANES), :], sem)


def _lanes_to_smem(vals, vm_ref, sm_ref, sem):
    vm_ref[...] = vals.T[0:8, :].astype(jnp.int32)
    cp = pltpu.make_async_copy(vm_ref, sm_ref, sem)
    cp.start()
    cp.wait()


def _outroute(x2, hm, hs, w_out, g_ffn, w_rg, b_rg, w_re, b_re, tm):
    t, d = x2.shape
    mw = hm.shape[1]
    sw = hs.shape[1]
    n_groups = w_rg.shape[1]
    n_exp = w_re.shape[1]
    wr = jnp.zeros((d, LANES), F32)
    wr = wr.at[:, 0:n_groups].set(w_rg).at[:, LOGIT_LANE_E:LOGIT_LANE_E + n_exp].set(w_re)
    wr_hi, wr_lo, _ = _split3(wr)
    wr2 = jnp.concatenate([wr_hi, wr_lo], axis=1)
    br = jnp.zeros((1, LANES), F32)
    br = br.at[0, 0:n_groups].set(b_rg).at[0, LOGIT_LANE_E:LOGIT_LANE_E + n_exp].set(b_re)
    row = lambda i: (i, 0)
    const = lambda i: (0, 0)
    kern = functools.partial(_outroute_kernel, n_groups=n_groups, per_group=n_exp // n_groups)
    return pl.pallas_call(
        kern,
        grid=(t // tm,),
        in_specs=[
            pl.BlockSpec((tm, d), row),
            pl.BlockSpec((tm, mw), row),
            pl.BlockSpec((tm, sw), row),
            pl.BlockSpec((mw, d), const),
            pl.BlockSpec((sw, d), const),
            pl.BlockSpec((1, d), const),
            pl.BlockSpec((d, 2 * LANES), const),
            pl.BlockSpec((1, LANES), const),
        ],
        out_specs=[
            pl.BlockSpec((tm, d), row),
            pl.BlockSpec((tm, d), row),
            pl.BlockSpec((tm, LANES), row),
            pl.BlockSpec((8, LANES), const),
        ],
        out_shape=[
            jax.ShapeDtypeStruct((t, d), F32),
            jax.ShapeDtypeStruct((t, d), F32),
            jax.ShapeDtypeStruct((t, LANES), F32),
            jax.ShapeDtypeStruct((8, LANES), F32),
        ],
        scratch_shapes=[pltpu.VMEM((1, LANES), F32)],
        compiler_params=pltpu.CompilerParams(
            dimension_semantics=("arbitrary",), vmem_limit_bytes=VMEM_LIMIT),
        name="outroute",
    )(x2, hm, hs, w_out[0:mw].astype(BF16), w_out[mw:].astype(BF16), g_ffn.reshape(1, d),
      wr2, br)


def _slotpos_kernel(route_ref, offs_ref, pos_ref):
    route = route_ref[...]
    tm = route.shape[0]
    lane = lax.broadcasted_iota(jnp.int32, (tm, LANES), 1)
    offs = offs_ref[...]
    out = jnp.zeros((tm, LANES), F32)
    for j in range(TOP_K):
        e = route[:, ROUTE_LANE_E + j:ROUTE_LANE_E + j + 1].astype(jnp.int32)
        base = jnp.sum(jnp.where(lane == e, offs, 0.0), axis=1, keepdims=True)
        out = jnp.where(lane == j, base + route[:, ROUTE_LANE_R + j:ROUTE_LANE_R + j + 1], out)
    pos_ref[...] = out.T[0:SUBLANES, :].astype(jnp.int32)


def _slotpos(route, offs_row, tm):
    t = route.shape[0]
    return pl.pallas_call(
        _slotpos_kernel,
        grid=(t // tm,),
        in_specs=[pl.BlockSpec((tm, LANES), lambda i: (i, 0)),
                  pl.BlockSpec((1, LANES), lambda i: (0, 0))],
        out_specs=pl.BlockSpec((SUBLANES, tm), lambda i: (0, i)),
        out_shape=jax.ShapeDtypeStruct((SUBLANES, t), jnp.int32),
        compiler_params=pltpu.CompilerParams(dimension_semantics=("arbitrary",)),
        name="slotpos",
    )(route, offs_row)


def _dispatch_kernel(pad_lo, pad_hi, nv_ref, pos_ref, c_ref, route_ref, xin_ref,
                     rows_ref, zero_ref, sem, zsem, *, n_tokens, n_blocks):
    i = pl.program_id(0)
    tm, d = c_ref.shape
    blk = zero_ref.shape[0] // SUBLANES

    @pl.when(i == 0)
    def _():
        zero_ref[...] = jnp.zeros(zero_ref.shape, jnp.uint32)

        def whole_block(b):
            return pltpu.make_async_copy(
                zero_ref, xin_ref.at[pl.ds(pl.multiple_of(b * blk * SUBLANES, SUBLANES),
                                           blk * SUBLANES), :], zsem)

        def each_pad(fn):
            for e in range(pad_lo.shape[0]):
                lax.fori_loop(pad_lo[e], pad_hi[e], lambda r, _: fn(_tile_copy(zero_ref, 0, xin_ref, r, zsem)) or 0, 0)
            lax.fori_loop(nv_ref[0], n_blocks, lambda b, _: fn(whole_block(b)) or 0, 0)

        each_pad(lambda cp: cp.start())
        each_pad(lambda cp: cp.wait())

    half = d // 2
    lane = lax.broadcasted_iota(jnp.int32, (tm, LANES), 1)
    row_id = (i * tm + lax.broadcasted_iota(jnp.int32, (tm, 1), 0)).astype(F32)
    route = route_ref[...]
    bits = lax.bitcast_convert_type(c_ref[...].astype(BF16).astype(F32), jnp.uint32)
    packed = (bits[:, 0:half] >> 16) | bits[:, half:d]
    for j in range(TOP_K):
        w = route[:, ROUTE_LANE_W + j:ROUTE_LANE_W + j + 1]
        meta = jnp.where(lane == META_DEST, row_id + j * n_tokens, jnp.where(lane == META_W, w, 0.0))
        for s in range(SUBLANES):
            if s < half // LANES:
                sub = packed[:, s * LANES:(s + 1) * LANES]
            elif s == META_SUBLANE:
                sub = lax.bitcast_convert_type(meta, jnp.uint32)
            else:
                sub = jnp.zeros((tm, LANES), jnp.uint32)
            _tile_rows(rows_ref.at[j], s, tm)[...] = sub

    def issue(t, _):
        for j in range(TOP_K):
            _tile_copy(rows_ref.at[j], t, xin_ref, pos_ref[j, t], sem).start(priority=j)
        return 0

    lax.fori_loop(0, tm, issue, 0, unroll=ROW_DMA_UNROLL)
    for j in range(TOP_K):
        pltpu.make_async_copy(rows_ref.at[j], rows_ref.at[j], sem).wait()


def _dispatch(pad_lo, pad_hi, n_valid, pos, c, route, n_blocks, tm):
    t, d = c.shape
    assert d // 2 // LANES <= META_SUBLANE < SUBLANES
    rows = EXPERT_ROWS
    grid_spec = pltpu.PrefetchScalarGridSpec(
        num_scalar_prefetch=3,
        grid=(t // tm,),
        in_specs=[
            pl.BlockSpec((SUBLANES, tm), lambda i, *_: (0, i), memory_space=pltpu.SMEM),
            pl.BlockSpec((tm, d), lambda i, *_: (i, 0)),
            pl.BlockSpec((tm, LANES), lambda i, *_: (i, 0)),
        ],
        out_specs=pl.BlockSpec(memory_space=pl.ANY),
        scratch_shapes=[pltpu.VMEM((TOP_K, tm * SUBLANES, LANES), jnp.uint32),
                        pltpu.VMEM((rows * SUBLANES, LANES), jnp.uint32),
                        pltpu.SemaphoreType.DMA(()),
                        pltpu.SemaphoreType.DMA(())],
    )
    return pl.pallas_call(
        functools.partial(_dispatch_kernel, n_tokens=t, n_blocks=n_blocks),
        grid_spec=grid_spec,
        out_shape=jax.ShapeDtypeStruct((n_blocks * rows * SUBLANES, LANES), jnp.uint32),
        compiler_params=pltpu.CompilerParams(
            dimension_semantics=("arbitrary",), vmem_limit_bytes=VMEM_LIMIT),
        name="dispatch",
    )(pad_lo, pad_hi, n_valid, pos, c, route)


def _experts_kernel(be_ref, bn_ref, nv_ref, x_ref, wg_ref, wu_ref, wd_ref, yout_ref,
                    wgb, wub, wdb, ybuf, dest_vm, dest_sm, sems, dsem, *, dump_row):
    i = pl.program_id(0)
    s = i % 2
    rows = ybuf.shape[1] // SUBLANES
    half_tiles = wgb.shape[0] // 2 // LANES
    prev = be_ref[jnp.maximum(i - 1, 0)]
    active = i < nv_ref[0]

    def to_dump(slot):
        def body(r, _):
            dest_sm[slot, 0, r] = dump_row + r
            return 0
        lax.fori_loop(0, rows, body, 0)

    def send(slot, r, queue=0):
        _tile_copy(ybuf.at[slot], r, yout_ref, dest_sm[slot, 0, r], sems.at[slot]).start(priority=queue)

    def wait(slot):
        pltpu.make_async_copy(ybuf.at[slot], ybuf.at[slot], sems.at[slot]).wait()

    @pl.when(i == 0)
    def _():
        ybuf[1] = jnp.zeros(ybuf.shape[1:], F32)
        to_dump(1)

    @pl.when(i > 0)
    def _():
        wait(s)

    @pl.when(active & ((i == 0) | (be_ref[i] != prev)))
    def _():
        wgb[...] = wg_ref[0].astype(BF16)
        wub[...] = wu_ref[0].astype(BF16)
        wdb[...] = wd_ref[0].astype(BF16)

    @pl.when(active)
    def _():
        pace = dest_sm[1 - s, 0, 0] >> 31
        for r in range(rows):
            dst = dest_sm[1 - s, 0, r] + pace
            _tile_copy(ybuf.at[1 - s], r, yout_ref, dst, sems.at[1 - s]).start(priority=r % 2)
            pace = dst
            for _ in range(SEND_PACE_SHIFTS):
                pace = pace >> 8
        words = [_tile_rows(x_ref, t, rows)[...] for t in range(half_tiles)]
        lo = [lax.bitcast_convert_type(w << 16, F32) for w in words]
        hi = [lax.bitcast_convert_type(w & jnp.uint32(0xFFFF0000), F32) for w in words]
        x = jnp.concatenate(lo + hi, axis=1).astype(BF16)
        meta = lax.bitcast_convert_type(_tile_rows(x_ref, META_SUBLANE, rows)[...], F32)
        gt = _dot(x, wgb[...])
        up = _dot(x, wub[...])
        hid = (gt * _sigmoid(gt) * up).astype(BF16)
        y = _dot(hid, wdb[...]) * meta[:, META_W:META_W + 1]
        for t in range(SUBLANES):
            _tile_rows(ybuf.at[s], t, rows)[...] = y[:, t * LANES:(t + 1) * LANES]
        row = lax.broadcasted_iota(jnp.int32, (rows, LANES), 0)
        dest = jnp.where(row < bn_ref[i], meta, (dump_row + row).astype(F32))
        _lanes_to_smem(dest, dest_vm, dest_sm.at[s], dsem)

    @pl.when(jnp.logical_not(active))
    def _():
        lax.fori_loop(0, rows, lambda r, _: send(1 - s, r) or 0, 0, unroll=ROW_DMA_UNROLL)
        to_dump(s)

    @pl.when(i == pl.num_programs(0) - 1)
    def _():
        wait(1 - s)


def _experts(blk_e, blk_n, n_valid, xin, w_gate, w_up, w_down, n_out_rows):
    r = xin.shape[0] // SUBLANES
    d, de = w_gate.shape[1], w_gate.shape[2]
    assert d == SUBLANES * LANES, "an output row is one (8, 128) f32 tile"
    rows = EXPERT_ROWS
    last = r // rows - 1
    grid_spec = pltpu.PrefetchScalarGridSpec(
        num_scalar_prefetch=3,
        grid=(r // rows + 1,),
        in_specs=[
            pl.BlockSpec((rows * SUBLANES, LANES), lambda i, be, bn, nv: (jnp.minimum(i, last), 0)),
            pl.BlockSpec((1, d, de), lambda i, be, bn, nv: (be[i], 0, 0)),
            pl.BlockSpec((1, d, de), lambda i, be, bn, nv: (be[i], 0, 0)),
            pl.BlockSpec((1, de, d), lambda i, be, bn, nv: (be[i], 0, 0)),
        ],
        out_specs=pl.BlockSpec(memory_space=pl.ANY),
        scratch_shapes=[pltpu.VMEM((d, de), BF16), pltpu.VMEM((d, de), BF16),
                        pltpu.VMEM((de, d), BF16),
                        pltpu.VMEM((2, rows * SUBLANES, LANES), F32),
                        pltpu.VMEM((8, rows), jnp.int32),
                        pltpu.SMEM((2, 8, rows), jnp.int32),
                        pltpu.SemaphoreType.DMA((2,)),
                        pltpu.SemaphoreType.DMA(())],
    )
    return pl.pallas_call(
        functools.partial(_experts_kernel, dump_row=n_out_rows),
        grid_spec=grid_spec,
        out_shape=jax.ShapeDtypeStruct(((n_out_rows + rows) * SUBLANES, LANES), F32),
        compiler_params=pltpu.CompilerParams(
            dimension_semantics=("arbitrary",), vmem_limit_bytes=VMEM_LIMIT),
        name="experts",
    )(blk_e, blk_n, n_valid, xin, w_gate, w_up, w_down)


def _combine_kernel(h1_ref, y0_ref, y1_ref, p_ref, wpg_ref, wpp_ref,
                    gple_ref, gpost_ref, gfin_ref, out_ref):
    tm = h1_ref.shape[0]
    y = jnp.concatenate([_tile_rows(y0_ref, s, tm)[...] + _tile_rows(y1_ref, s, tm)[...]
                         for s in range(SUBLANES)], axis=1)
    h2 = h1_ref[...] + y
    gate = _sigmoid(_dot(_rms(h2, gple_ref[...]).astype(BF16), wpg_ref[...]))
    ple = _rms(_dot(p_ref[...].astype(BF16), wpp_ref[...]), gpost_ref[...])
    h3 = h2 + gate * ple
    out_ref[...] = _rms(h3, gfin_ref[...])


def _combine(h1, p2, y, w_pg, w_pp, g_ple, g_post, g_final, tm):
    t, d = h1.shape
    pd = p2.shape[1]
    row = lambda i: (i, 0)
    const = lambda i: (0, 0)
    return pl.pallas_call(
        _combine_kernel,
        grid=(t // tm,),
        in_specs=[
            pl.BlockSpec((tm, d), row),
            pl.BlockSpec((tm * SUBLANES, LANES), lambda i: (i, 0)),
            pl.BlockSpec((tm * SUBLANES, LANES), lambda i: (t // tm + i, 0)),
            pl.BlockSpec((tm, pd), row),
            pl.BlockSpec((d, d), const),
            pl.BlockSpec((pd, d), const),
            pl.BlockSpec((1, d), const),
            pl.BlockSpec((1, d), const),
            pl.BlockSpec((1, d), const),
        ],
        out_specs=pl.BlockSpec((tm, d), row),
        out_shape=jax.ShapeDtypeStruct((t, d), F32),
        compiler_params=pltpu.CompilerParams(
            dimension_semantics=("arbitrary",), vmem_limit_bytes=VMEM_LIMIT),
        name="combine",
    )(h1, y, y, p2, w_pg.astype(BF16), w_pp.astype(BF16),
      g_ple.reshape(1, d), g_post.reshape(1, d), g_final.reshape(1, d))


def _largest_tile(n, cap):
    tile = cap
    while n % tile:
        tile //= 2
    return tile


def kernel(x, p, g_mix, w_in, b_gates, conv_q, conv_k, g_mhead, w_out, g_ffn, w_router_group,
           b_router_group, w_router_expert, b_router_expert, w_exp_gate, w_exp_up, w_exp_down,
           g_ple, w_ple_gate, w_ple_proj, g_ple_post, g_final):
    batch, seq_len, d = x.shape
    t = batch * seq_len
    tm = _largest_tile(seq_len, 512)
    n_exp = w_router_expert.shape[-1]
    rows = EXPERT_ROWS
    nblk = t * TOP_K // rows + n_exp

    assert w_in.shape[0] == 1, "single-layer block"
    l = 0
    h = x.reshape(t, d)
    mq, mk, mv, mo, gates, sq, sk, sv = _inproj(
        h, g_mix[l], w_in[l], conv_q[l], conv_k[l], seq_len, tm)
    hm = _mlstm(mq, mk, mv, mo, gates, b_gates[l], g_mhead[l], batch, seq_len,
                _largest_tile(seq_len, 1024))
    hs = _stickbreak(sq, sk, sv, batch, seq_len)
    h1, c, route, counts = _outroute(
        h, hm, hs, w_out[l], g_ffn[l], w_router_group[l], b_router_group[l],
        w_router_expert[l], b_router_expert[l], tm)

    cnt = counts[0, LOGIT_LANE_E:LOGIT_LANE_E + n_exp].astype(jnp.int32)
    nb_e = (cnt + rows - 1) // rows
    cum = jnp.cumsum(nb_e)
    offs_row = jnp.zeros((1, LANES), F32).at[0, 0:n_exp].set(((cum - nb_e) * rows).astype(F32))
    n_valid = cum[-1:]
    step = jnp.arange(nblk + 1, dtype=jnp.int32)
    blk_e = jnp.minimum(jnp.sum(cum[None, :] <= step[:, None], axis=1), n_exp - 1).astype(jnp.int32)
    mine = blk_e[:, None] == jnp.arange(n_exp, dtype=jnp.int32)[None, :]
    first = jnp.sum(jnp.where(mine, (cum - nb_e)[None, :], 0), axis=1)
    blk_n = jnp.clip(jnp.sum(jnp.where(mine, cnt[None, :], 0), axis=1) - rows * (step - first), 0, rows)
    pad_lo = (cum - nb_e) * rows + cnt
    pad_hi = cum * rows

    pos = _slotpos(route, offs_row, _largest_tile(t, 2048))
    xin = _dispatch(pad_lo, pad_hi, n_valid, pos, c, route, nblk, _largest_tile(seq_len, 256))
    y = _experts(blk_e, blk_n, n_valid, xin, w_exp_gate[l], w_exp_up[l], w_exp_down[l], TOP_K * t)
    out = _combine(h1, p[l].reshape(t, -1), y, w_ple_gate[l], w_ple_proj[l],
                   g_ple[l], g_ple_post[l], g_final, tm)
    return out.reshape(batch, seq_len, d)
```

```python
import functools

import jax
import jax.numpy as jnp
from jax import lax
from jax.experimental import pallas as pl
from jax.experimental.pallas import tpu as pltpu

F32 = jnp.float32
BF16 = jnp.bfloat16
EPS = 1e-6

M_HEADS = 4
M_HEAD_DIM = 128
SB_HEAD_DIM = 64
CONV_WIDTH = 4
TOP_K = 2
LANES = 128
VMEM_LIMIT = 56 * 1024 * 1024

MLSTM_CHUNK = 128
SB_BLOCK = 256
SB_ZERO_LOG = -105.0
EXPERT_ROWS = 256
ROW_DMA_UNROLL = 8
SEG_CHUNK = 8
ROUTE_LANE_E = 0
ROUTE_LANE_W = 2
ROUTE_LANE_R = 4
SUBLANES = 8
META_SUBLANE = 4
META_DEST = 0
META_W = 1
LOGIT_LANE_E = 4


def _rms(x, g):
    return x * lax.rsqrt(jnp.mean(x * x, axis=-1, keepdims=True) + EPS) * g


def _sigmoid(x):
    return 1.0 / (1.0 + jnp.exp(-x))


def _split3(a):
    a1 = a.astype(BF16)
    r1 = a - a1.astype(F32)
    a2 = r1.astype(BF16)
    a3 = (r1 - a2.astype(F32)).astype(BF16)
    return a1, a2, a3


def _dot(a, b):
    return jnp.dot(a, b, preferred_element_type=F32)


def _dot_nt(a, b):
    return lax.dot_general(a, b, (((1,), (1,)), ((), ())), preferred_element_type=F32)


def _dot_tn(a, b):
    return lax.dot_general(a, b, (((0,), (0,)), ((), ())), preferred_element_type=F32)


def _inproj_kernel(x_ref, g_ref, wqk_ref, wvo_ref, wg_ref, ws_ref, cq_ref, ck_ref,
                   mq_ref, mk_ref, mv_ref, mo_ref, gate_ref, sq_ref, sk_ref, sv_ref,
                   ext_ref, *, tiles_per_seq, k_scale):
    i = pl.program_id(0)
    tm = x_ref.shape[0]
    mw = mq_ref.shape[1]
    sw = sq_ref.shape[1]
    a = _rms(x_ref[...], g_ref[...]).astype(BF16)

    @pl.when(i % tiles_per_seq == 0)
    def _():
        ext_ref[0:8, :] = jnp.zeros((8, 2 * mw), F32)

    ext_ref[8:8 + tm, 0:mw] = _dot(a, wqk_ref[:, 0:mw])
    ext_ref[8:8 + tm, mw:2 * mw] = _dot(a, wqk_ref[:, mw:2 * mw])

    def conv_silu(w_ref, c0):
        acc = ext_ref[pl.ds(8 - (CONV_WIDTH - 1), tm), c0:c0 + mw] * w_ref[0:1, :]
        for j in range(1, CONV_WIDTH):
            acc = acc + ext_ref[pl.ds(8 - (CONV_WIDTH - 1) + j, tm), c0:c0 + mw] * w_ref[j:j + 1, :]
        return acc * _sigmoid(acc)

    mq_ref[...] = conv_silu(cq_ref, 0).astype(BF16)
    mk_ref[...] = (conv_silu(ck_ref, mw) * k_scale).astype(BF16)
    ext_ref[0:8, :] = ext_ref[tm:tm + 8, :]

    mv_ref[...] = _dot(a, wvo_ref[:, 0:mw]).astype(BF16)
    mo_ref[...] = _dot(a, wvo_ref[:, mw:2 * mw]).astype(BF16)
    gate_ref[...] = _dot(a, wg_ref[...])
    sq_ref[...] = _dot(a, ws_ref[:, 0:sw]).astype(BF16)
    sk_ref[...] = _dot(a, ws_ref[:, sw:2 * sw]).astype(BF16)
    sv_ref[...] = _dot(a, ws_ref[:, 2 * sw:3 * sw]).astype(BF16)


def _inproj(x2, g_mix, w_in, conv_q, conv_k, seq_len, tm):
    t, d = x2.shape
    mw = conv_q.shape[1]
    h = M_HEADS
    sw = (w_in.shape[1] - 4 * mw - 2 * h) // 3
    wqk = w_in[:, 0:2 * mw].astype(BF16)
    wvo = w_in[:, 2 * mw:4 * mw].astype(BF16)
    wg = jnp.zeros((d, 2 * LANES), F32)
    wg = wg.at[:, 0:h].set(w_in[:, 4 * mw:4 * mw + h])
    wg = wg.at[:, LANES:LANES + h].set(w_in[:, 4 * mw + h:4 * mw + 2 * h]).astype(BF16)
    ws = w_in[:, 4 * mw + 2 * h:]
    ws = jnp.concatenate([ws[:, 0:sw] * (SB_HEAD_DIM ** -0.5), ws[:, sw:]], axis=1).astype(BF16)
    row = lambda i: (i, 0)
    const = lambda i: (0, 0)
    kern = functools.partial(_inproj_kernel, tiles_per_seq=seq_len // tm, k_scale=M_HEAD_DIM ** -0.5)
    bf = lambda w: jax.ShapeDtypeStruct((t, w), BF16)
    return pl.pallas_call(
        kern,
        grid=(t // tm,),
        in_specs=[
            pl.BlockSpec((tm, d), row),
            pl.BlockSpec((1, d), const),
            pl.BlockSpec((d, 2 * mw), const),
            pl.BlockSpec((d, 2 * mw), const),
            pl.BlockSpec((d, 2 * LANES), const),
            pl.BlockSpec((d, 3 * sw), const),
            pl.BlockSpec((CONV_WIDTH, mw), const),
            pl.BlockSpec((CONV_WIDTH, mw), const),
        ],
        out_specs=[
            pl.BlockSpec((tm, mw), row), pl.BlockSpec((tm, mw), row),
            pl.BlockSpec((tm, mw), row), pl.BlockSpec((tm, mw), row),
            pl.BlockSpec((tm, 2 * LANES), row),
            pl.BlockSpec((tm, sw), row), pl.BlockSpec((tm, sw), row), pl.BlockSpec((tm, sw), row),
        ],
        out_shape=[bf(mw), bf(mw), bf(mw), bf(mw),
                   jax.ShapeDtypeStruct((t, 2 * LANES), F32), bf(sw), bf(sw), bf(sw)],
        scratch_shapes=[pltpu.VMEM((tm + 8, 2 * mw), F32)],
        compiler_params=pltpu.CompilerParams(
            dimension_semantics=("arbitrary",), vmem_limit_bytes=VMEM_LIMIT),
        name="inproj",
    )(x2, g_mix.reshape(1, d), wqk, wvo, wg, ws, conv_q, conv_k)


def _mlstm_kernel(q_ref, k_ref, v_ref, o_ref, gate_ref, bias_ref, gh_ref, out_ref,
                  c_ref, m_ref, *, chunk):
    L = chunk
    hd = M_HEAD_DIM
    nchunks = q_ref.shape[0] // L

    @pl.when(pl.program_id(1) == 0)
    def _():
        c_ref[...] = jnp.zeros(c_ref.shape, F32)
        m_ref[...] = jnp.zeros(m_ref.shape, F32)

    rows = lax.broadcasted_iota(jnp.int32, (L, L), 0)
    cols = lax.broadcasted_iota(jnp.int32, (L, L), 1)
    causal = cols <= rows
    tri = causal.astype(BF16)
    lane2 = lax.broadcasted_iota(jnp.int32, (L, 2 * hd), 1)
    ones_col = (lane2 == hd).astype(F32)

    def chunk_body(c, _):
        r0 = pl.multiple_of(c * L, L)
        g = gate_ref[pl.ds(r0, L), :] + bias_ref[...]
        gi = g[:, 0:LANES]
        gf = g[:, LANES:2 * LANES]
        lf = jnp.minimum(gf, 0.0) - jnp.log(1.0 + jnp.exp(-jnp.abs(gf)))
        l1, l2, l3 = _split3(lf)
        b = _dot(tri, l1) + _dot(tri, l2) + _dot(tri, l3)
        b_last = b[L - 1:L, :]
        w_end = b_last - b + gi
        m_loc = jnp.max(w_end, axis=0, keepdims=True)
        e_end = jnp.exp(w_end - m_loc)
        m_prev = m_ref[...]
        m_new = jnp.maximum(b_last + m_prev, m_loc)
        decay = jnp.exp(b_last + m_prev - m_new)
        scale = jnp.exp(m_loc - m_new)
        b_t = b.T
        gi_t = gi.T
        for h in range(M_HEADS):
            hs = slice(h * hd, (h + 1) * hd)
            qh = q_ref[pl.ds(r0, L), hs]
            kh = k_ref[pl.ds(r0, L), hs]
            vh = v_ref[pl.ds(r0, L), hs].astype(F32)
            vext = jnp.concatenate([vh, jnp.zeros((L, hd), F32)], axis=1) + ones_col
            bc = b[:, h:h + 1]
            e = jnp.where(causal, bc - b_t[h:h + 1, :] + gi_t[h:h + 1, :], -jnp.inf)
            log_inter = bc + m_prev[:, h:h + 1]
            m_t = jnp.maximum(log_inter, jnp.max(e, axis=1, keepdims=True))
            w = (jnp.exp(e - m_t) * _dot_nt(qh, kh)).astype(BF16)
            a_int = jnp.exp(log_inter - m_t)
            cext = c_ref[h]
            num = _dot(w, vext.astype(BF16)) + a_int * _dot(qh, cext.astype(BF16))
            den = num[:, hd:hd + 1]
            hh = num[:, 0:hd] / jnp.maximum(jnp.abs(den), jnp.exp(-m_t))
            hh = _rms(hh, gh_ref[:, hs])
            og = _sigmoid(o_ref[pl.ds(r0, L), hs].astype(F32))
            out_ref[pl.ds(r0, L), hs] = (og * hh).astype(BF16)
            ev = (e_end[:, h:h + 1] * vext).astype(BF16)
            c_ref[h] = decay[:, h:h + 1] * cext + scale[:, h:h + 1] * _dot_tn(kh, ev)
        m_ref[...] = m_new
        return 0

    lax.fori_loop(0, nchunks, chunk_body, 0)


def _mlstm(mq, mk, mv, mo, gates, b_gates, g_mhead, batch, seq_len, rows):
    t, mw = mq.shape
    h = M_HEADS
    bias = jnp.zeros((1, 2 * LANES), F32)
    bias = bias.at[0, 0:h].set(b_gates[0:h]).at[0, LANES:LANES + h].set(b_gates[h:2 * h])
    nb = seq_len // rows
    row = lambda b, i: (b * nb + i, 0)
    const = lambda b, i: (0, 0)
    return pl.pallas_call(
        functools.partial(_mlstm_kernel, chunk=MLSTM_CHUNK),
        grid=(batch, nb),
        in_specs=[pl.BlockSpec((rows, mw), row)] * 4 + [
            pl.BlockSpec((rows, 2 * LANES), row),
            pl.BlockSpec((1, 2 * LANES), const),
            pl.BlockSpec((1, mw), const),
        ],
        out_specs=pl.BlockSpec((rows, mw), row),
        out_shape=jax.ShapeDtypeStruct((t, mw), BF16),
        scratch_shapes=[pltpu.VMEM((h, M_HEAD_DIM, 2 * M_HEAD_DIM), F32),
                        pltpu.VMEM((1, LANES), F32)],
        compiler_params=pltpu.CompilerParams(
            dimension_semantics=("arbitrary", "arbitrary"), vmem_limit_bytes=VMEM_LIMIT),
        name="mlstm",
    )(mq, mk, mv, mo, gates, bias, g_mhead.reshape(1, mw))


def _sb_kernel(q_ref, k_ref, v_ref, out_ref, acc_ref, carry_ref):
    lax.fori_loop(0, q_ref.shape[0] // SB_BLOCK,
                  functools.partial(_sb_query_block, q_ref, k_ref, v_ref, out_ref, acc_ref, carry_ref),
                  0)


def _sb_query_block(q_ref, k_ref, v_ref, out_ref, acc_ref, carry_ref, qi, _):
    blk = SB_BLOCK
    q0 = pl.multiple_of(qi * blk, blk)
    lane = lax.broadcasted_iota(jnp.int32, (blk, LANES), 1)
    head0 = lane < SB_HEAD_DIM
    q = q_ref[pl.ds(q0, blk), :]
    zero = jnp.zeros_like(q)
    qm = (jnp.where(head0, q, zero), jnp.where(head0, zero, q))
    rows = lax.broadcasted_iota(jnp.int32, (blk, blk), 0)
    cols = lax.broadcasted_iota(jnp.int32, (blk, blk), 1)
    strict = cols < rows
    neg_suffix = jnp.where(rows >= cols, -1.0, 0.0).astype(BF16)

    def block(j, carries, mask=None):
        k0 = pl.multiple_of(j * blk, blk)
        kb = k_ref[pl.ds(k0, blk), :]
        vb = v_ref[pl.ds(k0, blk), :]
        vz = jnp.zeros_like(vb)
        vm = (jnp.where(head0, vb, vz), jnp.where(head0, vz, vb))
        upd = None
        new = []
        for h in range(2):
            z = _dot_nt(qm[h], kb)
            sp = jnp.maximum(z, 0.0) + jnp.log(1.0 + jnp.exp(-jnp.abs(z)))
            if mask is not None:
                sp = jnp.where(mask, sp, 0.0)
            rc = _dot(sp.astype(BF16), neg_suffix)
            p = jnp.exp(z + rc + carries[h])
            if mask is not None:
                p = jnp.where(mask, p, 0.0)
            new.append(carries[h] + rc[:, 0:1])
            d = _dot(p.astype(BF16), vm[h])
            upd = d if upd is None else upd + d
        return upd, new

    zeros = jnp.zeros((blk, 1), F32)
    u_diag, carries = block(qi, [zeros, zeros], strict)
    has_prev = jnp.broadcast_to(qi > 0, (blk, blk))
    u_prev, carries = block(jnp.maximum(qi - 1, 0), carries, has_prev)
    acc_ref[...] = u_diag + u_prev
    carry_ref[0] = carries[0]
    carry_ref[1] = carries[1]

    def cond(state):
        it, top = state
        return (it < qi) & (top > SB_ZERO_LOG)

    def body(state):
        it, _ = state
        upd, new = block(qi - 1 - it, [carry_ref[0], carry_ref[1]])
        acc_ref[...] += upd
        carry_ref[0] = new[0]
        carry_ref[1] = new[1]
        return it + 1, jnp.maximum(jnp.max(new[0]), jnp.max(new[1]))

    lax.while_loop(cond, body, (jnp.int32(1), jnp.maximum(jnp.max(carries[0]), jnp.max(carries[1]))))
    out_ref[pl.ds(q0, blk), :] = acc_ref[...].astype(BF16)
    return 0


def _stickbreak(sq, sk, sv, batch, seq_len):
    t, sw = sq.shape
    npair = sw // LANES
    seq = pl.BlockSpec((seq_len, LANES), lambda b, hp: (b, hp))
    return pl.pallas_call(
        _sb_kernel,
        grid=(batch, npair),
        in_specs=[seq, seq, seq],
        out_specs=seq,
        out_shape=jax.ShapeDtypeStruct((t, sw), BF16),
        scratch_shapes=[pltpu.VMEM((SB_BLOCK, LANES), F32), pltpu.VMEM((2, SB_BLOCK, 1), F32)],
        compiler_params=pltpu.CompilerParams(
            dimension_semantics=("arbitrary", "arbitrary"), vmem_limit_bytes=VMEM_LIMIT),
        name="stickbrk",
    )(sq, sk, sv)


def _outroute_kernel(x_ref, hm_ref, hs_ref, wom_ref, wos_ref, g_ref, wr_ref, br_ref,
                     h1_ref, c_ref, route_ref, cnt_ref, tcnt_ref, run_ref, *, n_groups, per_group):
    i = pl.program_id(0)
    tm = x_ref.shape[0]

    @pl.when(i == 0)
    def _():
        run_ref[...] = jnp.zeros(run_ref.shape, F32)

    h1 = x_ref[...] + _dot(hm_ref[...], wom_ref[...]) + _dot(hs_ref[...], wos_ref[...])
    h1_ref[...] = h1
    c = _rms(h1, g_ref[...])
    c_ref[...] = c

    c1, c2, _ = _split3(c)
    pa = _dot(c1, wr_ref[...])
    pb = _dot(c2, wr_ref[...])
    logits = (pa[:, 0:LANES] + (pa[:, LANES:] + pb[:, 0:LANES]) + pb[:, LANES:]) + br_ref[...]

    lane = lax.broadcasted_iota(jnp.int32, (tm, LANES), 1).astype(F32)
    ninf = -jnp.inf
    big = float(LANES)

    def first_max(v):
        mx = jnp.max(v, axis=1, keepdims=True)
        idx = jnp.min(jnp.where(v == mx, lane, big), axis=1, keepdims=True)
        return mx, idx

    gl = jnp.where(lane < n_groups, logits, ninf)
    gmax, gsel = first_max(gl)
    p_g = 1.0 / jnp.sum(jnp.exp(gl - gmax), axis=1, keepdims=True)
    lo = LOGIT_LANE_E + per_group * gsel
    el = jnp.where((lane >= lo) & (lane < lo + per_group), logits, ninf)
    v1, i1 = first_max(el)
    v2, i2 = first_max(jnp.where(lane == i1, ninf, el))
    tt = jnp.exp(v2 - v1)
    w0 = p_g / (1.0 + tt)
    w1_ = p_g * tt / (1.0 + tt)

    oh0 = lane == i1
    oh1 = lane == i2
    ohsum = oh0.astype(F32) + oh1.astype(F32)
    rows = lax.broadcasted_iota(jnp.int32, (tm, tm), 0)
    cols = lax.broadcasted_iota(jnp.int32, (tm, tm), 1)
    before = (cols < rows).astype(BF16)
    prefix = _dot(before, ohsum.astype(BF16)) + run_ref[...]
    r0 = jnp.sum(jnp.where(oh0, prefix, 0.0), axis=1, keepdims=True)
    r1 = jnp.sum(jnp.where(oh1, prefix, 0.0), axis=1, keepdims=True)
    run0 = run_ref[...]
    run = run0 + jnp.sum(ohsum, axis=0, keepdims=True)
    run_ref[...] = run
    cnt_ref[...] = jnp.broadcast_to(run, cnt_ref.shape)
    mid = prefix[tm // 2:tm // 2 + 1, :]
    sub = lax.broadcasted_iota(jnp.int32, (SUBLANES, LANES), 0)
    for k, (before, after) in enumerate(((run0, mid), (mid, run))):
        tcnt_ref[k] = jnp.where(sub == 0, before, jnp.where(sub == 1, after - before, 0.0))

    e0 = i1 - LOGIT_LANE_E
    e1 = i2 - LOGIT_LANE_E
    route = jnp.zeros((tm, LANES), F32)
    for ln, val in ((ROUTE_LANE_E, e0), (ROUTE_LANE_E + 1, e1), (ROUTE_LANE_W, w0),
                    (ROUTE_LANE_W + 1, w1_), (ROUTE_LANE_R, r0), (ROUTE_LANE_R + 1, r1)):
        route = jnp.where(lane == ln, val, route)
    route_ref[...] = route


def _tile_rows(ref, s, n):
    return ref.at[pl.ds(s, n, stride=SUBLANES), :]


def _tile_copy(src_ref, src_row, dst_ref, dst_row, sem):
    return pltpu.make_async_copy(
        src_ref.at[pl.ds(pl.multiple_of(src_row * SUBLANES, SUBLANES), SUBLANES), :],
        dst_ref.at[pl.ds(pl.multiple_of(dst_row * SUBLANES, SUBLANES), SUBLANES), :], sem)


def _lanes_to_smem(vals, vm_ref, sm_ref, sem):
    vm_ref[...] = vals.T[0:8, :].astype(jnp.int32)
    cp = pltpu.make_async_copy(vm_ref, sm_ref, sem)
    cp.start()
    cp.wait()


def _outroute(x2, hm, hs, w_out, g_ffn, w_rg, b_rg, w_re, b_re, tm):
    t, d = x2.shape
    mw = hm.shape[1]
    sw = hs.shape[1]
    n_groups = w_rg.shape[1]
    n_exp = w_re.shape[1]
    wr = jnp.zeros((d, LANES), F32)
    wr = wr.at[:, 0:n_groups].set(w_rg).at[:, LOGIT_LANE_E:LOGIT_LANE_E + n_exp].set(w_re)
    wr_hi, wr_lo, _ = _split3(wr)
    wr2 = jnp.concatenate([wr_hi, wr_lo], axis=1)
    br = jnp.zeros((1, LANES), F32)
    br = br.at[0, 0:n_groups].set(b_rg).at[0, LOGIT_LANE_E:LOGIT_LANE_E + n_exp].set(b_re)
    row = lambda i: (i, 0)
    const = lambda i: (0, 0)
    kern = functools.partial(_outroute_kernel, n_groups=n_groups, per_group=n_exp // n_groups)
    return pl.pallas_call(
        kern,
        grid=(t // tm,),
        in_specs=[
            pl.BlockSpec((tm, d), row),
            pl.BlockSpec((tm, mw), row),
            pl.BlockSpec((tm, sw), row),
            pl.BlockSpec((mw, d), const),
            pl.BlockSpec((sw, d), const),
            pl.BlockSpec((1, d), const),
            pl.BlockSpec((d, 2 * LANES), const),
            pl.BlockSpec((1, LANES), const),
        ],
        out_specs=[
            pl.BlockSpec((tm, d), row),
            pl.BlockSpec((tm, d), row),
            pl.BlockSpec((tm, LANES), row),
            pl.BlockSpec((8, LANES), const),
            pl.BlockSpec((2, SUBLANES, LANES), lambda i: (i, 0, 0)),
        ],
        out_shape=[
            jax.ShapeDtypeStruct((t, d), F32),
            jax.ShapeDtypeStruct((t, d), F32),
            jax.ShapeDtypeStruct((t, LANES), F32),
            jax.ShapeDtypeStruct((8, LANES), F32),
            jax.ShapeDtypeStruct((2 * (t // tm), SUBLANES, LANES), F32),
        ],
        scratch_shapes=[pltpu.VMEM((1, LANES), F32)],
        compiler_params=pltpu.CompilerParams(
            dimension_semantics=("arbitrary",), vmem_limit_bytes=VMEM_LIMIT),
        name="outroute",
    )(x2, hm, hs, w_out[0:mw].astype(BF16), w_out[mw:].astype(BF16), g_ffn.reshape(1, d),
      wr2, br)


def _slotpos_kernel(route_ref, offs_ref, pos_ref):
    route = route_ref[...]
    tm = route.shape[0]
    lane = lax.broadcasted_iota(jnp.int32, (tm, LANES), 1)
    offs = offs_ref[...]
    out = jnp.zeros((tm, LANES), F32)
    for j in range(TOP_K):
        e = route[:, ROUTE_LANE_E + j:ROUTE_LANE_E + j + 1].astype(jnp.int32)
        base = jnp.sum(jnp.where(lane == e, offs, 0.0), axis=1, keepdims=True)
        out = jnp.where(lane == j, base + route[:, ROUTE_LANE_R + j:ROUTE_LANE_R + j + 1], out)
    pos_ref[...] = out.T[0:SUBLANES, :].astype(jnp.int32)


def _slotpos(route, offs_row, tm):
    t = route.shape[0]
    return pl.pallas_call(
        _slotpos_kernel,
        grid=(t // tm,),
        in_specs=[pl.BlockSpec((tm, LANES), lambda i: (i, 0)),
                  pl.BlockSpec((1, LANES), lambda i: (0, 0))],
        out_specs=pl.BlockSpec((SUBLANES, tm), lambda i: (0, i)),
        out_shape=jax.ShapeDtypeStruct((SUBLANES, t), jnp.int32),
        compiler_params=pltpu.CompilerParams(dimension_semantics=("arbitrary",)),
        name="slotpos",
    )(route, offs_row)


def _dispatch_kernel(zblk_ref, pos_ref, c_ref, route_ref, xin_ref, rows_ref, zero_ref, sem, zsem,
                     *, n_tokens):
    i = pl.program_id(0)
    tm, d = c_ref.shape

    @pl.when(i == 0)
    def _():
        zero_ref[...] = jnp.zeros(zero_ref.shape, jnp.uint32)
        n = zero_ref.shape[0]

        def zero_block(k):
            return pltpu.make_async_copy(
                zero_ref, xin_ref.at[pl.ds(pl.multiple_of(zblk_ref[k] * n, n), n), :], zsem)

        for k in range(zblk_ref.shape[0]):
            zero_block(k).start()
            zero_block(k).wait()

    half = d // 2
    lane = lax.broadcasted_iota(jnp.int32, (tm, LANES), 1)
    route = route_ref[...]
    bits = lax.bitcast_convert_type(c_ref[...].astype(BF16).astype(F32), jnp.uint32)
    packed = (bits[:, 0:half] >> 16) | bits[:, half:d]
    for j in range(TOP_K):
        w = route[:, ROUTE_LANE_W + j:ROUTE_LANE_W + j + 1]
        meta = jnp.where(lane == META_W, w, 0.0)
        for s in range(SUBLANES):
            if s < half // LANES:
                sub = packed[:, s * LANES:(s + 1) * LANES]
            elif s == META_SUBLANE:
                sub = lax.bitcast_convert_type(meta, jnp.uint32)
            else:
                sub = jnp.zeros((tm, LANES), jnp.uint32)
            _tile_rows(rows_ref.at[j], s, tm)[...] = sub

    def issue(t, _):
        for j in range(TOP_K):
            _tile_copy(rows_ref.at[j], t, xin_ref, pos_ref[j, t], sem).start(priority=j)
        return 0

    lax.fori_loop(0, tm, issue, 0, unroll=ROW_DMA_UNROLL)
    for j in range(TOP_K):
        pltpu.make_async_copy(rows_ref.at[j], rows_ref.at[j], sem).wait()


def _dispatch(zero_blocks, pos, c, route, n_blocks, tm):
    t, d = c.shape
    assert d // 2 // LANES <= META_SUBLANE < SUBLANES
    rows = EXPERT_ROWS
    grid_spec = pltpu.PrefetchScalarGridSpec(
        num_scalar_prefetch=1,
        grid=(t // tm,),
        in_specs=[
            pl.BlockSpec((SUBLANES, tm), lambda i, *_: (0, i), memory_space=pltpu.SMEM),
            pl.BlockSpec((tm, d), lambda i, *_: (i, 0)),
            pl.BlockSpec((tm, LANES), lambda i, *_: (i, 0)),
        ],
        out_specs=pl.BlockSpec(memory_space=pl.ANY),
        scratch_shapes=[pltpu.VMEM((TOP_K, tm * SUBLANES, LANES), jnp.uint32),
                        pltpu.VMEM((rows * SUBLANES, LANES), jnp.uint32),
                        pltpu.SemaphoreType.DMA(()),
                        pltpu.SemaphoreType.DMA(())],
    )
    return pl.pallas_call(
        functools.partial(_dispatch_kernel, n_tokens=t),
        grid_spec=grid_spec,
        out_shape=jax.ShapeDtypeStruct((n_blocks * rows * SUBLANES, LANES), jnp.uint32),
        compiler_params=pltpu.CompilerParams(
            dimension_semantics=("arbitrary",), vmem_limit_bytes=VMEM_LIMIT),
        name="dispatch",
    )(zero_blocks, pos, c, route)


def _experts_kernel(be_ref, nv_ref, x_ref, wg_ref, wu_ref, wd_ref, y_ref, wgb, wub, wdb):
    i = pl.program_id(0)
    rows = y_ref.shape[0] // SUBLANES
    half_tiles = wgb.shape[0] // 2 // LANES
    prev = be_ref[jnp.maximum(i - 1, 0)]
    active = i < nv_ref[0]

    @pl.when(active & ((i == 0) | (be_ref[i] != prev)))
    def _():
        wgb[...] = wg_ref[0].astype(BF16)
        wub[...] = wu_ref[0].astype(BF16)
        wdb[...] = wd_ref[0].astype(BF16)

    @pl.when(active)
    def _():
        words = [_tile_rows(x_ref, t, rows)[...] for t in range(half_tiles)]
        lo = [lax.bitcast_convert_type(w << 16, F32) for w in words]
        hi = [lax.bitcast_convert_type(w & jnp.uint32(0xFFFF0000), F32) for w in words]
        x = jnp.concatenate(lo + hi, axis=1).astype(BF16)
        meta = lax.bitcast_convert_type(_tile_rows(x_ref, META_SUBLANE, rows)[...], F32)
        gt = _dot(x, wgb[...])
        up = _dot(x, wub[...])
        hid = (gt * _sigmoid(gt) * up).astype(BF16)
        y = _dot(hid, wdb[...]) * meta[:, META_W:META_W + 1]
        for t in range(SUBLANES):
            _tile_rows(y_ref, t, rows)[...] = y[:, t * LANES:(t + 1) * LANES]

    @pl.when(jnp.logical_not(active))
    def _():
        y_ref[...] = jnp.zeros(y_ref.shape, F32)


def _experts(blk_e, n_valid, xin, w_gate, w_up, w_down):
    r = xin.shape[0] // SUBLANES
    d, de = w_gate.shape[1], w_gate.shape[2]
    assert d == SUBLANES * LANES, "an output row is one (8, 128) f32 tile"
    rows = EXPERT_ROWS
    blk = pl.BlockSpec((rows * SUBLANES, LANES), lambda i, be, nv: (i, 0))
    grid_spec = pltpu.PrefetchScalarGridSpec(
        num_scalar_prefetch=2,
        grid=(r // rows,),
        in_specs=[
            blk,
            pl.BlockSpec((1, d, de), lambda i, be, nv: (be[i], 0, 0)),
            pl.BlockSpec((1, d, de), lambda i, be, nv: (be[i], 0, 0)),
            pl.BlockSpec((1, de, d), lambda i, be, nv: (be[i], 0, 0)),
        ],
        out_specs=blk,
        scratch_shapes=[pltpu.VMEM((d, de), BF16), pltpu.VMEM((d, de), BF16),
                        pltpu.VMEM((de, d), BF16)],
    )
    return pl.pallas_call(
        _experts_kernel,
        grid_spec=grid_spec,
        out_shape=jax.ShapeDtypeStruct((r * SUBLANES, LANES), F32),
        compiler_params=pltpu.CompilerParams(
            dimension_semantics=("arbitrary",), vmem_limit_bytes=VMEM_LIMIT),
        name="experts",
    )(blk_e, n_valid, xin, w_gate, w_up, w_down)


def _combine_kernel(src_ref, dst_ref, n_ref, shift_ref, h1_ref, route_ref, p_ref, y_ref,
                    wpg_ref, wpp_ref, gple_ref, gpost_ref, gfin_ref, out_ref, ybuf, sem, *, n_exp):
    i = pl.program_id(0)
    tm = h1_ref.shape[0]
    buf_rows = ybuf.shape[0] // SUBLANES
    n = SEG_CHUNK * SUBLANES

    @pl.when(i == 0)
    def _():
        ybuf[...] = jnp.zeros(ybuf.shape, F32)

    def chunk(e, k):
        src = pl.multiple_of((src_ref[0, 0, e] + k * SEG_CHUNK) * SUBLANES, SUBLANES)
        dst = pl.multiple_of((dst_ref[0, 0, e] + k * SEG_CHUNK) * SUBLANES, SUBLANES)
        return pltpu.make_async_copy(y_ref.at[pl.ds(src, n), :], ybuf.at[pl.ds(dst, n), :], sem)

    for e in range(n_exp):
        lax.fori_loop(0, n_ref[0, 0, e], lambda k, _, e=e: chunk(e, k).start() or 0, 0)
    for e in range(n_exp):
        lax.fori_loop(0, n_ref[0, 0, e], lambda k, _, e=e: chunk(e, k).wait() or 0, 0)

    yb = jnp.concatenate([_tile_rows(ybuf, s, buf_rows)[...] for s in range(SUBLANES)],
                         axis=1).astype(BF16)
    route = route_ref[...]
    lane = lax.broadcasted_iota(jnp.int32, (tm, LANES), 1).astype(F32)
    slot = lax.broadcasted_iota(jnp.int32, (tm, buf_rows), 1).astype(F32)
    shift = shift_ref[0, 0:1, :]
    hit = None
    for j in range(TOP_K):
        e = route[:, ROUTE_LANE_E + j:ROUTE_LANE_E + j + 1]
        q = (route[:, ROUTE_LANE_R + j:ROUTE_LANE_R + j + 1]
             + jnp.sum(jnp.where(lane == e, shift, 0.0), axis=1, keepdims=True))
        hit = (slot == q) if hit is None else (hit | (slot == q))
    moe = _dot(jnp.where(hit, 1.0, 0.0).astype(BF16), yb)

    h2 = h1_ref[...] + moe
    gate = _sigmoid(_dot(_rms(h2, gple_ref[...]).astype(BF16), wpg_ref[...]))
    ple = _rms(_dot(p_ref[...].astype(BF16), wpp_ref[...]), gpost_ref[...])
    h3 = h2 + gate * ple
    out_ref[...] = _rms(h3, gfin_ref[...])


def _combine(seg_src, seg_dst, seg_n, shift, h1, route, p2, y, w_pg, w_pp, g_ple, g_post, g_final,
             tm, n_exp):
    t, d = h1.shape
    pd = p2.shape[1]
    row = lambda i: (i, 0)
    const = lambda i: (0, 0)
    seg = pl.BlockSpec((1, 1, LANES), lambda i: (i, 0, 0), memory_space=pltpu.SMEM)
    buf_rows = -(-(TOP_K * tm + n_exp * (SEG_CHUNK - 1)) // LANES) * LANES
    return pl.pallas_call(
        functools.partial(_combine_kernel, n_exp=n_exp),
        grid=(t // tm,),
        in_specs=[
            seg, seg, seg,
            pl.BlockSpec((1, SUBLANES, LANES), lambda i: (i, 0, 0)),
            pl.BlockSpec((tm, d), row),
            pl.BlockSpec((tm, LANES), row),
            pl.BlockSpec((tm, pd), row),
            pl.BlockSpec(memory_space=pl.ANY),
            pl.BlockSpec((d, d), const),
            pl.BlockSpec((pd, d), const),
            pl.BlockSpec((1, d), const),
            pl.BlockSpec((1, d), const),
            pl.BlockSpec((1, d), const),
        ],
        out_specs=pl.BlockSpec((tm, d), row),
        out_shape=jax.ShapeDtypeStruct((t, d), F32),
        scratch_shapes=[pltpu.VMEM((buf_rows * SUBLANES, LANES), F32), pltpu.SemaphoreType.DMA(())],
        compiler_params=pltpu.CompilerParams(
            dimension_semantics=("arbitrary",), vmem_limit_bytes=VMEM_LIMIT),
        name="combine",
    )(seg_src, seg_dst, seg_n, shift, h1, route, p2, y, w_pg.astype(BF16), w_pp.astype(BF16),
      g_ple.reshape(1, d), g_post.reshape(1, d), g_final.reshape(1, d))


def _largest_tile(n, cap):
    tile = cap
    while n % tile:
        tile //= 2
    return tile


def kernel(x, p, g_mix, w_in, b_gates, conv_q, conv_k, g_mhead, w_out, g_ffn, w_router_group,
           b_router_group, w_router_expert, b_router_expert, w_exp_gate, w_exp_up, w_exp_down,
           g_ple, w_ple_gate, w_ple_proj, g_ple_post, g_final):
    batch, seq_len, d = x.shape
    t = batch * seq_len
    tm = _largest_tile(seq_len, 512)
    n_exp = w_router_expert.shape[-1]
    rows = EXPERT_ROWS
    nblk = t * TOP_K // rows + n_exp + 1
    tm_c = tm // 2

    assert w_in.shape[0] == 1, "single-layer block"
    l = 0
    h = x.reshape(t, d)
    mq, mk, mv, mo, gates, sq, sk, sv = _inproj(
        h, g_mix[l], w_in[l], conv_q[l], conv_k[l], seq_len, tm)
    hm = _mlstm(mq, mk, mv, mo, gates, b_gates[l], g_mhead[l], batch, seq_len,
                _largest_tile(seq_len, 1024))
    hs = _stickbreak(sq, sk, sv, batch, seq_len)
    h1, c, route, counts, tcnt = _outroute(
        h, hm, hs, w_out[l], g_ffn[l], w_router_group[l], b_router_group[l],
        w_router_expert[l], b_router_expert[l], tm)

    cnt = counts[0, LOGIT_LANE_E:LOGIT_LANE_E + n_exp].astype(jnp.int32)
    nb_e = (cnt + rows - 1) // rows
    cum = jnp.cumsum(nb_e)
    offs = (cum - nb_e) * rows
    offs_row = jnp.zeros((1, LANES), F32).at[0, 0:n_exp].set(offs.astype(F32))
    n_valid = cum[-1:]
    step = jnp.arange(nblk, dtype=jnp.int32)
    blk_e = jnp.minimum(jnp.sum(cum[None, :] <= step[:, None], axis=1), n_exp - 1).astype(jnp.int32)
    zero_blocks = jnp.concatenate([
        jnp.where(nb_e > 0, cum - 1, nblk - 1),
        jnp.minimum(n_valid + jnp.arange(n_exp + 1, dtype=jnp.int32), nblk - 1)]).astype(jnp.int32)
    before = tcnt[:, 0, LOGIT_LANE_E:LOGIT_LANE_E + n_exp].astype(jnp.int32)
    n_run = (tcnt[:, 1, LOGIT_LANE_E:LOGIT_LANE_E + n_exp].astype(jnp.int32) + SEG_CHUNK - 1) // SEG_CHUNK
    land = (jnp.cumsum(n_run, axis=1) - n_run) * SEG_CHUNK
    lanes = lambda a: jnp.zeros((a.shape[0], 1, LANES), a.dtype).at[:, 0, 0:n_exp].set(a)
    shift = jnp.zeros((before.shape[0], SUBLANES, LANES), F32).at[:, 0, 0:n_exp].set(
        (land - before).astype(F32))

    pos = _slotpos(route, offs_row, _largest_tile(t, 2048))
    xin = _dispatch(zero_blocks, pos, c, route, nblk, tm_c)
    y = _experts(blk_e, n_valid, xin, w_exp_gate[l], w_exp_up[l], w_exp_down[l])
    out = _combine(lanes(offs[None, :] + before), lanes(land), lanes(n_run), shift, h1, route,
                   p[l].reshape(t, -1), y, w_ple_gate[l], w_ple_proj[l], g_ple[l], g_ple_post[l],
                   g_final, tm_c, n_exp)
    return out.reshape(batch, seq_len, d)
```

```python
import functools

import jax
import jax.numpy as jnp
from jax import lax
from jax.experimental import pallas as pl
from jax.experimental.pallas import tpu as pltpu

F32 = jnp.float32
BF16 = jnp.bfloat16
EPS = 1e-6

M_HEADS = 4
M_HEAD_DIM = 128
SB_HEAD_DIM = 64
CONV_WIDTH = 4
TOP_K = 2
LANES = 128
VMEM_LIMIT = 56 * 1024 * 1024

MLSTM_CHUNK = 128
SB_BLOCK = 256
SB_ZERO_LOG = -105.0
EXPERT_ROWS = 256
ROW_DMA_UNROLL = 8
X_SUBLANES = 5
Y_SUBLANES = 4
ROUTE_LANE_E = 0
ROUTE_LANE_W = 2
ROUTE_LANE_R = 4
SUBLANES = 8
META_SUBLANE = 4
META_DEST = 0
META_W = 1
LOGIT_LANE_E = 4


def _rms(x, g):
    return x * lax.rsqrt(jnp.mean(x * x, axis=-1, keepdims=True) + EPS) * g


def _sigmoid(x):
    return 1.0 / (1.0 + jnp.exp(-x))


def _split3(a):
    a1 = a.astype(BF16)
    r1 = a - a1.astype(F32)
    a2 = r1.astype(BF16)
    a3 = (r1 - a2.astype(F32)).astype(BF16)
    return a1, a2, a3


def _dot(a, b):
    return jnp.dot(a, b, preferred_element_type=F32)


def _dot_nt(a, b):
    return lax.dot_general(a, b, (((1,), (1,)), ((), ())), preferred_element_type=F32)


def _dot_tn(a, b):
    return lax.dot_general(a, b, (((0,), (0,)), ((), ())), preferred_element_type=F32)


def _inproj_kernel(x_ref, g_ref, wqk_ref, wvo_ref, wg_ref, ws_ref, cq_ref, ck_ref,
                   mq_ref, mk_ref, mv_ref, mo_ref, gate_ref, sq_ref, sk_ref, sv_ref,
                   ext_ref, *, tiles_per_seq, k_scale):
    i = pl.program_id(0)
    tm = x_ref.shape[0]
    mw = mq_ref.shape[1]
    sw = sq_ref.shape[1]
    a = _rms(x_ref[...], g_ref[...]).astype(BF16)

    @pl.when(i % tiles_per_seq == 0)
    def _():
        ext_ref[0:8, :] = jnp.zeros((8, 2 * mw), F32)

    ext_ref[8:8 + tm, 0:mw] = _dot(a, wqk_ref[:, 0:mw])
    ext_ref[8:8 + tm, mw:2 * mw] = _dot(a, wqk_ref[:, mw:2 * mw])

    def conv_silu(w_ref, c0):
        acc = ext_ref[pl.ds(8 - (CONV_WIDTH - 1), tm), c0:c0 + mw] * w_ref[0:1, :]
        for j in range(1, CONV_WIDTH):
            acc = acc + ext_ref[pl.ds(8 - (CONV_WIDTH - 1) + j, tm), c0:c0 + mw] * w_ref[j:j + 1, :]
        return acc * _sigmoid(acc)

    mq_ref[...] = conv_silu(cq_ref, 0).astype(BF16)
    mk_ref[...] = (conv_silu(ck_ref, mw) * k_scale).astype(BF16)
    ext_ref[0:8, :] = ext_ref[tm:tm + 8, :]

    mv_ref[...] = _dot(a, wvo_ref[:, 0:mw]).astype(BF16)
    mo_ref[...] = _dot(a, wvo_ref[:, mw:2 * mw]).astype(BF16)
    gate_ref[...] = _dot(a, wg_ref[...])
    sq_ref[...] = _dot(a, ws_ref[:, 0:sw]).astype(BF16)
    sk_ref[...] = _dot(a, ws_ref[:, sw:2 * sw]).astype(BF16)
    sv_ref[...] = _dot(a, ws_ref[:, 2 * sw:3 * sw]).astype(BF16)


def _inproj(x2, g_mix, w_in, conv_q, conv_k, seq_len, tm):
    t, d = x2.shape
    mw = conv_q.shape[1]
    h = M_HEADS
    sw = (w_in.shape[1] - 4 * mw - 2 * h) // 3
    wqk = w_in[:, 0:2 * mw].astype(BF16)
    wvo = w_in[:, 2 * mw:4 * mw].astype(BF16)
    wg = jnp.zeros((d, 2 * LANES), F32)
    wg = wg.at[:, 0:h].set(w_in[:, 4 * mw:4 * mw + h])
    wg = wg.at[:, LANES:LANES + h].set(w_in[:, 4 * mw + h:4 * mw + 2 * h]).astype(BF16)
    ws = w_in[:, 4 * mw + 2 * h:]
    ws = jnp.concatenate([ws[:, 0:sw] * (SB_HEAD_DIM ** -0.5), ws[:, sw:]], axis=1).astype(BF16)
    row = lambda i: (i, 0)
    const = lambda i: (0, 0)
    kern = functools.partial(_inproj_kernel, tiles_per_seq=seq_len // tm, k_scale=M_HEAD_DIM ** -0.5)
    bf = lambda w: jax.ShapeDtypeStruct((t, w), BF16)
    return pl.pallas_call(
        kern,
        grid=(t // tm,),
        in_specs=[
            pl.BlockSpec((tm, d), row),
            pl.BlockSpec((1, d), const),
            pl.BlockSpec((d, 2 * mw), const),
            pl.BlockSpec((d, 2 * mw), const),
            pl.BlockSpec((d, 2 * LANES), const),
            pl.BlockSpec((d, 3 * sw), const),
            pl.BlockSpec((CONV_WIDTH, mw), const),
            pl.BlockSpec((CONV_WIDTH, mw), const),
        ],
        out_specs=[
            pl.BlockSpec((tm, mw), row), pl.BlockSpec((tm, mw), row),
            pl.BlockSpec((tm, mw), row), pl.BlockSpec((tm, mw), row),
            pl.BlockSpec((tm, 2 * LANES), row),
            pl.BlockSpec((tm, sw), row), pl.BlockSpec((tm, sw), row), pl.BlockSpec((tm, sw), row),
        ],
        out_shape=[bf(mw), bf(mw), bf(mw), bf(mw),
                   jax.ShapeDtypeStruct((t, 2 * LANES), F32), bf(sw), bf(sw), bf(sw)],
        scratch_shapes=[pltpu.VMEM((tm + 8, 2 * mw), F32)],
        compiler_params=pltpu.CompilerParams(
            dimension_semantics=("arbitrary",), vmem_limit_bytes=VMEM_LIMIT),
        name="inproj",
    )(x2, g_mix.reshape(1, d), wqk, wvo, wg, ws, conv_q, conv_k)


def _mlstm_kernel(q_ref, k_ref, v_ref, o_ref, gate_ref, bias_ref, gh_ref, out_ref,
                  c_ref, m_ref, *, chunk):
    L = chunk
    hd = M_HEAD_DIM
    nchunks = q_ref.shape[0] // L

    @pl.when(pl.program_id(1) == 0)
    def _():
        c_ref[...] = jnp.zeros(c_ref.shape, F32)
        m_ref[...] = jnp.zeros(m_ref.shape, F32)

    rows = lax.broadcasted_iota(jnp.int32, (L, L), 0)
    cols = lax.broadcasted_iota(jnp.int32, (L, L), 1)
    causal = cols <= rows
    tri = causal.astype(BF16)
    lane2 = lax.broadcasted_iota(jnp.int32, (L, 2 * hd), 1)
    ones_col = (lane2 == hd).astype(F32)

    def chunk_body(c, _):
        r0 = pl.multiple_of(c * L, L)
        g = gate_ref[pl.ds(r0, L), :] + bias_ref[...]
        gi = g[:, 0:LANES]
        gf = g[:, LANES:2 * LANES]
        lf = jnp.minimum(gf, 0.0) - jnp.log(1.0 + jnp.exp(-jnp.abs(gf)))
        l1, l2, l3 = _split3(lf)
        b = _dot(tri, l1) + _dot(tri, l2) + _dot(tri, l3)
        b_last = b[L - 1:L, :]
        w_end = b_last - b + gi
        m_loc = jnp.max(w_end, axis=0, keepdims=True)
        e_end = jnp.exp(w_end - m_loc)
        m_prev = m_ref[...]
        m_new = jnp.maximum(b_last + m_prev, m_loc)
        decay = jnp.exp(b_last + m_prev - m_new)
        scale = jnp.exp(m_loc - m_new)
        b_t = b.T
        gi_t = gi.T
        for h in range(M_HEADS):
            hs = slice(h * hd, (h + 1) * hd)
            qh = q_ref[pl.ds(r0, L), hs]
            kh = k_ref[pl.ds(r0, L), hs]
            vh = v_ref[pl.ds(r0, L), hs].astype(F32)
            vext = jnp.concatenate([vh, jnp.zeros((L, hd), F32)], axis=1) + ones_col
            bc = b[:, h:h + 1]
            e = jnp.where(causal, bc - b_t[h:h + 1, :] + gi_t[h:h + 1, :], -jnp.inf)
            log_inter = bc + m_prev[:, h:h + 1]
            m_t = jnp.maximum(log_inter, jnp.max(e, axis=1, keepdims=True))
            w = (jnp.exp(e - m_t) * _dot_nt(qh, kh)).astype(BF16)
            a_int = jnp.exp(log_inter - m_t)
            cext = c_ref[h]
            num = _dot(w, vext.astype(BF16)) + a_int * _dot(qh, cext.astype(BF16))
            den = num[:, hd:hd + 1]
            hh = num[:, 0:hd] / jnp.maximum(jnp.abs(den), jnp.exp(-m_t))
            hh = _rms(hh, gh_ref[:, hs])
            og = _sigmoid(o_ref[pl.ds(r0, L), hs].astype(F32))
            out_ref[pl.ds(r0, L), hs] = (og * hh).astype(BF16)
            ev = (e_end[:, h:h + 1] * vext).astype(BF16)
            c_ref[h] = decay[:, h:h + 1] * cext + scale[:, h:h + 1] * _dot_tn(kh, ev)
        m_ref[...] = m_new
        return 0

    lax.fori_loop(0, nchunks, chunk_body, 0)


def _mlstm(mq, mk, mv, mo, gates, b_gates, g_mhead, batch, seq_len, rows):
    t, mw = mq.shape
    h = M_HEADS
    bias = jnp.zeros((1, 2 * LANES), F32)
    bias = bias.at[0, 0:h].set(b_gates[0:h]).at[0, LANES:LANES + h].set(b_gates[h:2 * h])
    nb = seq_len // rows
    row = lambda b, i: (b * nb + i, 0)
    const = lambda b, i: (0, 0)
    return pl.pallas_call(
        functools.partial(_mlstm_kernel, chunk=MLSTM_CHUNK),
        grid=(batch, nb),
        in_specs=[pl.BlockSpec((rows, mw), row)] * 4 + [
            pl.BlockSpec((rows, 2 * LANES), row),
            pl.BlockSpec((1, 2 * LANES), const),
            pl.BlockSpec((1, mw), const),
        ],
        out_specs=pl.BlockSpec((rows, mw), row),
        out_shape=jax.ShapeDtypeStruct((t, mw), BF16),
        scratch_shapes=[pltpu.VMEM((h, M_HEAD_DIM, 2 * M_HEAD_DIM), F32),
                        pltpu.VMEM((1, LANES), F32)],
        compiler_params=pltpu.CompilerParams(
            dimension_semantics=("arbitrary", "arbitrary"), vmem_limit_bytes=VMEM_LIMIT),
        name="mlstm",
    )(mq, mk, mv, mo, gates, bias, g_mhead.reshape(1, mw))


def _sb_kernel(q_ref, k_ref, v_ref, out_ref, acc_ref, carry_ref):
    lax.fori_loop(0, q_ref.shape[0] // SB_BLOCK,
                  functools.partial(_sb_query_block, q_ref, k_ref, v_ref, out_ref, acc_ref, carry_ref),
                  0)


def _sb_query_block(q_ref, k_ref, v_ref, out_ref, acc_ref, carry_ref, qi, _):
    blk = SB_BLOCK
    q0 = pl.multiple_of(qi * blk, blk)
    lane = lax.broadcasted_iota(jnp.int32, (blk, LANES), 1)
    head0 = lane < SB_HEAD_DIM
    q = q_ref[pl.ds(q0, blk), :]
    zero = jnp.zeros_like(q)
    qm = (jnp.where(head0, q, zero), jnp.where(head0, zero, q))
    rows = lax.broadcasted_iota(jnp.int32, (blk, blk), 0)
    cols = lax.broadcasted_iota(jnp.int32, (blk, blk), 1)
    strict = cols < rows
    neg_suffix = jnp.where(rows >= cols, -1.0, 0.0).astype(BF16)

    def block(j, carries, mask=None):
        k0 = pl.multiple_of(j * blk, blk)
        kb = k_ref[pl.ds(k0, blk), :]
        vb = v_ref[pl.ds(k0, blk), :]
        vz = jnp.zeros_like(vb)
        vm = (jnp.where(head0, vb, vz), jnp.where(head0, vz, vb))
        upd = None
        new = []
        for h in range(2):
            z = _dot_nt(qm[h], kb)
            sp = jnp.maximum(z, 0.0) + jnp.log(1.0 + jnp.exp(-jnp.abs(z)))
            if mask is not None:
                sp = jnp.where(mask, sp, 0.0)
            rc = _dot(sp.astype(BF16), neg_suffix)
            p = jnp.exp(z + rc + carries[h])
            if mask is not None:
                p = jnp.where(mask, p, 0.0)
            new.append(carries[h] + rc[:, 0:1])
            d = _dot(p.astype(BF16), vm[h])
            upd = d if upd is None else upd + d
        return upd, new

    zeros = jnp.zeros((blk, 1), F32)
    u_diag, carries = block(qi, [zeros, zeros], strict)
    has_prev = jnp.broadcast_to(qi > 0, (blk, blk))
    u_prev, carries = block(jnp.maximum(qi - 1, 0), carries, has_prev)
    acc_ref[...] = u_diag + u_prev
    carry_ref[0] = carries[0]
    carry_ref[1] = carries[1]

    def cond(state):
        it, top = state
        return (it < qi) & (top > SB_ZERO_LOG)

    def body(state):
        it, _ = state
        upd, new = block(qi - 1 - it, [carry_ref[0], carry_ref[1]])
        acc_ref[...] += upd
        carry_ref[0] = new[0]
        carry_ref[1] = new[1]
        return it + 1, jnp.maximum(jnp.max(new[0]), jnp.max(new[1]))

    lax.while_loop(cond, body, (jnp.int32(1), jnp.maximum(jnp.max(carries[0]), jnp.max(carries[1]))))
    out_ref[pl.ds(q0, blk), :] = acc_ref[...].astype(BF16)
    return 0


def _stickbreak(sq, sk, sv, batch, seq_len):
    t, sw = sq.shape
    npair = sw // LANES
    seq = pl.BlockSpec((seq_len, LANES), lambda b, hp: (b, hp))
    return pl.pallas_call(
        _sb_kernel,
        grid=(batch, npair),
        in_specs=[seq, seq, seq],
        out_specs=seq,
        out_shape=jax.ShapeDtypeStruct((t, sw), BF16),
        scratch_shapes=[pltpu.VMEM((SB_BLOCK, LANES), F32), pltpu.VMEM((2, SB_BLOCK, 1), F32)],
        compiler_params=pltpu.CompilerParams(
            dimension_semantics=("arbitrary", "arbitrary"), vmem_limit_bytes=VMEM_LIMIT),
        name="stickbrk",
    )(sq, sk, sv)


def _outroute_kernel(x_ref, hm_ref, hs_ref, wom_ref, wos_ref, g_ref, wr_ref, br_ref,
                     h1_ref, c_ref, route_ref, cnt_ref, run_ref, *, n_groups, per_group):
    i = pl.program_id(0)
    tm = x_ref.shape[0]

    @pl.when(i == 0)
    def _():
        run_ref[...] = jnp.zeros(run_ref.shape, F32)

    h1 = x_ref[...] + _dot(hm_ref[...], wom_ref[...]) + _dot(hs_ref[...], wos_ref[...])
    h1_ref[...] = h1
    c = _rms(h1, g_ref[...])
    c_ref[...] = c

    c1, c2, _ = _split3(c)
    pa = _dot(c1, wr_ref[...])
    pb = _dot(c2, wr_ref[...])
    logits = (pa[:, 0:LANES] + (pa[:, LANES:] + pb[:, 0:LANES]) + pb[:, LANES:]) + br_ref[...]

    lane = lax.broadcasted_iota(jnp.int32, (tm, LANES), 1).astype(F32)
    ninf = -jnp.inf
    big = float(LANES)

    def first_max(v):
        mx = jnp.max(v, axis=1, keepdims=True)
        idx = jnp.min(jnp.where(v == mx, lane, big), axis=1, keepdims=True)
        return mx, idx

    gl = jnp.where(lane < n_groups, logits, ninf)
    gmax, gsel = first_max(gl)
    p_g = 1.0 / jnp.sum(jnp.exp(gl - gmax), axis=1, keepdims=True)
    lo = LOGIT_LANE_E + per_group * gsel
    el = jnp.where((lane >= lo) & (lane < lo + per_group), logits, ninf)
    v1, i1 = first_max(el)
    v2, i2 = first_max(jnp.where(lane == i1, ninf, el))
    tt = jnp.exp(v2 - v1)
    w0 = p_g / (1.0 + tt)
    w1_ = p_g * tt / (1.0 + tt)

    oh0 = lane == i1
    oh1 = lane == i2
    ohsum = oh0.astype(F32) + oh1.astype(F32)
    rows = lax.broadcasted_iota(jnp.int32, (tm, tm), 0)
    cols = lax.broadcasted_iota(jnp.int32, (tm, tm), 1)
    before = (cols < rows).astype(BF16)
    prefix = _dot(before, ohsum.astype(BF16)) + run_ref[...]
    r0 = jnp.sum(jnp.where(oh0, prefix, 0.0), axis=1, keepdims=True)
    r1 = jnp.sum(jnp.where(oh1, prefix, 0.0), axis=1, keepdims=True)
    run = run_ref[...] + jnp.sum(ohsum, axis=0, keepdims=True)
    run_ref[...] = run
    cnt_ref[...] = jnp.broadcast_to(run, cnt_ref.shape)

    e0 = i1 - LOGIT_LANE_E
    e1 = i2 - LOGIT_LANE_E
    route = jnp.zeros((tm, LANES), F32)
    for ln, val in ((ROUTE_LANE_E, e0), (ROUTE_LANE_E + 1, e1), (ROUTE_LANE_W, w0),
                    (ROUTE_LANE_W + 1, w1_), (ROUTE_LANE_R, r0), (ROUTE_LANE_R + 1, r1)):
        route = jnp.where(lane == ln, val, route)
    route_ref[...] = route


def _tile_rows(ref, s, n, pitch=SUBLANES):
    return ref.at[pl.ds(s, n, stride=pitch), :]


def _tile_copy(src_ref, src_row, dst_ref, dst_row, sem, pitch):
    return pltpu.make_async_copy(src_ref.at[pl.ds(src_row * pitch, pitch), :],
                                 dst_ref.at[pl.ds(dst_row * pitch, pitch), :], sem)


def _lanes_to_smem(vals, vm_ref, sm_ref, sem):
    vm_ref[...] = vals.T[0:8, :].astype(jnp.int32)
    cp = pltpu.make_async_copy(vm_ref, sm_ref, sem)
    cp.start()
    cp.wait()


def _outroute(x2, hm, hs, w_out, g_ffn, w_rg, b_rg, w_re, b_re, tm):
    t, d = x2.shape
    mw = hm.shape[1]
    sw = hs.shape[1]
    n_groups = w_rg.shape[1]
    n_exp = w_re.shape[1]
    wr = jnp.zeros((d, LANES), F32)
    wr = wr.at[:, 0:n_groups].set(w_rg).at[:, LOGIT_LANE_E:LOGIT_LANE_E + n_exp].set(w_re)
    wr_hi, wr_lo, _ = _split3(wr)
    wr2 = jnp.concatenate([wr_hi, wr_lo], axis=1)
    br = jnp.zeros((1, LANES), F32)
    br = br.at[0, 0:n_groups].set(b_rg).at[0, LOGIT_LANE_E:LOGIT_LANE_E + n_exp].set(b_re)
    row = lambda i: (i, 0)
    const = lambda i: (0, 0)
    kern = functools.partial(_outroute_kernel, n_groups=n_groups, per_group=n_exp // n_groups)
    return pl.pallas_call(
        kern,
        grid=(t // tm,),
        in_specs=[
            pl.BlockSpec((tm, d), row),
            pl.BlockSpec((tm, mw), row),
            pl.BlockSpec((tm, sw), row),
            pl.BlockSpec((mw, d), const),
            pl.BlockSpec((sw, d), const),
            pl.BlockSpec((1, d), const),
            pl.BlockSpec((d, 2 * LANES), const),
            pl.BlockSpec((1, LANES), const),
        ],
        out_specs=[
            pl.BlockSpec((tm, d), row),
            pl.BlockSpec((tm, d), row),
            pl.BlockSpec((tm, LANES), row),
            pl.BlockSpec((8, LANES), const),
        ],
        out_shape=[
            jax.ShapeDtypeStruct((t, d), F32),
            jax.ShapeDtypeStruct((t, d), F32),
            jax.ShapeDtypeStruct((t, LANES), F32),
            jax.ShapeDtypeStruct((8, LANES), F32),
        ],
        scratch_shapes=[pltpu.VMEM((1, LANES), F32)],
        compiler_params=pltpu.CompilerParams(
            dimension_semantics=("arbitrary",), vmem_limit_bytes=VMEM_LIMIT),
        name="outroute",
    )(x2, hm, hs, w_out[0:mw].astype(BF16), w_out[mw:].astype(BF16), g_ffn.reshape(1, d),
      wr2, br)


def _slotpos_kernel(route_ref, offs_ref, pos_ref):
    route = route_ref[...]
    tm = route.shape[0]
    lane = lax.broadcasted_iota(jnp.int32, (tm, LANES), 1)
    offs = offs_ref[...]
    out = jnp.zeros((tm, LANES), F32)
    for j in range(TOP_K):
        e = route[:, ROUTE_LANE_E + j:ROUTE_LANE_E + j + 1].astype(jnp.int32)
        base = jnp.sum(jnp.where(lane == e, offs, 0.0), axis=1, keepdims=True)
        out = jnp.where(lane == j, base + route[:, ROUTE_LANE_R + j:ROUTE_LANE_R + j + 1], out)
    pos_ref[...] = out.T[0:SUBLANES, :].astype(jnp.int32)


def _slotpos(route, offs_row, tm):
    t = route.shape[0]
    return pl.pallas_call(
        _slotpos_kernel,
        grid=(t // tm,),
        in_specs=[pl.BlockSpec((tm, LANES), lambda i: (i, 0)),
                  pl.BlockSpec((1, LANES), lambda i: (0, 0))],
        out_specs=pl.BlockSpec((SUBLANES, tm), lambda i: (0, i)),
        out_shape=jax.ShapeDtypeStruct((SUBLANES, t), jnp.int32),
        compiler_params=pltpu.CompilerParams(dimension_semantics=("arbitrary",)),
        name="slotpos",
    )(route, offs_row)


def _dispatch_kernel(zblk_ref, zuse_ref, pos_ref, c_ref, route_ref, xin_ref, rows_ref, zero_ref,
                     sem, zsem, *, n_tokens):
    i = pl.program_id(0)
    tm, d = c_ref.shape

    @pl.when(i == 0)
    def _():
        zero_ref[...] = jnp.zeros(zero_ref.shape, jnp.uint32)
        n = zero_ref.shape[0]

        def zero_block(k):
            return pltpu.make_async_copy(
                zero_ref, xin_ref.at[pl.ds(pl.multiple_of(zblk_ref[k] * n, n), n), :], zsem)

        for k in range(zblk_ref.shape[0]):
            pl.when(zuse_ref[k] != 0)(lambda k=k: zero_block(k).start())
        for k in range(zblk_ref.shape[0]):
            pl.when(zuse_ref[k] != 0)(lambda k=k: zero_block(k).wait())

    half = d // 2
    lane = lax.broadcasted_iota(jnp.int32, (tm, LANES), 1)
    row_id = (i * tm + lax.broadcasted_iota(jnp.int32, (tm, 1), 0)).astype(F32)
    route = route_ref[...]
    packed = _pack_bf16_pairs(c_ref[...])
    for j in range(TOP_K):
        w = route[:, ROUTE_LANE_W + j:ROUTE_LANE_W + j + 1]
        meta = jnp.where(lane == META_DEST, row_id + j * n_tokens, jnp.where(lane == META_W, w, 0.0))
        for s in range(X_SUBLANES):
            if s < half // LANES:
                sub = packed[:, s * LANES:(s + 1) * LANES]
            else:
                sub = lax.bitcast_convert_type(meta, jnp.uint32)
            _tile_rows(rows_ref.at[j], s, tm, X_SUBLANES)[...] = sub

    def issue(t, _):
        for j in range(TOP_K):
            _tile_copy(rows_ref.at[j], t, xin_ref, pos_ref[j, t], sem, X_SUBLANES).start(priority=j)
        return 0

    lax.fori_loop(0, tm, issue, 0, unroll=ROW_DMA_UNROLL)
    for j in range(TOP_K):
        pltpu.make_async_copy(rows_ref.at[j], rows_ref.at[j], sem).wait()


def _dispatch(zero_blocks, zero_use, pos, c, route, n_blocks, tm):
    t, d = c.shape
    assert d // 2 // LANES == META_SUBLANE == X_SUBLANES - 1
    rows = EXPERT_ROWS
    grid_spec = pltpu.PrefetchScalarGridSpec(
        num_scalar_prefetch=2,
        grid=(t // tm,),
        in_specs=[
            pl.BlockSpec((SUBLANES, tm), lambda i, *_: (0, i), memory_space=pltpu.SMEM),
            pl.BlockSpec((tm, d), lambda i, *_: (i, 0)),
            pl.BlockSpec((tm, LANES), lambda i, *_: (i, 0)),
        ],
        out_specs=pl.BlockSpec(memory_space=pl.ANY),
        scratch_shapes=[pltpu.VMEM((TOP_K, tm * X_SUBLANES, LANES), jnp.uint32),
                        pltpu.VMEM((rows * X_SUBLANES, LANES), jnp.uint32),
                        pltpu.SemaphoreType.DMA(()),
                        pltpu.SemaphoreType.DMA(())],
    )
    return pl.pallas_call(
        functools.partial(_dispatch_kernel, n_tokens=t),
        grid_spec=grid_spec,
        out_shape=jax.ShapeDtypeStruct((n_blocks * rows * X_SUBLANES, LANES), jnp.uint32),
        compiler_params=pltpu.CompilerParams(
            dimension_semantics=("arbitrary",), vmem_limit_bytes=VMEM_LIMIT),
        name="dispatch",
    )(zero_blocks, zero_use, pos, c, route)


def _pack_bf16_pairs(v):
    half = v.shape[1] // 2
    bits = lax.bitcast_convert_type(v.astype(BF16).astype(F32), jnp.uint32)
    return (bits[:, 0:half] >> 16) | bits[:, half:]


def _unpack_bf16_pairs(words):
    lo = [lax.bitcast_convert_type(w << 16, F32) for w in words]
    hi = [lax.bitcast_convert_type(w & jnp.uint32(0xFFFF0000), F32) for w in words]
    return jnp.concatenate(lo + hi, axis=1)


def _experts_kernel(be_ref, bn_ref, nv_ref, x_ref, wg_ref, wu_ref, wd_ref, yout_ref,
                    wgb, wub, wdb, ybuf, dest_vm, dest_sm, sems, dsem, *, dump_row):
    i = pl.program_id(0)
    s = i % 2
    rows = ybuf.shape[1] // Y_SUBLANES
    prev = be_ref[jnp.maximum(i - 1, 0)]
    active = i < nv_ref[0]

    def to_dump(slot):
        def body(r, _):
            dest_sm[slot, 0, r] = dump_row + r
            return 0
        lax.fori_loop(0, rows, body, 0)

    def send(slot, r, queue=0):
        _tile_copy(ybuf.at[slot], r, yout_ref, dest_sm[slot, 0, r], sems.at[slot],
                   Y_SUBLANES).start(priority=queue)

    def wait(slot):
        pltpu.make_async_copy(ybuf.at[slot], ybuf.at[slot], sems.at[slot]).wait()

    @pl.when(i == 0)
    def _():
        ybuf[1] = jnp.zeros(ybuf.shape[1:], jnp.uint32)
        to_dump(1)

    @pl.when(i > 0)
    def _():
        wait(s)

    @pl.when(active & ((i == 0) | (be_ref[i] != prev)))
    def _():
        wgb[...] = wg_ref[0].astype(BF16)
        wub[...] = wu_ref[0].astype(BF16)
        wdb[...] = wd_ref[0].astype(BF16)

    @pl.when(active)
    def _():
        for r in range(rows):
            send(1 - s, r, r % 2)
        x = _unpack_bf16_pairs([_tile_rows(x_ref, t, rows, X_SUBLANES)[...]
                                for t in range(META_SUBLANE)]).astype(BF16)
        meta = lax.bitcast_convert_type(_tile_rows(x_ref, META_SUBLANE, rows, X_SUBLANES)[...], F32)
        gt = _dot(x, wgb[...])
        up = _dot(x, wub[...])
        hid = (gt * _sigmoid(gt) * up).astype(BF16)
        y = _pack_bf16_pairs(_dot(hid, wdb[...]) * meta[:, META_W:META_W + 1])
        for t in range(Y_SUBLANES):
            _tile_rows(ybuf.at[s], t, rows, Y_SUBLANES)[...] = y[:, t * LANES:(t + 1) * LANES]
        row = lax.broadcasted_iota(jnp.int32, (rows, LANES), 0)
        dest = jnp.where(row < bn_ref[i], meta, (dump_row + row).astype(F32))
        _lanes_to_smem(dest, dest_vm, dest_sm.at[s], dsem)

    @pl.when(jnp.logical_not(active))
    def _():
        lax.fori_loop(0, rows, lambda r, _: send(1 - s, r) or 0, 0, unroll=ROW_DMA_UNROLL)
        to_dump(s)

    @pl.when(i == pl.num_programs(0) - 1)
    def _():
        wait(1 - s)


def _experts(blk_e, blk_n, n_valid, xin, w_gate, w_up, w_down, n_out_rows):
    rows = EXPERT_ROWS
    nblk = xin.shape[0] // (rows * X_SUBLANES)
    d, de = w_gate.shape[1], w_gate.shape[2]
    assert d == 2 * Y_SUBLANES * LANES, "an output row is 4 sublanes of packed bf16 pairs"
    grid_spec = pltpu.PrefetchScalarGridSpec(
        num_scalar_prefetch=3,
        grid=(nblk + 1,),
        in_specs=[
            pl.BlockSpec((rows * X_SUBLANES, LANES),
                         lambda i, be, bn, nv: (jnp.minimum(i, nblk - 1), 0)),
            pl.BlockSpec((1, d, de), lambda i, be, bn, nv: (be[i], 0, 0)),
            pl.BlockSpec((1, d, de), lambda i, be, bn, nv: (be[i], 0, 0)),
            pl.BlockSpec((1, de, d), lambda i, be, bn, nv: (be[i], 0, 0)),
        ],
        out_specs=pl.BlockSpec(memory_space=pl.ANY),
        scratch_shapes=[pltpu.VMEM((d, de), BF16), pltpu.VMEM((d, de), BF16),
                        pltpu.VMEM((de, d), BF16),
                        pltpu.VMEM((2, rows * Y_SUBLANES, LANES), jnp.uint32),
                        pltpu.VMEM((8, rows), jnp.int32),
                        pltpu.SMEM((2, 8, rows), jnp.int32),
                        pltpu.SemaphoreType.DMA((2,)),
                        pltpu.SemaphoreType.DMA(())],
    )
    return pl.pallas_call(
        functools.partial(_experts_kernel, dump_row=n_out_rows),
        grid_spec=grid_spec,
        out_shape=jax.ShapeDtypeStruct(((n_out_rows + rows) * Y_SUBLANES, LANES), jnp.uint32),
        compiler_params=pltpu.CompilerParams(
            dimension_semantics=("arbitrary",), vmem_limit_bytes=VMEM_LIMIT),
        name="experts",
    )(blk_e, blk_n, n_valid, xin, w_gate, w_up, w_down)


def _combine_kernel(h1_ref, y0_ref, y1_ref, p_ref, wpg_ref, wpp_ref,
                    gple_ref, gpost_ref, gfin_ref, out_ref):
    tm = h1_ref.shape[0]
    y0, y1 = (_unpack_bf16_pairs([_tile_rows(ref, s, tm, Y_SUBLANES)[...] for s in range(Y_SUBLANES)])
              for ref in (y0_ref, y1_ref))
    h2 = h1_ref[...] + (y0 + y1)
    gate = _sigmoid(_dot(_rms(h2, gple_ref[...]).astype(BF16), wpg_ref[...]))
    ple = _rms(_dot(p_ref[...].astype(BF16), wpp_ref[...]), gpost_ref[...])
    h3 = h2 + gate * ple
    out_ref[...] = _rms(h3, gfin_ref[...])


def _combine(h1, p2, y, w_pg, w_pp, g_ple, g_post, g_final, tm):
    t, d = h1.shape
    pd = p2.shape[1]
    row = lambda i: (i, 0)
    const = lambda i: (0, 0)
    return pl.pallas_call(
        _combine_kernel,
        grid=(t // tm,),
        in_specs=[
            pl.BlockSpec((tm, d), row),
            pl.BlockSpec((tm * Y_SUBLANES, LANES), lambda i: (i, 0)),
            pl.BlockSpec((tm * Y_SUBLANES, LANES), lambda i: (t // tm + i, 0)),
            pl.BlockSpec((tm, pd), row),
            pl.BlockSpec((d, d), const),
            pl.BlockSpec((pd, d), const),
            pl.BlockSpec((1, d), const),
            pl.BlockSpec((1, d), const),
            pl.BlockSpec((1, d), const),
        ],
        out_specs=pl.BlockSpec((tm, d), row),
        out_shape=jax.ShapeDtypeStruct((t, d), F32),
        compiler_params=pltpu.CompilerParams(
            dimension_semantics=("arbitrary",), vmem_limit_bytes=VMEM_LIMIT),
        name="combine",
    )(h1, y, y, p2, w_pg.astype(BF16), w_pp.astype(BF16),
      g_ple.reshape(1, d), g_post.reshape(1, d), g_final.reshape(1, d))


def _largest_tile(n, cap):
    tile = cap
    while n % tile:
        tile //= 2
    return tile


def kernel(x, p, g_mix, w_in, b_gates, conv_q, conv_k, g_mhead, w_out, g_ffn, w_router_group,
           b_router_group, w_router_expert, b_router_expert, w_exp_gate, w_exp_up, w_exp_down,
           g_ple, w_ple_gate, w_ple_proj, g_ple_post, g_final):
    batch, seq_len, d = x.shape
    t = batch * seq_len
    tm = _largest_tile(seq_len, 512)
    n_exp = w_router_expert.shape[-1]
    rows = EXPERT_ROWS
    nblk = t * TOP_K // rows + n_exp

    assert w_in.shape[0] == 1, "single-layer block"
    l = 0
    h = x.reshape(t, d)
    mq, mk, mv, mo, gates, sq, sk, sv = _inproj(
        h, g_mix[l], w_in[l], conv_q[l], conv_k[l], seq_len, tm)
    hm = _mlstm(mq, mk, mv, mo, gates, b_gates[l], g_mhead[l], batch, seq_len,
                _largest_tile(seq_len, 1024))
    hs = _stickbreak(sq, sk, sv, batch, seq_len)
    h1, c, route, counts = _outroute(
        h, hm, hs, w_out[l], g_ffn[l], w_router_group[l], b_router_group[l],
        w_router_expert[l], b_router_expert[l], tm)

    cnt = counts[0, LOGIT_LANE_E:LOGIT_LANE_E + n_exp].astype(jnp.int32)
    nb_e = (cnt + rows - 1) // rows
    cum = jnp.cumsum(nb_e)
    offs = (cum - nb_e) * rows
    offs_row = jnp.zeros((1, LANES), F32).at[0, 0:n_exp].set(offs.astype(F32))
    n_valid = cum[-1:]
    step = jnp.arange(nblk + 1, dtype=jnp.int32)
    blk_e = jnp.minimum(jnp.sum(cum[None, :] <= step[:, None], axis=1), n_exp - 1).astype(jnp.int32)
    mine = blk_e[:, None] == jnp.arange(n_exp, dtype=jnp.int32)[None, :]
    first = jnp.sum(jnp.where(mine, (cum - nb_e)[None, :], 0), axis=1)
    blk_n = jnp.clip(jnp.sum(jnp.where(mine, cnt[None, :], 0), axis=1) - rows * (step - first), 0, rows)
    tail = n_valid + jnp.arange(n_exp, dtype=jnp.int32)
    zero_blocks = jnp.concatenate([jnp.maximum(cum - 1, 0), jnp.minimum(tail, nblk - 1)])
    zero_use = jnp.concatenate([nb_e > 0, tail < nblk]).astype(jnp.int32)

    pos = _slotpos(route, offs_row, _largest_tile(t, 2048))
    xin = _dispatch(zero_blocks, zero_use, pos, c, route, nblk, _largest_tile(seq_len, 256))
    y = _experts(blk_e, blk_n, n_valid, xin, w_exp_gate[l], w_exp_up[l], w_exp_down[l], TOP_K * t)
    out = _combine(h1, p[l].reshape(t, -1), y, w_ple_gate[l], w_ple_proj[l],
                   g_ple[l], g_ple_post[l], g_final, tm)
    return out.reshape(batch, seq_len, d)
```

```python
import functools

import jax
import jax.numpy as jnp
from jax import lax
from jax.experimental import pallas as pl
from jax.experimental.pallas import tpu as pltpu

F32 = jnp.float32
BF16 = jnp.bfloat16
EPS = 1e-6

M_HEADS = 4
M_HEAD_DIM = 128
SB_HEAD_DIM = 64
CONV_WIDTH = 4
TOP_K = 2
LANES = 128
VMEM_LIMIT = 56 * 1024 * 1024

MLSTM_CHUNK = 128
SB_BLOCK = 256
SB_ZERO_LOG = -105.0
EXPERT_ROWS = 512
ROW_DMA_UNROLL = 8
X_SUBLANES = 5
Y_SUBLANES = 4
ROUTE_LANE_E = 0
ROUTE_LANE_W = 2
ROUTE_LANE_R = 4
SUBLANES = 8
META_SUBLANE = 4
META_DEST = 0
META_W = 1
LOGIT_LANE_E = 4


def _rms(x, g):
    return x * lax.rsqrt(jnp.mean(x * x, axis=-1, keepdims=True) + EPS) * g


def _sigmoid(x):
    return 1.0 / (1.0 + jnp.exp(-x))


def _split3(a):
    a1 = a.astype(BF16)
    r1 = a - a1.astype(F32)
    a2 = r1.astype(BF16)
    a3 = (r1 - a2.astype(F32)).astype(BF16)
    return a1, a2, a3


def _dot(a, b):
    return jnp.dot(a, b, preferred_element_type=F32)


def _dot_nt(a, b):
    return lax.dot_general(a, b, (((1,), (1,)), ((), ())), preferred_element_type=F32)


def _dot_tn(a, b):
    return lax.dot_general(a, b, (((0,), (0,)), ((), ())), preferred_element_type=F32)


def _inproj_kernel(x_ref, g_ref, wqk_ref, wvo_ref, wg_ref, ws_ref, cq_ref, ck_ref,
                   mq_ref, mk_ref, mv_ref, mo_ref, gate_ref, sq_ref, sk_ref, sv_ref,
                   ext_ref, *, tiles_per_seq, k_scale):
    i = pl.program_id(0)
    tm = x_ref.shape[0]
    mw = mq_ref.shape[1]
    sw = sq_ref.shape[1]
    a = _rms(x_ref[...], g_ref[...]).astype(BF16)

    @pl.when(i % tiles_per_seq == 0)
    def _():
        ext_ref[0:8, :] = jnp.zeros((8, 2 * mw), F32)

    ext_ref[8:8 + tm, 0:mw] = _dot(a, wqk_ref[:, 0:mw])
    ext_ref[8:8 + tm, mw:2 * mw] = _dot(a, wqk_ref[:, mw:2 * mw])

    def conv_silu(w_ref, c0):
        acc = ext_ref[pl.ds(8 - (CONV_WIDTH - 1), tm), c0:c0 + mw] * w_ref[0:1, :]
        for j in range(1, CONV_WIDTH):
            acc = acc + ext_ref[pl.ds(8 - (CONV_WIDTH - 1) + j, tm), c0:c0 + mw] * w_ref[j:j + 1, :]
        return acc * _sigmoid(acc)

    mq_ref[...] = conv_silu(cq_ref, 0).astype(BF16)
    mk_ref[...] = (conv_silu(ck_ref, mw) * k_scale).astype(BF16)
    ext_ref[0:8, :] = ext_ref[tm:tm + 8, :]

    mv_ref[...] = _dot(a, wvo_ref[:, 0:mw]).astype(BF16)
    mo_ref[...] = _dot(a, wvo_ref[:, mw:2 * mw]).astype(BF16)
    gate_ref[...] = _dot(a, wg_ref[...])
    sq_ref[...] = _dot(a, ws_ref[:, 0:sw]).astype(BF16)
    sk_ref[...] = _dot(a, ws_ref[:, sw:2 * sw]).astype(BF16)
    sv_ref[...] = _dot(a, ws_ref[:, 2 * sw:3 * sw]).astype(BF16)


def _inproj(x2, g_mix, w_in, conv_q, conv_k, seq_len, tm):
    t, d = x2.shape
    mw = conv_q.shape[1]
    h = M_HEADS
    sw = (w_in.shape[1] - 4 * mw - 2 * h) // 3
    wqk = w_in[:, 0:2 * mw].astype(BF16)
    wvo = w_in[:, 2 * mw:4 * mw].astype(BF16)
    wg = jnp.zeros((d, 2 * LANES), F32)
    wg = wg.at[:, 0:h].set(w_in[:, 4 * mw:4 * mw + h])
    wg = wg.at[:, LANES:LANES + h].set(w_in[:, 4 * mw + h:4 * mw + 2 * h]).astype(BF16)
    ws = w_in[:, 4 * mw + 2 * h:]
    ws = jnp.concatenate([ws[:, 0:sw] * (SB_HEAD_DIM ** -0.5), ws[:, sw:]], axis=1).astype(BF16)
    row = lambda i: (i, 0)
    const = lambda i: (0, 0)
    kern = functools.partial(_inproj_kernel, tiles_per_seq=seq_len // tm, k_scale=M_HEAD_DIM ** -0.5)
    bf = lambda w: jax.ShapeDtypeStruct((t, w), BF16)
    return pl.pallas_call(
        kern,
        grid=(t // tm,),
        in_specs=[
            pl.BlockSpec((tm, d), row),
            pl.BlockSpec((1, d), const),
            pl.BlockSpec((d, 2 * mw), const),
            pl.BlockSpec((d, 2 * mw), const),
            pl.BlockSpec((d, 2 * LANES), const),
            pl.BlockSpec((d, 3 * sw), const),
            pl.BlockSpec((CONV_WIDTH, mw), const),
            pl.BlockSpec((CONV_WIDTH, mw), const),
        ],
        out_specs=[
            pl.BlockSpec((tm, mw), row), pl.BlockSpec((tm, mw), row),
            pl.BlockSpec((tm, mw), row), pl.BlockSpec((tm, mw), row),
            pl.BlockSpec((tm, 2 * LANES), row),
            pl.BlockSpec((tm, sw), row), pl.BlockSpec((tm, sw), row), pl.BlockSpec((tm, sw), row),
        ],
        out_shape=[bf(mw), bf(mw), bf(mw), bf(mw),
                   jax.ShapeDtypeStruct((t, 2 * LANES), F32), bf(sw), bf(sw), bf(sw)],
        scratch_shapes=[pltpu.VMEM((tm + 8, 2 * mw), F32)],
        compiler_params=pltpu.CompilerParams(
            dimension_semantics=("arbitrary",), vmem_limit_bytes=VMEM_LIMIT),
        name="inproj",
    )(x2, g_mix.reshape(1, d), wqk, wvo, wg, ws, conv_q, conv_k)


def _mlstm_kernel(q_ref, k_ref, v_ref, o_ref, gate_ref, bias_ref, gh_ref, out_ref,
                  c_ref, m_ref, *, chunk):
    L = chunk
    hd = M_HEAD_DIM
    nchunks = q_ref.shape[0] // L

    @pl.when(pl.program_id(1) == 0)
    def _():
        c_ref[...] = jnp.zeros(c_ref.shape, F32)
        m_ref[...] = jnp.zeros(m_ref.shape, F32)

    rows = lax.broadcasted_iota(jnp.int32, (L, L), 0)
    cols = lax.broadcasted_iota(jnp.int32, (L, L), 1)
    causal = cols <= rows
    tri = causal.astype(BF16)
    lane2 = lax.broadcasted_iota(jnp.int32, (L, 2 * hd), 1)
    ones_col = (lane2 == hd).astype(F32)

    def chunk_body(c, _):
        r0 = pl.multiple_of(c * L, L)
        g = gate_ref[pl.ds(r0, L), :] + bias_ref[...]
        gi = g[:, 0:LANES]
        gf = g[:, LANES:2 * LANES]
        lf = jnp.minimum(gf, 0.0) - jnp.log(1.0 + jnp.exp(-jnp.abs(gf)))
        l1, l2, l3 = _split3(lf)
        b = _dot(tri, l1) + _dot(tri, l2) + _dot(tri, l3)
        b_last = b[L - 1:L, :]
        w_end = b_last - b + gi
        m_loc = jnp.max(w_end, axis=0, keepdims=True)
        e_end = jnp.exp(w_end - m_loc)
        m_prev = m_ref[...]
        m_new = jnp.maximum(b_last + m_prev, m_loc)
        decay = jnp.exp(b_last + m_prev - m_new)
        scale = jnp.exp(m_loc - m_new)
        b_t = b.T
        gi_t = gi.T
        for h in range(M_HEADS):
            hs = slice(h * hd, (h + 1) * hd)
            qh = q_ref[pl.ds(r0, L), hs]
            kh = k_ref[pl.ds(r0, L), hs]
            vh = v_ref[pl.ds(r0, L), hs].astype(F32)
            vext = jnp.concatenate([vh, jnp.zeros((L, hd), F32)], axis=1) + ones_col
            bc = b[:, h:h + 1]
            e = jnp.where(causal, bc - b_t[h:h + 1, :] + gi_t[h:h + 1, :], -jnp.inf)
            log_inter = bc + m_prev[:, h:h + 1]
            m_t = jnp.maximum(log_inter, jnp.max(e, axis=1, keepdims=True))
            w = (jnp.exp(e - m_t) * _dot_nt(qh, kh)).astype(BF16)
            a_int = jnp.exp(log_inter - m_t)
            cext = c_ref[h]
            num = _dot(w, vext.astype(BF16)) + a_int * _dot(qh, cext.astype(BF16))
            den = num[:, hd:hd + 1]
            hh = num[:, 0:hd] / jnp.maximum(jnp.abs(den), jnp.exp(-m_t))
            hh = _rms(hh, gh_ref[:, hs])
            og = _sigmoid(o_ref[pl.ds(r0, L), hs].astype(F32))
            out_ref[pl.ds(r0, L), hs] = (og * hh).astype(BF16)
            ev = (e_end[:, h:h + 1] * vext).astype(BF16)
            c_ref[h] = decay[:, h:h + 1] * cext + scale[:, h:h + 1] * _dot_tn(kh, ev)
        m_ref[...] = m_new
        return 0

    lax.fori_loop(0, nchunks, chunk_body, 0)


def _mlstm(mq, mk, mv, mo, gates, b_gates, g_mhead, batch, seq_len, rows):
    t, mw = mq.shape
    h = M_HEADS
    bias = jnp.zeros((1, 2 * LANES), F32)
    bias = bias.at[0, 0:h].set(b_gates[0:h]).at[0, LANES:LANES + h].set(b_gates[h:2 * h])
    nb = seq_len // rows
    row = lambda b, i: (b * nb + i, 0)
    const = lambda b, i: (0, 0)
    return pl.pallas_call(
        functools.partial(_mlstm_kernel, chunk=MLSTM_CHUNK),
        grid=(batch, nb),
        in_specs=[pl.BlockSpec((rows, mw), row)] * 4 + [
            pl.BlockSpec((rows, 2 * LANES), row),
            pl.BlockSpec((1, 2 * LANES), const),
            pl.BlockSpec((1, mw), const),
        ],
        out_specs=pl.BlockSpec((rows, mw), row),
        out_shape=jax.ShapeDtypeStruct((t, mw), BF16),
        scratch_shapes=[pltpu.VMEM((h, M_HEAD_DIM, 2 * M_HEAD_DIM), F32),
                        pltpu.VMEM((1, LANES), F32)],
        compiler_params=pltpu.CompilerParams(
            dimension_semantics=("arbitrary", "arbitrary"), vmem_limit_bytes=VMEM_LIMIT),
        name="mlstm",
    )(mq, mk, mv, mo, gates, bias, g_mhead.reshape(1, mw))


def _sb_kernel(q_ref, k_ref, v_ref, out_ref, acc_ref, carry_ref):
    lax.fori_loop(0, q_ref.shape[0] // SB_BLOCK,
                  functools.partial(_sb_query_block, q_ref, k_ref, v_ref, out_ref, acc_ref, carry_ref),
                  0)


def _sb_query_block(q_ref, k_ref, v_ref, out_ref, acc_ref, carry_ref, qi, _):
    blk = SB_BLOCK
    q0 = pl.multiple_of(qi * blk, blk)
    lane = lax.broadcasted_iota(jnp.int32, (blk, LANES), 1)
    head0 = lane < SB_HEAD_DIM
    q = q_ref[pl.ds(q0, blk), :]
    zero = jnp.zeros_like(q)
    qm = (jnp.where(head0, q, zero), jnp.where(head0, zero, q))
    rows = lax.broadcasted_iota(jnp.int32, (blk, blk), 0)
    cols = lax.broadcasted_iota(jnp.int32, (blk, blk), 1)
    strict = cols < rows
    neg_suffix = jnp.where(rows >= cols, -1.0, 0.0).astype(BF16)

    def block(j, carries, mask=None, live=None):
        k0 = pl.multiple_of(j * blk, blk)
        kb = k_ref[pl.ds(k0, blk), :]
        vb = v_ref[pl.ds(k0, blk), :]
        vz = jnp.zeros_like(vb)
        vm = [jnp.where(head0, vb, vz), jnp.where(head0, vz, vb)]
        if live is not None:
            vm = [jnp.where(live, v, vz) for v in vm]
        upd = None
        new = []
        for h in range(2):
            z = _dot_nt(qm[h], kb)
            neg_abs = lax.bitcast_convert_type(
                lax.bitcast_convert_type(z, jnp.uint32) | jnp.uint32(0x80000000), F32)
            sp = jnp.maximum(z, 0.0) + jnp.log(1.0 + jnp.exp(neg_abs))
            if mask is not None:
                sp = jnp.where(mask, sp, 0.0)
            rc = _dot(sp.astype(BF16), neg_suffix)
            p = jnp.exp(z + rc + carries[h])
            if mask is not None:
                p = jnp.where(mask, p, 0.0)
            carry = carries[h] + rc[:, 0:1]
            new.append(carry if live is None else jnp.where(live, carry, carries[h]))
            d = _dot(p.astype(BF16), vm[h])
            upd = d if upd is None else upd + d
        return upd, new

    zeros = jnp.zeros((blk, 1), F32)
    u_diag, carries = block(qi, [zeros, zeros], mask=strict)
    u_prev, carries = block(jnp.maximum(qi - 1, 0), carries, live=qi > 0)
    acc_ref[...] = u_diag + u_prev
    carry_ref[0] = carries[0]
    carry_ref[1] = carries[1]

    def cond(state):
        it, top = state
        return (it < qi) & (top > SB_ZERO_LOG)

    def body(state):
        it, _ = state
        upd, new = block(qi - 1 - it, [carry_ref[0], carry_ref[1]])
        acc_ref[...] += upd
        carry_ref[0] = new[0]
        carry_ref[1] = new[1]
        return it + 1, jnp.maximum(jnp.max(new[0]), jnp.max(new[1]))

    lax.while_loop(cond, body, (jnp.int32(1), jnp.maximum(jnp.max(carries[0]), jnp.max(carries[1]))))
    out_ref[pl.ds(q0, blk), :] = acc_ref[...].astype(BF16)
    return 0


def _stickbreak(sq, sk, sv, batch, seq_len):
    t, sw = sq.shape
    npair = sw // LANES
    seq = pl.BlockSpec((seq_len, LANES), lambda b, hp: (b, hp))
    return pl.pallas_call(
        _sb_kernel,
        grid=(batch, npair),
        in_specs=[seq, seq, seq],
        out_specs=seq,
        out_shape=jax.ShapeDtypeStruct((t, sw), BF16),
        scratch_shapes=[pltpu.VMEM((SB_BLOCK, LANES), F32), pltpu.VMEM((2, SB_BLOCK, 1), F32)],
        compiler_params=pltpu.CompilerParams(
            dimension_semantics=("arbitrary", "arbitrary"), vmem_limit_bytes=VMEM_LIMIT),
        name="stickbrk",
    )(sq, sk, sv)


def _outroute_kernel(x_ref, hm_ref, hs_ref, wom_ref, wos_ref, g_ref, wr_ref, br_ref,
                     h1_ref, c_ref, route_ref, cnt_ref, run_ref, *, n_groups, per_group):
    i = pl.program_id(0)
    tm = x_ref.shape[0]

    @pl.when(i == 0)
    def _():
        run_ref[...] = jnp.zeros(run_ref.shape, F32)

    h1 = x_ref[...] + _dot(hm_ref[...], wom_ref[...]) + _dot(hs_ref[...], wos_ref[...])
    h1_ref[...] = h1
    c = _rms(h1, g_ref[...])
    c_ref[...] = c

    c1, c2, _ = _split3(c)
    pa = _dot(c1, wr_ref[...])
    pb = _dot(c2, wr_ref[...])
    logits = (pa[:, 0:LANES] + (pa[:, LANES:] + pb[:, 0:LANES]) + pb[:, LANES:]) + br_ref[...]

    lane = lax.broadcasted_iota(jnp.int32, (tm, LANES), 1).astype(F32)
    ninf = -jnp.inf
    big = float(LANES)

    def first_max(v):
        mx = jnp.max(v, axis=1, keepdims=True)
        idx = jnp.min(jnp.where(v == mx, lane, big), axis=1, keepdims=True)
        return mx, idx

    gl = jnp.where(lane < n_groups, logits, ninf)
    gmax, gsel = first_max(gl)
    p_g = 1.0 / jnp.sum(jnp.exp(gl - gmax), axis=1, keepdims=True)
    lo = LOGIT_LANE_E + per_group * gsel
    el = jnp.where((lane >= lo) & (lane < lo + per_group), logits, ninf)
    v1, i1 = first_max(el)
    v2, i2 = first_max(jnp.where(lane == i1, ninf, el))
    tt = jnp.exp(v2 - v1)
    w0 = p_g / (1.0 + tt)
    w1_ = p_g * tt / (1.0 + tt)

    oh0 = lane == i1
    oh1 = lane == i2
    ohsum = oh0.astype(F32) + oh1.astype(F32)
    rows = lax.broadcasted_iota(jnp.int32, (tm, tm), 0)
    cols = lax.broadcasted_iota(jnp.int32, (tm, tm), 1)
    before = (cols < rows).astype(BF16)
    prefix = _dot(before, ohsum.astype(BF16)) + run_ref[...]
    r0 = jnp.sum(jnp.where(oh0, prefix, 0.0), axis=1, keepdims=True)
    r1 = jnp.sum(jnp.where(oh1, prefix, 0.0), axis=1, keepdims=True)
    run = run_ref[...] + jnp.sum(ohsum, axis=0, keepdims=True)
    run_ref[...] = run
    cnt_ref[...] = jnp.broadcast_to(run, cnt_ref.shape)

    e0 = i1 - LOGIT_LANE_E
    e1 = i2 - LOGIT_LANE_E
    route = jnp.zeros((tm, LANES), F32)
    for ln, val in ((ROUTE_LANE_E, e0), (ROUTE_LANE_E + 1, e1), (ROUTE_LANE_W, w0),
                    (ROUTE_LANE_W + 1, w1_), (ROUTE_LANE_R, r0), (ROUTE_LANE_R + 1, r1)):
        route = jnp.where(lane == ln, val, route)
    route_ref[...] = route


def _tile_rows(ref, s, n, pitch=SUBLANES):
    return ref.at[pl.ds(s, n, stride=pitch), :]


def _tile_copy(src_ref, src_row, dst_ref, dst_row, sem, pitch):
    return pltpu.make_async_copy(src_ref.at[pl.ds(src_row * pitch, pitch), :],
                                 dst_ref.at[pl.ds(dst_row * pitch, pitch), :], sem)


def _lanes_to_smem(vals, vm_ref, sm_ref, sem):
    vm_ref[...] = vals.T[0:8, :].astype(jnp.int32)
    cp = pltpu.make_async_copy(vm_ref, sm_ref, sem)
    cp.start()
    cp.wait()


def _outroute(x2, hm, hs, w_out, g_ffn, w_rg, b_rg, w_re, b_re, tm):
    t, d = x2.shape
    mw = hm.shape[1]
    sw = hs.shape[1]
    n_groups = w_rg.shape[1]
    n_exp = w_re.shape[1]
    wr = jnp.zeros((d, LANES), F32)
    wr = wr.at[:, 0:n_groups].set(w_rg).at[:, LOGIT_LANE_E:LOGIT_LANE_E + n_exp].set(w_re)
    wr_hi, wr_lo, _ = _split3(wr)
    wr2 = jnp.concatenate([wr_hi, wr_lo], axis=1)
    br = jnp.zeros((1, LANES), F32)
    br = br.at[0, 0:n_groups].set(b_rg).at[0, LOGIT_LANE_E:LOGIT_LANE_E + n_exp].set(b_re)
    row = lambda i: (i, 0)
    const = lambda i: (0, 0)
    kern = functools.partial(_outroute_kernel, n_groups=n_groups, per_group=n_exp // n_groups)
    return pl.pallas_call(
        kern,
        grid=(t // tm,),
        in_specs=[
            pl.BlockSpec((tm, d), row),
            pl.BlockSpec((tm, mw), row),
            pl.BlockSpec((tm, sw), row),
            pl.BlockSpec((mw, d), const),
            pl.BlockSpec((sw, d), const),
            pl.BlockSpec((1, d), const),
            pl.BlockSpec((d, 2 * LANES), const),
            pl.BlockSpec((1, LANES), const),
        ],
        out_specs=[
            pl.BlockSpec((tm, d), row),
            pl.BlockSpec((tm, d), row),
            pl.BlockSpec((tm, LANES), row),
            pl.BlockSpec((8, LANES), const),
        ],
        out_shape=[
            jax.ShapeDtypeStruct((t, d), F32),
            jax.ShapeDtypeStruct((t, d), F32),
            jax.ShapeDtypeStruct((t, LANES), F32),
            jax.ShapeDtypeStruct((8, LANES), F32),
        ],
        scratch_shapes=[pltpu.VMEM((1, LANES), F32)],
        compiler_params=pltpu.CompilerParams(
            dimension_semantics=("arbitrary",), vmem_limit_bytes=VMEM_LIMIT),
        name="outroute",
    )(x2, hm, hs, w_out[0:mw].astype(BF16), w_out[mw:].astype(BF16), g_ffn.reshape(1, d),
      wr2, br)


def _slotpos_kernel(route_ref, offs_ref, pos_ref):
    route = route_ref[...]
    tm = route.shape[0]
    lane = lax.broadcasted_iota(jnp.int32, (tm, LANES), 1)
    offs = offs_ref[...]
    out = jnp.zeros((tm, LANES), F32)
    for j in range(TOP_K):
        e = route[:, ROUTE_LANE_E + j:ROUTE_LANE_E + j + 1].astype(jnp.int32)
        base = jnp.sum(jnp.where(lane == e, offs, 0.0), axis=1, keepdims=True)
        out = jnp.where(lane == j, base + route[:, ROUTE_LANE_R + j:ROUTE_LANE_R + j + 1], out)
    pos_ref[...] = out.T[0:SUBLANES, :].astype(jnp.int32)


def _slotpos(route, offs_row, tm):
    t = route.shape[0]
    return pl.pallas_call(
        _slotpos_kernel,
        grid=(t // tm,),
        in_specs=[pl.BlockSpec((tm, LANES), lambda i: (i, 0)),
                  pl.BlockSpec((1, LANES), lambda i: (0, 0))],
        out_specs=pl.BlockSpec((SUBLANES, tm), lambda i: (0, i)),
        out_shape=jax.ShapeDtypeStruct((SUBLANES, t), jnp.int32),
        compiler_params=pltpu.CompilerParams(dimension_semantics=("arbitrary",)),
        name="slotpos",
    )(route, offs_row)


def _dispatch_kernel(zblk_ref, zuse_ref, pos_ref, c_ref, route_ref, xin_ref, rows_ref, zero_ref,
                     sem, zsem, *, n_tokens):
    i = pl.program_id(0)
    tm, d = c_ref.shape

    @pl.when(i == 0)
    def _():
        zero_ref[...] = jnp.zeros(zero_ref.shape, jnp.uint32)
        n = zero_ref.shape[0]

        def zero_block(k):
            return pltpu.make_async_copy(
                zero_ref, xin_ref.at[pl.ds(pl.multiple_of(zblk_ref[k] * n, n), n), :], zsem)

        for k in range(zblk_ref.shape[0]):
            pl.when(zuse_ref[k] != 0)(lambda k=k: zero_block(k).start())
        for k in range(zblk_ref.shape[0]):
            pl.when(zuse_ref[k] != 0)(lambda k=k: zero_block(k).wait())

    half = d // 2
    lane = lax.broadcasted_iota(jnp.int32, (tm, LANES), 1)
    row_id = (i * tm + lax.broadcasted_iota(jnp.int32, (tm, 1), 0)).astype(F32)
    route = route_ref[...]
    packed = _pack_bf16_pairs(c_ref[...])
    for j in range(TOP_K):
        w = route[:, ROUTE_LANE_W + j:ROUTE_LANE_W + j + 1]
        meta = jnp.where(lane == META_DEST, row_id + j * n_tokens, jnp.where(lane == META_W, w, 0.0))
        for s in range(X_SUBLANES):
            if s < half // LANES:
                sub = packed[:, s * LANES:(s + 1) * LANES]
            else:
                sub = lax.bitcast_convert_type(meta, jnp.uint32)
            _tile_rows(rows_ref.at[j], s, tm, X_SUBLANES)[...] = sub

    def issue(t, _):
        for j in range(TOP_K):
            _tile_copy(rows_ref.at[j], t, xin_ref, pos_ref[j, t], sem, X_SUBLANES).start(priority=j)
        return 0

    lax.fori_loop(0, tm, issue, 0, unroll=ROW_DMA_UNROLL)
    for j in range(TOP_K):
        pltpu.make_async_copy(rows_ref.at[j], rows_ref.at[j], sem).wait()


def _dispatch(zero_blocks, zero_use, pos, c, route, n_blocks, tm):
    t, d = c.shape
    assert d // 2 // LANES == META_SUBLANE == X_SUBLANES - 1
    rows = EXPERT_ROWS
    grid_spec = pltpu.PrefetchScalarGridSpec(
        num_scalar_prefetch=2,
        grid=(t // tm,),
        in_specs=[
            pl.BlockSpec((SUBLANES, tm), lambda i, *_: (0, i), memory_space=pltpu.SMEM),
            pl.BlockSpec((tm, d), lambda i, *_: (i, 0)),
            pl.BlockSpec((tm, LANES), lambda i, *_: (i, 0)),
        ],
        out_specs=pl.BlockSpec(memory_space=pl.ANY),
        scratch_shapes=[pltpu.VMEM((TOP_K, tm * X_SUBLANES, LANES), jnp.uint32),
                        pltpu.VMEM((rows * X_SUBLANES, LANES), jnp.uint32),
                        pltpu.SemaphoreType.DMA(()),
                        pltpu.SemaphoreType.DMA(())],
    )
    return pl.pallas_call(
        functools.partial(_dispatch_kernel, n_tokens=t),
        grid_spec=grid_spec,
        out_shape=jax.ShapeDtypeStruct((n_blocks * rows * X_SUBLANES, LANES), jnp.uint32),
        compiler_params=pltpu.CompilerParams(
            dimension_semantics=("arbitrary",), vmem_limit_bytes=VMEM_LIMIT),
        name="dispatch",
    )(zero_blocks, zero_use, pos, c, route)


def _pack_bf16_pairs(v):
    half = v.shape[1] // 2
    bits = lax.bitcast_convert_type(v.astype(BF16).astype(F32), jnp.uint32)
    return (bits[:, 0:half] >> 16) | bits[:, half:]


def _unpack_bf16_pairs(words):
    lo = [lax.bitcast_convert_type(w << 16, F32) for w in words]
    hi = [lax.bitcast_convert_type(w & jnp.uint32(0xFFFF0000), F32) for w in words]
    return jnp.concatenate(lo + hi, axis=1)


def _experts_kernel(be_ref, bn_ref, nv_ref, x_ref, wg_ref, wu_ref, wd_ref, yout_ref,
                    wgb, wub, wdb, ybuf, dest_vm, dest_sm, sems, dsem, *, dump_row):
    i = pl.program_id(0)
    s = i % 2
    rows = ybuf.shape[1] // Y_SUBLANES
    prev = be_ref[jnp.maximum(i - 1, 0)]
    active = i < nv_ref[0]

    def to_dump(slot):
        def body(r, _):
            dest_sm[slot, 0, r] = dump_row + r
            return 0
        lax.fori_loop(0, rows, body, 0)

    def send(slot, r, queue=0):
        _tile_copy(ybuf.at[slot], r, yout_ref, dest_sm[slot, 0, r], sems.at[slot],
                   Y_SUBLANES).start(priority=queue)

    def wait(slot):
        pltpu.make_async_copy(ybuf.at[slot], ybuf.at[slot], sems.at[slot]).wait()

    @pl.when(i == 0)
    def _():
        ybuf[1] = jnp.zeros(ybuf.shape[1:], jnp.uint32)
        to_dump(1)

    @pl.when(i > 0)
    def _():
        wait(s)

    @pl.when(active & ((i == 0) | (be_ref[i] != prev)))
    def _():
        wgb[...] = wg_ref[0].astype(BF16)
        wub[...] = wu_ref[0].astype(BF16)
        wdb[...] = wd_ref[0].astype(BF16)

    @pl.when(active)
    def _():
        for r in range(rows):
            send(1 - s, r, r % 2)
        x = _unpack_bf16_pairs([_tile_rows(x_ref, t, rows, X_SUBLANES)[...]
                                for t in range(META_SUBLANE)]).astype(BF16)
        meta = lax.bitcast_convert_type(_tile_rows(x_ref, META_SUBLANE, rows, X_SUBLANES)[...], F32)
        gt = _dot(x, wgb[...])
        up = _dot(x, wub[...])
        hid = (gt * _sigmoid(gt) * up).astype(BF16)
        y = _pack_bf16_pairs(_dot(hid, wdb[...]) * meta[:, META_W:META_W + 1])
        for t in range(Y_SUBLANES):
            _tile_rows(ybuf.at[s], t, rows, Y_SUBLANES)[...] = y[:, t * LANES:(t + 1) * LANES]
        row = lax.broadcasted_iota(jnp.int32, (rows, LANES), 0)
        dest = jnp.where(row < bn_ref[i], meta, (dump_row + row).astype(F32))
        _lanes_to_smem(dest, dest_vm, dest_sm.at[s], dsem)

    @pl.when(jnp.logical_not(active))
    def _():
        lax.fori_loop(0, rows, lambda r, _: send(1 - s, r) or 0, 0, unroll=ROW_DMA_UNROLL)
        to_dump(s)

    @pl.when(i == pl.num_programs(0) - 1)
    def _():
        wait(1 - s)


def _experts(blk_e, blk_n, n_valid, xin, w_gate, w_up, w_down, n_out_rows):
    rows = EXPERT_ROWS
    nblk = xin.shape[0] // (rows * X_SUBLANES)
    d, de = w_gate.shape[1], w_gate.shape[2]
    assert d == 2 * Y_SUBLANES * LANES, "an output row is 4 sublanes of packed bf16 pairs"
    grid_spec = pltpu.PrefetchScalarGridSpec(
        num_scalar_prefetch=3,
        grid=(nblk + 1,),
        in_specs=[
            pl.BlockSpec((rows * X_SUBLANES, LANES),
                         lambda i, be, bn, nv: (jnp.minimum(i, nblk - 1), 0)),
            pl.BlockSpec((1, d, de), lambda i, be, bn, nv: (be[i], 0, 0)),
            pl.BlockSpec((1, d, de), lambda i, be, bn, nv: (be[i], 0, 0)),
            pl.BlockSpec((1, de, d), lambda i, be, bn, nv: (be[i], 0, 0)),
        ],
        out_specs=pl.BlockSpec(memory_space=pl.ANY),
        scratch_shapes=[pltpu.VMEM((d, de), BF16), pltpu.VMEM((d, de), BF16),
                        pltpu.VMEM((de, d), BF16),
                        pltpu.VMEM((2, rows * Y_SUBLANES, LANES), jnp.uint32),
                        pltpu.VMEM((8, rows), jnp.int32),
                        pltpu.SMEM((2, 8, rows), jnp.int32),
                        pltpu.SemaphoreType.DMA((2,)),
                        pltpu.SemaphoreType.DMA(())],
    )
    return pl.pallas_call(
        functools.partial(_experts_kernel, dump_row=n_out_rows),
        grid_spec=grid_spec,
        out_shape=jax.ShapeDtypeStruct(((n_out_rows + rows) * Y_SUBLANES, LANES), jnp.uint32),
        compiler_params=pltpu.CompilerParams(
            dimension_semantics=("arbitrary",), vmem_limit_bytes=VMEM_LIMIT),
        name="experts",
    )(blk_e, blk_n, n_valid, xin, w_gate, w_up, w_down)


def _combine_kernel(h1_ref, y0_ref, y1_ref, p_ref, wpg_ref, wpp_ref,
                    gple_ref, gpost_ref, gfin_ref, out_ref):
    tm = h1_ref.shape[0]
    y0, y1 = (_unpack_bf16_pairs([_tile_rows(ref, s, tm, Y_SUBLANES)[...] for s in range(Y_SUBLANES)])
              for ref in (y0_ref, y1_ref))
    h2 = h1_ref[...] + (y0 + y1)
    gate = _sigmoid(_dot(_rms(h2, gple_ref[...]).astype(BF16), wpg_ref[...]))
    ple = _rms(_dot(p_ref[...].astype(BF16), wpp_ref[...]), gpost_ref[...])
    h3 = h2 + gate * ple
    out_ref[...] = _rms(h3, gfin_ref[...])


def _combine(h1, p2, y, w_pg, w_pp, g_ple, g_post, g_final, tm):
    t, d = h1.shape
    pd = p2.shape[1]
    row = lambda i: (i, 0)
    const = lambda i: (0, 0)
    return pl.pallas_call(
        _combine_kernel,
        grid=(t // tm,),
        in_specs=[
            pl.BlockSpec((tm, d), row),
            pl.BlockSpec((tm * Y_SUBLANES, LANES), lambda i: (i, 0)),
            pl.BlockSpec((tm * Y_SUBLANES, LANES), lambda i: (t // tm + i, 0)),
            pl.BlockSpec((tm, pd), row),
            pl.BlockSpec((d, d), const),
            pl.BlockSpec((pd, d), const),
            pl.BlockSpec((1, d), const),
            pl.BlockSpec((1, d), const),
            pl.BlockSpec((1, d), const),
        ],
        out_specs=pl.BlockSpec((tm, d), row),
        out_shape=jax.ShapeDtypeStruct((t, d), F32),
        compiler_params=pltpu.CompilerParams(
            dimension_semantics=("arbitrary",), vmem_limit_bytes=VMEM_LIMIT),
        name="combine",
    )(h1, y, y, p2, w_pg.astype(BF16), w_pp.astype(BF16),
      g_ple.reshape(1, d), g_post.reshape(1, d), g_final.reshape(1, d))


def _largest_tile(n, cap):
    tile = cap
    while n % tile:
        tile //= 2
    return tile


def kernel(x, p, g_mix, w_in, b_gates, conv_q, conv_k, g_mhead, w_out, g_ffn, w_router_group,
           b_router_group, w_router_expert, b_router_expert, w_exp_gate, w_exp_up, w_exp_down,
           g_ple, w_ple_gate, w_ple_proj, g_ple_post, g_final):
    batch, seq_len, d = x.shape
    t = batch * seq_len
    tm = _largest_tile(seq_len, 512)
    n_exp = w_router_expert.shape[-1]
    rows = EXPERT_ROWS
    nblk = t * TOP_K // rows + n_exp

    assert w_in.shape[0] == 1, "single-layer block"
    l = 0
    h = x.reshape(t, d)
    mq, mk, mv, mo, gates, sq, sk, sv = _inproj(
        h, g_mix[l], w_in[l], conv_q[l], conv_k[l], seq_len, _largest_tile(seq_len, 1024))
    hm = _mlstm(mq, mk, mv, mo, gates, b_gates[l], g_mhead[l], batch, seq_len,
                _largest_tile(seq_len, 1024))
    hs = _stickbreak(sq, sk, sv, batch, seq_len)
    h1, c, route, counts = _outroute(
        h, hm, hs, w_out[l], g_ffn[l], w_router_group[l], b_router_group[l],
        w_router_expert[l], b_router_expert[l], tm)

    cnt = counts[0, LOGIT_LANE_E:LOGIT_LANE_E + n_exp].astype(jnp.int32)
    nb_e = (cnt + rows - 1) // rows
    cum = jnp.cumsum(nb_e)
    offs = (cum - nb_e) * rows
    offs_row = jnp.zeros((1, LANES), F32).at[0, 0:n_exp].set(offs.astype(F32))
    n_valid = cum[-1:]
    step = jnp.arange(nblk + 1, dtype=jnp.int32)
    blk_e = jnp.minimum(jnp.sum(cum[None, :] <= step[:, None], axis=1), n_exp - 1).astype(jnp.int32)
    mine = blk_e[:, None] == jnp.arange(n_exp, dtype=jnp.int32)[None, :]
    first = jnp.sum(jnp.where(mine, (cum - nb_e)[None, :], 0), axis=1)
    blk_n = jnp.clip(jnp.sum(jnp.where(mine, cnt[None, :], 0), axis=1) - rows * (step - first), 0, rows)
    tail = n_valid + jnp.arange(n_exp, dtype=jnp.int32)
    zero_blocks = jnp.concatenate([jnp.maximum(cum - 1, 0), jnp.minimum(tail, nblk - 1)])
    zero_use = jnp.concatenate([nb_e > 0, tail < nblk]).astype(jnp.int32)

    pos = _slotpos(route, offs_row, _largest_tile(t, 2048))
    xin = _dispatch(zero_blocks, zero_use, pos, c, route, nblk, _largest_tile(seq_len, 256))
    y = _experts(blk_e, blk_n, n_valid, xin, w_exp_gate[l], w_exp_up[l], w_exp_down[l], TOP_K * t)
    out = _combine(h1, p[l].reshape(t, -1), y, w_ple_gate[l], w_ple_proj[l],
                   g_ple[l], g_ple_post[l], g_final, tm)
    return out.reshape(batch, seq_len, d)
```

```python
import functools

import jax
import jax.numpy as jnp
from jax import lax
from jax.experimental import pallas as pl
from jax.experimental.pallas import tpu as pltpu

F32 = jnp.float32
BF16 = jnp.bfloat16
EPS = 1e-6

M_HEADS = 4
M_HEAD_DIM = 128
SB_HEAD_DIM = 64
CONV_WIDTH = 4
TOP_K = 2
LANES = 128
VMEM_LIMIT = 56 * 1024 * 1024

MLSTM_CHUNK = 128
SB_BLOCK = 256
SB_ZERO_LOG = -105.0
EXPERT_ROWS = 512
ROW_DMA_UNROLL = 8
X_SUBLANES = 5
Y_SUBLANES = 4
ROUTE_LANE_E = 0
ROUTE_LANE_W = 2
ROUTE_LANE_R = 4
SUBLANES = 8
META_SUBLANE = 4
META_DEST = 0
META_W = 1
LOGIT_LANE_E = 4


def _rms(x, g):
    return x * lax.rsqrt(jnp.mean(x * x, axis=-1, keepdims=True) + EPS) * g


def _sigmoid(x):
    return 1.0 / (1.0 + jnp.exp(-x))


def _split3(a):
    a1 = a.astype(BF16)
    r1 = a - a1.astype(F32)
    a2 = r1.astype(BF16)
    a3 = (r1 - a2.astype(F32)).astype(BF16)
    return a1, a2, a3


def _dot(a, b):
    return jnp.dot(a, b, preferred_element_type=F32)


def _dot_nt(a, b):
    return lax.dot_general(a, b, (((1,), (1,)), ((), ())), preferred_element_type=F32)


def _dot_tn(a, b):
    return lax.dot_general(a, b, (((0,), (0,)), ((), ())), preferred_element_type=F32)


def _inproj_kernel(x_ref, g_ref, wqk_ref, wvo_ref, wg_ref, ws_ref, cq_ref, ck_ref,
                   mq_ref, mk_ref, mv_ref, mo_ref, gate_ref, sq_ref, sk_ref, sv_ref,
                   ext_ref, *, tiles_per_seq, k_scale):
    i = pl.program_id(0)
    tm = x_ref.shape[0]
    mw = mq_ref.shape[1]
    sw = sq_ref.shape[1]
    a = _rms(x_ref[...], g_ref[...]).astype(BF16)

    @pl.when(i % tiles_per_seq == 0)
    def _():
        ext_ref[0:8, :] = jnp.zeros((8, 2 * mw), F32)

    ext_ref[8:8 + tm, 0:mw] = _dot(a, wqk_ref[:, 0:mw])
    ext_ref[8:8 + tm, mw:2 * mw] = _dot(a, wqk_ref[:, mw:2 * mw])

    def conv_silu(w_ref, c0):
        acc = ext_ref[pl.ds(8 - (CONV_WIDTH - 1), tm), c0:c0 + mw] * w_ref[0:1, :]
        for j in range(1, CONV_WIDTH):
            acc = acc + ext_ref[pl.ds(8 - (CONV_WIDTH - 1) + j, tm), c0:c0 + mw] * w_ref[j:j + 1, :]
        return acc * _sigmoid(acc)

    mq_ref[...] = conv_silu(cq_ref, 0).astype(BF16)
    mk_ref[...] = (conv_silu(ck_ref, mw) * k_scale).astype(BF16)
    ext_ref[0:8, :] = ext_ref[tm:tm + 8, :]

    mv_ref[...] = _dot(a, wvo_ref[:, 0:mw]).astype(BF16)
    mo_ref[...] = _dot(a, wvo_ref[:, mw:2 * mw]).astype(BF16)
    gate_ref[...] = _dot(a, wg_ref[...])
    sq_ref[...] = _dot(a, ws_ref[:, 0:sw]).astype(BF16)
    sk_ref[...] = _dot(a, ws_ref[:, sw:2 * sw]).astype(BF16)
    sv_ref[...] = _dot(a, ws_ref[:, 2 * sw:3 * sw]).astype(BF16)


def _inproj(x2, g_mix, w_in, conv_q, conv_k, seq_len, tm):
    t, d = x2.shape
    mw = conv_q.shape[1]
    h = M_HEADS
    sw = (w_in.shape[1] - 4 * mw - 2 * h) // 3
    wqk = w_in[:, 0:2 * mw].astype(BF16)
    wvo = w_in[:, 2 * mw:4 * mw].astype(BF16)
    wg = jnp.zeros((d, 2 * LANES), F32)
    wg = wg.at[:, 0:h].set(w_in[:, 4 * mw:4 * mw + h])
    wg = wg.at[:, LANES:LANES + h].set(w_in[:, 4 * mw + h:4 * mw + 2 * h]).astype(BF16)
    ws = w_in[:, 4 * mw + 2 * h:]
    ws = jnp.concatenate([ws[:, 0:sw] * (SB_HEAD_DIM ** -0.5), ws[:, sw:]], axis=1).astype(BF16)
    row = lambda i: (i, 0)
    const = lambda i: (0, 0)
    kern = functools.partial(_inproj_kernel, tiles_per_seq=seq_len // tm, k_scale=M_HEAD_DIM ** -0.5)
    bf = lambda w: jax.ShapeDtypeStruct((t, w), BF16)
    return pl.pallas_call(
        kern,
        grid=(t // tm,),
        in_specs=[
            pl.BlockSpec((tm, d), row),
            pl.BlockSpec((1, d), const),
            pl.BlockSpec((d, 2 * mw), const),
            pl.BlockSpec((d, 2 * mw), const),
            pl.BlockSpec((d, 2 * LANES), const),
            pl.BlockSpec((d, 3 * sw), const),
            pl.BlockSpec((CONV_WIDTH, mw), const),
            pl.BlockSpec((CONV_WIDTH, mw), const),
        ],
        out_specs=[
            pl.BlockSpec((tm, mw), row), pl.BlockSpec((tm, mw), row),
            pl.BlockSpec((tm, mw), row), pl.BlockSpec((tm, mw), row),
            pl.BlockSpec((tm, 2 * LANES), row),
            pl.BlockSpec((tm, sw), row), pl.BlockSpec((tm, sw), row), pl.BlockSpec((tm, sw), row),
        ],
        out_shape=[bf(mw), bf(mw), bf(mw), bf(mw),
                   jax.ShapeDtypeStruct((t, 2 * LANES), F32), bf(sw), bf(sw), bf(sw)],
        scratch_shapes=[pltpu.VMEM((tm + 8, 2 * mw), F32)],
        compiler_params=pltpu.CompilerParams(
            dimension_semantics=("arbitrary",), vmem_limit_bytes=VMEM_LIMIT),
        name="inproj",
    )(x2, g_mix.reshape(1, d), wqk, wvo, wg, ws, conv_q, conv_k)


def _mlstm_kernel(q_ref, k_ref, v_ref, o_ref, gate_ref, bias_ref, gh_ref, out_ref,
                  c_ref, m_ref, *, chunk):
    L = chunk
    hd = M_HEAD_DIM
    nchunks = q_ref.shape[0] // L

    @pl.when(pl.program_id(1) == 0)
    def _():
        c_ref[...] = jnp.zeros(c_ref.shape, F32)
        m_ref[...] = jnp.zeros(m_ref.shape, F32)

    rows = lax.broadcasted_iota(jnp.int32, (L, L), 0)
    cols = lax.broadcasted_iota(jnp.int32, (L, L), 1)
    causal = cols <= rows
    tri = causal.astype(BF16)
    lane2 = lax.broadcasted_iota(jnp.int32, (L, 2 * hd), 1)
    ones_col = (lane2 == hd).astype(F32)

    def chunk_body(c, _):
        r0 = pl.multiple_of(c * L, L)
        g = gate_ref[pl.ds(r0, L), :] + bias_ref[...]
        gi = g[:, 0:LANES]
        gf = g[:, LANES:2 * LANES]
        lf = jnp.minimum(gf, 0.0) - jnp.log(1.0 + jnp.exp(-jnp.abs(gf)))
        l1, l2, l3 = _split3(lf)
        b = _dot(tri, l1) + _dot(tri, l2) + _dot(tri, l3)
        b_last = b[L - 1:L, :]
        w_end = b_last - b + gi
        m_loc = jnp.max(w_end, axis=0, keepdims=True)
        e_end = jnp.exp(w_end - m_loc)
        m_prev = m_ref[...]
        m_new = jnp.maximum(b_last + m_prev, m_loc)
        decay = jnp.exp(b_last + m_prev - m_new)
        scale = jnp.exp(m_loc - m_new)
        b_t = b.T
        gi_t = gi.T
        for h in range(M_HEADS):
            hs = slice(h * hd, (h + 1) * hd)
            qh = q_ref[pl.ds(r0, L), hs]
            kh = k_ref[pl.ds(r0, L), hs]
            vh = v_ref[pl.ds(r0, L), hs].astype(F32)
            vext = jnp.concatenate([vh, jnp.zeros((L, hd), F32)], axis=1) + ones_col
            bc = b[:, h:h + 1]
            e = jnp.where(causal, bc - b_t[h:h + 1, :] + gi_t[h:h + 1, :], -jnp.inf)
            log_inter = bc + m_prev[:, h:h + 1]
            m_t = jnp.maximum(log_inter, jnp.max(e, axis=1, keepdims=True))
            w = (jnp.exp(e - m_t) * _dot_nt(qh, kh)).astype(BF16)
            a_int = jnp.exp(log_inter - m_t)
            cext = c_ref[h]
            num = _dot(w, vext.astype(BF16)) + a_int * _dot(qh, cext.astype(BF16))
            den = num[:, hd:hd + 1]
            hh = num[:, 0:hd] / jnp.maximum(jnp.abs(den), jnp.exp(-m_t))
            hh = _rms(hh, gh_ref[:, hs])
            og = _sigmoid(o_ref[pl.ds(r0, L), hs].astype(F32))
            out_ref[pl.ds(r0, L), hs] = (og * hh).astype(BF16)
            ev = (e_end[:, h:h + 1] * vext).astype(BF16)
            c_ref[h] = decay[:, h:h + 1] * cext + scale[:, h:h + 1] * _dot_tn(kh, ev)
        m_ref[...] = m_new
        return 0

    lax.fori_loop(0, nchunks, chunk_body, 0)


def _mlstm(mq, mk, mv, mo, gates, b_gates, g_mhead, batch, seq_len, rows):
    t, mw = mq.shape
    h = M_HEADS
    bias = jnp.zeros((1, 2 * LANES), F32)
    bias = bias.at[0, 0:h].set(b_gates[0:h]).at[0, LANES:LANES + h].set(b_gates[h:2 * h])
    nb = seq_len // rows
    row = lambda b, i: (b * nb + i, 0)
    const = lambda b, i: (0, 0)
    return pl.pallas_call(
        functools.partial(_mlstm_kernel, chunk=MLSTM_CHUNK),
        grid=(batch, nb),
        in_specs=[pl.BlockSpec((rows, mw), row)] * 4 + [
            pl.BlockSpec((rows, 2 * LANES), row),
            pl.BlockSpec((1, 2 * LANES), const),
            pl.BlockSpec((1, mw), const),
        ],
        out_specs=pl.BlockSpec((rows, mw), row),
        out_shape=jax.ShapeDtypeStruct((t, mw), BF16),
        scratch_shapes=[pltpu.VMEM((h, M_HEAD_DIM, 2 * M_HEAD_DIM), F32),
                        pltpu.VMEM((1, LANES), F32)],
        compiler_params=pltpu.CompilerParams(
            dimension_semantics=("arbitrary", "arbitrary"), vmem_limit_bytes=VMEM_LIMIT),
        name="mlstm",
    )(mq, mk, mv, mo, gates, bias, g_mhead.reshape(1, mw))


def _sb_kernel(q_ref, k_ref, v_ref, out_ref, acc_ref, carry_ref, z_ref, sp_ref):
    blk = SB_BLOCK
    nq = q_ref.shape[0] // blk
    lane = lax.broadcasted_iota(jnp.int32, (blk, LANES), 1)
    head0 = lane < SB_HEAD_DIM
    rows = lax.broadcasted_iota(jnp.int32, (blk, blk), 0)
    cols = lax.broadcasted_iota(jnp.int32, (blk, blk), 1)
    strict = cols < rows
    neg_suffix = jnp.where(rows >= cols, -1.0, 0.0).astype(BF16)

    def split_heads(x):
        zero = jnp.zeros_like(x)
        return [jnp.where(head0, x, zero), jnp.where(head0, zero, x)]

    def rows_of(ref, j):
        return ref[pl.ds(pl.multiple_of(j * blk, blk), blk), :]

    def scores(qm_h, kb, mask):
        z = _dot_nt(qm_h, kb)
        neg_abs = lax.bitcast_convert_type(
            lax.bitcast_convert_type(z, jnp.uint32) | jnp.uint32(0x80000000), F32)
        sp = jnp.maximum(z, 0.0) + jnp.log(1.0 + jnp.exp(neg_abs))
        if mask is not None:
            sp = jnp.where(mask, sp, 0.0)
        return z, sp.astype(BF16)

    def weights(z, sp, carry, vm_h, mask):
        rc = _dot(sp, neg_suffix)
        p = jnp.exp(z + rc + carry)
        if mask is not None:
            p = jnp.where(mask, p, 0.0)
        return _dot(p.astype(BF16), vm_h), rc[:, 0:1]

    def first_half(qi, slot):
        qm = split_heads(rows_of(q_ref, qi))
        for n, (j, mask) in enumerate(((qi, strict), (jnp.maximum(qi - 1, 0), None))):
            kb = rows_of(k_ref, j)
            for h in range(2):
                z_ref[slot, 2 * n + h], sp_ref[slot, 2 * n + h] = scores(qm[h], kb, mask)

    def second_half(qi, slot):
        carries = [jnp.zeros((blk, 1), F32)] * 2
        upd = None
        for n, (j, mask, live) in enumerate(((qi, strict, None), (jnp.maximum(qi - 1, 0), None, qi > 0))):
            vm = split_heads(rows_of(v_ref, j))
            if live is not None:
                vm = [jnp.where(live, v, jnp.zeros_like(v)) for v in vm]
            new = []
            for h in range(2):
                d, total = weights(z_ref[slot, 2 * n + h], sp_ref[slot, 2 * n + h], carries[h], vm[h], mask)
                carry = carries[h] + total
                new.append(carry if live is None else jnp.where(live, carry, carries[h]))
                upd = d if upd is None else upd + d
            carries = new
        acc_ref[...] = upd
        carry_ref[0] = carries[0]
        carry_ref[1] = carries[1]
        return jnp.maximum(jnp.max(carries[0]), jnp.max(carries[1]))

    def remaining(qi, top):
        def cond(state):
            it, top = state
            return (it < qi) & (top > SB_ZERO_LOG)

        def body(state):
            it, _ = state
            j = qi - 1 - it
            qm = split_heads(rows_of(q_ref, qi))
            kb = rows_of(k_ref, j)
            vm = split_heads(rows_of(v_ref, j))
            upd = None
            tops = []
            for h in range(2):
                z, sp = scores(qm[h], kb, None)
                d, total = weights(z, sp, carry_ref[h], vm[h], None)
                carry_ref[h] = carry_ref[h] + total
                tops.append(jnp.max(carry_ref[h]))
                upd = d if upd is None else upd + d
            acc_ref[...] += upd
            return it + 1, jnp.maximum(tops[0], tops[1])

        lax.while_loop(cond, body, (jnp.int32(1), top))

    first_half(0, 0)

    def query_block(qi, _):
        slot = qi % 2
        top = second_half(qi, slot)
        first_half(jnp.minimum(qi + 1, nq - 1), 1 - slot)
        remaining(qi, top)
        out_ref[pl.ds(pl.multiple_of(qi * blk, blk), blk), :] = acc_ref[...].astype(BF16)
        return 0

    lax.fori_loop(0, nq, query_block, 0)


def _stickbreak(sq, sk, sv, batch, seq_len):
    t, sw = sq.shape
    npair = sw // LANES
    seq = pl.BlockSpec((seq_len, LANES), lambda b, hp: (b, hp))
    return pl.pallas_call(
        _sb_kernel,
        grid=(batch, npair),
        in_specs=[seq, seq, seq],
        out_specs=seq,
        out_shape=jax.ShapeDtypeStruct((t, sw), BF16),
        scratch_shapes=[pltpu.VMEM((SB_BLOCK, LANES), F32), pltpu.VMEM((2, SB_BLOCK, 1), F32),
                        pltpu.VMEM((2, 4, SB_BLOCK, SB_BLOCK), F32),
                        pltpu.VMEM((2, 4, SB_BLOCK, SB_BLOCK), BF16)],
        compiler_params=pltpu.CompilerParams(
            dimension_semantics=("arbitrary", "arbitrary"), vmem_limit_bytes=VMEM_LIMIT),
        name="stickbrk",
    )(sq, sk, sv)


def _outroute_kernel(x_ref, hm_ref, hs_ref, wom_ref, wos_ref, g_ref, wr_ref, br_ref,
                     h1_ref, c_ref, route_ref, cnt_ref, run_ref, *, n_groups, per_group):
    i = pl.program_id(0)
    tm = x_ref.shape[0]

    @pl.when(i == 0)
    def _():
        run_ref[...] = jnp.zeros(run_ref.shape, F32)

    h1 = x_ref[...] + _dot(hm_ref[...], wom_ref[...]) + _dot(hs_ref[...], wos_ref[...])
    h1_ref[...] = h1
    c = _rms(h1, g_ref[...])
    c_ref[...] = c

    c1, c2, _ = _split3(c)
    pa = _dot(c1, wr_ref[...])
    pb = _dot(c2, wr_ref[...])
    logits = (pa[:, 0:LANES] + (pa[:, LANES:] + pb[:, 0:LANES]) + pb[:, LANES:]) + br_ref[...]

    lane = lax.broadcasted_iota(jnp.int32, (tm, LANES), 1).astype(F32)
    ninf = -jnp.inf
    big = float(LANES)

    def first_max(v):
        mx = jnp.max(v, axis=1, keepdims=True)
        idx = jnp.min(jnp.where(v == mx, lane, big), axis=1, keepdims=True)
        return mx, idx

    gl = jnp.where(lane < n_groups, logits, ninf)
    gmax, gsel = first_max(gl)
    p_g = 1.0 / jnp.sum(jnp.exp(gl - gmax), axis=1, keepdims=True)
    lo = LOGIT_LANE_E + per_group * gsel
    el = jnp.where((lane >= lo) & (lane < lo + per_group), logits, ninf)
    v1, i1 = first_max(el)
    v2, i2 = first_max(jnp.where(lane == i1, ninf, el))
    tt = jnp.exp(v2 - v1)
    w0 = p_g / (1.0 + tt)
    w1_ = p_g * tt / (1.0 + tt)

    oh0 = lane == i1
    oh1 = lane == i2
    ohsum = oh0.astype(F32) + oh1.astype(F32)
    rows = lax.broadcasted_iota(jnp.int32, (tm, tm), 0)
    cols = lax.broadcasted_iota(jnp.int32, (tm, tm), 1)
    before = (cols < rows).astype(BF16)
    prefix = _dot(before, ohsum.astype(BF16)) + run_ref[...]
    r0 = jnp.sum(jnp.where(oh0, prefix, 0.0), axis=1, keepdims=True)
    r1 = jnp.sum(jnp.where(oh1, prefix, 0.0), axis=1, keepdims=True)
    run = run_ref[...] + jnp.sum(ohsum, axis=0, keepdims=True)
    run_ref[...] = run
    cnt_ref[...] = jnp.broadcast_to(run, cnt_ref.shape)

    e0 = i1 - LOGIT_LANE_E
    e1 = i2 - LOGIT_LANE_E
    route = jnp.zeros((tm, LANES), F32)
    for ln, val in ((ROUTE_LANE_E, e0), (ROUTE_LANE_E + 1, e1), (ROUTE_LANE_W, w0),
                    (ROUTE_LANE_W + 1, w1_), (ROUTE_LANE_R, r0), (ROUTE_LANE_R + 1, r1)):
        route = jnp.where(lane == ln, val, route)
    route_ref[...] = route


def _tile_rows(ref, s, n, pitch=SUBLANES):
    return ref.at[pl.ds(s, n, stride=pitch), :]


def _tile_copy(src_ref, src_row, dst_ref, dst_row, sem, pitch):
    return pltpu.make_async_copy(src_ref.at[pl.ds(src_row * pitch, pitch), :],
                                 dst_ref.at[pl.ds(dst_row * pitch, pitch), :], sem)


def _lanes_to_smem(vals, vm_ref, sm_ref, sem):
    vm_ref[...] = vals.T[0:8, :].astype(jnp.int32)
    cp = pltpu.make_async_copy(vm_ref, sm_ref, sem)
    cp.start()
    cp.wait()


def _outroute(x2, hm, hs, w_out, g_ffn, w_rg, b_rg, w_re, b_re, tm):
    t, d = x2.shape
    mw = hm.shape[1]
    sw = hs.shape[1]
    n_groups = w_rg.shape[1]
    n_exp = w_re.shape[1]
    wr = jnp.zeros((d, LANES), F32)
    wr = wr.at[:, 0:n_groups].set(w_rg).at[:, LOGIT_LANE_E:LOGIT_LANE_E + n_exp].set(w_re)
    wr_hi, wr_lo, _ = _split3(wr)
    wr2 = jnp.concatenate([wr_hi, wr_lo], axis=1)
    br = jnp.zeros((1, LANES), F32)
    br = br.at[0, 0:n_groups].set(b_rg).at[0, LOGIT_LANE_E:LOGIT_LANE_E + n_exp].set(b_re)
    row = lambda i: (i, 0)
    const = lambda i: (0, 0)
    kern = functools.partial(_outroute_kernel, n_groups=n_groups, per_group=n_exp // n_groups)
    return pl.pallas_call(
        kern,
        grid=(t // tm,),
        in_specs=[
            pl.BlockSpec((tm, d), row),
            pl.BlockSpec((tm, mw), row),
            pl.BlockSpec((tm, sw), row),
            pl.BlockSpec((mw, d), const),
            pl.BlockSpec((sw, d), const),
            pl.BlockSpec((1, d), const),
            pl.BlockSpec((d, 2 * LANES), const),
            pl.BlockSpec((1, LANES), const),
        ],
        out_specs=[
            pl.BlockSpec((tm, d), row),
            pl.BlockSpec((tm, d), row),
            pl.BlockSpec((tm, LANES), row),
            pl.BlockSpec((8, LANES), const),
        ],
        out_shape=[
            jax.ShapeDtypeStruct((t, d), F32),
            jax.ShapeDtypeStruct((t, d), F32),
            jax.ShapeDtypeStruct((t, LANES), F32),
            jax.ShapeDtypeStruct((8, LANES), F32),
        ],
        scratch_shapes=[pltpu.VMEM((1, LANES), F32)],
        compiler_params=pltpu.CompilerParams(
            dimension_semantics=("arbitrary",), vmem_limit_bytes=VMEM_LIMIT),
        name="outroute",
    )(x2, hm, hs, w_out[0:mw].astype(BF16), w_out[mw:].astype(BF16), g_ffn.reshape(1, d),
      wr2, br)


def _slotpos_kernel(route_ref, offs_ref, pos_ref):
    route = route_ref[...]
    tm = route.shape[0]
    lane = lax.broadcasted_iota(jnp.int32, (tm, LANES), 1)
    offs = offs_ref[...]
    out = jnp.zeros((tm, LANES), F32)
    for j in range(TOP_K):
        e = route[:, ROUTE_LANE_E + j:ROUTE_LANE_E + j + 1].astype(jnp.int32)
        base = jnp.sum(jnp.where(lane == e, offs, 0.0), axis=1, keepdims=True)
        out = jnp.where(lane == j, base + route[:, ROUTE_LANE_R + j:ROUTE_LANE_R + j + 1], out)
    pos_ref[...] = out.T[0:SUBLANES, :].astype(jnp.int32)


def _slotpos(route, offs_row, tm):
    t = route.shape[0]
    return pl.pallas_call(
        _slotpos_kernel,
        grid=(t // tm,),
        in_specs=[pl.BlockSpec((tm, LANES), lambda i: (i, 0)),
                  pl.BlockSpec((1, LANES), lambda i: (0, 0))],
        out_specs=pl.BlockSpec((SUBLANES, tm), lambda i: (0, i)),
        out_shape=jax.ShapeDtypeStruct((SUBLANES, t), jnp.int32),
        compiler_params=pltpu.CompilerParams(dimension_semantics=("arbitrary",)),
        name="slotpos",
    )(route, offs_row)


def _dispatch_kernel(zblk_ref, zuse_ref, pos_ref, c_ref, route_ref, xin_ref, rows_ref, zero_ref,
                     sem, zsem, *, n_tokens):
    i = pl.program_id(0)
    tm, d = c_ref.shape

    @pl.when(i == 0)
    def _():
        zero_ref[...] = jnp.zeros(zero_ref.shape, jnp.uint32)
        n = zero_ref.shape[0]

        def zero_block(k):
            return pltpu.make_async_copy(
                zero_ref, xin_ref.at[pl.ds(pl.multiple_of(zblk_ref[k] * n, n), n), :], zsem)

        for k in range(zblk_ref.shape[0]):
            pl.when(zuse_ref[k] != 0)(lambda k=k: zero_block(k).start())
        for k in range(zblk_ref.shape[0]):
            pl.when(zuse_ref[k] != 0)(lambda k=k: zero_block(k).wait())

    half = d // 2
    lane = lax.broadcasted_iota(jnp.int32, (tm, LANES), 1)
    row_id = (i * tm + lax.broadcasted_iota(jnp.int32, (tm, 1), 0)).astype(F32)
    route = route_ref[...]
    packed = _pack_bf16_pairs(c_ref[...])
    for j in range(TOP_K):
        w = route[:, ROUTE_LANE_W + j:ROUTE_LANE_W + j + 1]
        meta = jnp.where(lane == META_DEST, row_id + j * n_tokens, jnp.where(lane == META_W, w, 0.0))
        for s in range(X_SUBLANES):
            if s < half // LANES:
                sub = packed[:, s * LANES:(s + 1) * LANES]
            else:
                sub = lax.bitcast_convert_type(meta, jnp.uint32)
            _tile_rows(rows_ref.at[j], s, tm, X_SUBLANES)[...] = sub

    def issue(t, _):
        for j in range(TOP_K):
            _tile_copy(rows_ref.at[j], t, xin_ref, pos_ref[j, t], sem, X_SUBLANES).start(priority=j)
        return 0

    lax.fori_loop(0, tm, issue, 0, unroll=ROW_DMA_UNROLL)
    for j in range(TOP_K):
        pltpu.make_async_copy(rows_ref.at[j], rows_ref.at[j], sem).wait()


def _dispatch(zero_blocks, zero_use, pos, c, route, n_blocks, tm):
    t, d = c.shape
    assert d // 2 // LANES == META_SUBLANE == X_SUBLANES - 1
    rows = EXPERT_ROWS
    grid_spec = pltpu.PrefetchScalarGridSpec(
        num_scalar_prefetch=2,
        grid=(t // tm,),
        in_specs=[
            pl.BlockSpec((SUBLANES, tm), lambda i, *_: (0, i), memory_space=pltpu.SMEM),
            pl.BlockSpec((tm, d), lambda i, *_: (i, 0)),
            pl.BlockSpec((tm, LANES), lambda i, *_: (i, 0)),
        ],
        out_specs=pl.BlockSpec(memory_space=pl.ANY),
        scratch_shapes=[pltpu.VMEM((TOP_K, tm * X_SUBLANES, LANES), jnp.uint32),
                        pltpu.VMEM((rows * X_SUBLANES, LANES), jnp.uint32),
                        pltpu.SemaphoreType.DMA(()),
                        pltpu.SemaphoreType.DMA(())],
    )
    return pl.pallas_call(
        functools.partial(_dispatch_kernel, n_tokens=t),
        grid_spec=grid_spec,
        out_shape=jax.ShapeDtypeStruct((n_blocks * rows * X_SUBLANES, LANES), jnp.uint32),
        compiler_params=pltpu.CompilerParams(
            dimension_semantics=("arbitrary",), vmem_limit_bytes=VMEM_LIMIT),
        name="dispatch",
    )(zero_blocks, zero_use, pos, c, route)


def _pack_bf16_pairs(v):
    half = v.shape[1] // 2
    bits = lax.bitcast_convert_type(v.astype(BF16).astype(F32), jnp.uint32)
    return (bits[:, 0:half] >> 16) | bits[:, half:]


def _unpack_bf16_pairs(words):
    lo = [lax.bitcast_convert_type(w << 16, F32) for w in words]
    hi = [lax.bitcast_convert_type(w & jnp.uint32(0xFFFF0000), F32) for w in words]
    return jnp.concatenate(lo + hi, axis=1)


def _experts_kernel(be_ref, bn_ref, nv_ref, x_ref, wg_ref, wu_ref, wd_ref, yout_ref,
                    wgb, wub, wdb, ybuf, dest_vm, dest_sm, sems, dsem, *, dump_row):
    i = pl.program_id(0)
    s = i % 2
    rows = ybuf.shape[1] // Y_SUBLANES
    prev = be_ref[jnp.maximum(i - 1, 0)]
    active = i < nv_ref[0]

    def to_dump(slot):
        def body(r, _):
            dest_sm[slot, 0, r] = dump_row + r
            return 0
        lax.fori_loop(0, rows, body, 0)

    def send(slot, r, queue=0):
        _tile_copy(ybuf.at[slot], r, yout_ref, dest_sm[slot, 0, r], sems.at[slot],
                   Y_SUBLANES).start(priority=queue)

    def wait(slot):
        pltpu.make_async_copy(ybuf.at[slot], ybuf.at[slot], sems.at[slot]).wait()

    @pl.when(i == 0)
    def _():
        ybuf[1] = jnp.zeros(ybuf.shape[1:], jnp.uint32)
        to_dump(1)

    @pl.when(i > 0)
    def _():
        wait(s)

    @pl.when(active & ((i == 0) | (be_ref[i] != prev)))
    def _():
        wgb[...] = wg_ref[0].astype(BF16)
        wub[...] = wu_ref[0].astype(BF16)
        wdb[...] = wd_ref[0].astype(BF16)

    @pl.when(active)
    def _():
        for r in range(rows):
            send(1 - s, r, r % 2)
        x = _unpack_bf16_pairs([_tile_rows(x_ref, t, rows, X_SUBLANES)[...]
                                for t in range(META_SUBLANE)]).astype(BF16)
        meta = lax.bitcast_convert_type(_tile_rows(x_ref, META_SUBLANE, rows, X_SUBLANES)[...], F32)
        gt = _dot(x, wgb[...])
        up = _dot(x, wub[...])
        hid = (gt * _sigmoid(gt) * up).astype(BF16)
        y = _pack_bf16_pairs(_dot(hid, wdb[...]) * meta[:, META_W:META_W + 1])
        for t in range(Y_SUBLANES):
            _tile_rows(ybuf.at[s], t, rows, Y_SUBLANES)[...] = y[:, t * LANES:(t + 1) * LANES]
        row = lax.broadcasted_iota(jnp.int32, (rows, LANES), 0)
        dest = jnp.where(row < bn_ref[i], meta, (dump_row + row).astype(F32))
        _lanes_to_smem(dest, dest_vm, dest_sm.at[s], dsem)

    @pl.when(jnp.logical_not(active))
    def _():
        lax.fori_loop(0, rows, lambda r, _: send(1 - s, r) or 0, 0, unroll=ROW_DMA_UNROLL)
        to_dump(s)

    @pl.when(i == pl.num_programs(0) - 1)
    def _():
        wait(1 - s)


def _experts(blk_e, blk_n, n_valid, xin, w_gate, w_up, w_down, n_out_rows):
    rows = EXPERT_ROWS
    nblk = xin.shape[0] // (rows * X_SUBLANES)
    d, de = w_gate.shape[1], w_gate.shape[2]
    assert d == 2 * Y_SUBLANES * LANES, "an output row is 4 sublanes of packed bf16 pairs"
    grid_spec = pltpu.PrefetchScalarGridSpec(
        num_scalar_prefetch=3,
        grid=(nblk + 1,),
        in_specs=[
            pl.BlockSpec((rows * X_SUBLANES, LANES),
                         lambda i, be, bn, nv: (jnp.minimum(i, nblk - 1), 0)),
            pl.BlockSpec((1, d, de), lambda i, be, bn, nv: (be[i], 0, 0)),
            pl.BlockSpec((1, d, de), lambda i, be, bn, nv: (be[i], 0, 0)),
            pl.BlockSpec((1, de, d), lambda i, be, bn, nv: (be[i], 0, 0)),
        ],
        out_specs=pl.BlockSpec(memory_space=pl.ANY),
        scratch_shapes=[pltpu.VMEM((d, de), BF16), pltpu.VMEM((d, de), BF16),
                        pltpu.VMEM((de, d), BF16),
                        pltpu.VMEM((2, rows * Y_SUBLANES, LANES), jnp.uint32),
                        pltpu.VMEM((8, rows), jnp.int32),
                        pltpu.SMEM((2, 8, rows), jnp.int32),
                        pltpu.SemaphoreType.DMA((2,)),
                        pltpu.SemaphoreType.DMA(())],
    )
    return pl.pallas_call(
        functools.partial(_experts_kernel, dump_row=n_out_rows),
        grid_spec=grid_spec,
        out_shape=jax.ShapeDtypeStruct(((n_out_rows + rows) * Y_SUBLANES, LANES), jnp.uint32),
        compiler_params=pltpu.CompilerParams(
            dimension_semantics=("arbitrary",), vmem_limit_bytes=VMEM_LIMIT),
        name="experts",
    )(blk_e, blk_n, n_valid, xin, w_gate, w_up, w_down)


def _combine_kernel(h1_ref, y0_ref, y1_ref, p_ref, wpg_ref, wpp_ref,
                    gple_ref, gpost_ref, gfin_ref, out_ref):
    tm = h1_ref.shape[0]
    y0, y1 = (_unpack_bf16_pairs([_tile_rows(ref, s, tm, Y_SUBLANES)[...] for s in range(Y_SUBLANES)])
              for ref in (y0_ref, y1_ref))
    h2 = h1_ref[...] + (y0 + y1)
    gate = _sigmoid(_dot(_rms(h2, gple_ref[...]).astype(BF16), wpg_ref[...]))
    ple = _rms(_dot(p_ref[...].astype(BF16), wpp_ref[...]), gpost_ref[...])
    h3 = h2 + gate * ple
    out_ref[...] = _rms(h3, gfin_ref[...])


def _combine(h1, p2, y, w_pg, w_pp, g_ple, g_post, g_final, tm):
    t, d = h1.shape
    pd = p2.shape[1]
    row = lambda i: (i, 0)
    const = lambda i: (0, 0)
    return pl.pallas_call(
        _combine_kernel,
        grid=(t // tm,),
        in_specs=[
            pl.BlockSpec((tm, d), row),
            pl.BlockSpec((tm * Y_SUBLANES, LANES), lambda i: (i, 0)),
            pl.BlockSpec((tm * Y_SUBLANES, LANES), lambda i: (t // tm + i, 0)),
            pl.BlockSpec((tm, pd), row),
            pl.BlockSpec((d, d), const),
            pl.BlockSpec((pd, d), const),
            pl.BlockSpec((1, d), const),
            pl.BlockSpec((1, d), const),
            pl.BlockSpec((1, d), const),
        ],
        out_specs=pl.BlockSpec((tm, d), row),
        out_shape=jax.ShapeDtypeStruct((t, d), F32),
        compiler_params=pltpu.CompilerParams(
            dimension_semantics=("arbitrary",), vmem_limit_bytes=VMEM_LIMIT),
        name="combine",
    )(h1, y, y, p2, w_pg.astype(BF16), w_pp.astype(BF16),
      g_ple.reshape(1, d), g_post.reshape(1, d), g_final.reshape(1, d))


def _largest_tile(n, cap):
    tile = cap
    while n % tile:
        tile //= 2
    return tile


def kernel(x, p, g_mix, w_in, b_gates, conv_q, conv_k, g_mhead, w_out, g_ffn, w_router_group,
           b_router_group, w_router_expert, b_router_expert, w_exp_gate, w_exp_up, w_exp_down,
           g_ple, w_ple_gate, w_ple_proj, g_ple_post, g_final):
    batch, seq_len, d = x.shape
    t = batch * seq_len
    tm = _largest_tile(seq_len, 512)
    n_exp = w_router_expert.shape[-1]
    rows = EXPERT_ROWS
    nblk = t * TOP_K // rows + n_exp

    assert w_in.shape[0] == 1, "single-layer block"
    l = 0
    h = x.reshape(t, d)
    mq, mk, mv, mo, gates, sq, sk, sv = _inproj(
        h, g_mix[l], w_in[l], conv_q[l], conv_k[l], seq_len, _largest_tile(seq_len, 1024))
    hm = _mlstm(mq, mk, mv, mo, gates, b_gates[l], g_mhead[l], batch, seq_len,
                _largest_tile(seq_len, 1024))
    hs = _stickbreak(sq, sk, sv, batch, seq_len)
    h1, c, route, counts = _outroute(
        h, hm, hs, w_out[l], g_ffn[l], w_router_group[l], b_router_group[l],
        w_router_expert[l], b_router_expert[l], tm)

    cnt = counts[0, LOGIT_LANE_E:LOGIT_LANE_E + n_exp].astype(jnp.int32)
    nb_e = (cnt + rows - 1) // rows
    cum = jnp.cumsum(nb_e)
    offs = (cum - nb_e) * rows
    offs_row = jnp.zeros((1, LANES), F32).at[0, 0:n_exp].set(offs.astype(F32))
    n_valid = cum[-1:]
    step = jnp.arange(nblk + 1, dtype=jnp.int32)
    blk_e = jnp.minimum(jnp.sum(cum[None, :] <= step[:, None], axis=1), n_exp - 1).astype(jnp.int32)
    mine = blk_e[:, None] == jnp.arange(n_exp, dtype=jnp.int32)[None, :]
    first = jnp.sum(jnp.where(mine, (cum - nb_e)[None, :], 0), axis=1)
    blk_n = jnp.clip(jnp.sum(jnp.where(mine, cnt[None, :], 0), axis=1) - rows * (step - first), 0, rows)
    tail = n_valid + jnp.arange(n_exp, dtype=jnp.int32)
    zero_blocks = jnp.concatenate([jnp.maximum(cum - 1, 0), jnp.minimum(tail, nblk - 1)])
    zero_use = jnp.concatenate([nb_e > 0, tail < nblk]).astype(jnp.int32)

    pos = _slotpos(route, offs_row, _largest_tile(t, 2048))
    xin = _dispatch(zero_blocks, zero_use, pos, c, route, nblk, _largest_tile(seq_len, 256))
    y = _experts(blk_e, blk_n, n_valid, xin, w_exp_gate[l], w_exp_up[l], w_exp_down[l], TOP_K * t)
    out = _combine(h1, p[l].reshape(t, -1), y, w_ple_gate[l], w_ple_proj[l],
                   g_ple[l], g_ple_post[l], g_final, tm)
    return out.reshape(batch, seq_len, d)
```

```python
import functools

import jax
import jax.numpy as jnp
from jax import lax
from jax.experimental import pallas as pl
from jax.experimental.pallas import tpu as pltpu

F32 = jnp.float32
BF16 = jnp.bfloat16
EPS = 1e-6

M_HEADS = 4
M_HEAD_DIM = 128
SB_HEAD_DIM = 64
CONV_WIDTH = 4
TOP_K = 2
LANES = 128
VMEM_LIMIT = 56 * 1024 * 1024

MLSTM_CHUNK = 128
SB_BLOCK = 256
SB_ZERO_LOG = -105.0
EXPERT_ROWS = 512
ROW_DMA_UNROLL = 8
X_SUBLANES = 5
Y_SUBLANES = 4
ROUTE_LANE_E = 0
ROUTE_LANE_W = 2
ROUTE_LANE_R = 4
SUBLANES = 8
META_SUBLANE = 4
META_DEST = 0
META_W = 1
LOGIT_LANE_E = 4


def _rms(x, g):
    return x * lax.rsqrt(jnp.mean(x * x, axis=-1, keepdims=True) + EPS) * g


def _sigmoid(x):
    return 1.0 / (1.0 + jnp.exp(-x))


def _split3(a):
    a1 = a.astype(BF16)
    r1 = a - a1.astype(F32)
    a2 = r1.astype(BF16)
    a3 = (r1 - a2.astype(F32)).astype(BF16)
    return a1, a2, a3


def _dot(a, b):
    return jnp.dot(a, b, preferred_element_type=F32)


def _dot_nt(a, b):
    return lax.dot_general(a, b, (((1,), (1,)), ((), ())), preferred_element_type=F32)


def _dot_tn(a, b):
    return lax.dot_general(a, b, (((0,), (0,)), ((), ())), preferred_element_type=F32)


def _inproj_kernel(x_ref, g_ref, wqk_ref, wvo_ref, wvt_ref, wg_ref, ws_ref, cq_ref, ck_ref,
                   mq_ref, mk_ref, mv_ref, mo_ref, gate_ref, sq_ref, sk_ref, sv_ref,
                   ext_ref, *, tiles_per_seq, k_scale):
    i = pl.program_id(0)
    tm = x_ref.shape[0]
    mw = mk_ref.shape[1]
    sw = sq_ref.shape[1]
    a = _rms(x_ref[...], g_ref[...]).astype(BF16)

    @pl.when(i % tiles_per_seq == 0)
    def _():
        ext_ref[0:8, :] = jnp.zeros((8, 2 * mw), F32)

    ext_ref[8:8 + tm, 0:mw] = _dot(a, wqk_ref[:, 0:mw])
    ext_ref[8:8 + tm, mw:2 * mw] = _dot(a, wqk_ref[:, mw:2 * mw])

    def conv_silu(w_ref, c0):
        acc = ext_ref[pl.ds(8 - (CONV_WIDTH - 1), tm), c0:c0 + mw] * w_ref[0:1, :]
        for j in range(1, CONV_WIDTH):
            acc = acc + ext_ref[pl.ds(8 - (CONV_WIDTH - 1) + j, tm), c0:c0 + mw] * w_ref[j:j + 1, :]
        return acc * _sigmoid(acc)

    mq_ref[...] = conv_silu(cq_ref, 0).T.astype(BF16)
    mk_ref[...] = (conv_silu(ck_ref, mw) * k_scale).astype(BF16)
    ext_ref[0:8, :] = ext_ref[tm:tm + 8, :]

    mv_ref[...] = _dot_nt(wvt_ref[...], a).astype(BF16)
    mo_ref[...] = _dot(a, wvo_ref[:, mw:2 * mw]).astype(BF16)
    gate_ref[...] = _dot(a, wg_ref[...])
    sq_ref[...] = _dot(a, ws_ref[:, 0:sw]).astype(BF16)
    sk_ref[...] = _dot(a, ws_ref[:, sw:2 * sw]).astype(BF16)
    sv_ref[...] = _dot(a, ws_ref[:, 2 * sw:3 * sw]).astype(BF16)


def _inproj(x2, g_mix, w_in, conv_q, conv_k, seq_len, tm):
    t, d = x2.shape
    mw = conv_q.shape[1]
    h = M_HEADS
    sw = (w_in.shape[1] - 4 * mw - 2 * h) // 3
    wqk = w_in[:, 0:2 * mw].astype(BF16)
    wvo = w_in[:, 2 * mw:4 * mw].astype(BF16)
    wvt = w_in[:, 2 * mw:3 * mw].T.astype(BF16)
    wg = jnp.zeros((d, 2 * LANES), F32)
    wg = wg.at[:, 0:h].set(w_in[:, 4 * mw:4 * mw + h])
    wg = wg.at[:, LANES:LANES + h].set(w_in[:, 4 * mw + h:4 * mw + 2 * h]).astype(BF16)
    ws = w_in[:, 4 * mw + 2 * h:]
    ws = jnp.concatenate([ws[:, 0:sw] * (SB_HEAD_DIM ** -0.5), ws[:, sw:]], axis=1).astype(BF16)
    row = lambda i: (i, 0)
    const = lambda i: (0, 0)
    kern = functools.partial(_inproj_kernel, tiles_per_seq=seq_len // tm, k_scale=M_HEAD_DIM ** -0.5)
    bf = lambda w: jax.ShapeDtypeStruct((t, w), BF16)
    bft = jax.ShapeDtypeStruct((mw, t), BF16)
    col = lambda i: (0, i)
    return pl.pallas_call(
        kern,
        grid=(t // tm,),
        in_specs=[
            pl.BlockSpec((tm, d), row),
            pl.BlockSpec((1, d), const),
            pl.BlockSpec((d, 2 * mw), const),
            pl.BlockSpec((d, 2 * mw), const),
            pl.BlockSpec((mw, d), const),
            pl.BlockSpec((d, 2 * LANES), const),
            pl.BlockSpec((d, 3 * sw), const),
            pl.BlockSpec((CONV_WIDTH, mw), const),
            pl.BlockSpec((CONV_WIDTH, mw), const),
        ],
        out_specs=[
            pl.BlockSpec((mw, tm), col), pl.BlockSpec((tm, mw), row),
            pl.BlockSpec((mw, tm), col), pl.BlockSpec((tm, mw), row),
            pl.BlockSpec((tm, 2 * LANES), row),
            pl.BlockSpec((tm, sw), row), pl.BlockSpec((tm, sw), row), pl.BlockSpec((tm, sw), row),
        ],
        out_shape=[bft, bf(mw), bft, bf(mw),
                   jax.ShapeDtypeStruct((t, 2 * LANES), F32), bf(sw), bf(sw), bf(sw)],
        scratch_shapes=[pltpu.VMEM((tm + 8, 2 * mw), F32)],
        compiler_params=pltpu.CompilerParams(
            dimension_semantics=("arbitrary",), vmem_limit_bytes=VMEM_LIMIT),
        name="inproj",
    )(x2, g_mix.reshape(1, d), wqk, wvo, wvt, wg, ws, conv_q, conv_k)


def _mlstm_kernel(q_ref, k_ref, v_ref, o_ref, gate_ref, bias_ref, gh_ref, out_ref,
                  c_ref, m_ref, *, chunk):
    L = chunk
    hd = M_HEAD_DIM
    nchunks = k_ref.shape[0] // L

    @pl.when(pl.program_id(1) == 0)
    def _():
        c_ref[...] = jnp.zeros(c_ref.shape, F32)
        m_ref[...] = jnp.zeros(m_ref.shape, F32)

    rows = lax.broadcasted_iota(jnp.int32, (L, L), 0)
    cols = lax.broadcasted_iota(jnp.int32, (L, L), 1)
    tri = (cols <= rows).astype(BF16)
    seen = rows <= cols
    ones_rows = (lax.broadcasted_iota(jnp.int32, (hd, L), 0) == 0).astype(BF16)

    def chunk_body(c, _):
        r0 = pl.multiple_of(c * L, L)
        g = gate_ref[pl.ds(r0, L), :] + bias_ref[...]
        gi = g[:, 0:LANES]
        gf = g[:, LANES:2 * LANES]
        lf = jnp.minimum(gf, 0.0) - jnp.log(1.0 + jnp.exp(-jnp.abs(gf)))
        l1, l2, l3 = _split3(lf)
        b = _dot(tri, l1) + _dot(tri, l2) + _dot(tri, l3)
        b_last = b[L - 1:L, :]
        w_end = b_last - b + gi
        m_loc = jnp.max(w_end, axis=0, keepdims=True)
        m_prev = m_ref[...]
        m_new = jnp.maximum(b_last + m_prev, m_loc)
        decay = jnp.exp(b_last + m_prev - m_new)
        scale = jnp.exp(m_loc - m_new)
        gmb = gi - b
        b_t = b.T
        e_end_t = jnp.exp(w_end - m_loc).T
        for h in range(M_HEADS):
            hs = slice(h * hd, (h + 1) * hd)
            qt = q_ref[hs, pl.ds(r0, L)]
            kh = k_ref[pl.ds(r0, L), hs]
            vext = jnp.concatenate([v_ref[hs, pl.ds(r0, L)], ones_rows], axis=0)
            b_row = b_t[h:h + 1, :]
            e = jnp.where(seen, b_row + gmb[:, h:h + 1], -jnp.inf)
            log_inter = b_row + m_prev[:, h:h + 1]
            m_t = jnp.maximum(log_inter, jnp.max(e, axis=0, keepdims=True))
            w = (jnp.exp(e - m_t) * _dot(kh, qt)).astype(BF16)
            a_int = jnp.exp(log_inter - m_t)
            cext = c_ref[h]
            num = _dot(vext, w) + a_int * _dot(cext.astype(BF16), qt)
            den = num[hd:hd + 1, :]
            hh = num[0:hd, :] / jnp.maximum(jnp.abs(den), jnp.exp(-m_t))
            hh = hh * lax.rsqrt(jnp.mean(hh * hh, axis=0, keepdims=True) + EPS) * gh_ref[hs, :]
            og = _sigmoid(o_ref[pl.ds(r0, L), hs].astype(F32))
            out_ref[pl.ds(r0, L), hs] = (og * hh.T).astype(BF16)
            ev = (vext.astype(F32) * e_end_t[h:h + 1, :]).astype(BF16)
            c_ref[h] = decay[:, h:h + 1] * cext + scale[:, h:h + 1] * _dot(ev, kh)
        m_ref[...] = m_new
        return 0

    lax.fori_loop(0, nchunks, chunk_body, 0)


def _mlstm(mq, mk, mv, mo, gates, b_gates, g_mhead, batch, seq_len, rows):
    t, mw = mk.shape
    h = M_HEADS
    bias = jnp.zeros((1, 2 * LANES), F32)
    bias = bias.at[0, 0:h].set(b_gates[0:h]).at[0, LANES:LANES + h].set(b_gates[h:2 * h])
    nb = seq_len // rows
    row = lambda b, i: (b * nb + i, 0)
    col = lambda b, i: (0, b * nb + i)
    const = lambda b, i: (0, 0)
    timed = pl.BlockSpec((mw, rows), col)
    rowed = pl.BlockSpec((rows, mw), row)
    return pl.pallas_call(
        functools.partial(_mlstm_kernel, chunk=MLSTM_CHUNK),
        grid=(batch, nb),
        in_specs=[timed, rowed, timed, rowed,
                  pl.BlockSpec((rows, 2 * LANES), row),
                  pl.BlockSpec((1, 2 * LANES), const),
                  pl.BlockSpec((mw, MLSTM_CHUNK), const)],
        out_specs=rowed,
        out_shape=jax.ShapeDtypeStruct((t, mw), BF16),
        scratch_shapes=[pltpu.VMEM((h, 2 * M_HEAD_DIM, M_HEAD_DIM), F32),
                        pltpu.VMEM((1, LANES), F32)],
        compiler_params=pltpu.CompilerParams(
            dimension_semantics=("arbitrary", "arbitrary"), vmem_limit_bytes=VMEM_LIMIT),
        name="mlstm",
    )(mq, mk, mv, mo, gates, bias,
      jnp.broadcast_to(g_mhead.reshape(mw, 1), (mw, MLSTM_CHUNK)))


def _sb_kernel(q_ref, k_ref, v_ref, out_ref, acc_ref, carry_ref, z_ref, sp_ref):
    blk = SB_BLOCK
    nq = q_ref.shape[0] // blk
    lane = lax.broadcasted_iota(jnp.int32, (blk, LANES), 1)
    head0 = lane < SB_HEAD_DIM
    rows = lax.broadcasted_iota(jnp.int32, (blk, blk), 0)
    cols = lax.broadcasted_iota(jnp.int32, (blk, blk), 1)
    strict = cols < rows
    neg_suffix = jnp.where(rows >= cols, -1.0, 0.0).astype(BF16)

    def split_heads(x):
        zero = jnp.zeros_like(x)
        return [jnp.where(head0, x, zero), jnp.where(head0, zero, x)]

    def rows_of(ref, j):
        return ref[pl.ds(pl.multiple_of(j * blk, blk), blk), :]

    def scores(qm_h, kb, mask):
        z = _dot_nt(qm_h, kb)
        neg_abs = lax.bitcast_convert_type(
            lax.bitcast_convert_type(z, jnp.uint32) | jnp.uint32(0x80000000), F32)
        sp = jnp.maximum(z, 0.0) + jnp.log(1.0 + jnp.exp(neg_abs))
        if mask is not None:
            sp = jnp.where(mask, sp, 0.0)
        return z, sp.astype(BF16)

    def weights(z, sp, carry, vm_h, mask):
        rc = _dot(sp, neg_suffix)
        p = jnp.exp(z + rc + carry)
        if mask is not None:
            p = jnp.where(mask, p, 0.0)
        return _dot(p.astype(BF16), vm_h), rc[:, 0:1]

    def first_half(qi, slot):
        qm = split_heads(rows_of(q_ref, qi))
        for n, (j, mask) in enumerate(((qi, strict), (jnp.maximum(qi - 1, 0), None))):
            kb = rows_of(k_ref, j)
            for h in range(2):
                z_ref[slot, 2 * n + h], sp_ref[slot, 2 * n + h] = scores(qm[h], kb, mask)

    def second_half(qi, slot):
        carries = [jnp.zeros((blk, 1), F32)] * 2
        upd = None
        for n, (j, mask, live) in enumerate(((qi, strict, None), (jnp.maximum(qi - 1, 0), None, qi > 0))):
            vm = split_heads(rows_of(v_ref, j))
            if live is not None:
                vm = [jnp.where(live, v, jnp.zeros_like(v)) for v in vm]
            new = []
            for h in range(2):
                d, total = weights(z_ref[slot, 2 * n + h], sp_ref[slot, 2 * n + h], carries[h], vm[h], mask)
                carry = carries[h] + total
                new.append(carry if live is None else jnp.where(live, carry, carries[h]))
                upd = d if upd is None else upd + d
            carries = new
        acc_ref[...] = upd
        carry_ref[0] = carries[0]
        carry_ref[1] = carries[1]
        return jnp.maximum(jnp.max(carries[0]), jnp.max(carries[1]))

    def remaining(qi, top):
        def cond(state):
            it, top = state
            return (it < qi) & (top > SB_ZERO_LOG)

        def body(state):
            it, _ = state
            j = qi - 1 - it
            qm = split_heads(rows_of(q_ref, qi))
            kb = rows_of(k_ref, j)
            vm = split_heads(rows_of(v_ref, j))
            upd = None
            tops = []
            for h in range(2):
                z, sp = scores(qm[h], kb, None)
                d, total = weights(z, sp, carry_ref[h], vm[h], None)
                carry_ref[h] = carry_ref[h] + total
                tops.append(jnp.max(carry_ref[h]))
                upd = d if upd is None else upd + d
            acc_ref[...] += upd
            return it + 1, jnp.maximum(tops[0], tops[1])

        lax.while_loop(cond, body, (jnp.int32(1), top))

    first_half(0, 0)

    def query_block(qi, _):
        slot = qi % 2
        top = second_half(qi, slot)
        first_half(jnp.minimum(qi + 1, nq - 1), 1 - slot)
        remaining(qi, top)
        out_ref[pl.ds(pl.multiple_of(qi * blk, blk), blk), :] = acc_ref[...].astype(BF16)
        return 0

    lax.fori_loop(0, nq, query_block, 0)


def _stickbreak(sq, sk, sv, batch, seq_len):
    t, sw = sq.shape
    npair = sw // LANES
    seq = pl.BlockSpec((seq_len, LANES), lambda b, hp: (b, hp))
    return pl.pallas_call(
        _sb_kernel,
        grid=(batch, npair),
        in_specs=[seq, seq, seq],
        out_specs=seq,
        out_shape=jax.ShapeDtypeStruct((t, sw), BF16),
        scratch_shapes=[pltpu.VMEM((SB_BLOCK, LANES), F32), pltpu.VMEM((2, SB_BLOCK, 1), F32),
                        pltpu.VMEM((2, 4, SB_BLOCK, SB_BLOCK), F32),
                        pltpu.VMEM((2, 4, SB_BLOCK, SB_BLOCK), BF16)],
        compiler_params=pltpu.CompilerParams(
            dimension_semantics=("arbitrary", "arbitrary"), vmem_limit_bytes=VMEM_LIMIT),
        name="stickbrk",
    )(sq, sk, sv)


def _outroute_kernel(x_ref, hm_ref, hs_ref, wom_ref, wos_ref, g_ref, wr_ref, br_ref,
                     h1_ref, c_ref, route_ref, cnt_ref, run_ref, *, n_groups, per_group):
    i = pl.program_id(0)
    tm = x_ref.shape[0]

    @pl.when(i == 0)
    def _():
        run_ref[...] = jnp.zeros(run_ref.shape, F32)

    h1 = x_ref[...] + _dot(hm_ref[...], wom_ref[...]) + _dot(hs_ref[...], wos_ref[...])
    h1_ref[...] = h1
    c = _rms(h1, g_ref[...])
    c_ref[...] = c

    c1, c2, _ = _split3(c)
    pa = _dot(c1, wr_ref[...])
    pb = _dot(c2, wr_ref[...])
    logits = (pa[:, 0:LANES] + (pa[:, LANES:] + pb[:, 0:LANES]) + pb[:, LANES:]) + br_ref[...]

    lane = lax.broadcasted_iota(jnp.int32, (tm, LANES), 1).astype(F32)
    ninf = -jnp.inf
    big = float(LANES)

    def first_max(v):
        mx = jnp.max(v, axis=1, keepdims=True)
        idx = jnp.min(jnp.where(v == mx, lane, big), axis=1, keepdims=True)
        return mx, idx

    gl = jnp.where(lane < n_groups, logits, ninf)
    gmax, gsel = first_max(gl)
    p_g = 1.0 / jnp.sum(jnp.exp(gl - gmax), axis=1, keepdims=True)
    lo = LOGIT_LANE_E + per_group * gsel
    el = jnp.where((lane >= lo) & (lane < lo + per_group), logits, ninf)
    v1, i1 = first_max(el)
    v2, i2 = first_max(jnp.where(lane == i1, ninf, el))
    tt = jnp.exp(v2 - v1)
    w0 = p_g / (1.0 + tt)
    w1_ = p_g * tt / (1.0 + tt)

    oh0 = lane == i1
    oh1 = lane == i2
    ohsum = oh0.astype(F32) + oh1.astype(F32)
    rows = lax.broadcasted_iota(jnp.int32, (tm, tm), 0)
    cols = lax.broadcasted_iota(jnp.int32, (tm, tm), 1)
    before = (cols < rows).astype(BF16)
    prefix = _dot(before, ohsum.astype(BF16)) + run_ref[...]
    r0 = jnp.sum(jnp.where(oh0, prefix, 0.0), axis=1, keepdims=True)
    r1 = jnp.sum(jnp.where(oh1, prefix, 0.0), axis=1, keepdims=True)
    run = run_ref[...] + jnp.sum(ohsum, axis=0, keepdims=True)
    run_ref[...] = run
    cnt_ref[...] = jnp.broadcast_to(run, cnt_ref.shape)

    e0 = i1 - LOGIT_LANE_E
    e1 = i2 - LOGIT_LANE_E
    route = jnp.zeros((tm, LANES), F32)
    for ln, val in ((ROUTE_LANE_E, e0), (ROUTE_LANE_E + 1, e1), (ROUTE_LANE_W, w0),
                    (ROUTE_LANE_W + 1, w1_), (ROUTE_LANE_R, r0), (ROUTE_LANE_R + 1, r1)):
        route = jnp.where(lane == ln, val, route)
    route_ref[...] = route


def _tile_rows(ref, s, n, pitch=SUBLANES):
    return ref.at[pl.ds(s, n, stride=pitch), :]


def _tile_copy(src_ref, src_row, dst_ref, dst_row, sem, pitch):
    return pltpu.make_async_copy(src_ref.at[pl.ds(src_row * pitch, pitch), :],
                                 dst_ref.at[pl.ds(dst_row * pitch, pitch), :], sem)


def _lanes_to_smem(vals, vm_ref, sm_ref, sem):
    vm_ref[...] = vals.T[0:8, :].astype(jnp.int32)
    cp = pltpu.make_async_copy(vm_ref, sm_ref, sem)
    cp.start()
    cp.wait()


def _outroute(x2, hm, hs, w_out, g_ffn, w_rg, b_rg, w_re, b_re, tm):
    t, d = x2.shape
    mw = hm.shape[1]
    sw = hs.shape[1]
    n_groups = w_rg.shape[1]
    n_exp = w_re.shape[1]
    wr = jnp.zeros((d, LANES), F32)
    wr = wr.at[:, 0:n_groups].set(w_rg).at[:, LOGIT_LANE_E:LOGIT_LANE_E + n_exp].set(w_re)
    wr_hi, wr_lo, _ = _split3(wr)
    wr2 = jnp.concatenate([wr_hi, wr_lo], axis=1)
    br = jnp.zeros((1, LANES), F32)
    br = br.at[0, 0:n_groups].set(b_rg).at[0, LOGIT_LANE_E:LOGIT_LANE_E + n_exp].set(b_re)
    row = lambda i: (i, 0)
    const = lambda i: (0, 0)
    kern = functools.partial(_outroute_kernel, n_groups=n_groups, per_group=n_exp // n_groups)
    return pl.pallas_call(
        kern,
        grid=(t // tm,),
        in_specs=[
            pl.BlockSpec((tm, d), row),
            pl.BlockSpec((tm, mw), row),
            pl.BlockSpec((tm, sw), row),
            pl.BlockSpec((mw, d), const),
            pl.BlockSpec((sw, d), const),
            pl.BlockSpec((1, d), const),
            pl.BlockSpec((d, 2 * LANES), const),
            pl.BlockSpec((1, LANES), const),
        ],
        out_specs=[
            pl.BlockSpec((tm, d), row),
            pl.BlockSpec((tm, d), row),
            pl.BlockSpec((tm, LANES), row),
            pl.BlockSpec((8, LANES), const),
        ],
        out_shape=[
            jax.ShapeDtypeStruct((t, d), F32),
            jax.ShapeDtypeStruct((t, d), F32),
            jax.ShapeDtypeStruct((t, LANES), F32),
            jax.ShapeDtypeStruct((8, LANES), F32),
        ],
        scratch_shapes=[pltpu.VMEM((1, LANES), F32)],
        compiler_params=pltpu.CompilerParams(
            dimension_semantics=("arbitrary",), vmem_limit_bytes=VMEM_LIMIT),
        name="outroute",
    )(x2, hm, hs, w_out[0:mw].astype(BF16), w_out[mw:].astype(BF16), g_ffn.reshape(1, d),
      wr2, br)


def _slotpos_kernel(route_ref, offs_ref, pos_ref):
    route = route_ref[...]
    tm = route.shape[0]
    lane = lax.broadcasted_iota(jnp.int32, (tm, LANES), 1)
    offs = offs_ref[...]
    out = jnp.zeros((tm, LANES), F32)
    for j in range(TOP_K):
        e = route[:, ROUTE_LANE_E + j:ROUTE_LANE_E + j + 1].astype(jnp.int32)
        base = jnp.sum(jnp.where(lane == e, offs, 0.0), axis=1, keepdims=True)
        out = jnp.where(lane == j, base + route[:, ROUTE_LANE_R + j:ROUTE_LANE_R + j + 1], out)
    pos_ref[...] = out.T[0:SUBLANES, :].astype(jnp.int32)


def _slotpos(route, offs_row, tm):
    t = route.shape[0]
    return pl.pallas_call(
        _slotpos_kernel,
        grid=(t // tm,),
        in_specs=[pl.BlockSpec((tm, LANES), lambda i: (i, 0)),
                  pl.BlockSpec((1, LANES), lambda i: (0, 0))],
        out_specs=pl.BlockSpec((SUBLANES, tm), lambda i: (0, i)),
        out_shape=jax.ShapeDtypeStruct((SUBLANES, t), jnp.int32),
        compiler_params=pltpu.CompilerParams(dimension_semantics=("arbitrary",)),
        name="slotpos",
    )(route, offs_row)


def _dispatch_kernel(zblk_ref, zuse_ref, pos_ref, c_ref, route_ref, xin_ref, rows_ref, zero_ref,
                     sem, zsem, *, n_tokens):
    i = pl.program_id(0)
    tm, d = c_ref.shape

    @pl.when(i == 0)
    def _():
        zero_ref[...] = jnp.zeros(zero_ref.shape, jnp.uint32)
        n = zero_ref.shape[0]

        def zero_block(k):
            return pltpu.make_async_copy(
                zero_ref, xin_ref.at[pl.ds(pl.multiple_of(zblk_ref[k] * n, n), n), :], zsem)

        for k in range(zblk_ref.shape[0]):
            pl.when(zuse_ref[k] != 0)(lambda k=k: zero_block(k).start())
        for k in range(zblk_ref.shape[0]):
            pl.when(zuse_ref[k] != 0)(lambda k=k: zero_block(k).wait())

    half = d // 2
    lane = lax.broadcasted_iota(jnp.int32, (tm, LANES), 1)
    row_id = (i * tm + lax.broadcasted_iota(jnp.int32, (tm, 1), 0)).astype(F32)
    route = route_ref[...]
    packed = _pack_bf16_pairs(c_ref[...])
    for j in range(TOP_K):
        w = route[:, ROUTE_LANE_W + j:ROUTE_LANE_W + j + 1]
        meta = jnp.where(lane == META_DEST, row_id + j * n_tokens, jnp.where(lane == META_W, w, 0.0))
        for s in range(X_SUBLANES):
            if s < half // LANES:
                sub = packed[:, s * LANES:(s + 1) * LANES]
            else:
                sub = lax.bitcast_convert_type(meta, jnp.uint32)
            _tile_rows(rows_ref.at[j], s, tm, X_SUBLANES)[...] = sub

    def issue(t, _):
        for j in range(TOP_K):
            _tile_copy(rows_ref.at[j], t, xin_ref, pos_ref[j, t], sem, X_SUBLANES).start(priority=j)
        return 0

    lax.fori_loop(0, tm, issue, 0, unroll=ROW_DMA_UNROLL)
    for j in range(TOP_K):
        pltpu.make_async_copy(rows_ref.at[j], rows_ref.at[j], sem).wait()


def _dispatch(zero_blocks, zero_use, pos, c, route, n_blocks, tm):
    t, d = c.shape
    assert d // 2 // LANES == META_SUBLANE == X_SUBLANES - 1
    rows = EXPERT_ROWS
    grid_spec = pltpu.PrefetchScalarGridSpec(
        num_scalar_prefetch=2,
        grid=(t // tm,),
        in_specs=[
            pl.BlockSpec((SUBLANES, tm), lambda i, *_: (0, i), memory_space=pltpu.SMEM),
            pl.BlockSpec((tm, d), lambda i, *_: (i, 0)),
            pl.BlockSpec((tm, LANES), lambda i, *_: (i, 0)),
        ],
        out_specs=pl.BlockSpec(memory_space=pl.ANY),
        scratch_shapes=[pltpu.VMEM((TOP_K, tm * X_SUBLANES, LANES), jnp.uint32),
                        pltpu.VMEM((rows * X_SUBLANES, LANES), jnp.uint32),
                        pltpu.SemaphoreType.DMA(()),
                        pltpu.SemaphoreType.DMA(())],
    )
    return pl.pallas_call(
        functools.partial(_dispatch_kernel, n_tokens=t),
        grid_spec=grid_spec,
        out_shape=jax.ShapeDtypeStruct((n_blocks * rows * X_SUBLANES, LANES), jnp.uint32),
        compiler_params=pltpu.CompilerParams(
            dimension_semantics=("arbitrary",), vmem_limit_bytes=VMEM_LIMIT),
        name="dispatch",
    )(zero_blocks, zero_use, pos, c, route)


def _pack_bf16_pairs(v):
    half = v.shape[1] // 2
    bits = lax.bitcast_convert_type(v.astype(BF16).astype(F32), jnp.uint32)
    return (bits[:, 0:half] >> 16) | bits[:, half:]


def _unpack_bf16_pairs(words):
    lo = [lax.bitcast_convert_type(w << 16, F32) for w in words]
    hi = [lax.bitcast_convert_type(w & jnp.uint32(0xFFFF0000), F32) for w in words]
    return jnp.concatenate(lo + hi, axis=1)


def _experts_kernel(be_ref, bn_ref, nv_ref, x_ref, wg_ref, wu_ref, wd_ref, yout_ref,
                    wgb, wub, wdb, ybuf, dest_vm, dest_sm, sems, dsem, *, dump_row):
    i = pl.program_id(0)
    s = i % 2
    rows = ybuf.shape[1] // Y_SUBLANES
    prev = be_ref[jnp.maximum(i - 1, 0)]
    active = i < nv_ref[0]

    def to_dump(slot):
        def body(r, _):
            dest_sm[slot, 0, r] = dump_row + r
            return 0
        lax.fori_loop(0, rows, body, 0)

    def send(slot, r, queue=0):
        _tile_copy(ybuf.at[slot], r, yout_ref, dest_sm[slot, 0, r], sems.at[slot],
                   Y_SUBLANES).start(priority=queue)

    def wait(slot):
        pltpu.make_async_copy(ybuf.at[slot], ybuf.at[slot], sems.at[slot]).wait()

    @pl.when(i == 0)
    def _():
        ybuf[1] = jnp.zeros(ybuf.shape[1:], jnp.uint32)
        to_dump(1)

    @pl.when(i > 0)
    def _():
        wait(s)

    @pl.when(active & ((i == 0) | (be_ref[i] != prev)))
    def _():
        wgb[...] = wg_ref[0].astype(BF16)
        wub[...] = wu_ref[0].astype(BF16)
        wdb[...] = wd_ref[0].astype(BF16)

    @pl.when(active)
    def _():
        for r in range(rows):
            send(1 - s, r, r % 2)
        x = _unpack_bf16_pairs([_tile_rows(x_ref, t, rows, X_SUBLANES)[...]
                                for t in range(META_SUBLANE)]).astype(BF16)
        meta = lax.bitcast_convert_type(_tile_rows(x_ref, META_SUBLANE, rows, X_SUBLANES)[...], F32)
        gt = _dot(x, wgb[...])
        up = _dot(x, wub[...])
        hid = (gt * _sigmoid(gt) * up).astype(BF16)
        y = _pack_bf16_pairs(_dot(hid, wdb[...]) * meta[:, META_W:META_W + 1])
        for t in range(Y_SUBLANES):
            _tile_rows(ybuf.at[s], t, rows, Y_SUBLANES)[...] = y[:, t * LANES:(t + 1) * LANES]
        row = lax.broadcasted_iota(jnp.int32, (rows, LANES), 0)
        dest = jnp.where(row < bn_ref[i], meta, (dump_row + row).astype(F32))
        _lanes_to_smem(dest, dest_vm, dest_sm.at[s], dsem)

    @pl.when(jnp.logical_not(active))
    def _():
        lax.fori_loop(0, rows, lambda r, _: send(1 - s, r) or 0, 0, unroll=ROW_DMA_UNROLL)
        to_dump(s)

    @pl.when(i == pl.num_programs(0) - 1)
    def _():
        wait(1 - s)


def _experts(blk_e, blk_n, n_valid, xin, w_gate, w_up, w_down, n_out_rows):
    rows = EXPERT_ROWS
    nblk = xin.shape[0] // (rows * X_SUBLANES)
    d, de = w_gate.shape[1], w_gate.shape[2]
    assert d == 2 * Y_SUBLANES * LANES, "an output row is 4 sublanes of packed bf16 pairs"
    grid_spec = pltpu.PrefetchScalarGridSpec(
        num_scalar_prefetch=3,
        grid=(nblk + 1,),
        in_specs=[
            pl.BlockSpec((rows * X_SUBLANES, LANES),
                         lambda i, be, bn, nv: (jnp.minimum(i, nblk - 1), 0)),
            pl.BlockSpec((1, d, de), lambda i, be, bn, nv: (be[i], 0, 0)),
            pl.BlockSpec((1, d, de), lambda i, be, bn, nv: (be[i], 0, 0)),
            pl.BlockSpec((1, de, d), lambda i, be, bn, nv: (be[i], 0, 0)),
        ],
        out_specs=pl.BlockSpec(memory_space=pl.ANY),
        scratch_shapes=[pltpu.VMEM((d, de), BF16), pltpu.VMEM((d, de), BF16),
                        pltpu.VMEM((de, d), BF16),
                        pltpu.VMEM((2, rows * Y_SUBLANES, LANES), jnp.uint32),
                        pltpu.VMEM((8, rows), jnp.int32),
                        pltpu.SMEM((2, 8, rows), jnp.int32),
                        pltpu.SemaphoreType.DMA((2,)),
                        pltpu.SemaphoreType.DMA(())],
    )
    return pl.pallas_call(
        functools.partial(_experts_kernel, dump_row=n_out_rows),
        grid_spec=grid_spec,
        out_shape=jax.ShapeDtypeStruct(((n_out_rows + rows) * Y_SUBLANES, LANES), jnp.uint32),
        compiler_params=pltpu.CompilerParams(
            dimension_semantics=("arbitrary",), vmem_limit_bytes=VMEM_LIMIT),
        name="experts",
    )(blk_e, blk_n, n_valid, xin, w_gate, w_up, w_down)


def _combine_kernel(h1_ref, y0_ref, y1_ref, p_ref, wpg_ref, wpp_ref,
                    gple_ref, gpost_ref, gfin_ref, out_ref):
    tm = h1_ref.shape[0]
    y0, y1 = (_unpack_bf16_pairs([_tile_rows(ref, s, tm, Y_SUBLANES)[...] for s in range(Y_SUBLANES)])
              for ref in (y0_ref, y1_ref))
    h2 = h1_ref[...] + (y0 + y1)
    gate = _sigmoid(_dot(_rms(h2, gple_ref[...]).astype(BF16), wpg_ref[...]))
    ple = _rms(_dot(p_ref[...].astype(BF16), wpp_ref[...]), gpost_ref[...])
    h3 = h2 + gate * ple
    out_ref[...] = _rms(h3, gfin_ref[...])


def _combine(h1, p2, y, w_pg, w_pp, g_ple, g_post, g_final, tm):
    t, d = h1.shape
    pd = p2.shape[1]
    row = lambda i: (i, 0)
    const = lambda i: (0, 0)
    return pl.pallas_call(
        _combine_kernel,
        grid=(t // tm,),
        in_specs=[
            pl.BlockSpec((tm, d), row),
            pl.BlockSpec((tm * Y_SUBLANES, LANES), lambda i: (i, 0)),
            pl.BlockSpec((tm * Y_SUBLANES, LANES), lambda i: (t // tm + i, 0)),
            pl.BlockSpec((tm, pd), row),
            pl.BlockSpec((d, d), const),
            pl.BlockSpec((pd, d), const),
            pl.BlockSpec((1, d), const),
            pl.BlockSpec((1, d), const),
            pl.BlockSpec((1, d), const),
        ],
        out_specs=pl.BlockSpec((tm, d), row),
        out_shape=jax.ShapeDtypeStruct((t, d), F32),
        compiler_params=pltpu.CompilerParams(
            dimension_semantics=("arbitrary",), vmem_limit_bytes=VMEM_LIMIT),
        name="combine",
    )(h1, y, y, p2, w_pg.astype(BF16), w_pp.astype(BF16),
      g_ple.reshape(1, d), g_post.reshape(1, d), g_final.reshape(1, d))


def _largest_tile(n, cap):
    tile = cap
    while n % tile:
        tile //= 2
    return tile


def kernel(x, p, g_mix, w_in, b_gates, conv_q, conv_k, g_mhead, w_out, g_ffn, w_router_group,
           b_router_group, w_router_expert, b_router_expert, w_exp_gate, w_exp_up, w_exp_down,
           g_ple, w_ple_gate, w_ple_proj, g_ple_post, g_final):
    batch, seq_len, d = x.shape
    t = batch * seq_len
    tm = _largest_tile(seq_len, 512)
    n_exp = w_router_expert.shape[-1]
    rows = EXPERT_ROWS
    nblk = t * TOP_K // rows + n_exp

    assert w_in.shape[0] == 1, "single-layer block"
    l = 0
    h = x.reshape(t, d)
    mq, mk, mv, mo, gates, sq, sk, sv = _inproj(
        h, g_mix[l], w_in[l], conv_q[l], conv_k[l], seq_len, _largest_tile(seq_len, 1024))
    hm = _mlstm(mq, mk, mv, mo, gates, b_gates[l], g_mhead[l], batch, seq_len,
                _largest_tile(seq_len, 1024))
    hs = _stickbreak(sq, sk, sv, batch, seq_len)
    h1, c, route, counts = _outroute(
        h, hm, hs, w_out[l], g_ffn[l], w_router_group[l], b_router_group[l],
        w_router_expert[l], b_router_expert[l], tm)

    cnt = counts[0, LOGIT_LANE_E:LOGIT_LANE_E + n_exp].astype(jnp.int32)
    nb_e = (cnt + rows - 1) // rows
    cum = jnp.cumsum(nb_e)
    offs = (cum - nb_e) * rows
    offs_row = jnp.zeros((1, LANES), F32).at[0, 0:n_exp].set(offs.astype(F32))
    n_valid = cum[-1:]
    step = jnp.arange(nblk + 1, dtype=jnp.int32)
    blk_e = jnp.minimum(jnp.sum(cum[None, :] <= step[:, None], axis=1), n_exp - 1).astype(jnp.int32)
    mine = blk_e[:, None] == jnp.arange(n_exp, dtype=jnp.int32)[None, :]
    first = jnp.sum(jnp.where(mine, (cum - nb_e)[None, :], 0), axis=1)
    blk_n = jnp.clip(jnp.sum(jnp.where(mine, cnt[None, :], 0), axis=1) - rows * (step - first), 0, rows)
    tail = n_valid + jnp.arange(n_exp, dtype=jnp.int32)
    zero_blocks = jnp.concatenate([jnp.maximum(cum - 1, 0), jnp.minimum(tail, nblk - 1)])
    zero_use = jnp.concatenate([nb_e > 0, tail < nblk]).astype(jnp.int32)

    pos = _slotpos(route, offs_row, _largest_tile(t, 2048))
    xin = _dispatch(zero_blocks, zero_use, pos, c, route, nblk, _largest_tile(seq_len, 256))
    y = _experts(blk_e, blk_n, n_valid, xin, w_exp_gate[l], w_exp_up[l], w_exp_down[l], TOP_K * t)
    out = _combine(h1, p[l].reshape(t, -1), y, w_ple_gate[l], w_ple_proj[l],
                   g_ple[l], g_ple_post[l], g_final, tm)
    return out.reshape(batch, seq_len, d)
```

```python
import functools

import jax
import jax.numpy as jnp
from jax import lax
from jax.experimental import pallas as pl
from jax.experimental.pallas import tpu as pltpu

F32 = jnp.float32
BF16 = jnp.bfloat16
EPS = 1e-6

M_HEADS = 4
M_HEAD_DIM = 128
SB_HEAD_DIM = 64
CONV_WIDTH = 4
TOP_K = 2
LANES = 128
VMEM_LIMIT = 56 * 1024 * 1024

MLSTM_CHUNK = 256
SB_BLOCK = 256
SB_ZERO_LOG = -105.0
EXPERT_ROWS = 512
ROW_DMA_UNROLL = 8
X_SUBLANES = 5
Y_SUBLANES = 4
ROUTE_LANE_E = 0
ROUTE_LANE_W = 2
ROUTE_LANE_R = 4
SUBLANES = 8
META_SUBLANE = 4
META_DEST = 0
META_W = 1
LOGIT_LANE_E = 4


def _rms(x, g):
    return x * lax.rsqrt(jnp.mean(x * x, axis=-1, keepdims=True) + EPS) * g


def _sigmoid(x):
    return 1.0 / (1.0 + jnp.exp(-x))


def _split3(a):
    a1 = a.astype(BF16)
    r1 = a - a1.astype(F32)
    a2 = r1.astype(BF16)
    a3 = (r1 - a2.astype(F32)).astype(BF16)
    return a1, a2, a3


def _dot(a, b):
    return jnp.dot(a, b, preferred_element_type=F32)


def _dot_nt(a, b):
    return lax.dot_general(a, b, (((1,), (1,)), ((), ())), preferred_element_type=F32)


def _dot_tn(a, b):
    return lax.dot_general(a, b, (((0,), (0,)), ((), ())), preferred_element_type=F32)


def _inproj_kernel(x_ref, g_ref, wqk_ref, wvo_ref, wvt_ref, wg_ref, ws_ref, cq_ref, ck_ref,
                   mq_ref, mk_ref, mv_ref, mo_ref, gate_ref, sq_ref, sk_ref, sv_ref,
                   ext_ref, *, tiles_per_seq, k_scale):
    i = pl.program_id(0)
    tm = x_ref.shape[0]
    mw = mk_ref.shape[1]
    sw = sq_ref.shape[1]
    a = _rms(x_ref[...], g_ref[...]).astype(BF16)

    @pl.when(i % tiles_per_seq == 0)
    def _():
        ext_ref[0:8, :] = jnp.zeros((8, 2 * mw), F32)

    ext_ref[8:8 + tm, 0:mw] = _dot(a, wqk_ref[:, 0:mw])
    ext_ref[8:8 + tm, mw:2 * mw] = _dot(a, wqk_ref[:, mw:2 * mw])

    def conv_silu(w_ref, c0):
        acc = ext_ref[pl.ds(8 - (CONV_WIDTH - 1), tm), c0:c0 + mw] * w_ref[0:1, :]
        for j in range(1, CONV_WIDTH):
            acc = acc + ext_ref[pl.ds(8 - (CONV_WIDTH - 1) + j, tm), c0:c0 + mw] * w_ref[j:j + 1, :]
        return acc * _sigmoid(acc)

    mq_ref[...] = conv_silu(cq_ref, 0).T.astype(BF16)
    mk_ref[...] = (conv_silu(ck_ref, mw) * k_scale).astype(BF16)
    ext_ref[0:8, :] = ext_ref[tm:tm + 8, :]

    mv_ref[...] = _dot_nt(wvt_ref[...], a).astype(BF16)
    mo_ref[...] = _dot(a, wvo_ref[:, mw:2 * mw]).astype(BF16)
    gate_ref[...] = _dot(a, wg_ref[...])
    sq_ref[...] = _dot(a, ws_ref[:, 0:sw]).astype(BF16)
    sk_ref[...] = _dot(a, ws_ref[:, sw:2 * sw]).astype(BF16)
    sv_ref[...] = _dot(a, ws_ref[:, 2 * sw:3 * sw]).astype(BF16)


def _inproj(x2, g_mix, w_in, conv_q, conv_k, seq_len, tm):
    t, d = x2.shape
    mw = conv_q.shape[1]
    h = M_HEADS
    sw = (w_in.shape[1] - 4 * mw - 2 * h) // 3
    wqk = w_in[:, 0:2 * mw].astype(BF16)
    wvo = w_in[:, 2 * mw:4 * mw].astype(BF16)
    wvt = w_in[:, 2 * mw:3 * mw].T.astype(BF16)
    wg = jnp.zeros((d, 2 * LANES), F32)
    wg = wg.at[:, 0:h].set(w_in[:, 4 * mw:4 * mw + h])
    wg = wg.at[:, LANES:LANES + h].set(w_in[:, 4 * mw + h:4 * mw + 2 * h]).astype(BF16)
    ws = w_in[:, 4 * mw + 2 * h:]
    ws = jnp.concatenate([ws[:, 0:sw] * (SB_HEAD_DIM ** -0.5), ws[:, sw:]], axis=1).astype(BF16)
    row = lambda i: (i, 0)
    const = lambda i: (0, 0)
    kern = functools.partial(_inproj_kernel, tiles_per_seq=seq_len // tm, k_scale=M_HEAD_DIM ** -0.5)
    bf = lambda w: jax.ShapeDtypeStruct((t, w), BF16)
    bft = jax.ShapeDtypeStruct((mw, t), BF16)
    col = lambda i: (0, i)
    return pl.pallas_call(
        kern,
        grid=(t // tm,),
        in_specs=[
            pl.BlockSpec((tm, d), row),
            pl.BlockSpec((1, d), const),
            pl.BlockSpec((d, 2 * mw), const),
            pl.BlockSpec((d, 2 * mw), const),
            pl.BlockSpec((mw, d), const),
            pl.BlockSpec((d, 2 * LANES), const),
            pl.BlockSpec((d, 3 * sw), const),
            pl.BlockSpec((CONV_WIDTH, mw), const),
            pl.BlockSpec((CONV_WIDTH, mw), const),
        ],
        out_specs=[
            pl.BlockSpec((mw, tm), col), pl.BlockSpec((tm, mw), row),
            pl.BlockSpec((mw, tm), col), pl.BlockSpec((tm, mw), row),
            pl.BlockSpec((tm, 2 * LANES), row),
            pl.BlockSpec((tm, sw), row), pl.BlockSpec((tm, sw), row), pl.BlockSpec((tm, sw), row),
        ],
        out_shape=[bft, bf(mw), bft, bf(mw),
                   jax.ShapeDtypeStruct((t, 2 * LANES), F32), bf(sw), bf(sw), bf(sw)],
        scratch_shapes=[pltpu.VMEM((tm + 8, 2 * mw), F32)],
        compiler_params=pltpu.CompilerParams(
            dimension_semantics=("arbitrary",), vmem_limit_bytes=VMEM_LIMIT),
        name="inproj",
    )(x2, g_mix.reshape(1, d), wqk, wvo, wvt, wg, ws, conv_q, conv_k)


def _mlstm_kernel(q_ref, k_ref, v_ref, o_ref, gate_ref, bias_ref, gh_ref, out_ref,
                  c_ref, m_ref, *, chunk):
    L = chunk
    hd = M_HEAD_DIM
    nchunks = k_ref.shape[0] // L

    @pl.when(pl.program_id(1) == 0)
    def _():
        c_ref[...] = jnp.zeros(c_ref.shape, F32)
        m_ref[...] = jnp.zeros(m_ref.shape, F32)

    rows = lax.broadcasted_iota(jnp.int32, (L, L), 0)
    cols = lax.broadcasted_iota(jnp.int32, (L, L), 1)
    tri = (cols <= rows).astype(BF16)
    seen = rows <= cols
    ones_rows = (lax.broadcasted_iota(jnp.int32, (hd, L), 0) == 0).astype(BF16)

    def chunk_body(c, _):
        r0 = pl.multiple_of(c * L, L)
        g = gate_ref[pl.ds(r0, L), :] + bias_ref[...]
        gi = g[:, 0:LANES]
        gf = g[:, LANES:2 * LANES]
        lf = jnp.minimum(gf, 0.0) - jnp.log(1.0 + jnp.exp(-jnp.abs(gf)))
        l1, l2, l3 = _split3(lf)
        b = _dot(tri, l1) + _dot(tri, l2) + _dot(tri, l3)
        b_last = b[L - 1:L, :]
        w_end = b_last - b + gi
        m_loc = jnp.max(w_end, axis=0, keepdims=True)
        m_prev = m_ref[...]
        m_new = jnp.maximum(b_last + m_prev, m_loc)
        decay = jnp.exp(b_last + m_prev - m_new)
        scale = jnp.exp(m_loc - m_new)
        gmb = gi - b
        b_t = b.T
        e_end_t = jnp.exp(w_end - m_loc).T
        for h in range(M_HEADS):
            hs = slice(h * hd, (h + 1) * hd)
            qt = q_ref[hs, pl.ds(r0, L)]
            kh = k_ref[pl.ds(r0, L), hs]
            vext = jnp.concatenate([v_ref[hs, pl.ds(r0, L)], ones_rows], axis=0)
            b_row = b_t[h:h + 1, :]
            e = jnp.where(seen, b_row + gmb[:, h:h + 1], -jnp.inf)
            log_inter = b_row + m_prev[:, h:h + 1]
            m_t = jnp.maximum(log_inter, jnp.max(e, axis=0, keepdims=True))
            w = (jnp.exp(e - m_t) * _dot(kh, qt)).astype(BF16)
            a_int = jnp.exp(log_inter - m_t)
            cext = c_ref[h]
            num = _dot(vext, w) + a_int * _dot(cext.astype(BF16), qt)
            den = num[hd:hd + 1, :]
            hh = num[0:hd, :] / jnp.maximum(jnp.abs(den), jnp.exp(-m_t))
            hh = hh * lax.rsqrt(jnp.mean(hh * hh, axis=0, keepdims=True) + EPS) * gh_ref[hs, :]
            og = _sigmoid(o_ref[pl.ds(r0, L), hs].astype(F32))
            out_ref[pl.ds(r0, L), hs] = (og * hh.T).astype(BF16)
            ev = (vext.astype(F32) * e_end_t[h:h + 1, :]).astype(BF16)
            c_ref[h] = decay[:, h:h + 1] * cext + scale[:, h:h + 1] * _dot(ev, kh)
        m_ref[...] = m_new
        return 0

    lax.fori_loop(0, nchunks, chunk_body, 0, unroll=2)


def _mlstm(mq, mk, mv, mo, gates, b_gates, g_mhead, batch, seq_len, rows):
    t, mw = mk.shape
    h = M_HEADS
    bias = jnp.zeros((1, 2 * LANES), F32)
    bias = bias.at[0, 0:h].set(b_gates[0:h]).at[0, LANES:LANES + h].set(b_gates[h:2 * h])
    nb = seq_len // rows
    row = lambda b, i: (b * nb + i, 0)
    col = lambda b, i: (0, b * nb + i)
    const = lambda b, i: (0, 0)
    timed = pl.BlockSpec((mw, rows), col)
    rowed = pl.BlockSpec((rows, mw), row)
    return pl.pallas_call(
        functools.partial(_mlstm_kernel, chunk=MLSTM_CHUNK),
        grid=(batch, nb),
        in_specs=[timed, rowed, timed, rowed,
                  pl.BlockSpec((rows, 2 * LANES), row),
                  pl.BlockSpec((1, 2 * LANES), const),
                  pl.BlockSpec((mw, MLSTM_CHUNK), const)],
        out_specs=rowed,
        out_shape=jax.ShapeDtypeStruct((t, mw), BF16),
        scratch_shapes=[pltpu.VMEM((h, 2 * M_HEAD_DIM, M_HEAD_DIM), F32),
                        pltpu.VMEM((1, LANES), F32)],
        compiler_params=pltpu.CompilerParams(
            dimension_semantics=("arbitrary", "arbitrary"), vmem_limit_bytes=VMEM_LIMIT),
        name="mlstm",
    )(mq, mk, mv, mo, gates, bias,
      jnp.broadcast_to(g_mhead.reshape(mw, 1), (mw, MLSTM_CHUNK)))


def _sb_kernel(q_ref, k_ref, v_ref, out_ref, acc_ref, carry_ref, z_ref, sp_ref):
    blk = SB_BLOCK
    nq = q_ref.shape[0] // blk
    lane = lax.broadcasted_iota(jnp.int32, (blk, LANES), 1)
    head0 = lane < SB_HEAD_DIM
    rows = lax.broadcasted_iota(jnp.int32, (blk, blk), 0)
    cols = lax.broadcasted_iota(jnp.int32, (blk, blk), 1)
    strict = cols < rows
    neg_suffix = jnp.where(rows >= cols, -1.0, 0.0).astype(BF16)

    def split_heads(x):
        zero = jnp.zeros_like(x)
        return [jnp.where(head0, x, zero), jnp.where(head0, zero, x)]

    def rows_of(ref, j):
        return ref[pl.ds(pl.multiple_of(j * blk, blk), blk), :]

    def scores(qm_h, kb, mask):
        z = _dot_nt(qm_h, kb)
        neg_abs = lax.bitcast_convert_type(
            lax.bitcast_convert_type(z, jnp.uint32) | jnp.uint32(0x80000000), F32)
        sp = jnp.maximum(z, 0.0) + jnp.log(1.0 + jnp.exp(neg_abs))
        if mask is not None:
            sp = jnp.where(mask, sp, 0.0)
        return z, sp.astype(BF16)

    def weights(z, sp, carry, vm_h, mask):
        rc = _dot(sp, neg_suffix)
        p = jnp.exp(z + rc + carry)
        if mask is not None:
            p = jnp.where(mask, p, 0.0)
        return _dot(p.astype(BF16), vm_h), rc[:, 0:1]

    def first_half(qi, slot):
        qm = split_heads(rows_of(q_ref, qi))
        for n, (j, mask) in enumerate(((qi, strict), (jnp.maximum(qi - 1, 0), None))):
            kb = rows_of(k_ref, j)
            for h in range(2):
                z_ref[slot, 2 * n + h], sp_ref[slot, 2 * n + h] = scores(qm[h], kb, mask)

    def second_half(qi, slot):
        carries = [jnp.zeros((blk, 1), F32)] * 2
        upd = None
        for n, (j, mask, live) in enumerate(((qi, strict, None), (jnp.maximum(qi - 1, 0), None, qi > 0))):
            vm = split_heads(rows_of(v_ref, j))
            if live is not None:
                vm = [jnp.where(live, v, jnp.zeros_like(v)) for v in vm]
            new = []
            for h in range(2):
                d, total = weights(z_ref[slot, 2 * n + h], sp_ref[slot, 2 * n + h], carries[h], vm[h], mask)
                carry = carries[h] + total
                new.append(carry if live is None else jnp.where(live, carry, carries[h]))
                upd = d if upd is None else upd + d
            carries = new
        acc_ref[...] = upd
        carry_ref[0] = carries[0]
        carry_ref[1] = carries[1]
        return jnp.maximum(jnp.max(carries[0]), jnp.max(carries[1]))

    def remaining(qi, top):
        def cond(state):
            it, top = state
            return (it < qi) & (top > SB_ZERO_LOG)

        def body(state):
            it, _ = state
            j = qi - 1 - it
            qm = split_heads(rows_of(q_ref, qi))
            kb = rows_of(k_ref, j)
            vm = split_heads(rows_of(v_ref, j))
            upd = None
            tops = []
            for h in range(2):
                z, sp = scores(qm[h], kb, None)
                d, total = weights(z, sp, carry_ref[h], vm[h], None)
                carry_ref[h] = carry_ref[h] + total
                tops.append(jnp.max(carry_ref[h]))
                upd = d if upd is None else upd + d
            acc_ref[...] += upd
            return it + 1, jnp.maximum(tops[0], tops[1])

        lax.while_loop(cond, body, (jnp.int32(1), top))

    first_half(0, 0)

    def query_block(qi, _):
        slot = qi % 2
        top = second_half(qi, slot)
        first_half(jnp.minimum(qi + 1, nq - 1), 1 - slot)
        remaining(qi, top)
        out_ref[pl.ds(pl.multiple_of(qi * blk, blk), blk), :] = acc_ref[...].astype(BF16)
        return 0

    lax.fori_loop(0, nq, query_block, 0)


def _stickbreak(sq, sk, sv, batch, seq_len):
    t, sw = sq.shape
    npair = sw // LANES
    seq = pl.BlockSpec((seq_len, LANES), lambda b, hp: (b, hp))
    return pl.pallas_call(
        _sb_kernel,
        grid=(batch, npair),
        in_specs=[seq, seq, seq],
        out_specs=seq,
        out_shape=jax.ShapeDtypeStruct((t, sw), BF16),
        scratch_shapes=[pltpu.VMEM((SB_BLOCK, LANES), F32), pltpu.VMEM((2, SB_BLOCK, 1), F32),
                        pltpu.VMEM((2, 4, SB_BLOCK, SB_BLOCK), F32),
                        pltpu.VMEM((2, 4, SB_BLOCK, SB_BLOCK), BF16)],
        compiler_params=pltpu.CompilerParams(
            dimension_semantics=("arbitrary", "arbitrary"), vmem_limit_bytes=VMEM_LIMIT),
        name="stickbrk",
    )(sq, sk, sv)


def _outroute_kernel(x_ref, hm_ref, hs_ref, wom_ref, wos_ref, g_ref, wr_ref, br_ref,
                     h1_ref, c_ref, route_ref, cnt_ref, run_ref, *, n_groups, per_group):
    i = pl.program_id(0)
    tm = x_ref.shape[0]

    @pl.when(i == 0)
    def _():
        run_ref[...] = jnp.zeros(run_ref.shape, F32)

    h1 = x_ref[...] + _dot(hm_ref[...], wom_ref[...]) + _dot(hs_ref[...], wos_ref[...])
    h1_ref[...] = h1
    c = _rms(h1, g_ref[...])
    c_ref[...] = c

    c1, c2, _ = _split3(c)
    pa = _dot(c1, wr_ref[...])
    pb = _dot(c2, wr_ref[...])
    logits = (pa[:, 0:LANES] + (pa[:, LANES:] + pb[:, 0:LANES]) + pb[:, LANES:]) + br_ref[...]

    lane = lax.broadcasted_iota(jnp.int32, (tm, LANES), 1).astype(F32)
    ninf = -jnp.inf
    big = float(LANES)

    def first_max(v):
        mx = jnp.max(v, axis=1, keepdims=True)
        idx = jnp.min(jnp.where(v == mx, lane, big), axis=1, keepdims=True)
        return mx, idx

    gl = jnp.where(lane < n_groups, logits, ninf)
    gmax, gsel = first_max(gl)
    p_g = 1.0 / jnp.sum(jnp.exp(gl - gmax), axis=1, keepdims=True)
    lo = LOGIT_LANE_E + per_group * gsel
    el = jnp.where((lane >= lo) & (lane < lo + per_group), logits, ninf)
    v1, i1 = first_max(el)
    v2, i2 = first_max(jnp.where(lane == i1, ninf, el))
    tt = jnp.exp(v2 - v1)
    w0 = p_g / (1.0 + tt)
    w1_ = p_g * tt / (1.0 + tt)

    oh0 = lane == i1
    oh1 = lane == i2
    ohsum = oh0.astype(F32) + oh1.astype(F32)
    rows = lax.broadcasted_iota(jnp.int32, (tm, tm), 0)
    cols = lax.broadcasted_iota(jnp.int32, (tm, tm), 1)
    before = (cols < rows).astype(BF16)
    prefix = _dot(before, ohsum.astype(BF16)) + run_ref[...]
    r0 = jnp.sum(jnp.where(oh0, prefix, 0.0), axis=1, keepdims=True)
    r1 = jnp.sum(jnp.where(oh1, prefix, 0.0), axis=1, keepdims=True)
    run = run_ref[...] + jnp.sum(ohsum, axis=0, keepdims=True)
    run_ref[...] = run
    cnt_ref[...] = jnp.broadcast_to(run, cnt_ref.shape)

    e0 = i1 - LOGIT_LANE_E
    e1 = i2 - LOGIT_LANE_E
    route = jnp.zeros((tm, LANES), F32)
    for ln, val in ((ROUTE_LANE_E, e0), (ROUTE_LANE_E + 1, e1), (ROUTE_LANE_W, w0),
                    (ROUTE_LANE_W + 1, w1_), (ROUTE_LANE_R, r0), (ROUTE_LANE_R + 1, r1)):
        route = jnp.where(lane == ln, val, route)
    route_ref[...] = route


def _tile_rows(ref, s, n, pitch=SUBLANES):
    return ref.at[pl.ds(s, n, stride=pitch), :]


def _tile_copy(src_ref, src_row, dst_ref, dst_row, sem, pitch):
    return pltpu.make_async_copy(src_ref.at[pl.ds(src_row * pitch, pitch), :],
                                 dst_ref.at[pl.ds(dst_row * pitch, pitch), :], sem)


def _lanes_to_smem(vals, vm_ref, sm_ref, sem):
    vm_ref[...] = vals.T[0:8, :].astype(jnp.int32)
    cp = pltpu.make_async_copy(vm_ref, sm_ref, sem)
    cp.start()
    cp.wait()


def _outroute(x2, hm, hs, w_out, g_ffn, w_rg, b_rg, w_re, b_re, tm):
    t, d = x2.shape
    mw = hm.shape[1]
    sw = hs.shape[1]
    n_groups = w_rg.shape[1]
    n_exp = w_re.shape[1]
    wr = jnp.zeros((d, LANES), F32)
    wr = wr.at[:, 0:n_groups].set(w_rg).at[:, LOGIT_LANE_E:LOGIT_LANE_E + n_exp].set(w_re)
    wr_hi, wr_lo, _ = _split3(wr)
    wr2 = jnp.concatenate([wr_hi, wr_lo], axis=1)
    br = jnp.zeros((1, LANES), F32)
    br = br.at[0, 0:n_groups].set(b_rg).at[0, LOGIT_LANE_E:LOGIT_LANE_E + n_exp].set(b_re)
    row = lambda i: (i, 0)
    const = lambda i: (0, 0)
    kern = functools.partial(_outroute_kernel, n_groups=n_groups, per_group=n_exp // n_groups)
    return pl.pallas_call(
        kern,
        grid=(t // tm,),
        in_specs=[
            pl.BlockSpec((tm, d), row),
            pl.BlockSpec((tm, mw), row),
            pl.BlockSpec((tm, sw), row),
            pl.BlockSpec((mw, d), const),
            pl.BlockSpec((sw, d), const),
            pl.BlockSpec((1, d), const),
            pl.BlockSpec((d, 2 * LANES), const),
            pl.BlockSpec((1, LANES), const),
        ],
        out_specs=[
            pl.BlockSpec((tm, d), row),
            pl.BlockSpec((tm, d), row),
            pl.BlockSpec((tm, LANES), row),
            pl.BlockSpec((8, LANES), const),
        ],
        out_shape=[
            jax.ShapeDtypeStruct((t, d), F32),
            jax.ShapeDtypeStruct((t, d), F32),
            jax.ShapeDtypeStruct((t, LANES), F32),
            jax.ShapeDtypeStruct((8, LANES), F32),
        ],
        scratch_shapes=[pltpu.VMEM((1, LANES), F32)],
        compiler_params=pltpu.CompilerParams(
            dimension_semantics=("arbitrary",), vmem_limit_bytes=VMEM_LIMIT),
        name="outroute",
    )(x2, hm, hs, w_out[0:mw].astype(BF16), w_out[mw:].astype(BF16), g_ffn.reshape(1, d),
      wr2, br)


def _slotpos_kernel(route_ref, offs_ref, pos_ref):
    route = route_ref[...]
    tm = route.shape[0]
    lane = lax.broadcasted_iota(jnp.int32, (tm, LANES), 1)
    offs = offs_ref[...]
    out = jnp.zeros((tm, LANES), F32)
    for j in range(TOP_K):
        e = route[:, ROUTE_LANE_E + j:ROUTE_LANE_E + j + 1].astype(jnp.int32)
        base = jnp.sum(jnp.where(lane == e, offs, 0.0), axis=1, keepdims=True)
        out = jnp.where(lane == j, base + route[:, ROUTE_LANE_R + j:ROUTE_LANE_R + j + 1], out)
    pos_ref[...] = out.T[0:SUBLANES, :].astype(jnp.int32)


def _slotpos(route, offs_row, tm):
    t = route.shape[0]
    return pl.pallas_call(
        _slotpos_kernel,
        grid=(t // tm,),
        in_specs=[pl.BlockSpec((tm, LANES), lambda i: (i, 0)),
                  pl.BlockSpec((1, LANES), lambda i: (0, 0))],
        out_specs=pl.BlockSpec((SUBLANES, tm), lambda i: (0, i)),
        out_shape=jax.ShapeDtypeStruct((SUBLANES, t), jnp.int32),
        compiler_params=pltpu.CompilerParams(dimension_semantics=("arbitrary",)),
        name="slotpos",
    )(route, offs_row)


def _dispatch_kernel(zblk_ref, zuse_ref, pos_ref, c_ref, route_ref, xin_ref, rows_ref, zero_ref,
                     sem, zsem, *, n_tokens):
    i = pl.program_id(0)
    tm, d = c_ref.shape

    @pl.when(i == 0)
    def _():
        zero_ref[...] = jnp.zeros(zero_ref.shape, jnp.uint32)
        n = zero_ref.shape[0]

        def zero_block(k):
            return pltpu.make_async_copy(
                zero_ref, xin_ref.at[pl.ds(pl.multiple_of(zblk_ref[k] * n, n), n), :], zsem)

        for k in range(zblk_ref.shape[0]):
            pl.when(zuse_ref[k] != 0)(lambda k=k: zero_block(k).start())
        for k in range(zblk_ref.shape[0]):
            pl.when(zuse_ref[k] != 0)(lambda k=k: zero_block(k).wait())

    half = d // 2
    lane = lax.broadcasted_iota(jnp.int32, (tm, LANES), 1)
    row_id = (i * tm + lax.broadcasted_iota(jnp.int32, (tm, 1), 0)).astype(F32)
    route = route_ref[...]
    packed = _pack_bf16_pairs(c_ref[...])
    for j in range(TOP_K):
        w = route[:, ROUTE_LANE_W + j:ROUTE_LANE_W + j + 1]
        meta = jnp.where(lane == META_DEST, row_id + j * n_tokens, jnp.where(lane == META_W, w, 0.0))
        for s in range(X_SUBLANES):
            if s < half // LANES:
                sub = packed[:, s * LANES:(s + 1) * LANES]
            else:
                sub = lax.bitcast_convert_type(meta, jnp.uint32)
            _tile_rows(rows_ref.at[j], s, tm, X_SUBLANES)[...] = sub

    def issue(t, _):
        for j in range(TOP_K):
            _tile_copy(rows_ref.at[j], t, xin_ref, pos_ref[j, t], sem, X_SUBLANES).start(priority=j)
        return 0

    lax.fori_loop(0, tm, issue, 0, unroll=ROW_DMA_UNROLL)
    for j in range(TOP_K):
        pltpu.make_async_copy(rows_ref.at[j], rows_ref.at[j], sem).wait()


def _dispatch(zero_blocks, zero_use, pos, c, route, n_blocks, tm):
    t, d = c.shape
    assert d // 2 // LANES == META_SUBLANE == X_SUBLANES - 1
    rows = EXPERT_ROWS
    grid_spec = pltpu.PrefetchScalarGridSpec(
        num_scalar_prefetch=2,
        grid=(t // tm,),
        in_specs=[
            pl.BlockSpec((SUBLANES, tm), lambda i, *_: (0, i), memory_space=pltpu.SMEM),
            pl.BlockSpec((tm, d), lambda i, *_: (i, 0)),
            pl.BlockSpec((tm, LANES), lambda i, *_: (i, 0)),
        ],
        out_specs=pl.BlockSpec(memory_space=pl.ANY),
        scratch_shapes=[pltpu.VMEM((TOP_K, tm * X_SUBLANES, LANES), jnp.uint32),
                        pltpu.VMEM((rows * X_SUBLANES, LANES), jnp.uint32),
                        pltpu.SemaphoreType.DMA(()),
                        pltpu.SemaphoreType.DMA(())],
    )
    return pl.pallas_call(
        functools.partial(_dispatch_kernel, n_tokens=t),
        grid_spec=grid_spec,
        out_shape=jax.ShapeDtypeStruct((n_blocks * rows * X_SUBLANES, LANES), jnp.uint32),
        compiler_params=pltpu.CompilerParams(
            dimension_semantics=("arbitrary",), vmem_limit_bytes=VMEM_LIMIT),
        name="dispatch",
    )(zero_blocks, zero_use, pos, c, route)


def _pack_bf16_pairs(v):
    half = v.shape[1] // 2
    bits = lax.bitcast_convert_type(v.astype(BF16).astype(F32), jnp.uint32)
    return (bits[:, 0:half] >> 16) | bits[:, half:]


def _unpack_bf16_pairs(words):
    lo = [lax.bitcast_convert_type(w << 16, F32) for w in words]
    hi = [lax.bitcast_convert_type(w & jnp.uint32(0xFFFF0000), F32) for w in words]
    return jnp.concatenate(lo + hi, axis=1)


def _experts_kernel(be_ref, bn_ref, nv_ref, x_ref, wg_ref, wu_ref, wd_ref, yout_ref,
                    wgb, wub, wdb, ybuf, dest_vm, dest_sm, sems, dsem, *, dump_row):
    i = pl.program_id(0)
    s = i % 2
    rows = ybuf.shape[1] // Y_SUBLANES
    prev = be_ref[jnp.maximum(i - 1, 0)]
    active = i < nv_ref[0]

    def to_dump(slot):
        def body(r, _):
            dest_sm[slot, 0, r] = dump_row + r
            return 0
        lax.fori_loop(0, rows, body, 0)

    def send(slot, r, queue=0):
        _tile_copy(ybuf.at[slot], r, yout_ref, dest_sm[slot, 0, r], sems.at[slot],
                   Y_SUBLANES).start(priority=queue)

    def wait(slot):
        pltpu.make_async_copy(ybuf.at[slot], ybuf.at[slot], sems.at[slot]).wait()

    @pl.when(i == 0)
    def _():
        ybuf[1] = jnp.zeros(ybuf.shape[1:], jnp.uint32)
        to_dump(1)

    @pl.when(i > 0)
    def _():
        wait(s)

    @pl.when(active & ((i == 0) | (be_ref[i] != prev)))
    def _():
        wgb[...] = wg_ref[0].astype(BF16)
        wub[...] = wu_ref[0].astype(BF16)
        wdb[...] = wd_ref[0].astype(BF16)

    @pl.when(active)
    def _():
        for r in range(rows):
            send(1 - s, r, r % 2)
        x = _unpack_bf16_pairs([_tile_rows(x_ref, t, rows, X_SUBLANES)[...]
                                for t in range(META_SUBLANE)]).astype(BF16)
        meta = lax.bitcast_convert_type(_tile_rows(x_ref, META_SUBLANE, rows, X_SUBLANES)[...], F32)
        gt = _dot(x, wgb[...])
        up = _dot(x, wub[...])
        hid = (gt * _sigmoid(gt) * up).astype(BF16)
        y = _pack_bf16_pairs(_dot(hid, wdb[...]) * meta[:, META_W:META_W + 1])
        for t in range(Y_SUBLANES):
            _tile_rows(ybuf.at[s], t, rows, Y_SUBLANES)[...] = y[:, t * LANES:(t + 1) * LANES]
        row = lax.broadcasted_iota(jnp.int32, (rows, LANES), 0)
        dest = jnp.where(row < bn_ref[i], meta, (dump_row + row).astype(F32))
        _lanes_to_smem(dest, dest_vm, dest_sm.at[s], dsem)

    @pl.when(jnp.logical_not(active))
    def _():
        lax.fori_loop(0, rows, lambda r, _: send(1 - s, r) or 0, 0, unroll=ROW_DMA_UNROLL)
        to_dump(s)

    @pl.when(i == pl.num_programs(0) - 1)
    def _():
        wait(1 - s)


def _experts(blk_e, blk_n, n_valid, xin, w_gate, w_up, w_down, n_out_rows):
    rows = EXPERT_ROWS
    nblk = xin.shape[0] // (rows * X_SUBLANES)
    d, de = w_gate.shape[1], w_gate.shape[2]
    assert d == 2 * Y_SUBLANES * LANES, "an output row is 4 sublanes of packed bf16 pairs"
    grid_spec = pltpu.PrefetchScalarGridSpec(
        num_scalar_prefetch=3,
        grid=(nblk + 1,),
        in_specs=[
            pl.BlockSpec((rows * X_SUBLANES, LANES),
                         lambda i, be, bn, nv: (jnp.minimum(i, nblk - 1), 0)),
            pl.BlockSpec((1, d, de), lambda i, be, bn, nv: (be[i], 0, 0)),
            pl.BlockSpec((1, d, de), lambda i, be, bn, nv: (be[i], 0, 0)),
            pl.BlockSpec((1, de, d), lambda i, be, bn, nv: (be[i], 0, 0)),
        ],
        out_specs=pl.BlockSpec(memory_space=pl.ANY),
        scratch_shapes=[pltpu.VMEM((d, de), BF16), pltpu.VMEM((d, de), BF16),
                        pltpu.VMEM((de, d), BF16),
                        pltpu.VMEM((2, rows * Y_SUBLANES, LANES), jnp.uint32),
                        pltpu.VMEM((8, rows), jnp.int32),
                        pltpu.SMEM((2, 8, rows), jnp.int32),
                        pltpu.SemaphoreType.DMA((2,)),
                        pltpu.SemaphoreType.DMA(())],
    )
    return pl.pallas_call(
        functools.partial(_experts_kernel, dump_row=n_out_rows),
        grid_spec=grid_spec,
        out_shape=jax.ShapeDtypeStruct(((n_out_rows + rows) * Y_SUBLANES, LANES), jnp.uint32),
        compiler_params=pltpu.CompilerParams(
            dimension_semantics=("arbitrary",), vmem_limit_bytes=VMEM_LIMIT),
        name="experts",
    )(blk_e, blk_n, n_valid, xin, w_gate, w_up, w_down)


def _combine_kernel(h1_ref, y0_ref, y1_ref, p_ref, wpg_ref, wpp_ref,
                    gple_ref, gpost_ref, gfin_ref, out_ref):
    tm = h1_ref.shape[0]
    y0, y1 = (_unpack_bf16_pairs([_tile_rows(ref, s, tm, Y_SUBLANES)[...] for s in range(Y_SUBLANES)])
              for ref in (y0_ref, y1_ref))
    h2 = h1_ref[...] + (y0 + y1)
    gate = _sigmoid(_dot(_rms(h2, gple_ref[...]).astype(BF16), wpg_ref[...]))
    ple = _rms(_dot(p_ref[...].astype(BF16), wpp_ref[...]), gpost_ref[...])
    h3 = h2 + gate * ple
    out_ref[...] = _rms(h3, gfin_ref[...])


def _combine(h1, p2, y, w_pg, w_pp, g_ple, g_post, g_final, tm):
    t, d = h1.shape
    pd = p2.shape[1]
    row = lambda i: (i, 0)
    const = lambda i: (0, 0)
    return pl.pallas_call(
        _combine_kernel,
        grid=(t // tm,),
        in_specs=[
            pl.BlockSpec((tm, d), row),
            pl.BlockSpec((tm * Y_SUBLANES, LANES), lambda i: (i, 0)),
            pl.BlockSpec((tm * Y_SUBLANES, LANES), lambda i: (t // tm + i, 0)),
            pl.BlockSpec((tm, pd), row),
            pl.BlockSpec((d, d), const),
            pl.BlockSpec((pd, d), const),
            pl.BlockSpec((1, d), const),
            pl.BlockSpec((1, d), const),
            pl.BlockSpec((1, d), const),
        ],
        out_specs=pl.BlockSpec((tm, d), row),
        out_shape=jax.ShapeDtypeStruct((t, d), F32),
        compiler_params=pltpu.CompilerParams(
            dimension_semantics=("arbitrary",), vmem_limit_bytes=VMEM_LIMIT),
        name="combine",
    )(h1, y, y, p2, w_pg.astype(BF16), w_pp.astype(BF16),
      g_ple.reshape(1, d), g_post.reshape(1, d), g_final.reshape(1, d))


def _largest_tile(n, cap):
    tile = cap
    while n % tile:
        tile //= 2
    return tile


def kernel(x, p, g_mix, w_in, b_gates, conv_q, conv_k, g_mhead, w_out, g_ffn, w_router_group,
           b_router_group, w_router_expert, b_router_expert, w_exp_gate, w_exp_up, w_exp_down,
           g_ple, w_ple_gate, w_ple_proj, g_ple_post, g_final):
    batch, seq_len, d = x.shape
    t = batch * seq_len
    tm = _largest_tile(seq_len, 512)
    n_exp = w_router_expert.shape[-1]
    rows = EXPERT_ROWS
    nblk = t * TOP_K // rows + n_exp

    assert w_in.shape[0] == 1, "single-layer block"
    l = 0
    h = x.reshape(t, d)
    mq, mk, mv, mo, gates, sq, sk, sv = _inproj(
        h, g_mix[l], w_in[l], conv_q[l], conv_k[l], seq_len, _largest_tile(seq_len, 1024))
    hm = _mlstm(mq, mk, mv, mo, gates, b_gates[l], g_mhead[l], batch, seq_len,
                _largest_tile(seq_len, 1024))
    hs = _stickbreak(sq, sk, sv, batch, seq_len)
    h1, c, route, counts = _outroute(
        h, hm, hs, w_out[l], g_ffn[l], w_router_group[l], b_router_group[l],
        w_router_expert[l], b_router_expert[l], tm)

    cnt = counts[0, LOGIT_LANE_E:LOGIT_LANE_E + n_exp].astype(jnp.int32)
    nb_e = (cnt + rows - 1) // rows
    cum = jnp.cumsum(nb_e)
    offs = (cum - nb_e) * rows
    offs_row = jnp.zeros((1, LANES), F32).at[0, 0:n_exp].set(offs.astype(F32))
    n_valid = cum[-1:]
    step = jnp.arange(nblk + 1, dtype=jnp.int32)
    blk_e = jnp.minimum(jnp.sum(cum[None, :] <= step[:, None], axis=1), n_exp - 1).astype(jnp.int32)
    mine = blk_e[:, None] == jnp.arange(n_exp, dtype=jnp.int32)[None, :]
    first = jnp.sum(jnp.where(mine, (cum - nb_e)[None, :], 0), axis=1)
    blk_n = jnp.clip(jnp.sum(jnp.where(mine, cnt[None, :], 0), axis=1) - rows * (step - first), 0, rows)
    tail = n_valid + jnp.arange(n_exp, dtype=jnp.int32)
    zero_blocks = jnp.concatenate([jnp.maximum(cum - 1, 0), jnp.minimum(tail, nblk - 1)])
    zero_use = jnp.concatenate([nb_e > 0, tail < nblk]).astype(jnp.int32)

    pos = _slotpos(route, offs_row, _largest_tile(t, 2048))
    xin = _dispatch(zero_blocks, zero_use, pos, c, route, nblk, _largest_tile(seq_len, 256))
    y = _experts(blk_e, blk_n, n_valid, xin, w_exp_gate[l], w_exp_up[l], w_exp_down[l], TOP_K * t)
    out = _combine(h1, p[l].reshape(t, -1), y, w_ple_gate[l], w_ple_proj[l],
                   g_ple[l], g_ple_post[l], g_final, tm)
    return out.reshape(batch, seq_len, d)
```

```python
import functools

import jax
import jax.numpy as jnp
from jax import lax
from jax.experimental import pallas as pl
from jax.experimental.pallas import tpu as pltpu

F32 = jnp.float32
BF16 = jnp.bfloat16
EPS = 1e-6

M_HEADS = 4
M_HEAD_DIM = 128
SB_HEAD_DIM = 64
CONV_WIDTH = 4
TOP_K = 2
LANES = 128
VMEM_LIMIT = 56 * 1024 * 1024

MLSTM_CHUNK = 256
SB_BLOCK = 256
SB_ZERO_LOG = -105.0
EXPERT_ROWS = 512
ROW_DMA_UNROLL = 8
X_SUBLANES = 5
Y_SUBLANES = 4
ROUTE_LANE_E = 0
ROUTE_LANE_W = 2
ROUTE_LANE_R = 4
SUBLANES = 8
META_SUBLANE = 4
META_DEST = 0
META_W = 1
LOGIT_LANE_E = 4


def _rms(x, g):
    return x * lax.rsqrt(jnp.mean(x * x, axis=-1, keepdims=True) + EPS) * g


def _sigmoid(x):
    return 1.0 / (1.0 + jnp.exp(-x))


def _split3(a):
    a1 = a.astype(BF16)
    r1 = a - a1.astype(F32)
    a2 = r1.astype(BF16)
    a3 = (r1 - a2.astype(F32)).astype(BF16)
    return a1, a2, a3


def _dot(a, b):
    return jnp.dot(a, b, preferred_element_type=F32)


def _dot_nt(a, b):
    return lax.dot_general(a, b, (((1,), (1,)), ((), ())), preferred_element_type=F32)


def _dot_tn(a, b):
    return lax.dot_general(a, b, (((0,), (0,)), ((), ())), preferred_element_type=F32)


def _inproj_kernel(x_ref, g_ref, wqk_ref, wvo_ref, wvt_ref, wg_ref, ws_ref, cq_ref, ck_ref,
                   mq_ref, mk_ref, mv_ref, mo_ref, gate_ref, sq_ref, sk_ref, sv_ref,
                   ext_ref, *, tiles_per_seq, k_scale):
    i = pl.program_id(0)
    tm = x_ref.shape[0]
    mw = mk_ref.shape[1]
    sw = sq_ref.shape[1]
    a = _rms(x_ref[...], g_ref[...]).astype(BF16)

    @pl.when(i % tiles_per_seq == 0)
    def _():
        ext_ref[0:8, :] = jnp.zeros((8, 2 * mw), F32)

    ext_ref[8:8 + tm, 0:mw] = _dot(a, wqk_ref[:, 0:mw])
    ext_ref[8:8 + tm, mw:2 * mw] = _dot(a, wqk_ref[:, mw:2 * mw])

    def conv_silu(w_ref, c0):
        acc = ext_ref[pl.ds(8 - (CONV_WIDTH - 1), tm), c0:c0 + mw] * w_ref[0:1, :]
        for j in range(1, CONV_WIDTH):
            acc = acc + ext_ref[pl.ds(8 - (CONV_WIDTH - 1) + j, tm), c0:c0 + mw] * w_ref[j:j + 1, :]
        return acc * _sigmoid(acc)

    mq_ref[...] = conv_silu(cq_ref, 0).T.astype(BF16)
    mk_ref[...] = (conv_silu(ck_ref, mw) * k_scale).astype(BF16)
    ext_ref[0:8, :] = ext_ref[tm:tm + 8, :]

    mv_ref[...] = _dot_nt(wvt_ref[...], a).astype(BF16)
    mo_ref[...] = _dot(a, wvo_ref[:, mw:2 * mw]).astype(BF16)
    gate_ref[...] = _dot(a, wg_ref[...])
    sq_ref[...] = _dot(a, ws_ref[:, 0:sw]).astype(BF16)
    sk_ref[...] = _dot(a, ws_ref[:, sw:2 * sw]).astype(BF16)
    sv_ref[...] = _dot(a, ws_ref[:, 2 * sw:3 * sw]).astype(BF16)


def _inproj(x2, g_mix, w_in, conv_q, conv_k, seq_len, tm):
    t, d = x2.shape
    mw = conv_q.shape[1]
    h = M_HEADS
    sw = (w_in.shape[1] - 4 * mw - 2 * h) // 3
    wqk = w_in[:, 0:2 * mw].astype(BF16)
    wvo = w_in[:, 2 * mw:4 * mw].astype(BF16)
    wvt = w_in[:, 2 * mw:3 * mw].T.astype(BF16)
    wg = jnp.zeros((d, 2 * LANES), F32)
    wg = wg.at[:, 0:h].set(w_in[:, 4 * mw:4 * mw + h])
    wg = wg.at[:, LANES:LANES + h].set(w_in[:, 4 * mw + h:4 * mw + 2 * h]).astype(BF16)
    ws = w_in[:, 4 * mw + 2 * h:]
    ws = jnp.concatenate([ws[:, 0:sw] * (SB_HEAD_DIM ** -0.5), ws[:, sw:]], axis=1).astype(BF16)
    row = lambda i: (i, 0)
    const = lambda i: (0, 0)
    kern = functools.partial(_inproj_kernel, tiles_per_seq=seq_len // tm, k_scale=M_HEAD_DIM ** -0.5)
    bf = lambda w: jax.ShapeDtypeStruct((t, w), BF16)
    bft = jax.ShapeDtypeStruct((mw, t), BF16)
    col = lambda i: (0, i)
    return pl.pallas_call(
        kern,
        grid=(t // tm,),
        in_specs=[
            pl.BlockSpec((tm, d), row),
            pl.BlockSpec((1, d), const),
            pl.BlockSpec((d, 2 * mw), const),
            pl.BlockSpec((d, 2 * mw), const),
            pl.BlockSpec((mw, d), const),
            pl.BlockSpec((d, 2 * LANES), const),
            pl.BlockSpec((d, 3 * sw), const),
            pl.BlockSpec((CONV_WIDTH, mw), const),
            pl.BlockSpec((CONV_WIDTH, mw), const),
        ],
        out_specs=[
            pl.BlockSpec((mw, tm), col), pl.BlockSpec((tm, mw), row),
            pl.BlockSpec((mw, tm), col), pl.BlockSpec((tm, mw), row),
            pl.BlockSpec((tm, 2 * LANES), row),
            pl.BlockSpec((tm, sw), row), pl.BlockSpec((tm, sw), row), pl.BlockSpec((tm, sw), row),
        ],
        out_shape=[bft, bf(mw), bft, bf(mw),
                   jax.ShapeDtypeStruct((t, 2 * LANES), F32), bf(sw), bf(sw), bf(sw)],
        scratch_shapes=[pltpu.VMEM((tm + 8, 2 * mw), F32)],
        compiler_params=pltpu.CompilerParams(
            dimension_semantics=("arbitrary",), vmem_limit_bytes=VMEM_LIMIT),
        name="inproj",
    )(x2, g_mix.reshape(1, d), wqk, wvo, wvt, wg, ws, conv_q, conv_k)


def _mlstm_kernel(q_ref, k_ref, v_ref, o_ref, gate_ref, bias_ref, gh_ref, out_ref,
                  c_ref, m_ref, *, chunk):
    L = chunk
    hd = M_HEAD_DIM
    nchunks = k_ref.shape[0] // L

    @pl.when(pl.program_id(1) == 0)
    def _():
        c_ref[...] = jnp.zeros(c_ref.shape, F32)
        m_ref[...] = jnp.zeros(m_ref.shape, F32)

    rows = lax.broadcasted_iota(jnp.int32, (L, L), 0)
    cols = lax.broadcasted_iota(jnp.int32, (L, L), 1)
    tri = (cols <= rows).astype(BF16)
    seen = rows <= cols
    ones_rows = (lax.broadcasted_iota(jnp.int32, (hd, L), 0) == 0).astype(BF16)

    def chunk_body(c, _):
        r0 = pl.multiple_of(c * L, L)
        g = gate_ref[pl.ds(r0, L), :] + bias_ref[...]
        gi = g[:, 0:LANES]
        gf = g[:, LANES:2 * LANES]
        lf = jnp.minimum(gf, 0.0) - jnp.log(1.0 + jnp.exp(-jnp.abs(gf)))
        l1, l2, l3 = _split3(lf)
        b = _dot(tri, l1) + _dot(tri, l2) + _dot(tri, l3)
        b_last = b[L - 1:L, :]
        w_end = b_last - b + gi
        m_loc = jnp.max(w_end, axis=0, keepdims=True)
        m_prev = m_ref[...]
        m_new = jnp.maximum(b_last + m_prev, m_loc)
        decay = jnp.exp(b_last + m_prev - m_new)
        scale = jnp.exp(m_loc - m_new)
        gmb = gi - b
        b_t = b.T
        e_end_t = jnp.exp(w_end - m_loc).T
        for h in range(M_HEADS):
            hs = slice(h * hd, (h + 1) * hd)
            qt = q_ref[hs, pl.ds(r0, L)]
            kh = k_ref[pl.ds(r0, L), hs]
            vext = jnp.concatenate([v_ref[hs, pl.ds(r0, L)], ones_rows], axis=0)
            b_row = b_t[h:h + 1, :]
            e = jnp.where(seen, b_row + gmb[:, h:h + 1], -jnp.inf)
            log_inter = b_row + m_prev[:, h:h + 1]
            m_t = jnp.maximum(log_inter, jnp.max(e, axis=0, keepdims=True))
            w = (jnp.exp(e - m_t) * _dot(kh, qt)).astype(BF16)
            a_int = jnp.exp(log_inter - m_t)
            cext = c_ref[h]
            num = _dot(vext, w) + a_int * _dot(cext.astype(BF16), qt)
            den = num[hd:hd + 1, :]
            hh = num[0:hd, :] / jnp.maximum(jnp.abs(den), jnp.exp(-m_t))
            hh = hh * lax.rsqrt(jnp.mean(hh * hh, axis=0, keepdims=True) + EPS) * gh_ref[hs, :]
            og = _sigmoid(o_ref[pl.ds(r0, L), hs].astype(F32))
            out_ref[pl.ds(r0, L), hs] = (og * hh.T).astype(BF16)
            ev = (vext.astype(F32) * e_end_t[h:h + 1, :]).astype(BF16)
            c_ref[h] = decay[:, h:h + 1] * cext + scale[:, h:h + 1] * _dot(ev, kh)
        m_ref[...] = m_new
        return 0

    lax.fori_loop(0, nchunks, chunk_body, 0, unroll=2)


def _mlstm(mq, mk, mv, mo, gates, b_gates, g_mhead, batch, seq_len, rows):
    t, mw = mk.shape
    h = M_HEADS
    bias = jnp.zeros((1, 2 * LANES), F32)
    bias = bias.at[0, 0:h].set(b_gates[0:h]).at[0, LANES:LANES + h].set(b_gates[h:2 * h])
    nb = seq_len // rows
    row = lambda b, i: (b * nb + i, 0)
    col = lambda b, i: (0, b * nb + i)
    const = lambda b, i: (0, 0)
    timed = pl.BlockSpec((mw, rows), col)
    rowed = pl.BlockSpec((rows, mw), row)
    return pl.pallas_call(
        functools.partial(_mlstm_kernel, chunk=MLSTM_CHUNK),
        grid=(batch, nb),
        in_specs=[timed, rowed, timed, rowed,
                  pl.BlockSpec((rows, 2 * LANES), row),
                  pl.BlockSpec((1, 2 * LANES), const),
                  pl.BlockSpec((mw, MLSTM_CHUNK), const)],
        out_specs=rowed,
        out_shape=jax.ShapeDtypeStruct((t, mw), BF16),
        scratch_shapes=[pltpu.VMEM((h, 2 * M_HEAD_DIM, M_HEAD_DIM), F32),
                        pltpu.VMEM((1, LANES), F32)],
        compiler_params=pltpu.CompilerParams(
            dimension_semantics=("arbitrary", "arbitrary"), vmem_limit_bytes=VMEM_LIMIT),
        name="mlstm",
    )(mq, mk, mv, mo, gates, bias,
      jnp.broadcast_to(g_mhead.reshape(mw, 1), (mw, MLSTM_CHUNK)))


def _sb_kernel(q_ref, k_ref, v_ref, out_ref, acc_ref, carry_ref, z_ref, sp_ref):
    blk = SB_BLOCK
    nq = q_ref.shape[0] // blk
    lane = lax.broadcasted_iota(jnp.int32, (blk, LANES), 1)
    head0 = lane < SB_HEAD_DIM
    rows = lax.broadcasted_iota(jnp.int32, (blk, blk), 0)
    cols = lax.broadcasted_iota(jnp.int32, (blk, blk), 1)
    strict = cols < rows
    neg_suffix = jnp.where(rows >= cols, -1.0, 0.0).astype(BF16)

    def split_heads(x):
        zero = jnp.zeros_like(x)
        return [jnp.where(head0, x, zero), jnp.where(head0, zero, x)]

    def rows_of(ref, j):
        return ref[pl.ds(pl.multiple_of(j * blk, blk), blk), :]

    def scores(qm_h, kb, mask):
        z = _dot_nt(qm_h, kb)
        neg_abs = lax.bitcast_convert_type(
            lax.bitcast_convert_type(z, jnp.uint32) | jnp.uint32(0x80000000), F32)
        sp = jnp.maximum(z, 0.0) + jnp.log(1.0 + jnp.exp(neg_abs))
        if mask is not None:
            sp = jnp.where(mask, sp, 0.0)
        return z, sp.astype(BF16)

    def weights(z, sp, carry, vm_h, mask):
        rc = _dot(sp, neg_suffix)
        p = jnp.exp(z + rc + carry)
        if mask is not None:
            p = jnp.where(mask, p, 0.0)
        return _dot(p.astype(BF16), vm_h), rc[:, 0:1]

    def first_half(qi, slot):
        qm = split_heads(rows_of(q_ref, qi))
        for n, (j, mask) in enumerate(((qi, strict), (jnp.maximum(qi - 1, 0), None))):
            kb = rows_of(k_ref, j)
            for h in range(2):
                z_ref[slot, 2 * n + h], sp_ref[slot, 2 * n + h] = scores(qm[h], kb, mask)

    def second_half(qi, slot, acc_ref, carry_ref):
        carries = [jnp.zeros((blk, 1), F32)] * 2
        upd = None
        for n, (j, mask, live) in enumerate(((qi, strict, None), (jnp.maximum(qi - 1, 0), None, qi > 0))):
            vm = split_heads(rows_of(v_ref, j))
            if live is not None:
                vm = [jnp.where(live, v, jnp.zeros_like(v)) for v in vm]
            new = []
            for h in range(2):
                d, total = weights(z_ref[slot, 2 * n + h], sp_ref[slot, 2 * n + h], carries[h], vm[h], mask)
                carry = carries[h] + total
                new.append(carry if live is None else jnp.where(live, carry, carries[h]))
                upd = d if upd is None else upd + d
            carries = new
        acc_ref[...] = upd
        carry_ref[0] = carries[0]
        carry_ref[1] = carries[1]
        return jnp.maximum(jnp.max(carries[0]), jnp.max(carries[1]))

    def remaining(qi, top, acc_ref, carry_ref):
        def cond(state):
            it, top = state
            return (it < qi) & (top > SB_ZERO_LOG)

        def body(state):
            it, _ = state
            j = qi - 1 - it
            qm = split_heads(rows_of(q_ref, qi))
            kb = rows_of(k_ref, j)
            vm = split_heads(rows_of(v_ref, j))
            upd = None
            tops = []
            for h in range(2):
                z, sp = scores(qm[h], kb, None)
                d, total = weights(z, sp, carry_ref[h], vm[h], None)
                carry_ref[h] = carry_ref[h] + total
                tops.append(jnp.max(carry_ref[h]))
                upd = d if upd is None else upd + d
            acc_ref[...] += upd
            return it + 1, jnp.maximum(tops[0], tops[1])

        lax.while_loop(cond, body, (jnp.int32(1), top))

    first_half(0, 0)

    def query_pair(pair, _):
        qa = 2 * pair
        tops = []
        for s in range(2):
            tops.append(second_half(qa + s, s, acc_ref.at[s], carry_ref.at[s]))
            first_half(jnp.minimum(qa + s + 1, nq - 1), 1 - s)
        for s in range(2):
            remaining(qa + s, tops[s], acc_ref.at[s], carry_ref.at[s])
            out_ref[pl.ds(pl.multiple_of((qa + s) * blk, blk), blk), :] = acc_ref[s].astype(BF16)
        return 0

    assert nq % 2 == 0
    lax.fori_loop(0, nq // 2, query_pair, 0)


def _stickbreak(sq, sk, sv, batch, seq_len):
    t, sw = sq.shape
    npair = sw // LANES
    seq = pl.BlockSpec((seq_len, LANES), lambda b, hp: (b, hp))
    return pl.pallas_call(
        _sb_kernel,
        grid=(batch, npair),
        in_specs=[seq, seq, seq],
        out_specs=seq,
        out_shape=jax.ShapeDtypeStruct((t, sw), BF16),
        scratch_shapes=[pltpu.VMEM((2, SB_BLOCK, LANES), F32), pltpu.VMEM((2, 2, SB_BLOCK, 1), F32),
                        pltpu.VMEM((2, 4, SB_BLOCK, SB_BLOCK), F32),
                        pltpu.VMEM((2, 4, SB_BLOCK, SB_BLOCK), BF16)],
        compiler_params=pltpu.CompilerParams(
            dimension_semantics=("arbitrary", "arbitrary"), vmem_limit_bytes=VMEM_LIMIT),
        name="stickbrk",
    )(sq, sk, sv)


def _outroute_kernel(x_ref, hm_ref, hs_ref, wom_ref, wos_ref, g_ref, wr_ref, br_ref,
                     h1_ref, c_ref, route_ref, cnt_ref, run_ref, *, n_groups, per_group):
    i = pl.program_id(0)
    tm = x_ref.shape[0]

    @pl.when(i == 0)
    def _():
        run_ref[...] = jnp.zeros(run_ref.shape, F32)

    h1 = x_ref[...] + _dot(hm_ref[...], wom_ref[...]) + _dot(hs_ref[...], wos_ref[...])
    h1_ref[...] = h1
    c = _rms(h1, g_ref[...])
    c_ref[...] = c

    c1, c2, _ = _split3(c)
    pa = _dot(c1, wr_ref[...])
    pb = _dot(c2, wr_ref[...])
    logits = (pa[:, 0:LANES] + (pa[:, LANES:] + pb[:, 0:LANES]) + pb[:, LANES:]) + br_ref[...]

    lane = lax.broadcasted_iota(jnp.int32, (tm, LANES), 1).astype(F32)
    ninf = -jnp.inf
    big = float(LANES)

    def first_max(v):
        mx = jnp.max(v, axis=1, keepdims=True)
        idx = jnp.min(jnp.where(v == mx, lane, big), axis=1, keepdims=True)
        return mx, idx

    gl = jnp.where(lane < n_groups, logits, ninf)
    gmax, gsel = first_max(gl)
    p_g = 1.0 / jnp.sum(jnp.exp(gl - gmax), axis=1, keepdims=True)
    lo = LOGIT_LANE_E + per_group * gsel
    el = jnp.where((lane >= lo) & (lane < lo + per_group), logits, ninf)
    v1, i1 = first_max(el)
    v2, i2 = first_max(jnp.where(lane == i1, ninf, el))
    tt = jnp.exp(v2 - v1)
    w0 = p_g / (1.0 + tt)
    w1_ = p_g * tt / (1.0 + tt)

    oh0 = lane == i1
    oh1 = lane == i2
    ohsum = oh0.astype(F32) + oh1.astype(F32)
    rows = lax.broadcasted_iota(jnp.int32, (tm, tm), 0)
    cols = lax.broadcasted_iota(jnp.int32, (tm, tm), 1)
    before = (cols < rows).astype(BF16)
    prefix = _dot(before, ohsum.astype(BF16)) + run_ref[...]
    r0 = jnp.sum(jnp.where(oh0, prefix, 0.0), axis=1, keepdims=True)
    r1 = jnp.sum(jnp.where(oh1, prefix, 0.0), axis=1, keepdims=True)
    run = run_ref[...] + jnp.sum(ohsum, axis=0, keepdims=True)
    run_ref[...] = run
    cnt_ref[...] = jnp.broadcast_to(run, cnt_ref.shape)

    e0 = i1 - LOGIT_LANE_E
    e1 = i2 - LOGIT_LANE_E
    route = jnp.zeros((tm, LANES), F32)
    for ln, val in ((ROUTE_LANE_E, e0), (ROUTE_LANE_E + 1, e1), (ROUTE_LANE_W, w0),
                    (ROUTE_LANE_W + 1, w1_), (ROUTE_LANE_R, r0), (ROUTE_LANE_R + 1, r1)):
        route = jnp.where(lane == ln, val, route)
    route_ref[...] = route


def _tile_rows(ref, s, n, pitch=SUBLANES):
    return ref.at[pl.ds(s, n, stride=pitch), :]


def _tile_copy(src_ref, src_row, dst_ref, dst_row, sem, pitch):
    return pltpu.make_async_copy(src_ref.at[pl.ds(src_row * pitch, pitch), :],
                                 dst_ref.at[pl.ds(dst_row * pitch, pitch), :], sem)


def _lanes_to_smem(vals, vm_ref, sm_ref, sem):
    vm_ref[...] = vals.T[0:8, :].astype(jnp.int32)
    cp = pltpu.make_async_copy(vm_ref, sm_ref, sem)
    cp.start()
    cp.wait()


def _outroute(x2, hm, hs, w_out, g_ffn, w_rg, b_rg, w_re, b_re, tm):
    t, d = x2.shape
    mw = hm.shape[1]
    sw = hs.shape[1]
    n_groups = w_rg.shape[1]
    n_exp = w_re.shape[1]
    wr = jnp.zeros((d, LANES), F32)
    wr = wr.at[:, 0:n_groups].set(w_rg).at[:, LOGIT_LANE_E:LOGIT_LANE_E + n_exp].set(w_re)
    wr_hi, wr_lo, _ = _split3(wr)
    wr2 = jnp.concatenate([wr_hi, wr_lo], axis=1)
    br = jnp.zeros((1, LANES), F32)
    br = br.at[0, 0:n_groups].set(b_rg).at[0, LOGIT_LANE_E:LOGIT_LANE_E + n_exp].set(b_re)
    row = lambda i: (i, 0)
    const = lambda i: (0, 0)
    kern = functools.partial(_outroute_kernel, n_groups=n_groups, per_group=n_exp // n_groups)
    return pl.pallas_call(
        kern,
        grid=(t // tm,),
        in_specs=[
            pl.BlockSpec((tm, d), row),
            pl.BlockSpec((tm, mw), row),
            pl.BlockSpec((tm, sw), row),
            pl.BlockSpec((mw, d), const),
            pl.BlockSpec((sw, d), const),
            pl.BlockSpec((1, d), const),
            pl.BlockSpec((d, 2 * LANES), const),
            pl.BlockSpec((1, LANES), const),
        ],
        out_specs=[
            pl.BlockSpec((tm, d), row),
            pl.BlockSpec((tm, d), row),
            pl.BlockSpec((tm, LANES), row),
            pl.BlockSpec((8, LANES), const),
        ],
        out_shape=[
            jax.ShapeDtypeStruct((t, d), F32),
            jax.ShapeDtypeStruct((t, d), F32),
            jax.ShapeDtypeStruct((t, LANES), F32),
            jax.ShapeDtypeStruct((8, LANES), F32),
        ],
        scratch_shapes=[pltpu.VMEM((1, LANES), F32)],
        compiler_params=pltpu.CompilerParams(
            dimension_semantics=("arbitrary",), vmem_limit_bytes=VMEM_LIMIT),
        name="outroute",
    )(x2, hm, hs, w_out[0:mw].astype(BF16), w_out[mw:].astype(BF16), g_ffn.reshape(1, d),
      wr2, br)


def _slotpos_kernel(route_ref, offs_ref, pos_ref):
    route = route_ref[...]
    tm = route.shape[0]
    lane = lax.broadcasted_iota(jnp.int32, (tm, LANES), 1)
    offs = offs_ref[...]
    out = jnp.zeros((tm, LANES), F32)
    for j in range(TOP_K):
        e = route[:, ROUTE_LANE_E + j:ROUTE_LANE_E + j + 1].astype(jnp.int32)
        base = jnp.sum(jnp.where(lane == e, offs, 0.0), axis=1, keepdims=True)
        out = jnp.where(lane == j, base + route[:, ROUTE_LANE_R + j:ROUTE_LANE_R + j + 1], out)
    pos_ref[...] = out.T[0:SUBLANES, :].astype(jnp.int32)


def _slotpos(route, offs_row, tm):
    t = route.shape[0]
    return pl.pallas_call(
        _slotpos_kernel,
        grid=(t // tm,),
        in_specs=[pl.BlockSpec((tm, LANES), lambda i: (i, 0)),
                  pl.BlockSpec((1, LANES), lambda i: (0, 0))],
        out_specs=pl.BlockSpec((SUBLANES, tm), lambda i: (0, i)),
        out_shape=jax.ShapeDtypeStruct((SUBLANES, t), jnp.int32),
        compiler_params=pltpu.CompilerParams(dimension_semantics=("arbitrary",)),
        name="slotpos",
    )(route, offs_row)


def _dispatch_kernel(zblk_ref, zuse_ref, pos_ref, c_ref, route_ref, xin_ref, rows_ref, zero_ref,
                     sem, zsem, *, n_tokens):
    i = pl.program_id(0)
    tm, d = c_ref.shape

    @pl.when(i == 0)
    def _():
        zero_ref[...] = jnp.zeros(zero_ref.shape, jnp.uint32)
        n = zero_ref.shape[0]

        def zero_block(k):
            return pltpu.make_async_copy(
                zero_ref, xin_ref.at[pl.ds(pl.multiple_of(zblk_ref[k] * n, n), n), :], zsem)

        for k in range(zblk_ref.shape[0]):
            pl.when(zuse_ref[k] != 0)(lambda k=k: zero_block(k).start())
        for k in range(zblk_ref.shape[0]):
            pl.when(zuse_ref[k] != 0)(lambda k=k: zero_block(k).wait())

    half = d // 2
    lane = lax.broadcasted_iota(jnp.int32, (tm, LANES), 1)
    row_id = (i * tm + lax.broadcasted_iota(jnp.int32, (tm, 1), 0)).astype(F32)
    route = route_ref[...]
    packed = _pack_bf16_pairs(c_ref[...])
    for j in range(TOP_K):
        w = route[:, ROUTE_LANE_W + j:ROUTE_LANE_W + j + 1]
        meta = jnp.where(lane == META_DEST, row_id + j * n_tokens, jnp.where(lane == META_W, w, 0.0))
        for s in range(X_SUBLANES):
            if s < half // LANES:
                sub = packed[:, s * LANES:(s + 1) * LANES]
            else:
                sub = lax.bitcast_convert_type(meta, jnp.uint32)
            _tile_rows(rows_ref.at[j], s, tm, X_SUBLANES)[...] = sub

    def issue(t, _):
        for j in range(TOP_K):
            _tile_copy(rows_ref.at[j], t, xin_ref, pos_ref[j, t], sem, X_SUBLANES).start(priority=j)
        return 0

    lax.fori_loop(0, tm, issue, 0, unroll=ROW_DMA_UNROLL)
    for j in range(TOP_K):
        pltpu.make_async_copy(rows_ref.at[j], rows_ref.at[j], sem).wait()


def _dispatch(zero_blocks, zero_use, pos, c, route, n_blocks, tm):
    t, d = c.shape
    assert d // 2 // LANES == META_SUBLANE == X_SUBLANES - 1
    rows = EXPERT_ROWS
    grid_spec = pltpu.PrefetchScalarGridSpec(
        num_scalar_prefetch=2,
        grid=(t // tm,),
        in_specs=[
            pl.BlockSpec((SUBLANES, tm), lambda i, *_: (0, i), memory_space=pltpu.SMEM),
            pl.BlockSpec((tm, d), lambda i, *_: (i, 0)),
            pl.BlockSpec((tm, LANES), lambda i, *_: (i, 0)),
        ],
        out_specs=pl.BlockSpec(memory_space=pl.ANY),
        scratch_shapes=[pltpu.VMEM((TOP_K, tm * X_SUBLANES, LANES), jnp.uint32),
                        pltpu.VMEM((rows * X_SUBLANES, LANES), jnp.uint32),
                        pltpu.SemaphoreType.DMA(()),
                        pltpu.SemaphoreType.DMA(())],
    )
    return pl.pallas_call(
        functools.partial(_dispatch_kernel, n_tokens=t),
        grid_spec=grid_spec,
        out_shape=jax.ShapeDtypeStruct((n_blocks * rows * X_SUBLANES, LANES), jnp.uint32),
        compiler_params=pltpu.CompilerParams(
            dimension_semantics=("arbitrary",), vmem_limit_bytes=VMEM_LIMIT),
        name="dispatch",
    )(zero_blocks, zero_use, pos, c, route)


def _pack_bf16_pairs(v):
    half = v.shape[1] // 2
    bits = lax.bitcast_convert_type(v.astype(BF16).astype(F32), jnp.uint32)
    return (bits[:, 0:half] >> 16) | bits[:, half:]


def _unpack_bf16_pairs(words):
    lo = [lax.bitcast_convert_type(w << 16, F32) for w in words]
    hi = [lax.bitcast_convert_type(w & jnp.uint32(0xFFFF0000), F32) for w in words]
    return jnp.concatenate(lo + hi, axis=1)


def _experts_kernel(be_ref, bn_ref, nv_ref, x_ref, wg_ref, wu_ref, wd_ref, yout_ref,
                    wgb, wub, wdb, ybuf, dest_vm, dest_sm, sems, dsem, *, dump_row):
    i = pl.program_id(0)
    s = i % 2
    rows = ybuf.shape[1] // Y_SUBLANES
    prev = be_ref[jnp.maximum(i - 1, 0)]
    active = i < nv_ref[0]

    def to_dump(slot):
        def body(r, _):
            dest_sm[slot, 0, r] = dump_row + r
            return 0
        lax.fori_loop(0, rows, body, 0)

    def send(slot, r, queue=0):
        _tile_copy(ybuf.at[slot], r, yout_ref, dest_sm[slot, 0, r], sems.at[slot],
                   Y_SUBLANES).start(priority=queue)

    def wait(slot):
        pltpu.make_async_copy(ybuf.at[slot], ybuf.at[slot], sems.at[slot]).wait()

    @pl.when(i == 0)
    def _():
        ybuf[1] = jnp.zeros(ybuf.shape[1:], jnp.uint32)
        to_dump(1)

    @pl.when(i > 0)
    def _():
        wait(s)

    @pl.when(active & ((i == 0) | (be_ref[i] != prev)))
    def _():
        wgb[...] = wg_ref[0].astype(BF16)
        wub[...] = wu_ref[0].astype(BF16)
        wdb[...] = wd_ref[0].astype(BF16)

    @pl.when(active)
    def _():
        for r in range(rows):
            send(1 - s, r, r % 2)
        x = _unpack_bf16_pairs([_tile_rows(x_ref, t, rows, X_SUBLANES)[...]
                                for t in range(META_SUBLANE)]).astype(BF16)
        meta = lax.bitcast_convert_type(_tile_rows(x_ref, META_SUBLANE, rows, X_SUBLANES)[...], F32)
        gt = _dot(x, wgb[...])
        up = _dot(x, wub[...])
        hid = (gt * _sigmoid(gt) * up).astype(BF16)
        y = _pack_bf16_pairs(_dot(hid, wdb[...]) * meta[:, META_W:META_W + 1])
        for t in range(Y_SUBLANES):
            _tile_rows(ybuf.at[s], t, rows, Y_SUBLANES)[...] = y[:, t * LANES:(t + 1) * LANES]
        row = lax.broadcasted_iota(jnp.int32, (rows, LANES), 0)
        dest = jnp.where(row < bn_ref[i], meta, (dump_row + row).astype(F32))
        _lanes_to_smem(dest, dest_vm, dest_sm.at[s], dsem)

    @pl.when(jnp.logical_not(active))
    def _():
        lax.fori_loop(0, rows, lambda r, _: send(1 - s, r) or 0, 0, unroll=ROW_DMA_UNROLL)
        to_dump(s)

    @pl.when(i == pl.num_programs(0) - 1)
    def _():
        wait(1 - s)


def _experts(blk_e, blk_n, n_valid, xin, w_gate, w_up, w_down, n_out_rows):
    rows = EXPERT_ROWS
    nblk = xin.shape[0] // (rows * X_SUBLANES)
    d, de = w_gate.shape[1], w_gate.shape[2]
    assert d == 2 * Y_SUBLANES * LANES, "an output row is 4 sublanes of packed bf16 pairs"
    grid_spec = pltpu.PrefetchScalarGridSpec(
        num_scalar_prefetch=3,
        grid=(nblk + 1,),
        in_specs=[
            pl.BlockSpec((rows * X_SUBLANES, LANES),
                         lambda i, be, bn, nv: (jnp.minimum(i, nblk - 1), 0)),
            pl.BlockSpec((1, d, de), lambda i, be, bn, nv: (be[i], 0, 0)),
            pl.BlockSpec((1, d, de), lambda i, be, bn, nv: (be[i], 0, 0)),
            pl.BlockSpec((1, de, d), lambda i, be, bn, nv: (be[i], 0, 0)),
        ],
        out_specs=pl.BlockSpec(memory_space=pl.ANY),
        scratch_shapes=[pltpu.VMEM((d, de), BF16), pltpu.VMEM((d, de), BF16),
                        pltpu.VMEM((de, d), BF16),
                        pltpu.VMEM((2, rows * Y_SUBLANES, LANES), jnp.uint32),
                        pltpu.VMEM((8, rows), jnp.int32),
                        pltpu.SMEM((2, 8, rows), jnp.int32),
                        pltpu.SemaphoreType.DMA((2,)),
                        pltpu.SemaphoreType.DMA(())],
    )
    return pl.pallas_call(
        functools.partial(_experts_kernel, dump_row=n_out_rows),
        grid_spec=grid_spec,
        out_shape=jax.ShapeDtypeStruct(((n_out_rows + rows) * Y_SUBLANES, LANES), jnp.uint32),
        compiler_params=pltpu.CompilerParams(
            dimension_semantics=("arbitrary",), vmem_limit_bytes=VMEM_LIMIT),
        name="experts",
    )(blk_e, blk_n, n_valid, xin, w_gate, w_up, w_down)


def _combine_kernel(h1_ref, y0_ref, y1_ref, p_ref, wpg_ref, wpp_ref,
                    gple_ref, gpost_ref, gfin_ref, out_ref):
    tm = h1_ref.shape[0]
    y0, y1 = (_unpack_bf16_pairs([_tile_rows(ref, s, tm, Y_SUBLANES)[...] for s in range(Y_SUBLANES)])
              for ref in (y0_ref, y1_ref))
    h2 = h1_ref[...] + (y0 + y1)
    gate = _sigmoid(_dot(_rms(h2, gple_ref[...]).astype(BF16), wpg_ref[...]))
    ple = _rms(_dot(p_ref[...].astype(BF16), wpp_ref[...]), gpost_ref[...])
    h3 = h2 + gate * ple
    out_ref[...] = _rms(h3, gfin_ref[...])


def _combine(h1, p2, y, w_pg, w_pp, g_ple, g_post, g_final, tm):
    t, d = h1.shape
    pd = p2.shape[1]
    row = lambda i: (i, 0)
    const = lambda i: (0, 0)
    return pl.pallas_call(
        _combine_kernel,
        grid=(t // tm,),
        in_specs=[
            pl.BlockSpec((tm, d), row),
            pl.BlockSpec((tm * Y_SUBLANES, LANES), lambda i: (i, 0)),
            pl.BlockSpec((tm * Y_SUBLANES, LANES), lambda i: (t // tm + i, 0)),
            pl.BlockSpec((tm, pd), row),
            pl.BlockSpec((d, d), const),
            pl.BlockSpec((pd, d), const),
            pl.BlockSpec((1, d), const),
            pl.BlockSpec((1, d), const),
            pl.BlockSpec((1, d), const),
        ],
        out_specs=pl.BlockSpec((tm, d), row),
        out_shape=jax.ShapeDtypeStruct((t, d), F32),
        compiler_params=pltpu.CompilerParams(
            dimension_semantics=("arbitrary",), vmem_limit_bytes=VMEM_LIMIT),
        name="combine",
    )(h1, y, y, p2, w_pg.astype(BF16), w_pp.astype(BF16),
      g_ple.reshape(1, d), g_post.reshape(1, d), g_final.reshape(1, d))


def _largest_tile(n, cap):
    tile = cap
    while n % tile:
        tile //= 2
    return tile


def kernel(x, p, g_mix, w_in, b_gates, conv_q, conv_k, g_mhead, w_out, g_ffn, w_router_group,
           b_router_group, w_router_expert, b_router_expert, w_exp_gate, w_exp_up, w_exp_down,
           g_ple, w_ple_gate, w_ple_proj, g_ple_post, g_final):
    batch, seq_len, d = x.shape
    t = batch * seq_len
    tm = _largest_tile(seq_len, 512)
    n_exp = w_router_expert.shape[-1]
    rows = EXPERT_ROWS
    nblk = t * TOP_K // rows + n_exp

    assert w_in.shape[0] == 1, "single-layer block"
    l = 0
    h = x.reshape(t, d)
    mq, mk, mv, mo, gates, sq, sk, sv = _inproj(
        h, g_mix[l], w_in[l], conv_q[l], conv_k[l], seq_len, _largest_tile(seq_len, 1024))
    hm = _mlstm(mq, mk, mv, mo, gates, b_gates[l], g_mhead[l], batch, seq_len,
                _largest_tile(seq_len, 1024))
    hs = _stickbreak(sq, sk, sv, batch, seq_len)
    h1, c, route, counts = _outroute(
        h, hm, hs, w_out[l], g_ffn[l], w_router_group[l], b_router_group[l],
        w_router_expert[l], b_router_expert[l], tm)

    cnt = counts[0, LOGIT_LANE_E:LOGIT_LANE_E + n_exp].astype(jnp.int32)
    nb_e = (cnt + rows - 1) // rows
    cum = jnp.cumsum(nb_e)
    offs = (cum - nb_e) * rows
    offs_row = jnp.zeros((1, LANES), F32).at[0, 0:n_exp].set(offs.astype(F32))
    n_valid = cum[-1:]
    step = jnp.arange(nblk + 1, dtype=jnp.int32)
    blk_e = jnp.minimum(jnp.sum(cum[None, :] <= step[:, None], axis=1), n_exp - 1).astype(jnp.int32)
    mine = blk_e[:, None] == jnp.arange(n_exp, dtype=jnp.int32)[None, :]
    first = jnp.sum(jnp.where(mine, (cum - nb_e)[None, :], 0), axis=1)
    blk_n = jnp.clip(jnp.sum(jnp.where(mine, cnt[None, :], 0), axis=1) - rows * (step - first), 0, rows)
    tail = n_valid + jnp.arange(n_exp, dtype=jnp.int32)
    zero_blocks = jnp.concatenate([jnp.maximum(cum - 1, 0), jnp.minimum(tail, nblk - 1)])
    zero_use = jnp.concatenate([nb_e > 0, tail < nblk]).astype(jnp.int32)

    pos = _slotpos(route, offs_row, _largest_tile(t, 2048))
    xin = _dispatch(zero_blocks, zero_use, pos, c, route, nblk, tm)
    y = _experts(blk_e, blk_n, n_valid, xin, w_exp_gate[l], w_exp_up[l], w_exp_down[l], TOP_K * t)
    out = _combine(h1, p[l].reshape(t, -1), y, w_ple_gate[l], w_ple_proj[l],
                   g_ple[l], g_ple_post[l], g_final, _largest_tile(seq_len, 1024))
    return out.reshape(batch, seq_len, d)
```

```python
import functools

import jax
import jax.numpy as jnp
from jax import lax
from jax.experimental import pallas as pl
from jax.experimental.pallas import tpu as pltpu

F32 = jnp.float32
BF16 = jnp.bfloat16
EPS = 1e-6

M_HEADS = 4
M_HEAD_DIM = 128
SB_HEAD_DIM = 64
CONV_WIDTH = 4
TOP_K = 2
LANES = 128
VMEM_LIMIT = 56 * 1024 * 1024

MLSTM_CHUNK = 256
SB_BLOCK = 256
SB_GROUP = 4
SB_ZERO_LOG = -105.0
EXPERT_ROWS = 512
ROW_DMA_UNROLL = 8
X_SUBLANES = 5
Y_SUBLANES = 4
ROUTE_LANE_E = 0
ROUTE_LANE_W = 2
ROUTE_LANE_R = 4
SUBLANES = 8
META_SUBLANE = 4
META_DEST = 0
META_W = 1
LOGIT_LANE_E = 4


def _rms(x, g):
    return x * lax.rsqrt(jnp.mean(x * x, axis=-1, keepdims=True) + EPS) * g


def _sigmoid(x):
    return 1.0 / (1.0 + jnp.exp(-x))


def _split3(a):
    a1 = a.astype(BF16)
    r1 = a - a1.astype(F32)
    a2 = r1.astype(BF16)
    a3 = (r1 - a2.astype(F32)).astype(BF16)
    return a1, a2, a3


def _dot(a, b):
    return jnp.dot(a, b, preferred_element_type=F32)


def _dot_nt(a, b):
    return lax.dot_general(a, b, (((1,), (1,)), ((), ())), preferred_element_type=F32)


def _dot_tn(a, b):
    return lax.dot_general(a, b, (((0,), (0,)), ((), ())), preferred_element_type=F32)


def _inproj_kernel(x_ref, g_ref, wqk_ref, wvo_ref, wvt_ref, wg_ref, ws_ref, cq_ref, ck_ref,
                   mq_ref, mk_ref, mv_ref, mo_ref, gate_ref, sq_ref, sk_ref, sv_ref,
                   ext_ref, *, tiles_per_seq, k_scale):
    i = pl.program_id(0)
    tm = x_ref.shape[0]
    mw = mk_ref.shape[1]
    sw = sq_ref.shape[1]
    a = _rms(x_ref[...], g_ref[...]).astype(BF16)

    @pl.when(i % tiles_per_seq == 0)
    def _():
        ext_ref[0:8, :] = jnp.zeros((8, 2 * mw), F32)

    ext_ref[8:8 + tm, 0:mw] = _dot(a, wqk_ref[:, 0:mw])
    ext_ref[8:8 + tm, mw:2 * mw] = _dot(a, wqk_ref[:, mw:2 * mw])

    def conv_silu(w_ref, c0):
        acc = ext_ref[pl.ds(8 - (CONV_WIDTH - 1), tm), c0:c0 + mw] * w_ref[0:1, :]
        for j in range(1, CONV_WIDTH):
            acc = acc + ext_ref[pl.ds(8 - (CONV_WIDTH - 1) + j, tm), c0:c0 + mw] * w_ref[j:j + 1, :]
        return acc * _sigmoid(acc)

    mq_ref[...] = conv_silu(cq_ref, 0).T.astype(BF16)
    mk_ref[...] = (conv_silu(ck_ref, mw) * k_scale).astype(BF16)
    ext_ref[0:8, :] = ext_ref[tm:tm + 8, :]

    mv_ref[...] = _dot_nt(wvt_ref[...], a).astype(BF16)
    mo_ref[...] = _dot(a, wvo_ref[:, mw:2 * mw]).astype(BF16)
    gate_ref[...] = _dot(a, wg_ref[...])
    sq_ref[...] = _dot(a, ws_ref[:, 0:sw]).astype(BF16)
    sk_ref[...] = _dot(a, ws_ref[:, sw:2 * sw]).astype(BF16)
    sv_ref[...] = _dot(a, ws_ref[:, 2 * sw:3 * sw]).astype(BF16)


def _inproj(x2, g_mix, w_in, conv_q, conv_k, seq_len, tm):
    t, d = x2.shape
    mw = conv_q.shape[1]
    h = M_HEADS
    sw = (w_in.shape[1] - 4 * mw - 2 * h) // 3
    wqk = w_in[:, 0:2 * mw].astype(BF16)
    wvo = w_in[:, 2 * mw:4 * mw].astype(BF16)
    wvt = w_in[:, 2 * mw:3 * mw].T.astype(BF16)
    wg = jnp.zeros((d, 2 * LANES), F32)
    wg = wg.at[:, 0:h].set(w_in[:, 4 * mw:4 * mw + h])
    wg = wg.at[:, LANES:LANES + h].set(w_in[:, 4 * mw + h:4 * mw + 2 * h]).astype(BF16)
    ws = w_in[:, 4 * mw + 2 * h:]
    ws = jnp.concatenate([ws[:, 0:sw] * (SB_HEAD_DIM ** -0.5), ws[:, sw:]], axis=1).astype(BF16)
    row = lambda i: (i, 0)
    const = lambda i: (0, 0)
    kern = functools.partial(_inproj_kernel, tiles_per_seq=seq_len // tm, k_scale=M_HEAD_DIM ** -0.5)
    bf = lambda w: jax.ShapeDtypeStruct((t, w), BF16)
    bft = jax.ShapeDtypeStruct((mw, t), BF16)
    col = lambda i: (0, i)
    return pl.pallas_call(
        kern,
        grid=(t // tm,),
        in_specs=[
            pl.BlockSpec((tm, d), row),
            pl.BlockSpec((1, d), const),
            pl.BlockSpec((d, 2 * mw), const),
            pl.BlockSpec((d, 2 * mw), const),
            pl.BlockSpec((mw, d), const),
            pl.BlockSpec((d, 2 * LANES), const),
            pl.BlockSpec((d, 3 * sw), const),
            pl.BlockSpec((CONV_WIDTH, mw), const),
            pl.BlockSpec((CONV_WIDTH, mw), const),
        ],
        out_specs=[
            pl.BlockSpec((mw, tm), col), pl.BlockSpec((tm, mw), row),
            pl.BlockSpec((mw, tm), col), pl.BlockSpec((tm, mw), row),
            pl.BlockSpec((tm, 2 * LANES), row),
            pl.BlockSpec((tm, sw), row), pl.BlockSpec((tm, sw), row), pl.BlockSpec((tm, sw), row),
        ],
        out_shape=[bft, bf(mw), bft, bf(mw),
                   jax.ShapeDtypeStruct((t, 2 * LANES), F32), bf(sw), bf(sw), bf(sw)],
        scratch_shapes=[pltpu.VMEM((tm + 8, 2 * mw), F32)],
        compiler_params=pltpu.CompilerParams(
            dimension_semantics=("arbitrary",), vmem_limit_bytes=VMEM_LIMIT),
        name="inproj",
    )(x2, g_mix.reshape(1, d), wqk, wvo, wvt, wg, ws, conv_q, conv_k)


def _mlstm_kernel(q_ref, k_ref, v_ref, o_ref, gate_ref, bias_ref, gh_ref, out_ref,
                  c_ref, m_ref, *, chunk):
    L = chunk
    hd = M_HEAD_DIM
    nchunks = k_ref.shape[0] // L

    @pl.when(pl.program_id(1) == 0)
    def _():
        c_ref[...] = jnp.zeros(c_ref.shape, F32)
        m_ref[...] = jnp.zeros(m_ref.shape, F32)

    rows = lax.broadcasted_iota(jnp.int32, (L, L), 0)
    cols = lax.broadcasted_iota(jnp.int32, (L, L), 1)
    tri = (cols <= rows).astype(BF16)
    seen = rows <= cols
    ones_rows = (lax.broadcasted_iota(jnp.int32, (hd, L), 0) == 0).astype(BF16)

    def chunk_body(c, _):
        r0 = pl.multiple_of(c * L, L)
        g = gate_ref[pl.ds(r0, L), :] + bias_ref[...]
        gi = g[:, 0:LANES]
        gf = g[:, LANES:2 * LANES]
        lf = jnp.minimum(gf, 0.0) - jnp.log(1.0 + jnp.exp(-jnp.abs(gf)))
        l1, l2, l3 = _split3(lf)
        b = _dot(tri, l1) + _dot(tri, l2) + _dot(tri, l3)
        b_last = b[L - 1:L, :]
        w_end = b_last - b + gi
        m_loc = jnp.max(w_end, axis=0, keepdims=True)
        m_prev = m_ref[...]
        m_new = jnp.maximum(b_last + m_prev, m_loc)
        decay = jnp.exp(b_last + m_prev - m_new)
        scale = jnp.exp(m_loc - m_new)
        gmb = gi - b
        b_t = b.T
        e_end_t = jnp.exp(w_end - m_loc).T
        for h in range(M_HEADS):
            hs = slice(h * hd, (h + 1) * hd)
            qt = q_ref[hs, pl.ds(r0, L)]
            kh = k_ref[pl.ds(r0, L), hs]
            vext = jnp.concatenate([v_ref[hs, pl.ds(r0, L)], ones_rows], axis=0)
            b_row = b_t[h:h + 1, :]
            e = jnp.where(seen, b_row + gmb[:, h:h + 1], -jnp.inf)
            log_inter = b_row + m_prev[:, h:h + 1]
            m_t = jnp.maximum(log_inter, jnp.max(e, axis=0, keepdims=True))
            w = (jnp.exp(e - m_t) * _dot(kh, qt)).astype(BF16)
            a_int = jnp.exp(log_inter - m_t)
            cext = c_ref[h]
            num = _dot(vext, w) + a_int * _dot(cext.astype(BF16), qt)
            den = num[hd:hd + 1, :]
            hh = num[0:hd, :] / jnp.maximum(jnp.abs(den), jnp.exp(-m_t))
            hh = hh * lax.rsqrt(jnp.mean(hh * hh, axis=0, keepdims=True) + EPS) * gh_ref[hs, :]
            og = _sigmoid(o_ref[pl.ds(r0, L), hs].astype(F32))
            out_ref[pl.ds(r0, L), hs] = (og * hh.T).astype(BF16)
            ev = (vext.astype(F32) * e_end_t[h:h + 1, :]).astype(BF16)
            c_ref[h] = decay[:, h:h + 1] * cext + scale[:, h:h + 1] * _dot(ev, kh)
        m_ref[...] = m_new
        return 0

    lax.fori_loop(0, nchunks, chunk_body, 0, unroll=2)


def _mlstm(mq, mk, mv, mo, gates, b_gates, g_mhead, batch, seq_len, rows):
    t, mw = mk.shape
    h = M_HEADS
    bias = jnp.zeros((1, 2 * LANES), F32)
    bias = bias.at[0, 0:h].set(b_gates[0:h]).at[0, LANES:LANES + h].set(b_gates[h:2 * h])
    nb = seq_len // rows
    row = lambda b, i: (b * nb + i, 0)
    col = lambda b, i: (0, b * nb + i)
    const = lambda b, i: (0, 0)
    timed = pl.BlockSpec((mw, rows), col)
    rowed = pl.BlockSpec((rows, mw), row)
    return pl.pallas_call(
        functools.partial(_mlstm_kernel, chunk=MLSTM_CHUNK),
        grid=(batch, nb),
        in_specs=[timed, rowed, timed, rowed,
                  pl.BlockSpec((rows, 2 * LANES), row),
                  pl.BlockSpec((1, 2 * LANES), const),
                  pl.BlockSpec((mw, MLSTM_CHUNK), const)],
        out_specs=rowed,
        out_shape=jax.ShapeDtypeStruct((t, mw), BF16),
        scratch_shapes=[pltpu.VMEM((h, 2 * M_HEAD_DIM, M_HEAD_DIM), F32),
                        pltpu.VMEM((1, LANES), F32)],
        compiler_params=pltpu.CompilerParams(
            dimension_semantics=("arbitrary", "arbitrary"), vmem_limit_bytes=VMEM_LIMIT),
        name="mlstm",
    )(mq, mk, mv, mo, gates, bias,
      jnp.broadcast_to(g_mhead.reshape(mw, 1), (mw, MLSTM_CHUNK)))


def _sb_kernel(q_ref, k_ref, v_ref, out_ref, acc_ref, carry_ref, z_ref, sp_ref):
    blk = SB_BLOCK
    nq = q_ref.shape[0] // blk
    lane = lax.broadcasted_iota(jnp.int32, (blk, LANES), 1)
    head0 = lane < SB_HEAD_DIM
    rows = lax.broadcasted_iota(jnp.int32, (blk, blk), 0)
    cols = lax.broadcasted_iota(jnp.int32, (blk, blk), 1)
    strict = cols < rows
    neg_suffix = jnp.where(rows >= cols, -1.0, 0.0).astype(BF16)

    def split_heads(x):
        zero = jnp.zeros_like(x)
        return [jnp.where(head0, x, zero), jnp.where(head0, zero, x)]

    def rows_of(ref, j):
        return ref[pl.ds(pl.multiple_of(j * blk, blk), blk), :]

    def scores(qm_h, kb, mask):
        z = _dot_nt(qm_h, kb)
        neg_abs = lax.bitcast_convert_type(
            lax.bitcast_convert_type(z, jnp.uint32) | jnp.uint32(0x80000000), F32)
        sp = jnp.maximum(z, 0.0) + jnp.log(1.0 + jnp.exp(neg_abs))
        if mask is not None:
            sp = jnp.where(mask, sp, 0.0)
        return z, sp.astype(BF16)

    def weights(z, sp, carry, vm_h, mask):
        rc = _dot(sp, neg_suffix)
        p = jnp.exp(z + rc + carry)
        if mask is not None:
            p = jnp.where(mask, p, 0.0)
        return _dot(p.astype(BF16), vm_h), rc[:, 0:1]

    def first_half(qi, slot):
        qm = split_heads(rows_of(q_ref, qi))
        for n, (j, mask) in enumerate(((qi, strict), (jnp.maximum(qi - 1, 0), None))):
            kb = rows_of(k_ref, j)
            for h in range(2):
                z_ref[slot, 2 * n + h], sp_ref[slot, 2 * n + h] = scores(qm[h], kb, mask)

    def second_half(qi, slot, acc_ref, carry_ref):
        carries = [jnp.zeros((blk, 1), F32)] * 2
        upd = None
        for n, (j, mask, live) in enumerate(((qi, strict, None), (jnp.maximum(qi - 1, 0), None, qi > 0))):
            vm = split_heads(rows_of(v_ref, j))
            if live is not None:
                vm = [jnp.where(live, v, jnp.zeros_like(v)) for v in vm]
            new = []
            for h in range(2):
                d, total = weights(z_ref[slot, 2 * n + h], sp_ref[slot, 2 * n + h], carries[h], vm[h], mask)
                carry = carries[h] + total
                new.append(carry if live is None else jnp.where(live, carry, carries[h]))
                upd = d if upd is None else upd + d
            carries = new
        acc_ref[...] = upd
        carry_ref[0] = carries[0]
        carry_ref[1] = carries[1]
        return jnp.maximum(jnp.max(carries[0]), jnp.max(carries[1]))

    def remaining(qi, top, acc_ref, carry_ref):
        def cond(state):
            it, top = state
            return (it < qi) & (top > SB_ZERO_LOG)

        def body(state):
            it, _ = state
            j = qi - 1 - it
            qm = split_heads(rows_of(q_ref, qi))
            kb = rows_of(k_ref, j)
            vm = split_heads(rows_of(v_ref, j))
            upd = None
            tops = []
            for h in range(2):
                z, sp = scores(qm[h], kb, None)
                d, total = weights(z, sp, carry_ref[h], vm[h], None)
                carry_ref[h] = carry_ref[h] + total
                tops.append(jnp.max(carry_ref[h]))
                upd = d if upd is None else upd + d
            acc_ref[...] += upd
            return it + 1, jnp.maximum(tops[0], tops[1])

        lax.while_loop(cond, body, (jnp.int32(1), top))

    first_half(0, 0)

    group = acc_ref.shape[0]

    def query_group(g, _):
        qa = group * g
        tops = []
        for s in range(group):
            tops.append(second_half(qa + s, s % 2, acc_ref.at[s], carry_ref.at[s]))
            first_half(jnp.minimum(qa + s + 1, nq - 1), (s + 1) % 2)
        for s in range(group):
            remaining(qa + s, tops[s], acc_ref.at[s], carry_ref.at[s])
            out_ref[pl.ds(pl.multiple_of((qa + s) * blk, blk), blk), :] = acc_ref[s].astype(BF16)
        return 0

    assert group % 2 == 0 and nq % group == 0
    lax.fori_loop(0, nq // group, query_group, 0)


def _stickbreak(sq, sk, sv, batch, seq_len):
    t, sw = sq.shape
    npair = sw // LANES
    nq = seq_len // SB_BLOCK
    group = SB_GROUP if nq % SB_GROUP == 0 else 2
    seq = pl.BlockSpec((seq_len, LANES), lambda b, hp: (b, hp))
    return pl.pallas_call(
        _sb_kernel,
        grid=(batch, npair),
        in_specs=[seq, seq, seq],
        out_specs=seq,
        out_shape=jax.ShapeDtypeStruct((t, sw), BF16),
        scratch_shapes=[pltpu.VMEM((group, SB_BLOCK, LANES), F32),
                        pltpu.VMEM((group, 2, SB_BLOCK, 1), F32),
                        pltpu.VMEM((2, 4, SB_BLOCK, SB_BLOCK), F32),
                        pltpu.VMEM((2, 4, SB_BLOCK, SB_BLOCK), BF16)],
        compiler_params=pltpu.CompilerParams(
            dimension_semantics=("arbitrary", "arbitrary"), vmem_limit_bytes=VMEM_LIMIT),
        name="stickbrk",
    )(sq, sk, sv)


def _outroute_kernel(x_ref, hm_ref, hs_ref, wom_ref, wos_ref, g_ref, wr_ref, br_ref,
                     h1_ref, c_ref, route_ref, cnt_ref, run_ref, *, n_groups, per_group):
    i = pl.program_id(0)
    tm = x_ref.shape[0]

    @pl.when(i == 0)
    def _():
        run_ref[...] = jnp.zeros(run_ref.shape, F32)

    h1 = x_ref[...] + _dot(hm_ref[...], wom_ref[...]) + _dot(hs_ref[...], wos_ref[...])
    h1_ref[...] = h1
    c = _rms(h1, g_ref[...])
    c_ref[...] = c

    c1, c2, _ = _split3(c)
    pa = _dot(c1, wr_ref[...])
    pb = _dot(c2, wr_ref[...])
    logits = (pa[:, 0:LANES] + (pa[:, LANES:] + pb[:, 0:LANES]) + pb[:, LANES:]) + br_ref[...]

    lane = lax.broadcasted_iota(jnp.int32, (tm, LANES), 1).astype(F32)
    ninf = -jnp.inf
    big = float(LANES)

    def first_max(v):
        mx = jnp.max(v, axis=1, keepdims=True)
        idx = jnp.min(jnp.where(v == mx, lane, big), axis=1, keepdims=True)
        return mx, idx

    gl = jnp.where(lane < n_groups, logits, ninf)
    gmax, gsel = first_max(gl)
    p_g = 1.0 / jnp.sum(jnp.exp(gl - gmax), axis=1, keepdims=True)
    lo = LOGIT_LANE_E + per_group * gsel
    el = jnp.where((lane >= lo) & (lane < lo + per_group), logits, ninf)
    v1, i1 = first_max(el)
    v2, i2 = first_max(jnp.where(lane == i1, ninf, el))
    tt = jnp.exp(v2 - v1)
    w0 = p_g / (1.0 + tt)
    w1_ = p_g * tt / (1.0 + tt)

    oh0 = lane == i1
    oh1 = lane == i2
    ohsum = oh0.astype(F32) + oh1.astype(F32)
    rows = lax.broadcasted_iota(jnp.int32, (tm, tm), 0)
    cols = lax.broadcasted_iota(jnp.int32, (tm, tm), 1)
    before = (cols < rows).astype(BF16)
    prefix = _dot(before, ohsum.astype(BF16)) + run_ref[...]
    r0 = jnp.sum(jnp.where(oh0, prefix, 0.0), axis=1, keepdims=True)
    r1 = jnp.sum(jnp.where(oh1, prefix, 0.0), axis=1, keepdims=True)
    run = run_ref[...] + jnp.sum(ohsum, axis=0, keepdims=True)
    run_ref[...] = run
    cnt_ref[...] = jnp.broadcast_to(run, cnt_ref.shape)

    e0 = i1 - LOGIT_LANE_E
    e1 = i2 - LOGIT_LANE_E
    route = jnp.zeros((tm, LANES), F32)
    for ln, val in ((ROUTE_LANE_E, e0), (ROUTE_LANE_E + 1, e1), (ROUTE_LANE_W, w0),
                    (ROUTE_LANE_W + 1, w1_), (ROUTE_LANE_R, r0), (ROUTE_LANE_R + 1, r1)):
        route = jnp.where(lane == ln, val, route)
    route_ref[...] = route


def _tile_rows(ref, s, n, pitch=SUBLANES):
    return ref.at[pl.ds(s, n, stride=pitch), :]


def _tile_copy(src_ref, src_row, dst_ref, dst_row, sem, pitch):
    return pltpu.make_async_copy(src_ref.at[pl.ds(src_row * pitch, pitch), :],
                                 dst_ref.at[pl.ds(dst_row * pitch, pitch), :], sem)


def _lanes_to_smem(vals, vm_ref, sm_ref, sem):
    vm_ref[...] = vals.T[0:8, :].astype(jnp.int32)
    cp = pltpu.make_async_copy(vm_ref, sm_ref, sem)
    cp.start()
    cp.wait()


def _outroute(x2, hm, hs, w_out, g_ffn, w_rg, b_rg, w_re, b_re, tm):
    t, d = x2.shape
    mw = hm.shape[1]
    sw = hs.shape[1]
    n_groups = w_rg.shape[1]
    n_exp = w_re.shape[1]
    wr = jnp.zeros((d, LANES), F32)
    wr = wr.at[:, 0:n_groups].set(w_rg).at[:, LOGIT_LANE_E:LOGIT_LANE_E + n_exp].set(w_re)
    wr_hi, wr_lo, _ = _split3(wr)
    wr2 = jnp.concatenate([wr_hi, wr_lo], axis=1)
    br = jnp.zeros((1, LANES), F32)
    br = br.at[0, 0:n_groups].set(b_rg).at[0, LOGIT_LANE_E:LOGIT_LANE_E + n_exp].set(b_re)
    row = lambda i: (i, 0)
    const = lambda i: (0, 0)
    kern = functools.partial(_outroute_kernel, n_groups=n_groups, per_group=n_exp // n_groups)
    return pl.pallas_call(
        kern,
        grid=(t // tm,),
        in_specs=[
            pl.BlockSpec((tm, d), row),
            pl.BlockSpec((tm, mw), row),
            pl.BlockSpec((tm, sw), row),
            pl.BlockSpec((mw, d), const),
            pl.BlockSpec((sw, d), const),
            pl.BlockSpec((1, d), const),
            pl.BlockSpec((d, 2 * LANES), const),
            pl.BlockSpec((1, LANES), const),
        ],
        out_specs=[
            pl.BlockSpec((tm, d), row),
            pl.BlockSpec((tm, d), row),
            pl.BlockSpec((tm, LANES), row),
            pl.BlockSpec((8, LANES), const),
        ],
        out_shape=[
            jax.ShapeDtypeStruct((t, d), F32),
            jax.ShapeDtypeStruct((t, d), F32),
            jax.ShapeDtypeStruct((t, LANES), F32),
            jax.ShapeDtypeStruct((8, LANES), F32),
        ],
        scratch_shapes=[pltpu.VMEM((1, LANES), F32)],
        compiler_params=pltpu.CompilerParams(
            dimension_semantics=("arbitrary",), vmem_limit_bytes=VMEM_LIMIT),
        name="outroute",
    )(x2, hm, hs, w_out[0:mw].astype(BF16), w_out[mw:].astype(BF16), g_ffn.reshape(1, d),
      wr2, br)


def _slotpos_kernel(route_ref, offs_ref, pos_ref):
    route = route_ref[...]
    tm = route.shape[0]
    lane = lax.broadcasted_iota(jnp.int32, (tm, LANES), 1)
    offs = offs_ref[...]
    out = jnp.zeros((tm, LANES), F32)
    for j in range(TOP_K):
        e = route[:, ROUTE_LANE_E + j:ROUTE_LANE_E + j + 1].astype(jnp.int32)
        base = jnp.sum(jnp.where(lane == e, offs, 0.0), axis=1, keepdims=True)
        out = jnp.where(lane == j, base + route[:, ROUTE_LANE_R + j:ROUTE_LANE_R + j + 1], out)
    pos_ref[...] = out.T[0:SUBLANES, :].astype(jnp.int32)


def _slotpos(route, offs_row, tm):
    t = route.shape[0]
    return pl.pallas_call(
        _slotpos_kernel,
        grid=(t // tm,),
        in_specs=[pl.BlockSpec((tm, LANES), lambda i: (i, 0)),
                  pl.BlockSpec((1, LANES), lambda i: (0, 0))],
        out_specs=pl.BlockSpec((SUBLANES, tm), lambda i: (0, i)),
        out_shape=jax.ShapeDtypeStruct((SUBLANES, t), jnp.int32),
        compiler_params=pltpu.CompilerParams(dimension_semantics=("arbitrary",)),
        name="slotpos",
    )(route, offs_row)


def _dispatch_kernel(zblk_ref, zuse_ref, pos_ref, c_ref, route_ref, xin_ref, rows_ref, zero_ref,
                     sem, zsem, *, n_tokens):
    i = pl.program_id(0)
    tm, d = c_ref.shape

    @pl.when(i == 0)
    def _():
        zero_ref[...] = jnp.zeros(zero_ref.shape, jnp.uint32)
        n = zero_ref.shape[0]

        def zero_block(k):
            return pltpu.make_async_copy(
                zero_ref, xin_ref.at[pl.ds(pl.multiple_of(zblk_ref[k] * n, n), n), :], zsem)

        for k in range(zblk_ref.shape[0]):
            pl.when(zuse_ref[k] != 0)(lambda k=k: zero_block(k).start())
        for k in range(zblk_ref.shape[0]):
            pl.when(zuse_ref[k] != 0)(lambda k=k: zero_block(k).wait())

    half = d // 2
    lane = lax.broadcasted_iota(jnp.int32, (tm, LANES), 1)
    row_id = (i * tm + lax.broadcasted_iota(jnp.int32, (tm, 1), 0)).astype(F32)
    route = route_ref[...]
    packed = _pack_bf16_pairs(c_ref[...])
    for j in range(TOP_K):
        w = route[:, ROUTE_LANE_W + j:ROUTE_LANE_W + j + 1]
        meta = jnp.where(lane == META_DEST, row_id + j * n_tokens, jnp.where(lane == META_W, w, 0.0))
        for s in range(X_SUBLANES):
            if s < half // LANES:
                sub = packed[:, s * LANES:(s + 1) * LANES]
            else:
                sub = lax.bitcast_convert_type(meta, jnp.uint32)
            _tile_rows(rows_ref.at[j], s, tm, X_SUBLANES)[...] = sub

    def issue(t, _):
        for j in range(TOP_K):
            _tile_copy(rows_ref.at[j], t, xin_ref, pos_ref[j, t], sem, X_SUBLANES).start(priority=j)
        return 0

    lax.fori_loop(0, tm, issue, 0, unroll=ROW_DMA_UNROLL)
    for j in range(TOP_K):
        pltpu.make_async_copy(rows_ref.at[j], rows_ref.at[j], sem).wait()


def _dispatch(zero_blocks, zero_use, pos, c, route, n_blocks, tm):
    t, d = c.shape
    assert d // 2 // LANES == META_SUBLANE == X_SUBLANES - 1
    rows = EXPERT_ROWS
    grid_spec = pltpu.PrefetchScalarGridSpec(
        num_scalar_prefetch=2,
        grid=(t // tm,),
        in_specs=[
            pl.BlockSpec((SUBLANES, tm), lambda i, *_: (0, i), memory_space=pltpu.SMEM),
            pl.BlockSpec((tm, d), lambda i, *_: (i, 0)),
            pl.BlockSpec((tm, LANES), lambda i, *_: (i, 0)),
        ],
        out_specs=pl.BlockSpec(memory_space=pl.ANY),
        scratch_shapes=[pltpu.VMEM((TOP_K, tm * X_SUBLANES, LANES), jnp.uint32),
                        pltpu.VMEM((rows * X_SUBLANES, LANES), jnp.uint32),
                        pltpu.SemaphoreType.DMA(()),
                        pltpu.SemaphoreType.DMA(())],
    )
    return pl.pallas_call(
        functools.partial(_dispatch_kernel, n_tokens=t),
        grid_spec=grid_spec,
        out_shape=jax.ShapeDtypeStruct((n_blocks * rows * X_SUBLANES, LANES), jnp.uint32),
        compiler_params=pltpu.CompilerParams(
            dimension_semantics=("arbitrary",), vmem_limit_bytes=VMEM_LIMIT),
        name="dispatch",
    )(zero_blocks, zero_use, pos, c, route)


def _pack_bf16_pairs(v):
    half = v.shape[1] // 2
    bits = lax.bitcast_convert_type(v.astype(BF16).astype(F32), jnp.uint32)
    return (bits[:, 0:half] >> 16) | bits[:, half:]


def _unpack_bf16_pairs(words):
    lo = [lax.bitcast_convert_type(w << 16, F32) for w in words]
    hi = [lax.bitcast_convert_type(w & jnp.uint32(0xFFFF0000), F32) for w in words]
    return jnp.concatenate(lo + hi, axis=1)


def _experts_kernel(be_ref, bn_ref, nv_ref, x_ref, wg_ref, wu_ref, wd_ref, yout_ref,
                    wgb, wub, wdb, ybuf, dest_vm, dest_sm, sems, dsem, *, dump_row):
    i = pl.program_id(0)
    s = i % 2
    rows = ybuf.shape[1] // Y_SUBLANES
    prev = be_ref[jnp.maximum(i - 1, 0)]
    active = i < nv_ref[0]

    def to_dump(slot):
        def body(r, _):
            dest_sm[slot, 0, r] = dump_row + r
            return 0
        lax.fori_loop(0, rows, body, 0)

    def send(slot, r, queue=0):
        _tile_copy(ybuf.at[slot], r, yout_ref, dest_sm[slot, 0, r], sems.at[slot],
                   Y_SUBLANES).start(priority=queue)

    def wait(slot):
        pltpu.make_async_copy(ybuf.at[slot], ybuf.at[slot], sems.at[slot]).wait()

    @pl.when(i == 0)
    def _():
        ybuf[1] = jnp.zeros(ybuf.shape[1:], jnp.uint32)
        to_dump(1)

    @pl.when(i > 0)
    def _():
        wait(s)

    @pl.when(active & ((i == 0) | (be_ref[i] != prev)))
    def _():
        wgb[...] = wg_ref[0].astype(BF16)
        wub[...] = wu_ref[0].astype(BF16)
        wdb[...] = wd_ref[0].astype(BF16)

    @pl.when(active)
    def _():
        for r in range(rows):
            send(1 - s, r, r % 2)
        x = _unpack_bf16_pairs([_tile_rows(x_ref, t, rows, X_SUBLANES)[...]
                                for t in range(META_SUBLANE)]).astype(BF16)
        meta = lax.bitcast_convert_type(_tile_rows(x_ref, META_SUBLANE, rows, X_SUBLANES)[...], F32)
        gt = _dot(x, wgb[...])
        up = _dot(x, wub[...])
        hid = (gt * _sigmoid(gt) * up).astype(BF16)
        y = _pack_bf16_pairs(_dot(hid, wdb[...]) * meta[:, META_W:META_W + 1])
        for t in range(Y_SUBLANES):
            _tile_rows(ybuf.at[s], t, rows, Y_SUBLANES)[...] = y[:, t * LANES:(t + 1) * LANES]
        row = lax.broadcasted_iota(jnp.int32, (rows, LANES), 0)
        dest = jnp.where(row < bn_ref[i], meta, (dump_row + row).astype(F32))
        _lanes_to_smem(dest, dest_vm, dest_sm.at[s], dsem)

    @pl.when(jnp.logical_not(active))
    def _():
        lax.fori_loop(0, rows, lambda r, _: send(1 - s, r) or 0, 0, unroll=ROW_DMA_UNROLL)
        to_dump(s)

    @pl.when(i == pl.num_programs(0) - 1)
    def _():
        wait(1 - s)


def _experts(blk_e, blk_n, n_valid, xin, w_gate, w_up, w_down, n_out_rows):
    rows = EXPERT_ROWS
    nblk = xin.shape[0] // (rows * X_SUBLANES)
    d, de = w_gate.shape[1], w_gate.shape[2]
    assert d == 2 * Y_SUBLANES * LANES, "an output row is 4 sublanes of packed bf16 pairs"
    grid_spec = pltpu.PrefetchScalarGridSpec(
        num_scalar_prefetch=3,
        grid=(nblk + 1,),
        in_specs=[
            pl.BlockSpec((rows * X_SUBLANES, LANES),
                         lambda i, be, bn, nv: (jnp.minimum(i, nblk - 1), 0)),
            pl.BlockSpec((1, d, de), lambda i, be, bn, nv: (be[i], 0, 0)),
            pl.BlockSpec((1, d, de), lambda i, be, bn, nv: (be[i], 0, 0)),
            pl.BlockSpec((1, de, d), lambda i, be, bn, nv: (be[i], 0, 0)),
        ],
        out_specs=pl.BlockSpec(memory_space=pl.ANY),
        scratch_shapes=[pltpu.VMEM((d, de), BF16), pltpu.VMEM((d, de), BF16),
                        pltpu.VMEM((de, d), BF16),
                        pltpu.VMEM((2, rows * Y_SUBLANES, LANES), jnp.uint32),
                        pltpu.VMEM((8, rows), jnp.int32),
                        pltpu.SMEM((2, 8, rows), jnp.int32),
                        pltpu.SemaphoreType.DMA((2,)),
                        pltpu.SemaphoreType.DMA(())],
    )
    return pl.pallas_call(
        functools.partial(_experts_kernel, dump_row=n_out_rows),
        grid_spec=grid_spec,
        out_shape=jax.ShapeDtypeStruct(((n_out_rows + rows) * Y_SUBLANES, LANES), jnp.uint32),
        compiler_params=pltpu.CompilerParams(
            dimension_semantics=("arbitrary",), vmem_limit_bytes=VMEM_LIMIT),
        name="experts",
    )(blk_e, blk_n, n_valid, xin, w_gate, w_up, w_down)


def _combine_kernel(h1_ref, y0_ref, y1_ref, p_ref, wpg_ref, wpp_ref,
                    gple_ref, gpost_ref, gfin_ref, out_ref):
    tm = h1_ref.shape[0]
    y0, y1 = (_unpack_bf16_pairs([_tile_rows(ref, s, tm, Y_SUBLANES)[...] for s in range(Y_SUBLANES)])
              for ref in (y0_ref, y1_ref))
    h2 = h1_ref[...] + (y0 + y1)
    gate = _sigmoid(_dot(_rms(h2, gple_ref[...]).astype(BF16), wpg_ref[...]))
    ple = _rms(_dot(p_ref[...].astype(BF16), wpp_ref[...]), gpost_ref[...])
    h3 = h2 + gate * ple
    out_ref[...] = _rms(h3, gfin_ref[...])


def _combine(h1, p2, y, w_pg, w_pp, g_ple, g_post, g_final, tm):
    t, d = h1.shape
    pd = p2.shape[1]
    row = lambda i: (i, 0)
    const = lambda i: (0, 0)
    return pl.pallas_call(
        _combine_kernel,
        grid=(t // tm,),
        in_specs=[
            pl.BlockSpec((tm, d), row),
            pl.BlockSpec((tm * Y_SUBLANES, LANES), lambda i: (i, 0)),
            pl.BlockSpec((tm * Y_SUBLANES, LANES), lambda i: (t // tm + i, 0)),
            pl.BlockSpec((tm, pd), row),
            pl.BlockSpec((d, d), const),
            pl.BlockSpec((pd, d), const),
            pl.BlockSpec((1, d), const),
            pl.BlockSpec((1, d), const),
            pl.BlockSpec((1, d), const),
        ],
        out_specs=pl.BlockSpec((tm, d), row),
        out_shape=jax.ShapeDtypeStruct((t, d), F32),
        compiler_params=pltpu.CompilerParams(
            dimension_semantics=("arbitrary",), vmem_limit_bytes=VMEM_LIMIT),
        name="combine",
    )(h1, y, y, p2, w_pg.astype(BF16), w_pp.astype(BF16),
      g_ple.reshape(1, d), g_post.reshape(1, d), g_final.reshape(1, d))


def _largest_tile(n, cap):
    tile = cap
    while n % tile:
        tile //= 2
    return tile


def kernel(x, p, g_mix, w_in, b_gates, conv_q, conv_k, g_mhead, w_out, g_ffn, w_router_group,
           b_router_group, w_router_expert, b_router_expert, w_exp_gate, w_exp_up, w_exp_down,
           g_ple, w_ple_gate, w_ple_proj, g_ple_post, g_final):
    batch, seq_len, d = x.shape
    t = batch * seq_len
    tm = _largest_tile(seq_len, 512)
    n_exp = w_router_expert.shape[-1]
    rows = EXPERT_ROWS
    nblk = t * TOP_K // rows + n_exp

    assert w_in.shape[0] == 1, "single-layer block"
    l = 0
    h = x.reshape(t, d)
    mq, mk, mv, mo, gates, sq, sk, sv = _inproj(
        h, g_mix[l], w_in[l], conv_q[l], conv_k[l], seq_len, _largest_tile(seq_len, 1024))
    hm = _mlstm(mq, mk, mv, mo, gates, b_gates[l], g_mhead[l], batch, seq_len,
                _largest_tile(seq_len, 1024))
    hs = _stickbreak(sq, sk, sv, batch, seq_len)
    h1, c, route, counts = _outroute(
        h, hm, hs, w_out[l], g_ffn[l], w_router_group[l], b_router_group[l],
        w_router_expert[l], b_router_expert[l], tm)

    cnt = counts[0, LOGIT_LANE_E:LOGIT_LANE_E + n_exp].astype(jnp.int32)
    nb_e = (cnt + rows - 1) // rows
    cum = jnp.cumsum(nb_e)
    offs = (cum - nb_e) * rows
    offs_row = jnp.zeros((1, LANES), F32).at[0, 0:n_exp].set(offs.astype(F32))
    n_valid = cum[-1:]
    step = jnp.arange(nblk + 1, dtype=jnp.int32)
    blk_e = jnp.minimum(jnp.sum(cum[None, :] <= step[:, None], axis=1), n_exp - 1).astype(jnp.int32)
    mine = blk_e[:, None] == jnp.arange(n_exp, dtype=jnp.int32)[None, :]
    first = jnp.sum(jnp.where(mine, (cum - nb_e)[None, :], 0), axis=1)
    blk_n = jnp.clip(jnp.sum(jnp.where(mine, cnt[None, :], 0), axis=1) - rows * (step - first), 0, rows)
    tail = n_valid + jnp.arange(n_exp, dtype=jnp.int32)
    zero_blocks = jnp.concatenate([jnp.maximum(cum - 1, 0), jnp.minimum(tail, nblk - 1)])
    zero_use = jnp.concatenate([nb_e > 0, tail < nblk]).astype(jnp.int32)

    pos = _slotpos(route, offs_row, _largest_tile(t, 2048))
    xin = _dispatch(zero_blocks, zero_use, pos, c, route, nblk, _largest_tile(seq_len, 1024))
    y = _experts(blk_e, blk_n, n_valid, xin, w_exp_gate[l], w_exp_up[l], w_exp_down[l], TOP_K * t)
    out = _combine(h1, p[l].reshape(t, -1), y, w_ple_gate[l], w_ple_proj[l],
                   g_ple[l], g_ple_post[l], g_final, _largest_tile(seq_len, 1024))
    return out.reshape(batch, seq_len, d)
```

```python
import functools

import jax
import jax.numpy as jnp
from jax import lax
from jax.experimental import pallas as pl
from jax.experimental.pallas import tpu as pltpu

F32 = jnp.float32
BF16 = jnp.bfloat16
EPS = 1e-6

M_HEADS = 4
M_HEAD_DIM = 128
SB_HEAD_DIM = 64
CONV_WIDTH = 4
TOP_K = 2
LANES = 128
VMEM_LIMIT = 56 * 1024 * 1024

MLSTM_CHUNK = 256
SB_BLOCK = 256
SB_GROUP = 8
SB_ZERO_LOG = -105.0
EXPERT_ROWS = 512
ROW_DMA_UNROLL = 8
X_SUBLANES = 5
Y_SUBLANES = 4
ROUTE_LANE_E = 0
ROUTE_LANE_W = 2
ROUTE_LANE_R = 4
SUBLANES = 8
META_SUBLANE = 4
META_DEST = 0
META_W = 1
LOGIT_LANE_E = 4


def _rms(x, g):
    return x * lax.rsqrt(jnp.mean(x * x, axis=-1, keepdims=True) + EPS) * g


def _sigmoid(x):
    return 1.0 / (1.0 + jnp.exp(-x))


def _split3(a):
    a1 = a.astype(BF16)
    r1 = a - a1.astype(F32)
    a2 = r1.astype(BF16)
    a3 = (r1 - a2.astype(F32)).astype(BF16)
    return a1, a2, a3


def _dot(a, b):
    return jnp.dot(a, b, preferred_element_type=F32)


def _dot_nt(a, b):
    return lax.dot_general(a, b, (((1,), (1,)), ((), ())), preferred_element_type=F32)


def _dot_tn(a, b):
    return lax.dot_general(a, b, (((0,), (0,)), ((), ())), preferred_element_type=F32)


def _inproj_kernel(x_ref, g_ref, wqk_ref, wvo_ref, wvt_ref, wg_ref, ws_ref, cq_ref, ck_ref,
                   mq_ref, mk_ref, mv_ref, mo_ref, gate_ref, sq_ref, sk_ref, sv_ref,
                   ext_ref, *, tiles_per_seq, k_scale):
    i = pl.program_id(0)
    tm = x_ref.shape[0]
    mw = mk_ref.shape[1]
    sw = sq_ref.shape[1]
    a = _rms(x_ref[...], g_ref[...]).astype(BF16)

    @pl.when(i % tiles_per_seq == 0)
    def _():
        ext_ref[0:8, :] = jnp.zeros((8, 2 * mw), F32)

    ext_ref[8:8 + tm, 0:mw] = _dot(a, wqk_ref[:, 0:mw])
    ext_ref[8:8 + tm, mw:2 * mw] = _dot(a, wqk_ref[:, mw:2 * mw])

    def conv_silu(w_ref, c0):
        acc = ext_ref[pl.ds(8 - (CONV_WIDTH - 1), tm), c0:c0 + mw] * w_ref[0:1, :]
        for j in range(1, CONV_WIDTH):
            acc = acc + ext_ref[pl.ds(8 - (CONV_WIDTH - 1) + j, tm), c0:c0 + mw] * w_ref[j:j + 1, :]
        return acc * _sigmoid(acc)

    mq_ref[...] = conv_silu(cq_ref, 0).T.astype(BF16)
    mk_ref[...] = (conv_silu(ck_ref, mw) * k_scale).astype(BF16)
    ext_ref[0:8, :] = ext_ref[tm:tm + 8, :]

    mv_ref[...] = _dot_nt(wvt_ref[...], a).astype(BF16)
    mo_ref[...] = _dot(a, wvo_ref[:, mw:2 * mw]).astype(BF16)
    gate_ref[...] = _dot(a, wg_ref[...])
    sq_ref[...] = _dot(a, ws_ref[:, 0:sw]).astype(BF16)
    sk_ref[...] = _dot(a, ws_ref[:, sw:2 * sw]).astype(BF16)
    sv_ref[...] = _dot(a, ws_ref[:, 2 * sw:3 * sw]).astype(BF16)


def _inproj(x2, g_mix, w_in, conv_q, conv_k, seq_len, tm):
    t, d = x2.shape
    mw = conv_q.shape[1]
    h = M_HEADS
    sw = (w_in.shape[1] - 4 * mw - 2 * h) // 3
    wqk = w_in[:, 0:2 * mw].astype(BF16)
    wvo = w_in[:, 2 * mw:4 * mw].astype(BF16)
    wvt = w_in[:, 2 * mw:3 * mw].T.astype(BF16)
    wg = jnp.zeros((d, 2 * LANES), F32)
    wg = wg.at[:, 0:h].set(w_in[:, 4 * mw:4 * mw + h])
    wg = wg.at[:, LANES:LANES + h].set(w_in[:, 4 * mw + h:4 * mw + 2 * h]).astype(BF16)
    ws = w_in[:, 4 * mw + 2 * h:]
    ws = jnp.concatenate([ws[:, 0:sw] * (SB_HEAD_DIM ** -0.5), ws[:, sw:]], axis=1).astype(BF16)
    row = lambda i: (i, 0)
    const = lambda i: (0, 0)
    kern = functools.partial(_inproj_kernel, tiles_per_seq=seq_len // tm, k_scale=M_HEAD_DIM ** -0.5)
    bf = lambda w: jax.ShapeDtypeStruct((t, w), BF16)
    bft = jax.ShapeDtypeStruct((mw, t), BF16)
    col = lambda i: (0, i)
    return pl.pallas_call(
        kern,
        grid=(t // tm,),
        in_specs=[
            pl.BlockSpec((tm, d), row),
            pl.BlockSpec((1, d), const),
            pl.BlockSpec((d, 2 * mw), const),
            pl.BlockSpec((d, 2 * mw), const),
            pl.BlockSpec((mw, d), const),
            pl.BlockSpec((d, 2 * LANES), const),
            pl.BlockSpec((d, 3 * sw), const),
            pl.BlockSpec((CONV_WIDTH, mw), const),
            pl.BlockSpec((CONV_WIDTH, mw), const),
        ],
        out_specs=[
            pl.BlockSpec((mw, tm), col), pl.BlockSpec((tm, mw), row),
            pl.BlockSpec((mw, tm), col), pl.BlockSpec((tm, mw), row),
            pl.BlockSpec((tm, 2 * LANES), row),
            pl.BlockSpec((tm, sw), row), pl.BlockSpec((tm, sw), row), pl.BlockSpec((tm, sw), row),
        ],
        out_shape=[bft, bf(mw), bft, bf(mw),
                   jax.ShapeDtypeStruct((t, 2 * LANES), F32), bf(sw), bf(sw), bf(sw)],
        scratch_shapes=[pltpu.VMEM((tm + 8, 2 * mw), F32)],
        compiler_params=pltpu.CompilerParams(
            dimension_semantics=("arbitrary",), vmem_limit_bytes=VMEM_LIMIT),
        name="inproj",
    )(x2, g_mix.reshape(1, d), wqk, wvo, wvt, wg, ws, conv_q, conv_k)


def _mlstm_kernel(q_ref, k_ref, v_ref, o_ref, gate_ref, bias_ref, gh_ref, out_ref,
                  c_ref, m_ref, *, chunk):
    L = chunk
    hd = M_HEAD_DIM
    nchunks = k_ref.shape[0] // L

    @pl.when(pl.program_id(1) == 0)
    def _():
        c_ref[...] = jnp.zeros(c_ref.shape, F32)
        m_ref[...] = jnp.zeros(m_ref.shape, F32)

    rows = lax.broadcasted_iota(jnp.int32, (L, L), 0)
    cols = lax.broadcasted_iota(jnp.int32, (L, L), 1)
    tri = (cols <= rows).astype(BF16)
    seen = rows <= cols
    ones_rows = (lax.broadcasted_iota(jnp.int32, (hd, L), 0) == 0).astype(BF16)

    def chunk_body(c, _):
        r0 = pl.multiple_of(c * L, L)
        g = gate_ref[pl.ds(r0, L), :] + bias_ref[...]
        gi = g[:, 0:LANES]
        gf = g[:, LANES:2 * LANES]
        lf = jnp.minimum(gf, 0.0) - jnp.log(1.0 + jnp.exp(-jnp.abs(gf)))
        l1, l2, l3 = _split3(lf)
        b = _dot(tri, l1) + _dot(tri, l2) + _dot(tri, l3)
        b_last = b[L - 1:L, :]
        w_end = b_last - b + gi
        m_loc = jnp.max(w_end, axis=0, keepdims=True)
        m_prev = m_ref[...]
        m_new = jnp.maximum(b_last + m_prev, m_loc)
        decay = jnp.exp(b_last + m_prev - m_new)
        scale = jnp.exp(m_loc - m_new)
        gmb = gi - b
        b_t = b.T
        e_end_t = jnp.exp(w_end - m_loc).T
        for h in range(M_HEADS):
            hs = slice(h * hd, (h + 1) * hd)
            qt = q_ref[hs, pl.ds(r0, L)]
            kh = k_ref[pl.ds(r0, L), hs]
            vext = jnp.concatenate([v_ref[hs, pl.ds(r0, L)], ones_rows], axis=0)
            b_row = b_t[h:h + 1, :]
            e = jnp.where(seen, b_row + gmb[:, h:h + 1], -jnp.inf)
            log_inter = b_row + m_prev[:, h:h + 1]
            m_t = jnp.maximum(log_inter, jnp.max(e, axis=0, keepdims=True))
            w = (jnp.exp(e - m_t) * _dot(kh, qt)).astype(BF16)
            a_int = jnp.exp(log_inter - m_t)
            cext = c_ref[h]
            num = _dot(vext, w) + a_int * _dot(cext.astype(BF16), qt)
            den = num[hd:hd + 1, :]
            hh = num[0:hd, :] / jnp.maximum(jnp.abs(den), jnp.exp(-m_t))
            hh = hh * lax.rsqrt(jnp.mean(hh * hh, axis=0, keepdims=True) + EPS) * gh_ref[hs, :]
            og = _sigmoid(o_ref[pl.ds(r0, L), hs].astype(F32))
            out_ref[pl.ds(r0, L), hs] = (og * hh.T).astype(BF16)
            ev = (vext.astype(F32) * e_end_t[h:h + 1, :]).astype(BF16)
            c_ref[h] = decay[:, h:h + 1] * cext + scale[:, h:h + 1] * _dot(ev, kh)
        m_ref[...] = m_new
        return 0

    lax.fori_loop(0, nchunks, chunk_body, 0, unroll=4)


def _mlstm(mq, mk, mv, mo, gates, b_gates, g_mhead, batch, seq_len, rows):
    t, mw = mk.shape
    h = M_HEADS
    bias = jnp.zeros((1, 2 * LANES), F32)
    bias = bias.at[0, 0:h].set(b_gates[0:h]).at[0, LANES:LANES + h].set(b_gates[h:2 * h])
    nb = seq_len // rows
    row = lambda b, i: (b * nb + i, 0)
    col = lambda b, i: (0, b * nb + i)
    const = lambda b, i: (0, 0)
    timed = pl.BlockSpec((mw, rows), col)
    rowed = pl.BlockSpec((rows, mw), row)
    return pl.pallas_call(
        functools.partial(_mlstm_kernel, chunk=MLSTM_CHUNK),
        grid=(batch, nb),
        in_specs=[timed, rowed, timed, rowed,
                  pl.BlockSpec((rows, 2 * LANES), row),
                  pl.BlockSpec((1, 2 * LANES), const),
                  pl.BlockSpec((mw, MLSTM_CHUNK), const)],
        out_specs=rowed,
        out_shape=jax.ShapeDtypeStruct((t, mw), BF16),
        scratch_shapes=[pltpu.VMEM((h, 2 * M_HEAD_DIM, M_HEAD_DIM), F32),
                        pltpu.VMEM((1, LANES), F32)],
        compiler_params=pltpu.CompilerParams(
            dimension_semantics=("arbitrary", "arbitrary"), vmem_limit_bytes=VMEM_LIMIT),
        name="mlstm",
    )(mq, mk, mv, mo, gates, bias,
      jnp.broadcast_to(g_mhead.reshape(mw, 1), (mw, MLSTM_CHUNK)))


def _sb_kernel(q_ref, k_ref, v_ref, out_ref, acc_ref, carry_ref, z_ref, sp_ref):
    blk = SB_BLOCK
    nq = q_ref.shape[0] // blk
    lane = lax.broadcasted_iota(jnp.int32, (blk, LANES), 1)
    head0 = lane < SB_HEAD_DIM
    rows = lax.broadcasted_iota(jnp.int32, (blk, blk), 0)
    cols = lax.broadcasted_iota(jnp.int32, (blk, blk), 1)
    strict = cols < rows
    neg_suffix = jnp.where(rows >= cols, -1.0, 0.0).astype(BF16)

    def split_heads(x):
        zero = jnp.zeros_like(x)
        return [jnp.where(head0, x, zero), jnp.where(head0, zero, x)]

    def rows_of(ref, j):
        return ref[pl.ds(pl.multiple_of(j * blk, blk), blk), :]

    def scores(qm_h, kb, mask):
        z = _dot_nt(qm_h, kb)
        neg_abs = lax.bitcast_convert_type(
            lax.bitcast_convert_type(z, jnp.uint32) | jnp.uint32(0x80000000), F32)
        sp = jnp.maximum(z, 0.0) + jnp.log(1.0 + jnp.exp(neg_abs))
        if mask is not None:
            sp = jnp.where(mask, sp, 0.0)
        return z, sp.astype(BF16)

    def weights(z, sp, carry, vm_h, mask):
        rc = _dot(sp, neg_suffix)
        p = jnp.exp(z + rc + carry)
        if mask is not None:
            p = jnp.where(mask, p, 0.0)
        return _dot(p.astype(BF16), vm_h), rc[:, 0:1]

    def first_half(qi, slot):
        qm = split_heads(rows_of(q_ref, qi))
        for n, (j, mask) in enumerate(((qi, strict), (jnp.maximum(qi - 1, 0), None))):
            kb = rows_of(k_ref, j)
            for h in range(2):
                z_ref[slot, 2 * n + h], sp_ref[slot, 2 * n + h] = scores(qm[h], kb, mask)

    def second_half(qi, slot, acc_ref, carry_ref):
        carries = [jnp.zeros((blk, 1), F32)] * 2
        upd = None
        for n, (j, mask, live) in enumerate(((qi, strict, None), (jnp.maximum(qi - 1, 0), None, qi > 0))):
            vm = split_heads(rows_of(v_ref, j))
            if live is not None:
                vm = [jnp.where(live, v, jnp.zeros_like(v)) for v in vm]
            new = []
            for h in range(2):
                d, total = weights(z_ref[slot, 2 * n + h], sp_ref[slot, 2 * n + h], carries[h], vm[h], mask)
                carry = carries[h] + total
                new.append(carry if live is None else jnp.where(live, carry, carries[h]))
                upd = d if upd is None else upd + d
            carries = new
        acc_ref[...] = upd
        carry_ref[0] = carries[0]
        carry_ref[1] = carries[1]
        return jnp.maximum(jnp.max(carries[0]), jnp.max(carries[1]))

    def remaining(qi, top, acc_ref, carry_ref):
        def cond(state):
            it, top = state
            return (it < qi) & (top > SB_ZERO_LOG)

        def body(state):
            it, _ = state
            j = qi - 1 - it
            qm = split_heads(rows_of(q_ref, qi))
            kb = rows_of(k_ref, j)
            vm = split_heads(rows_of(v_ref, j))
            upd = None
            tops = []
            for h in range(2):
                z, sp = scores(qm[h], kb, None)
                d, total = weights(z, sp, carry_ref[h], vm[h], None)
                carry_ref[h] = carry_ref[h] + total
                tops.append(jnp.max(carry_ref[h]))
                upd = d if upd is None else upd + d
            acc_ref[...] += upd
            return it + 1, jnp.maximum(tops[0], tops[1])

        lax.while_loop(cond, body, (jnp.int32(1), top))

    first_half(0, 0)

    group = acc_ref.shape[0]

    def query_group(g, _):
        qa = group * g
        tops = []
        for s in range(group):
            tops.append(second_half(qa + s, s % 2, acc_ref.at[s], carry_ref.at[s]))
            first_half(jnp.minimum(qa + s + 1, nq - 1), (s + 1) % 2)
        for s in range(group):
            remaining(qa + s, tops[s], acc_ref.at[s], carry_ref.at[s])
            out_ref[pl.ds(pl.multiple_of((qa + s) * blk, blk), blk), :] = acc_ref[s].astype(BF16)
        return 0

    assert group % 2 == 0 and nq % group == 0
    lax.fori_loop(0, nq // group, query_group, 0)


def _stickbreak(sq, sk, sv, batch, seq_len):
    t, sw = sq.shape
    npair = sw // LANES
    nq = seq_len // SB_BLOCK
    group = SB_GROUP if nq % SB_GROUP == 0 else 2
    seq = pl.BlockSpec((seq_len, LANES), lambda b, hp: (b, hp))
    return pl.pallas_call(
        _sb_kernel,
        grid=(batch, npair),
        in_specs=[seq, seq, seq],
        out_specs=seq,
        out_shape=jax.ShapeDtypeStruct((t, sw), BF16),
        scratch_shapes=[pltpu.VMEM((group, SB_BLOCK, LANES), F32),
                        pltpu.VMEM((group, 2, SB_BLOCK, 1), F32),
                        pltpu.VMEM((2, 4, SB_BLOCK, SB_BLOCK), F32),
                        pltpu.VMEM((2, 4, SB_BLOCK, SB_BLOCK), BF16)],
        compiler_params=pltpu.CompilerParams(
            dimension_semantics=("arbitrary", "arbitrary"), vmem_limit_bytes=VMEM_LIMIT),
        name="stickbrk",
    )(sq, sk, sv)


def _outroute_kernel(x_ref, hm_ref, hs_ref, wom_ref, wos_ref, g_ref, wr_ref, br_ref,
                     h1_ref, c_ref, route_ref, cnt_ref, run_ref, *, n_groups, per_group):
    i = pl.program_id(0)
    tm = x_ref.shape[0]

    @pl.when(i == 0)
    def _():
        run_ref[...] = jnp.zeros(run_ref.shape, F32)

    h1 = x_ref[...] + _dot(hm_ref[...], wom_ref[...]) + _dot(hs_ref[...], wos_ref[...])
    h1_ref[...] = h1
    c = _rms(h1, g_ref[...])
    c_ref[...] = c

    c1, c2, _ = _split3(c)
    pa = _dot(c1, wr_ref[...])
    pb = _dot(c2, wr_ref[...])
    logits = (pa[:, 0:LANES] + (pa[:, LANES:] + pb[:, 0:LANES]) + pb[:, LANES:]) + br_ref[...]

    lane = lax.broadcasted_iota(jnp.int32, (tm, LANES), 1).astype(F32)
    ninf = -jnp.inf
    big = float(LANES)

    def first_max(v):
        mx = jnp.max(v, axis=1, keepdims=True)
        idx = jnp.min(jnp.where(v == mx, lane, big), axis=1, keepdims=True)
        return mx, idx

    gl = jnp.where(lane < n_groups, logits, ninf)
    gmax, gsel = first_max(gl)
    p_g = 1.0 / jnp.sum(jnp.exp(gl - gmax), axis=1, keepdims=True)
    lo = LOGIT_LANE_E + per_group * gsel
    el = jnp.where((lane >= lo) & (lane < lo + per_group), logits, ninf)
    v1, i1 = first_max(el)
    v2, i2 = first_max(jnp.where(lane == i1, ninf, el))
    tt = jnp.exp(v2 - v1)
    w0 = p_g / (1.0 + tt)
    w1_ = p_g * tt / (1.0 + tt)

    oh0 = lane == i1
    oh1 = lane == i2
    ohsum = oh0.astype(F32) + oh1.astype(F32)
    rows = lax.broadcasted_iota(jnp.int32, (tm, tm), 0)
    cols = lax.broadcasted_iota(jnp.int32, (tm, tm), 1)
    before = (cols < rows).astype(BF16)
    prefix = _dot(before, ohsum.astype(BF16)) + run_ref[...]
    r0 = jnp.sum(jnp.where(oh0, prefix, 0.0), axis=1, keepdims=True)
    r1 = jnp.sum(jnp.where(oh1, prefix, 0.0), axis=1, keepdims=True)
    run = run_ref[...] + jnp.sum(ohsum, axis=0, keepdims=True)
    run_ref[...] = run
    cnt_ref[...] = jnp.broadcast_to(run, cnt_ref.shape)

    e0 = i1 - LOGIT_LANE_E
    e1 = i2 - LOGIT_LANE_E
    route = jnp.zeros((tm, LANES), F32)
    for ln, val in ((ROUTE_LANE_E, e0), (ROUTE_LANE_E + 1, e1), (ROUTE_LANE_W, w0),
                    (ROUTE_LANE_W + 1, w1_), (ROUTE_LANE_R, r0), (ROUTE_LANE_R + 1, r1)):
        route = jnp.where(lane == ln, val, route)
    route_ref[...] = route


def _tile_rows(ref, s, n, pitch=SUBLANES):
    return ref.at[pl.ds(s, n, stride=pitch), :]


def _tile_copy(src_ref, src_row, dst_ref, dst_row, sem, pitch):
    return pltpu.make_async_copy(src_ref.at[pl.ds(src_row * pitch, pitch), :],
                                 dst_ref.at[pl.ds(dst_row * pitch, pitch), :], sem)


def _lanes_to_smem(vals, vm_ref, sm_ref, sem):
    vm_ref[...] = vals.T[0:8, :].astype(jnp.int32)
    cp = pltpu.make_async_copy(vm_ref, sm_ref, sem)
    cp.start()
    cp.wait()


def _outroute(x2, hm, hs, w_out, g_ffn, w_rg, b_rg, w_re, b_re, tm):
    t, d = x2.shape
    mw = hm.shape[1]
    sw = hs.shape[1]
    n_groups = w_rg.shape[1]
    n_exp = w_re.shape[1]
    wr = jnp.zeros((d, LANES), F32)
    wr = wr.at[:, 0:n_groups].set(w_rg).at[:, LOGIT_LANE_E:LOGIT_LANE_E + n_exp].set(w_re)
    wr_hi, wr_lo, _ = _split3(wr)
    wr2 = jnp.concatenate([wr_hi, wr_lo], axis=1)
    br = jnp.zeros((1, LANES), F32)
    br = br.at[0, 0:n_groups].set(b_rg).at[0, LOGIT_LANE_E:LOGIT_LANE_E + n_exp].set(b_re)
    row = lambda i: (i, 0)
    const = lambda i: (0, 0)
    kern = functools.partial(_outroute_kernel, n_groups=n_groups, per_group=n_exp // n_groups)
    return pl.pallas_call(
        kern,
        grid=(t // tm,),
        in_specs=[
            pl.BlockSpec((tm, d), row),
            pl.BlockSpec((tm, mw), row),
            pl.BlockSpec((tm, sw), row),
            pl.BlockSpec((mw, d), const),
            pl.BlockSpec((sw, d), const),
            pl.BlockSpec((1, d), const),
            pl.BlockSpec((d, 2 * LANES), const),
            pl.BlockSpec((1, LANES), const),
        ],
        out_specs=[
            pl.BlockSpec((tm, d), row),
            pl.BlockSpec((tm, d), row),
            pl.BlockSpec((tm, LANES), row),
            pl.BlockSpec((8, LANES), const),
        ],
        out_shape=[
            jax.ShapeDtypeStruct((t, d), F32),
            jax.ShapeDtypeStruct((t, d), F32),
            jax.ShapeDtypeStruct((t, LANES), F32),
            jax.ShapeDtypeStruct((8, LANES), F32),
        ],
        scratch_shapes=[pltpu.VMEM((1, LANES), F32)],
        compiler_params=pltpu.CompilerParams(
            dimension_semantics=("arbitrary",), vmem_limit_bytes=VMEM_LIMIT),
        name="outroute",
    )(x2, hm, hs, w_out[0:mw].astype(BF16), w_out[mw:].astype(BF16), g_ffn.reshape(1, d),
      wr2, br)


def _slotpos_kernel(route_ref, offs_ref, pos_ref):
    route = route_ref[...]
    tm = route.shape[0]
    lane = lax.broadcasted_iota(jnp.int32, (tm, LANES), 1)
    offs = offs_ref[...]
    out = jnp.zeros((tm, LANES), F32)
    for j in range(TOP_K):
        e = route[:, ROUTE_LANE_E + j:ROUTE_LANE_E + j + 1].astype(jnp.int32)
        base = jnp.sum(jnp.where(lane == e, offs, 0.0), axis=1, keepdims=True)
        out = jnp.where(lane == j, base + route[:, ROUTE_LANE_R + j:ROUTE_LANE_R + j + 1], out)
    pos_ref[...] = out.T[0:SUBLANES, :].astype(jnp.int32)


def _slotpos(route, offs_row, tm):
    t = route.shape[0]
    return pl.pallas_call(
        _slotpos_kernel,
        grid=(t // tm,),
        in_specs=[pl.BlockSpec((tm, LANES), lambda i: (i, 0)),
                  pl.BlockSpec((1, LANES), lambda i: (0, 0))],
        out_specs=pl.BlockSpec((SUBLANES, tm), lambda i: (0, i)),
        out_shape=jax.ShapeDtypeStruct((SUBLANES, t), jnp.int32),
        compiler_params=pltpu.CompilerParams(dimension_semantics=("arbitrary",)),
        name="slotpos",
    )(route, offs_row)


def _dispatch_kernel(zblk_ref, zuse_ref, pos_ref, c_ref, route_ref, xin_ref, rows_ref, zero_ref,
                     sem, zsem, *, n_tokens):
    i = pl.program_id(0)
    tm, d = c_ref.shape

    @pl.when(i == 0)
    def _():
        zero_ref[...] = jnp.zeros(zero_ref.shape, jnp.uint32)
        n = zero_ref.shape[0]

        def zero_block(k):
            return pltpu.make_async_copy(
                zero_ref, xin_ref.at[pl.ds(pl.multiple_of(zblk_ref[k] * n, n), n), :], zsem)

        for k in range(zblk_ref.shape[0]):
            pl.when(zuse_ref[k] != 0)(lambda k=k: zero_block(k).start())
        for k in range(zblk_ref.shape[0]):
            pl.when(zuse_ref[k] != 0)(lambda k=k: zero_block(k).wait())

    half = d // 2
    lane = lax.broadcasted_iota(jnp.int32, (tm, LANES), 1)
    row_id = (i * tm + lax.broadcasted_iota(jnp.int32, (tm, 1), 0)).astype(F32)
    route = route_ref[...]
    packed = _pack_bf16_pairs(c_ref[...])
    for j in range(TOP_K):
        w = route[:, ROUTE_LANE_W + j:ROUTE_LANE_W + j + 1]
        meta = jnp.where(lane == META_DEST, row_id + j * n_tokens, jnp.where(lane == META_W, w, 0.0))
        for s in range(X_SUBLANES):
            if s < half // LANES:
                sub = packed[:, s * LANES:(s + 1) * LANES]
            else:
                sub = lax.bitcast_convert_type(meta, jnp.uint32)
            _tile_rows(rows_ref.at[j], s, tm, X_SUBLANES)[...] = sub

    def issue(t, _):
        for j in range(TOP_K):
            _tile_copy(rows_ref.at[j], t, xin_ref, pos_ref[j, t], sem, X_SUBLANES).start(priority=j)
        return 0

    lax.fori_loop(0, tm, issue, 0, unroll=ROW_DMA_UNROLL)
    for j in range(TOP_K):
        pltpu.make_async_copy(rows_ref.at[j], rows_ref.at[j], sem).wait()


def _dispatch(zero_blocks, zero_use, pos, c, route, n_blocks, tm):
    t, d = c.shape
    assert d // 2 // LANES == META_SUBLANE == X_SUBLANES - 1
    rows = EXPERT_ROWS
    grid_spec = pltpu.PrefetchScalarGridSpec(
        num_scalar_prefetch=2,
        grid=(t // tm,),
        in_specs=[
            pl.BlockSpec((SUBLANES, tm), lambda i, *_: (0, i), memory_space=pltpu.SMEM),
            pl.BlockSpec((tm, d), lambda i, *_: (i, 0)),
            pl.BlockSpec((tm, LANES), lambda i, *_: (i, 0)),
        ],
        out_specs=pl.BlockSpec(memory_space=pl.ANY),
        scratch_shapes=[pltpu.VMEM((TOP_K, tm * X_SUBLANES, LANES), jnp.uint32),
                        pltpu.VMEM((rows * X_SUBLANES, LANES), jnp.uint32),
                        pltpu.SemaphoreType.DMA(()),
                        pltpu.SemaphoreType.DMA(())],
    )
    return pl.pallas_call(
        functools.partial(_dispatch_kernel, n_tokens=t),
        grid_spec=grid_spec,
        out_shape=jax.ShapeDtypeStruct((n_blocks * rows * X_SUBLANES, LANES), jnp.uint32),
        compiler_params=pltpu.CompilerParams(
            dimension_semantics=("arbitrary",), vmem_limit_bytes=VMEM_LIMIT),
        name="dispatch",
    )(zero_blocks, zero_use, pos, c, route)


def _pack_bf16_pairs(v):
    half = v.shape[1] // 2
    bits = lax.bitcast_convert_type(v.astype(BF16).astype(F32), jnp.uint32)
    return (bits[:, 0:half] >> 16) | bits[:, half:]


def _unpack_bf16_pairs(words):
    lo = [lax.bitcast_convert_type(w << 16, F32) for w in words]
    hi = [lax.bitcast_convert_type(w & jnp.uint32(0xFFFF0000), F32) for w in words]
    return jnp.concatenate(lo + hi, axis=1)


def _experts_kernel(be_ref, bn_ref, nv_ref, x_ref, wg_ref, wu_ref, wd_ref, yout_ref,
                    wgb, wub, wdb, ybuf, dest_vm, dest_sm, sems, dsem, *, dump_row):
    i = pl.program_id(0)
    s = i % 2
    rows = ybuf.shape[1] // Y_SUBLANES
    prev = be_ref[jnp.maximum(i - 1, 0)]
    active = i < nv_ref[0]

    def to_dump(slot):
        def body(r, _):
            dest_sm[slot, 0, r] = dump_row + r
            return 0
        lax.fori_loop(0, rows, body, 0)

    def send(slot, r, queue=0):
        _tile_copy(ybuf.at[slot], r, yout_ref, dest_sm[slot, 0, r], sems.at[slot],
                   Y_SUBLANES).start(priority=queue)

    def wait(slot):
        pltpu.make_async_copy(ybuf.at[slot], ybuf.at[slot], sems.at[slot]).wait()

    @pl.when(i == 0)
    def _():
        ybuf[1] = jnp.zeros(ybuf.shape[1:], jnp.uint32)
        to_dump(1)

    @pl.when(i > 0)
    def _():
        wait(s)

    @pl.when(active & ((i == 0) | (be_ref[i] != prev)))
    def _():
        wgb[...] = wg_ref[0].astype(BF16)
        wub[...] = wu_ref[0].astype(BF16)
        wdb[...] = wd_ref[0].astype(BF16)

    @pl.when(active)
    def _():
        for r in range(rows):
            send(1 - s, r, r % 2)
        x = _unpack_bf16_pairs([_tile_rows(x_ref, t, rows, X_SUBLANES)[...]
                                for t in range(META_SUBLANE)]).astype(BF16)
        meta = lax.bitcast_convert_type(_tile_rows(x_ref, META_SUBLANE, rows, X_SUBLANES)[...], F32)
        gt = _dot(x, wgb[...])
        up = _dot(x, wub[...])
        hid = (gt * _sigmoid(gt) * up).astype(BF16)
        y = _pack_bf16_pairs(_dot(hid, wdb[...]) * meta[:, META_W:META_W + 1])
        for t in range(Y_SUBLANES):
            _tile_rows(ybuf.at[s], t, rows, Y_SUBLANES)[...] = y[:, t * LANES:(t + 1) * LANES]
        row = lax.broadcasted_iota(jnp.int32, (rows, LANES), 0)
        dest = jnp.where(row < bn_ref[i], meta, (dump_row + row).astype(F32))
        _lanes_to_smem(dest, dest_vm, dest_sm.at[s], dsem)

    @pl.when(jnp.logical_not(active))
    def _():
        lax.fori_loop(0, rows, lambda r, _: send(1 - s, r) or 0, 0, unroll=ROW_DMA_UNROLL)
        to_dump(s)

    @pl.when(i == pl.num_programs(0) - 1)
    def _():
        wait(1 - s)


def _experts(blk_e, blk_n, n_valid, xin, w_gate, w_up, w_down, n_out_rows):
    rows = EXPERT_ROWS
    nblk = xin.shape[0] // (rows * X_SUBLANES)
    d, de = w_gate.shape[1], w_gate.shape[2]
    assert d == 2 * Y_SUBLANES * LANES, "an output row is 4 sublanes of packed bf16 pairs"
    grid_spec = pltpu.PrefetchScalarGridSpec(
        num_scalar_prefetch=3,
        grid=(nblk + 1,),
        in_specs=[
            pl.BlockSpec((rows * X_SUBLANES, LANES),
                         lambda i, be, bn, nv: (jnp.minimum(i, nblk - 1), 0)),
            pl.BlockSpec((1, d, de), lambda i, be, bn, nv: (be[i], 0, 0)),
            pl.BlockSpec((1, d, de), lambda i, be, bn, nv: (be[i], 0, 0)),
            pl.BlockSpec((1, de, d), lambda i, be, bn, nv: (be[i], 0, 0)),
        ],
        out_specs=pl.BlockSpec(memory_space=pl.ANY),
        scratch_shapes=[pltpu.VMEM((d, de), BF16), pltpu.VMEM((d, de), BF16),
                        pltpu.VMEM((de, d), BF16),
                        pltpu.VMEM((2, rows * Y_SUBLANES, LANES), jnp.uint32),
                        pltpu.VMEM((8, rows), jnp.int32),
                        pltpu.SMEM((2, 8, rows), jnp.int32),
                        pltpu.SemaphoreType.DMA((2,)),
                        pltpu.SemaphoreType.DMA(())],
    )
    return pl.pallas_call(
        functools.partial(_experts_kernel, dump_row=n_out_rows),
        grid_spec=grid_spec,
        out_shape=jax.ShapeDtypeStruct(((n_out_rows + rows) * Y_SUBLANES, LANES), jnp.uint32),
        compiler_params=pltpu.CompilerParams(
            dimension_semantics=("arbitrary",), vmem_limit_bytes=VMEM_LIMIT),
        name="experts",
    )(blk_e, blk_n, n_valid, xin, w_gate, w_up, w_down)


def _combine_kernel(h1_ref, y0_ref, y1_ref, p_ref, wpg_ref, wpp_ref,
                    gple_ref, gpost_ref, gfin_ref, out_ref):
    tm = h1_ref.shape[0]
    y0, y1 = (_unpack_bf16_pairs([_tile_rows(ref, s, tm, Y_SUBLANES)[...] for s in range(Y_SUBLANES)])
              for ref in (y0_ref, y1_ref))
    h2 = h1_ref[...] + (y0 + y1)
    gate = _sigmoid(_dot(_rms(h2, gple_ref[...]).astype(BF16), wpg_ref[...]))
    ple = _rms(_dot(p_ref[...].astype(BF16), wpp_ref[...]), gpost_ref[...])
    h3 = h2 + gate * ple
    out_ref[...] = _rms(h3, gfin_ref[...])


def _combine(h1, p2, y, w_pg, w_pp, g_ple, g_post, g_final, tm):
    t, d = h1.shape
    pd = p2.shape[1]
    row = lambda i: (i, 0)
    const = lambda i: (0, 0)
    return pl.pallas_call(
        _combine_kernel,
        grid=(t // tm,),
        in_specs=[
            pl.BlockSpec((tm, d), row),
            pl.BlockSpec((tm * Y_SUBLANES, LANES), lambda i: (i, 0)),
            pl.BlockSpec((tm * Y_SUBLANES, LANES), lambda i: (t // tm + i, 0)),
            pl.BlockSpec((tm, pd), row),
            pl.BlockSpec((d, d), const),
            pl.BlockSpec((pd, d), const),
            pl.BlockSpec((1, d), const),
            pl.BlockSpec((1, d), const),
            pl.BlockSpec((1, d), const),
        ],
        out_specs=pl.BlockSpec((tm, d), row),
        out_shape=jax.ShapeDtypeStruct((t, d), F32),
        compiler_params=pltpu.CompilerParams(
            dimension_semantics=("arbitrary",), vmem_limit_bytes=VMEM_LIMIT),
        name="combine",
    )(h1, y, y, p2, w_pg.astype(BF16), w_pp.astype(BF16),
      g_ple.reshape(1, d), g_post.reshape(1, d), g_final.reshape(1, d))


def _largest_tile(n, cap):
    tile = cap
    while n % tile:
        tile //= 2
    return tile


def kernel(x, p, g_mix, w_in, b_gates, conv_q, conv_k, g_mhead, w_out, g_ffn, w_router_group,
           b_router_group, w_router_expert, b_router_expert, w_exp_gate, w_exp_up, w_exp_down,
           g_ple, w_ple_gate, w_ple_proj, g_ple_post, g_final):
    batch, seq_len, d = x.shape
    t = batch * seq_len
    tm = _largest_tile(seq_len, 512)
    n_exp = w_router_expert.shape[-1]
    rows = EXPERT_ROWS
    nblk = t * TOP_K // rows + n_exp

    assert w_in.shape[0] == 1, "single-layer block"
    l = 0
    h = x.reshape(t, d)
    mq, mk, mv, mo, gates, sq, sk, sv = _inproj(
        h, g_mix[l], w_in[l], conv_q[l], conv_k[l], seq_len, _largest_tile(seq_len, 1024))
    hm = _mlstm(mq, mk, mv, mo, gates, b_gates[l], g_mhead[l], batch, seq_len,
                _largest_tile(seq_len, 1024))
    hs = _stickbreak(sq, sk, sv, batch, seq_len)
    h1, c, route, counts = _outroute(
        h, hm, hs, w_out[l], g_ffn[l], w_router_group[l], b_router_group[l],
        w_router_expert[l], b_router_expert[l], tm)

    cnt = counts[0, LOGIT_LANE_E:LOGIT_LANE_E + n_exp].astype(jnp.int32)
    nb_e = (cnt + rows - 1) // rows
    cum = jnp.cumsum(nb_e)
    offs = (cum - nb_e) * rows
    offs_row = jnp.zeros((1, LANES), F32).at[0, 0:n_exp].set(offs.astype(F32))
    n_valid = cum[-1:]
    step = jnp.arange(nblk + 1, dtype=jnp.int32)
    blk_e = jnp.minimum(jnp.sum(cum[None, :] <= step[:, None], axis=1), n_exp - 1).astype(jnp.int32)
    mine = blk_e[:, None] == jnp.arange(n_exp, dtype=jnp.int32)[None, :]
    first = jnp.sum(jnp.where(mine, (cum - nb_e)[None, :], 0), axis=1)
    blk_n = jnp.clip(jnp.sum(jnp.where(mine, cnt[None, :], 0), axis=1) - rows * (step - first), 0, rows)
    tail = n_valid + jnp.arange(n_exp, dtype=jnp.int32)
    zero_blocks = jnp.concatenate([jnp.maximum(cum - 1, 0), jnp.minimum(tail, nblk - 1)])
    zero_use = jnp.concatenate([nb_e > 0, tail < nblk]).astype(jnp.int32)

    pos = _slotpos(route, offs_row, _largest_tile(t, 2048))
    xin = _dispatch(zero_blocks, zero_use, pos, c, route, nblk, _largest_tile(seq_len, 1024))
    y = _experts(blk_e, blk_n, n_valid, xin, w_exp_gate[l], w_exp_up[l], w_exp_down[l], TOP_K * t)
    out = _combine(h1, p[l].reshape(t, -1), y, w_ple_gate[l], w_ple_proj[l],
                   g_ple[l], g_ple_post[l], g_final, _largest_tile(seq_len, 1024))
    return out.reshape(batch, seq_len, d)
```

```python
import functools

import jax
import jax.numpy as jnp
from jax import lax
from jax.experimental import pallas as pl
from jax.experimental.pallas import tpu as pltpu

F32 = jnp.float32
BF16 = jnp.bfloat16
EPS = 1e-6

M_HEADS = 4
M_HEAD_DIM = 128
SB_HEAD_DIM = 64
CONV_WIDTH = 4
TOP_K = 2
LANES = 128
VMEM_LIMIT = 56 * 1024 * 1024

MLSTM_CHUNK = 256
SB_BLOCK = 256
SB_GROUP = 8
SB_ZERO_LOG = -105.0
EXPERT_ROWS = 512
ROW_DMA_UNROLL = 8
X_SUBLANES = 5
Y_SUBLANES = 4
ROUTE_LANE_E = 0
ROUTE_LANE_W = 2
ROUTE_LANE_R = 4
SUBLANES = 8
META_SUBLANE = 4
META_DEST = 0
META_W = 1
LOGIT_LANE_E = 4


def _rms(x, g):
    return x * lax.rsqrt(jnp.mean(x * x, axis=-1, keepdims=True) + EPS) * g


def _sigmoid(x):
    return 1.0 / (1.0 + jnp.exp(-x))


def _split3(a):
    a1 = a.astype(BF16)
    r1 = a - a1.astype(F32)
    a2 = r1.astype(BF16)
    a3 = (r1 - a2.astype(F32)).astype(BF16)
    return a1, a2, a3


def _dot(a, b):
    return jnp.dot(a, b, preferred_element_type=F32)


def _dot_nt(a, b):
    return lax.dot_general(a, b, (((1,), (1,)), ((), ())), preferred_element_type=F32)


def _dot_tn(a, b):
    return lax.dot_general(a, b, (((0,), (0,)), ((), ())), preferred_element_type=F32)


def _inproj_kernel(x_ref, g_ref, wqk_ref, wvo_ref, wg_ref, ws_ref, cq_ref, ck_ref,
                   mq_ref, mk_ref, mv_ref, mo_ref, gate_ref, sq_ref, sk_ref, sv_ref,
                   ext_ref, *, tiles_per_seq, k_scale):
    i = pl.program_id(0)
    tm = x_ref.shape[0]
    mw = mk_ref.shape[1]
    sw = sq_ref.shape[1]
    a = _rms(x_ref[...], g_ref[...]).astype(BF16)

    @pl.when(i % tiles_per_seq == 0)
    def _():
        ext_ref[0:8, :] = jnp.zeros((8, 2 * mw), F32)

    ext_ref[8:8 + tm, 0:mw] = _dot_nt(a, wqk_ref[0:mw, :])
    ext_ref[8:8 + tm, mw:2 * mw] = _dot_nt(a, wqk_ref[mw:2 * mw, :])

    def conv_silu(w_ref, c0):
        acc = ext_ref[pl.ds(8 - (CONV_WIDTH - 1), tm), c0:c0 + mw] * w_ref[0:1, :]
        for j in range(1, CONV_WIDTH):
            acc = acc + ext_ref[pl.ds(8 - (CONV_WIDTH - 1) + j, tm), c0:c0 + mw] * w_ref[j:j + 1, :]
        return acc * _sigmoid(acc)

    mq_ref[...] = conv_silu(cq_ref, 0).T.astype(BF16)
    mk_ref[...] = (conv_silu(ck_ref, mw) * k_scale).astype(BF16)
    ext_ref[0:8, :] = ext_ref[tm:tm + 8, :]

    mv_ref[...] = _dot_nt(wvo_ref[0:mw, :], a).astype(BF16)
    mo_ref[...] = _dot_nt(a, wvo_ref[mw:2 * mw, :]).astype(BF16)
    gate_ref[...] = _dot_nt(a, wg_ref[...])
    sq_ref[...] = _dot_nt(a, ws_ref[0:sw, :]).astype(BF16)
    sk_ref[...] = _dot_nt(a, ws_ref[sw:2 * sw, :]).astype(BF16)
    sv_ref[...] = _dot_nt(a, ws_ref[2 * sw:3 * sw, :]).astype(BF16)


def _inproj(x2, g_mix, w_in, conv_q, conv_k, seq_len, tm):
    t, d = x2.shape
    mw = conv_q.shape[1]
    h = M_HEADS
    sw = (w_in.shape[1] - 4 * mw - 2 * h) // 3
    wt = w_in.T
    wqk = wt[0:2 * mw].astype(BF16)
    wvo = wt[2 * mw:4 * mw].astype(BF16)
    wg = jnp.zeros((2 * LANES, d), F32)
    wg = wg.at[0:h].set(wt[4 * mw:4 * mw + h])
    wg = wg.at[LANES:LANES + h].set(wt[4 * mw + h:4 * mw + 2 * h]).astype(BF16)
    ws = wt[4 * mw + 2 * h:]
    ws = jnp.concatenate([ws[0:sw] * (SB_HEAD_DIM ** -0.5), ws[sw:]], axis=0).astype(BF16)
    row = lambda i: (i, 0)
    const = lambda i: (0, 0)
    kern = functools.partial(_inproj_kernel, tiles_per_seq=seq_len // tm, k_scale=M_HEAD_DIM ** -0.5)
    bf = lambda w: jax.ShapeDtypeStruct((t, w), BF16)
    bft = jax.ShapeDtypeStruct((mw, t), BF16)
    col = lambda i: (0, i)
    return pl.pallas_call(
        kern,
        grid=(t // tm,),
        in_specs=[
            pl.BlockSpec((tm, d), row),
            pl.BlockSpec((1, d), const),
            pl.BlockSpec((2 * mw, d), const),
            pl.BlockSpec((2 * mw, d), const),
            pl.BlockSpec((2 * LANES, d), const),
            pl.BlockSpec((3 * sw, d), const),
            pl.BlockSpec((CONV_WIDTH, mw), const),
            pl.BlockSpec((CONV_WIDTH, mw), const),
        ],
        out_specs=[
            pl.BlockSpec((mw, tm), col), pl.BlockSpec((tm, mw), row),
            pl.BlockSpec((mw, tm), col), pl.BlockSpec((tm, mw), row),
            pl.BlockSpec((tm, 2 * LANES), row),
            pl.BlockSpec((tm, sw), row), pl.BlockSpec((tm, sw), row), pl.BlockSpec((tm, sw), row),
        ],
        out_shape=[bft, bf(mw), bft, bf(mw),
                   jax.ShapeDtypeStruct((t, 2 * LANES), F32), bf(sw), bf(sw), bf(sw)],
        scratch_shapes=[pltpu.VMEM((tm + 8, 2 * mw), F32)],
        compiler_params=pltpu.CompilerParams(
            dimension_semantics=("arbitrary",), vmem_limit_bytes=VMEM_LIMIT),
        name="inproj",
    )(x2, g_mix.reshape(1, d), wqk, wvo, wg, ws, conv_q, conv_k)


def _mlstm_kernel(q_ref, k_ref, v_ref, o_ref, gate_ref, bias_ref, gh_ref, out_ref,
                  c_ref, m_ref, *, chunk):
    L = chunk
    hd = M_HEAD_DIM
    nchunks = k_ref.shape[0] // L

    @pl.when(pl.program_id(1) == 0)
    def _():
        c_ref[...] = jnp.zeros(c_ref.shape, F32)
        m_ref[...] = jnp.zeros(m_ref.shape, F32)

    rows = lax.broadcasted_iota(jnp.int32, (L, L), 0)
    cols = lax.broadcasted_iota(jnp.int32, (L, L), 1)
    tri = (cols <= rows).astype(BF16)
    seen = rows <= cols
    ones_rows = (lax.broadcasted_iota(jnp.int32, (hd, L), 0) == 0).astype(BF16)

    def chunk_body(c, _):
        r0 = pl.multiple_of(c * L, L)
        g = gate_ref[pl.ds(r0, L), :] + bias_ref[...]
        gi = g[:, 0:LANES]
        gf = g[:, LANES:2 * LANES]
        lf = jnp.minimum(gf, 0.0) - jnp.log(1.0 + jnp.exp(-jnp.abs(gf)))
        l1, l2, l3 = _split3(lf)
        b = _dot(tri, l1) + _dot(tri, l2) + _dot(tri, l3)
        b_last = b[L - 1:L, :]
        w_end = b_last - b + gi
        m_loc = jnp.max(w_end, axis=0, keepdims=True)
        m_prev = m_ref[...]
        m_new = jnp.maximum(b_last + m_prev, m_loc)
        decay = jnp.exp(b_last + m_prev - m_new)
        scale = jnp.exp(m_loc - m_new)
        gmb = gi - b
        b_t = b.T
        e_end_t = jnp.exp(w_end - m_loc).T
        for h in range(M_HEADS):
            hs = slice(h * hd, (h + 1) * hd)
            qt = q_ref[hs, pl.ds(r0, L)]
            kh = k_ref[pl.ds(r0, L), hs]
            vext = jnp.concatenate([v_ref[hs, pl.ds(r0, L)], ones_rows], axis=0)
            b_row = b_t[h:h + 1, :]
            e = jnp.where(seen, b_row + gmb[:, h:h + 1], -jnp.inf)
            log_inter = b_row + m_prev[:, h:h + 1]
            m_t = jnp.maximum(log_inter, jnp.max(e, axis=0, keepdims=True))
            w = (jnp.exp(e - m_t) * _dot(kh, qt)).astype(BF16)
            a_int = jnp.exp(log_inter - m_t)
            cext = c_ref[h]
            num = _dot(vext, w) + a_int * _dot(cext.astype(BF16), qt)
            den = num[hd:hd + 1, :]
            hh = num[0:hd, :] / jnp.maximum(jnp.abs(den), jnp.exp(-m_t))
            hh = hh * lax.rsqrt(jnp.mean(hh * hh, axis=0, keepdims=True) + EPS) * gh_ref[hs, :]
            og = _sigmoid(o_ref[pl.ds(r0, L), hs].astype(F32))
            out_ref[pl.ds(r0, L), hs] = (og * hh.T).astype(BF16)
            ev = (vext.astype(F32) * e_end_t[h:h + 1, :]).astype(BF16)
            c_ref[h] = decay[:, h:h + 1] * cext + scale[:, h:h + 1] * _dot(ev, kh)
        m_ref[...] = m_new
        return 0

    lax.fori_loop(0, nchunks, chunk_body, 0, unroll=4)


def _mlstm(mq, mk, mv, mo, gates, b_gates, g_mhead, batch, seq_len, rows):
    t, mw = mk.shape
    h = M_HEADS
    bias = jnp.zeros((1, 2 * LANES), F32)
    bias = bias.at[0, 0:h].set(b_gates[0:h]).at[0, LANES:LANES + h].set(b_gates[h:2 * h])
    nb = seq_len // rows
    row = lambda b, i: (b * nb + i, 0)
    col = lambda b, i: (0, b * nb + i)
    const = lambda b, i: (0, 0)
    timed = pl.BlockSpec((mw, rows), col)
    rowed = pl.BlockSpec((rows, mw), row)
    return pl.pallas_call(
        functools.partial(_mlstm_kernel, chunk=MLSTM_CHUNK),
        grid=(batch, nb),
        in_specs=[timed, rowed, timed, rowed,
                  pl.BlockSpec((rows, 2 * LANES), row),
                  pl.BlockSpec((1, 2 * LANES), const),
                  pl.BlockSpec((mw, MLSTM_CHUNK), const)],
        out_specs=rowed,
        out_shape=jax.ShapeDtypeStruct((t, mw), BF16),
        scratch_shapes=[pltpu.VMEM((h, 2 * M_HEAD_DIM, M_HEAD_DIM), F32),
                        pltpu.VMEM((1, LANES), F32)],
        compiler_params=pltpu.CompilerParams(
            dimension_semantics=("arbitrary", "arbitrary"), vmem_limit_bytes=VMEM_LIMIT),
        name="mlstm",
    )(mq, mk, mv, mo, gates, bias,
      jnp.broadcast_to(g_mhead.reshape(mw, 1), (mw, MLSTM_CHUNK)))


def _sb_kernel(q_ref, k_ref, v_ref, out_ref, acc_ref, carry_ref, z_ref, sp_ref):
    blk = SB_BLOCK
    nq = q_ref.shape[0] // blk
    lane = lax.broadcasted_iota(jnp.int32, (blk, LANES), 1)
    head0 = lane < SB_HEAD_DIM
    rows = lax.broadcasted_iota(jnp.int32, (blk, blk), 0)
    cols = lax.broadcasted_iota(jnp.int32, (blk, blk), 1)
    strict = cols < rows
    neg_suffix = jnp.where(rows >= cols, -1.0, 0.0).astype(BF16)

    def split_heads(x):
        zero = jnp.zeros_like(x)
        return [jnp.where(head0, x, zero), jnp.where(head0, zero, x)]

    def rows_of(ref, j):
        return ref[pl.ds(pl.multiple_of(j * blk, blk), blk), :]

    def scores(qm_h, kb, mask):
        z = _dot_nt(qm_h, kb)
        neg_abs = lax.bitcast_convert_type(
            lax.bitcast_convert_type(z, jnp.uint32) | jnp.uint32(0x80000000), F32)
        sp = jnp.maximum(z, 0.0) + jnp.log(1.0 + jnp.exp(neg_abs))
        if mask is not None:
            sp = jnp.where(mask, sp, 0.0)
        return z, sp.astype(BF16)

    def weights(z, sp, carry, vm_h, mask):
        rc = _dot(sp, neg_suffix)
        p = jnp.exp(z + rc + carry)
        if mask is not None:
            p = jnp.where(mask, p, 0.0)
        return _dot(p.astype(BF16), vm_h), rc[:, 0:1]

    def first_half(qi, slot):
        qm = split_heads(rows_of(q_ref, qi))
        for n, (j, mask) in enumerate(((qi, strict), (jnp.maximum(qi - 1, 0), None))):
            kb = rows_of(k_ref, j)
            for h in range(2):
                z_ref[slot, 2 * n + h], sp_ref[slot, 2 * n + h] = scores(qm[h], kb, mask)

    def second_half(qi, slot, acc_ref, carry_ref):
        carries = [jnp.zeros((blk, 1), F32)] * 2
        upd = None
        for n, (j, mask, live) in enumerate(((qi, strict, None), (jnp.maximum(qi - 1, 0), None, qi > 0))):
            vm = split_heads(rows_of(v_ref, j))
            if live is not None:
                vm = [jnp.where(live, v, jnp.zeros_like(v)) for v in vm]
            new = []
            for h in range(2):
                d, total = weights(z_ref[slot, 2 * n + h], sp_ref[slot, 2 * n + h], carries[h], vm[h], mask)
                carry = carries[h] + total
                new.append(carry if live is None else jnp.where(live, carry, carries[h]))
                upd = d if upd is None else upd + d
            carries = new
        acc_ref[...] = upd
        carry_ref[0] = carries[0]
        carry_ref[1] = carries[1]
        return jnp.maximum(jnp.max(carries[0]), jnp.max(carries[1]))

    def remaining(qi, top, acc_ref, carry_ref):
        def cond(state):
            it, top = state
            return (it < qi) & (top > SB_ZERO_LOG)

        def body(state):
            it, _ = state
            j = qi - 1 - it
            qm = split_heads(rows_of(q_ref, qi))
            kb = rows_of(k_ref, j)
            vm = split_heads(rows_of(v_ref, j))
            upd = None
            tops = []
            for h in range(2):
                z, sp = scores(qm[h], kb, None)
                d, total = weights(z, sp, carry_ref[h], vm[h], None)
                carry_ref[h] = carry_ref[h] + total
                tops.append(jnp.max(carry_ref[h]))
                upd = d if upd is None else upd + d
            acc_ref[...] += upd
            return it + 1, jnp.maximum(tops[0], tops[1])

        lax.while_loop(cond, body, (jnp.int32(1), top))

    first_half(0, 0)

    group = acc_ref.shape[0]

    def query_group(g, _):
        qa = group * g
        tops = []
        for s in range(group):
            tops.append(second_half(qa + s, s % 2, acc_ref.at[s], carry_ref.at[s]))
            first_half(jnp.minimum(qa + s + 1, nq - 1), (s + 1) % 2)
        for s in range(group):
            remaining(qa + s, tops[s], acc_ref.at[s], carry_ref.at[s])
            out_ref[pl.ds(pl.multiple_of((qa + s) * blk, blk), blk), :] = acc_ref[s].astype(BF16)
        return 0

    assert group % 2 == 0 and nq % group == 0
    lax.fori_loop(0, nq // group, query_group, 0)


def _stickbreak(sq, sk, sv, batch, seq_len):
    t, sw = sq.shape
    npair = sw // LANES
    nq = seq_len // SB_BLOCK
    group = SB_GROUP if nq % SB_GROUP == 0 else 2
    seq = pl.BlockSpec((seq_len, LANES), lambda b, hp: (b, hp))
    return pl.pallas_call(
        _sb_kernel,
        grid=(batch, npair),
        in_specs=[seq, seq, seq],
        out_specs=seq,
        out_shape=jax.ShapeDtypeStruct((t, sw), BF16),
        scratch_shapes=[pltpu.VMEM((group, SB_BLOCK, LANES), F32),
                        pltpu.VMEM((group, 2, SB_BLOCK, 1), F32),
                        pltpu.VMEM((2, 4, SB_BLOCK, SB_BLOCK), F32),
                        pltpu.VMEM((2, 4, SB_BLOCK, SB_BLOCK), BF16)],
        compiler_params=pltpu.CompilerParams(
            dimension_semantics=("arbitrary", "arbitrary"), vmem_limit_bytes=VMEM_LIMIT),
        name="stickbrk",
    )(sq, sk, sv)


def _outroute_kernel(x_ref, hm_ref, hs_ref, wom_ref, wos_ref, g_ref, wr_ref, br_ref,
                     h1_ref, c_ref, route_ref, cnt_ref, run_ref, *, n_groups, per_group):
    i = pl.program_id(0)
    tm = x_ref.shape[0]

    @pl.when(i == 0)
    def _():
        run_ref[...] = jnp.zeros(run_ref.shape, F32)

    h1 = x_ref[...] + _dot(hm_ref[...], wom_ref[...]) + _dot(hs_ref[...], wos_ref[...])
    h1_ref[...] = h1
    c = _rms(h1, g_ref[...])
    c_ref[...] = c

    c1, c2, _ = _split3(c)
    pa = _dot(c1, wr_ref[...])
    pb = _dot(c2, wr_ref[...])
    logits = (pa[:, 0:LANES] + (pa[:, LANES:] + pb[:, 0:LANES]) + pb[:, LANES:]) + br_ref[...]

    lane = lax.broadcasted_iota(jnp.int32, (tm, LANES), 1).astype(F32)
    ninf = -jnp.inf
    big = float(LANES)

    def first_max(v):
        mx = jnp.max(v, axis=1, keepdims=True)
        idx = jnp.min(jnp.where(v == mx, lane, big), axis=1, keepdims=True)
        return mx, idx

    gl = jnp.where(lane < n_groups, logits, ninf)
    gmax, gsel = first_max(gl)
    p_g = 1.0 / jnp.sum(jnp.exp(gl - gmax), axis=1, keepdims=True)
    lo = LOGIT_LANE_E + per_group * gsel
    el = jnp.where((lane >= lo) & (lane < lo + per_group), logits, ninf)
    v1, i1 = first_max(el)
    v2, i2 = first_max(jnp.where(lane == i1, ninf, el))
    tt = jnp.exp(v2 - v1)
    w0 = p_g / (1.0 + tt)
    w1_ = p_g * tt / (1.0 + tt)

    oh0 = lane == i1
    oh1 = lane == i2
    ohsum = oh0.astype(F32) + oh1.astype(F32)
    rows = lax.broadcasted_iota(jnp.int32, (tm, tm), 0)
    cols = lax.broadcasted_iota(jnp.int32, (tm, tm), 1)
    before = (cols < rows).astype(BF16)
    prefix = _dot(before, ohsum.astype(BF16)) + run_ref[...]
    r0 = jnp.sum(jnp.where(oh0, prefix, 0.0), axis=1, keepdims=True)
    r1 = jnp.sum(jnp.where(oh1, prefix, 0.0), axis=1, keepdims=True)
    run = run_ref[...] + jnp.sum(ohsum, axis=0, keepdims=True)
    run_ref[...] = run
    cnt_ref[...] = jnp.broadcast_to(run, cnt_ref.shape)

    e0 = i1 - LOGIT_LANE_E
    e1 = i2 - LOGIT_LANE_E
    route = jnp.zeros((tm, LANES), F32)
    for ln, val in ((ROUTE_LANE_E, e0), (ROUTE_LANE_E + 1, e1), (ROUTE_LANE_W, w0),
                    (ROUTE_LANE_W + 1, w1_), (ROUTE_LANE_R, r0), (ROUTE_LANE_R + 1, r1)):
        route = jnp.where(lane == ln, val, route)
    route_ref[...] = route


def _tile_rows(ref, s, n, pitch=SUBLANES):
    return ref.at[pl.ds(s, n, stride=pitch), :]


def _tile_copy(src_ref, src_row, dst_ref, dst_row, sem, pitch):
    return pltpu.make_async_copy(src_ref.at[pl.ds(src_row * pitch, pitch), :],
                                 dst_ref.at[pl.ds(dst_row * pitch, pitch), :], sem)


def _lanes_to_smem(vals, vm_ref, sm_ref, sem):
    vm_ref[...] = vals.T[0:8, :].astype(jnp.int32)
    cp = pltpu.make_async_copy(vm_ref, sm_ref, sem)
    cp.start()
    cp.wait()


def _outroute(x2, hm, hs, w_out, g_ffn, w_rg, b_rg, w_re, b_re, tm):
    t, d = x2.shape
    mw = hm.shape[1]
    sw = hs.shape[1]
    n_groups = w_rg.shape[1]
    n_exp = w_re.shape[1]
    wr = jnp.zeros((d, LANES), F32)
    wr = wr.at[:, 0:n_groups].set(w_rg).at[:, LOGIT_LANE_E:LOGIT_LANE_E + n_exp].set(w_re)
    wr_hi, wr_lo, _ = _split3(wr)
    wr2 = jnp.concatenate([wr_hi, wr_lo], axis=1)
    br = jnp.zeros((1, LANES), F32)
    br = br.at[0, 0:n_groups].set(b_rg).at[0, LOGIT_LANE_E:LOGIT_LANE_E + n_exp].set(b_re)
    row = lambda i: (i, 0)
    const = lambda i: (0, 0)
    kern = functools.partial(_outroute_kernel, n_groups=n_groups, per_group=n_exp // n_groups)
    return pl.pallas_call(
        kern,
        grid=(t // tm,),
        in_specs=[
            pl.BlockSpec((tm, d), row),
            pl.BlockSpec((tm, mw), row),
            pl.BlockSpec((tm, sw), row),
            pl.BlockSpec((mw, d), const),
            pl.BlockSpec((sw, d), const),
            pl.BlockSpec((1, d), const),
            pl.BlockSpec((d, 2 * LANES), const),
            pl.BlockSpec((1, LANES), const),
        ],
        out_specs=[
            pl.BlockSpec((tm, d), row),
            pl.BlockSpec((tm, d), row),
            pl.BlockSpec((tm, LANES), row),
            pl.BlockSpec((8, LANES), const),
        ],
        out_shape=[
            jax.ShapeDtypeStruct((t, d), F32),
            jax.ShapeDtypeStruct((t, d), F32),
            jax.ShapeDtypeStruct((t, LANES), F32),
            jax.ShapeDtypeStruct((8, LANES), F32),
        ],
        scratch_shapes=[pltpu.VMEM((1, LANES), F32)],
        compiler_params=pltpu.CompilerParams(
            dimension_semantics=("arbitrary",), vmem_limit_bytes=VMEM_LIMIT),
        name="outroute",
    )(x2, hm, hs, w_out[0:mw].astype(BF16), w_out[mw:].astype(BF16), g_ffn.reshape(1, d),
      wr2, br)


def _slotpos_kernel(route_ref, offs_ref, pos_ref):
    route = route_ref[...]
    tm = route.shape[0]
    lane = lax.broadcasted_iota(jnp.int32, (tm, LANES), 1)
    offs = offs_ref[...]
    out = jnp.zeros((tm, LANES), F32)
    for j in range(TOP_K):
        e = route[:, ROUTE_LANE_E + j:ROUTE_LANE_E + j + 1].astype(jnp.int32)
        base = jnp.sum(jnp.where(lane == e, offs, 0.0), axis=1, keepdims=True)
        out = jnp.where(lane == j, base + route[:, ROUTE_LANE_R + j:ROUTE_LANE_R + j + 1], out)
    pos_ref[...] = out.T[0:SUBLANES, :].astype(jnp.int32)


def _slotpos(route, offs_row, tm):
    t = route.shape[0]
    return pl.pallas_call(
        _slotpos_kernel,
        grid=(t // tm,),
        in_specs=[pl.BlockSpec((tm, LANES), lambda i: (i, 0)),
                  pl.BlockSpec((1, LANES), lambda i: (0, 0))],
        out_specs=pl.BlockSpec((SUBLANES, tm), lambda i: (0, i)),
        out_shape=jax.ShapeDtypeStruct((SUBLANES, t), jnp.int32),
        compiler_params=pltpu.CompilerParams(dimension_semantics=("arbitrary",)),
        name="slotpos",
    )(route, offs_row)


def _dispatch_kernel(zblk_ref, zuse_ref, pos_ref, c_ref, route_ref, xin_ref, rows_ref, zero_ref,
                     sem, zsem, *, n_tokens):
    i = pl.program_id(0)
    tm, d = c_ref.shape

    @pl.when(i == 0)
    def _():
        zero_ref[...] = jnp.zeros(zero_ref.shape, jnp.uint32)
        n = zero_ref.shape[0]

        def zero_block(k):
            return pltpu.make_async_copy(
                zero_ref, xin_ref.at[pl.ds(pl.multiple_of(zblk_ref[k] * n, n), n), :], zsem)

        for k in range(zblk_ref.shape[0]):
            pl.when(zuse_ref[k] != 0)(lambda k=k: zero_block(k).start())
        for k in range(zblk_ref.shape[0]):
            pl.when(zuse_ref[k] != 0)(lambda k=k: zero_block(k).wait())

    half = d // 2
    lane = lax.broadcasted_iota(jnp.int32, (tm, LANES), 1)
    row_id = (i * tm + lax.broadcasted_iota(jnp.int32, (tm, 1), 0)).astype(F32)
    route = route_ref[...]
    packed = _pack_bf16_pairs(c_ref[...])
    for j in range(TOP_K):
        w = route[:, ROUTE_LANE_W + j:ROUTE_LANE_W + j + 1]
        meta = jnp.where(lane == META_DEST, row_id + j * n_tokens, jnp.where(lane == META_W, w, 0.0))
        for s in range(X_SUBLANES):
            if s < half // LANES:
                sub = packed[:, s * LANES:(s + 1) * LANES]
            else:
                sub = lax.bitcast_convert_type(meta, jnp.uint32)
            _tile_rows(rows_ref.at[j], s, tm, X_SUBLANES)[...] = sub

    def issue(t, _):
        for j in range(TOP_K):
            _tile_copy(rows_ref.at[j], t, xin_ref, pos_ref[j, t], sem, X_SUBLANES).start(priority=j)
        return 0

    lax.fori_loop(0, tm, issue, 0, unroll=ROW_DMA_UNROLL)
    for j in range(TOP_K):
        pltpu.make_async_copy(rows_ref.at[j], rows_ref.at[j], sem).wait()


def _dispatch(zero_blocks, zero_use, pos, c, route, n_blocks, tm):
    t, d = c.shape
    assert d // 2 // LANES == META_SUBLANE == X_SUBLANES - 1
    rows = EXPERT_ROWS
    grid_spec = pltpu.PrefetchScalarGridSpec(
        num_scalar_prefetch=2,
        grid=(t // tm,),
        in_specs=[
            pl.BlockSpec((SUBLANES, tm), lambda i, *_: (0, i), memory_space=pltpu.SMEM),
            pl.BlockSpec((tm, d), lambda i, *_: (i, 0)),
            pl.BlockSpec((tm, LANES), lambda i, *_: (i, 0)),
        ],
        out_specs=pl.BlockSpec(memory_space=pl.ANY),
        scratch_shapes=[pltpu.VMEM((TOP_K, tm * X_SUBLANES, LANES), jnp.uint32),
                        pltpu.VMEM((rows * X_SUBLANES, LANES), jnp.uint32),
                        pltpu.SemaphoreType.DMA(()),
                        pltpu.SemaphoreType.DMA(())],
    )
    return pl.pallas_call(
        functools.partial(_dispatch_kernel, n_tokens=t),
        grid_spec=grid_spec,
        out_shape=jax.ShapeDtypeStruct((n_blocks * rows * X_SUBLANES, LANES), jnp.uint32),
        compiler_params=pltpu.CompilerParams(
            dimension_semantics=("arbitrary",), vmem_limit_bytes=VMEM_LIMIT),
        name="dispatch",
    )(zero_blocks, zero_use, pos, c, route)


def _pack_bf16_pairs(v):
    half = v.shape[1] // 2
    bits = lax.bitcast_convert_type(v.astype(BF16).astype(F32), jnp.uint32)
    return (bits[:, 0:half] >> 16) | bits[:, half:]


def _unpack_bf16_pairs(words):
    lo = [lax.bitcast_convert_type(w << 16, F32) for w in words]
    hi = [lax.bitcast_convert_type(w & jnp.uint32(0xFFFF0000), F32) for w in words]
    return jnp.concatenate(lo + hi, axis=1)


def _experts_kernel(be_ref, bn_ref, nv_ref, x_ref, wg_ref, wu_ref, wd_ref, yout_ref,
                    wgb, wub, wdb, ybuf, dest_vm, dest_sm, sems, dsem, *, dump_row):
    i = pl.program_id(0)
    s = i % 2
    rows = ybuf.shape[1] // Y_SUBLANES
    prev = be_ref[jnp.maximum(i - 1, 0)]
    active = i < nv_ref[0]

    def to_dump(slot):
        def body(r, _):
            dest_sm[slot, 0, r] = dump_row + r
            return 0
        lax.fori_loop(0, rows, body, 0)

    def send(slot, r, queue=0):
        _tile_copy(ybuf.at[slot], r, yout_ref, dest_sm[slot, 0, r], sems.at[slot],
                   Y_SUBLANES).start(priority=queue)

    def wait(slot):
        pltpu.make_async_copy(ybuf.at[slot], ybuf.at[slot], sems.at[slot]).wait()

    @pl.when(i == 0)
    def _():
        ybuf[1] = jnp.zeros(ybuf.shape[1:], jnp.uint32)
        to_dump(1)

    @pl.when(i > 0)
    def _():
        wait(s)

    @pl.when(active & ((i == 0) | (be_ref[i] != prev)))
    def _():
        wgb[...] = wg_ref[0].astype(BF16)
        wub[...] = wu_ref[0].astype(BF16)
        wdb[...] = wd_ref[0].astype(BF16)

    @pl.when(active)
    def _():
        for r in range(rows):
            send(1 - s, r, r % 2)
        x = _unpack_bf16_pairs([_tile_rows(x_ref, t, rows, X_SUBLANES)[...]
                                for t in range(META_SUBLANE)]).astype(BF16)
        meta = lax.bitcast_convert_type(_tile_rows(x_ref, META_SUBLANE, rows, X_SUBLANES)[...], F32)
        gt = _dot(x, wgb[...])
        up = _dot(x, wub[...])
        hid = (gt * _sigmoid(gt) * up).astype(BF16)
        y = _pack_bf16_pairs(_dot(hid, wdb[...]) * meta[:, META_W:META_W + 1])
        for t in range(Y_SUBLANES):
            _tile_rows(ybuf.at[s], t, rows, Y_SUBLANES)[...] = y[:, t * LANES:(t + 1) * LANES]
        row = lax.broadcasted_iota(jnp.int32, (rows, LANES), 0)
        dest = jnp.where(row < bn_ref[i], meta, (dump_row + row).astype(F32))
        _lanes_to_smem(dest, dest_vm, dest_sm.at[s], dsem)

    @pl.when(jnp.logical_not(active))
    def _():
        lax.fori_loop(0, rows, lambda r, _: send(1 - s, r) or 0, 0, unroll=ROW_DMA_UNROLL)
        to_dump(s)

    @pl.when(i == pl.num_programs(0) - 1)
    def _():
        wait(1 - s)


def _experts(blk_e, blk_n, n_valid, xin, w_gate, w_up, w_down, n_out_rows):
    rows = EXPERT_ROWS
    nblk = xin.shape[0] // (rows * X_SUBLANES)
    d, de = w_gate.shape[1], w_gate.shape[2]
    assert d == 2 * Y_SUBLANES * LANES, "an output row is 4 sublanes of packed bf16 pairs"
    grid_spec = pltpu.PrefetchScalarGridSpec(
        num_scalar_prefetch=3,
        grid=(nblk + 1,),
        in_specs=[
            pl.BlockSpec((rows * X_SUBLANES, LANES),
                         lambda i, be, bn, nv: (jnp.minimum(i, nblk - 1), 0)),
            pl.BlockSpec((1, d, de), lambda i, be, bn, nv: (be[i], 0, 0)),
            pl.BlockSpec((1, d, de), lambda i, be, bn, nv: (be[i], 0, 0)),
            pl.BlockSpec((1, de, d), lambda i, be, bn, nv: (be[i], 0, 0)),
        ],
        out_specs=pl.BlockSpec(memory_space=pl.ANY),
        scratch_shapes=[pltpu.VMEM((d, de), BF16), pltpu.VMEM((d, de), BF16),
                        pltpu.VMEM((de, d), BF16),
                        pltpu.VMEM((2, rows * Y_SUBLANES, LANES), jnp.uint32),
                        pltpu.VMEM((8, rows), jnp.int32),
                        pltpu.SMEM((2, 8, rows), jnp.int32),
                        pltpu.SemaphoreType.DMA((2,)),
                        pltpu.SemaphoreType.DMA(())],
    )
    return pl.pallas_call(
        functools.partial(_experts_kernel, dump_row=n_out_rows),
        grid_spec=grid_spec,
        out_shape=jax.ShapeDtypeStruct(((n_out_rows + rows) * Y_SUBLANES, LANES), jnp.uint32),
        compiler_params=pltpu.CompilerParams(
            dimension_semantics=("arbitrary",), vmem_limit_bytes=VMEM_LIMIT),
        name="experts",
    )(blk_e, blk_n, n_valid, xin, w_gate, w_up, w_down)


def _combine_kernel(h1_ref, y0_ref, y1_ref, p_ref, wpg_ref, wpp_ref,
                    gple_ref, gpost_ref, gfin_ref, out_ref):
    tm = h1_ref.shape[0]
    y0, y1 = (_unpack_bf16_pairs([_tile_rows(ref, s, tm, Y_SUBLANES)[...] for s in range(Y_SUBLANES)])
              for ref in (y0_ref, y1_ref))
    h2 = h1_ref[...] + (y0 + y1)
    gate = _sigmoid(_dot(_rms(h2, gple_ref[...]).astype(BF16), wpg_ref[...]))
    ple = _rms(_dot(p_ref[...].astype(BF16), wpp_ref[...]), gpost_ref[...])
    h3 = h2 + gate * ple
    out_ref[...] = _rms(h3, gfin_ref[...])


def _combine(h1, p2, y, w_pg, w_pp, g_ple, g_post, g_final, tm):
    t, d = h1.shape
    pd = p2.shape[1]
    row = lambda i: (i, 0)
    const = lambda i: (0, 0)
    return pl.pallas_call(
        _combine_kernel,
        grid=(t // tm,),
        in_specs=[
            pl.BlockSpec((tm, d), row),
            pl.BlockSpec((tm * Y_SUBLANES, LANES), lambda i: (i, 0)),
            pl.BlockSpec((tm * Y_SUBLANES, LANES), lambda i: (t // tm + i, 0)),
            pl.BlockSpec((tm, pd), row),
            pl.BlockSpec((d, d), const),
            pl.BlockSpec((pd, d), const),
            pl.BlockSpec((1, d), const),
            pl.BlockSpec((1, d), const),
            pl.BlockSpec((1, d), const),
        ],
        out_specs=pl.BlockSpec((tm, d), row),
        out_shape=jax.ShapeDtypeStruct((t, d), F32),
        compiler_params=pltpu.CompilerParams(
            dimension_semantics=("arbitrary",), vmem_limit_bytes=VMEM_LIMIT),
        name="combine",
    )(h1, y, y, p2, w_pg.astype(BF16), w_pp.astype(BF16),
      g_ple.reshape(1, d), g_post.reshape(1, d), g_final.reshape(1, d))


def _largest_tile(n, cap):
    tile = cap
    while n % tile:
        tile //= 2
    return tile


def kernel(x, p, g_mix, w_in, b_gates, conv_q, conv_k, g_mhead, w_out, g_ffn, w_router_group,
           b_router_group, w_router_expert, b_router_expert, w_exp_gate, w_exp_up, w_exp_down,
           g_ple, w_ple_gate, w_ple_proj, g_ple_post, g_final):
    batch, seq_len, d = x.shape
    t = batch * seq_len
    tm = _largest_tile(seq_len, 512)
    n_exp = w_router_expert.shape[-1]
    rows = EXPERT_ROWS
    nblk = t * TOP_K // rows + n_exp

    assert w_in.shape[0] == 1, "single-layer block"
    l = 0
    h = x.reshape(t, d)
    mq, mk, mv, mo, gates, sq, sk, sv = _inproj(
        h, g_mix[l], w_in[l], conv_q[l], conv_k[l], seq_len, _largest_tile(seq_len, 1024))
    hm = _mlstm(mq, mk, mv, mo, gates, b_gates[l], g_mhead[l], batch, seq_len,
                _largest_tile(seq_len, 1024))
    hs = _stickbreak(sq, sk, sv, batch, seq_len)
    h1, c, route, counts = _outroute(
        h, hm, hs, w_out[l], g_ffn[l], w_router_group[l], b_router_group[l],
        w_router_expert[l], b_router_expert[l], tm)

    cnt = counts[0, LOGIT_LANE_E:LOGIT_LANE_E + n_exp].astype(jnp.int32)
    nb_e = (cnt + rows - 1) // rows
    cum = jnp.cumsum(nb_e)
    offs = (cum - nb_e) * rows
    offs_row = jnp.zeros((1, LANES), F32).at[0, 0:n_exp].set(offs.astype(F32))
    n_valid = cum[-1:]
    step = jnp.arange(nblk + 1, dtype=jnp.int32)
    blk_e = jnp.minimum(jnp.sum(cum[None, :] <= step[:, None], axis=1), n_exp - 1).astype(jnp.int32)
    mine = blk_e[:, None] == jnp.arange(n_exp, dtype=jnp.int32)[None, :]
    first = jnp.sum(jnp.where(mine, (cum - nb_e)[None, :], 0), axis=1)
    blk_n = jnp.clip(jnp.sum(jnp.where(mine, cnt[None, :], 0), axis=1) - rows * (step - first), 0, rows)
    tail = n_valid + jnp.arange(n_exp, dtype=jnp.int32)
    zero_blocks = jnp.concatenate([jnp.maximum(cum - 1, 0), jnp.minimum(tail, nblk - 1)])
    zero_use = jnp.concatenate([nb_e > 0, tail < nblk]).astype(jnp.int32)

    pos = _slotpos(route, offs_row, _largest_tile(t, 2048))
    xin = _dispatch(zero_blocks, zero_use, pos, c, route, nblk, _largest_tile(seq_len, 1024))
    y = _experts(blk_e, blk_n, n_valid, xin, w_exp_gate[l], w_exp_up[l], w_exp_down[l], TOP_K * t)
    out = _combine(h1, p[l].reshape(t, -1), y, w_ple_gate[l], w_ple_proj[l],
                   g_ple[l], g_ple_post[l], g_final, _largest_tile(seq_len, 1024))
    return out.reshape(batch, seq_len, d)
```

```python
import functools

import jax
import jax.numpy as jnp
from jax import lax
from jax.experimental import pallas as pl
from jax.experimental.pallas import tpu as pltpu

F32 = jnp.float32
BF16 = jnp.bfloat16
EPS = 1e-6

M_HEADS = 4
M_HEAD_DIM = 128
SB_HEAD_DIM = 64
CONV_WIDTH = 4
TOP_K = 2
LANES = 128
VMEM_LIMIT = 56 * 1024 * 1024

MLSTM_CHUNK = 256
SB_BLOCK = 256
SB_GROUP = 8
SB_ZERO_LOG = -105.0
EXPERT_ROWS = 512
ROW_DMA_UNROLL = 8
X_SUBLANES = 5
Y_SUBLANES = 4
ROUTE_LANE_E = 0
ROUTE_LANE_W = 2
ROUTE_LANE_R = 4
SUBLANES = 8
META_SUBLANE = 4
META_DEST = 0
META_W = 1
LOGIT_LANE_E = 4


def _rms(x, g):
    return x * lax.rsqrt(jnp.mean(x * x, axis=-1, keepdims=True) + EPS) * g


def _sigmoid(x):
    return 1.0 / (1.0 + jnp.exp(-x))


def _split3(a):
    a1 = a.astype(BF16)
    r1 = a - a1.astype(F32)
    a2 = r1.astype(BF16)
    a3 = (r1 - a2.astype(F32)).astype(BF16)
    return a1, a2, a3


def _dot(a, b):
    return jnp.dot(a, b, preferred_element_type=F32)


def _dot_nt(a, b):
    return lax.dot_general(a, b, (((1,), (1,)), ((), ())), preferred_element_type=F32)


def _dot_tn(a, b):
    return lax.dot_general(a, b, (((0,), (0,)), ((), ())), preferred_element_type=F32)


def _inproj_kernel(x_ref, g_ref, wqk_ref, wvo_ref, wg_ref, ws_ref, cq_ref, ck_ref,
                   mq_ref, mk_ref, mv_ref, mo_ref, gate_ref, sq_ref, sk_ref, sv_ref,
                   ext_ref, *, tiles_per_seq, k_scale):
    i = pl.program_id(0)
    tm = x_ref.shape[0]
    mw = mk_ref.shape[1]
    sw = sq_ref.shape[1]
    a = _rms(x_ref[...], g_ref[...]).astype(BF16)

    @pl.when(i % tiles_per_seq == 0)
    def _():
        ext_ref[0:8, :] = jnp.zeros((8, 2 * mw), F32)

    ext_ref[8:8 + tm, 0:mw] = _dot_nt(a, wqk_ref[0:mw, :])
    ext_ref[8:8 + tm, mw:2 * mw] = _dot_nt(a, wqk_ref[mw:2 * mw, :])

    def conv_silu(w_ref, c0):
        acc = ext_ref[pl.ds(8 - (CONV_WIDTH - 1), tm), c0:c0 + mw] * w_ref[0:1, :]
        for j in range(1, CONV_WIDTH):
            acc = acc + ext_ref[pl.ds(8 - (CONV_WIDTH - 1) + j, tm), c0:c0 + mw] * w_ref[j:j + 1, :]
        return acc * _sigmoid(acc)

    mq_ref[...] = conv_silu(cq_ref, 0).T.astype(BF16)
    mk_ref[...] = (conv_silu(ck_ref, mw) * k_scale).astype(BF16)
    ext_ref[0:8, :] = ext_ref[tm:tm + 8, :]

    mv_ref[...] = _dot_nt(wvo_ref[0:mw, :], a).astype(BF16)
    mo_ref[...] = _dot_nt(a, wvo_ref[mw:2 * mw, :]).astype(BF16)
    gate_ref[...] = _dot_nt(a, wg_ref[...])
    sq_ref[...] = _dot_nt(a, ws_ref[0:sw, :]).astype(BF16)
    sk_ref[...] = _dot_nt(a, ws_ref[sw:2 * sw, :]).astype(BF16)
    sv_ref[...] = _dot_nt(a, ws_ref[2 * sw:3 * sw, :]).astype(BF16)


def _inproj(x2, g_mix, w_in, conv_q, conv_k, seq_len, tm):
    t, d = x2.shape
    mw = conv_q.shape[1]
    h = M_HEADS
    sw = (w_in.shape[1] - 4 * mw - 2 * h) // 3
    wt = w_in.T
    wqk = wt[0:2 * mw].astype(BF16)
    wvo = wt[2 * mw:4 * mw].astype(BF16)
    wg = jnp.zeros((2 * LANES, d), F32)
    wg = wg.at[0:h].set(wt[4 * mw:4 * mw + h])
    wg = wg.at[LANES:LANES + h].set(wt[4 * mw + h:4 * mw + 2 * h]).astype(BF16)
    ws = wt[4 * mw + 2 * h:]
    ws = jnp.concatenate([ws[0:sw] * (SB_HEAD_DIM ** -0.5), ws[sw:]], axis=0).astype(BF16)
    row = lambda i: (i, 0)
    const = lambda i: (0, 0)
    kern = functools.partial(_inproj_kernel, tiles_per_seq=seq_len // tm, k_scale=M_HEAD_DIM ** -0.5)
    bf = lambda w: jax.ShapeDtypeStruct((t, w), BF16)
    bft = jax.ShapeDtypeStruct((mw, t), BF16)
    col = lambda i: (0, i)
    return pl.pallas_call(
        kern,
        grid=(t // tm,),
        in_specs=[
            pl.BlockSpec((tm, d), row),
            pl.BlockSpec((1, d), const),
            pl.BlockSpec((2 * mw, d), const),
            pl.BlockSpec((2 * mw, d), const),
            pl.BlockSpec((2 * LANES, d), const),
            pl.BlockSpec((3 * sw, d), const),
            pl.BlockSpec((CONV_WIDTH, mw), const),
            pl.BlockSpec((CONV_WIDTH, mw), const),
        ],
        out_specs=[
            pl.BlockSpec((mw, tm), col), pl.BlockSpec((tm, mw), row),
            pl.BlockSpec((mw, tm), col), pl.BlockSpec((tm, mw), row),
            pl.BlockSpec((tm, 2 * LANES), row),
            pl.BlockSpec((tm, sw), row), pl.BlockSpec((tm, sw), row), pl.BlockSpec((tm, sw), row),
        ],
        out_shape=[bft, bf(mw), bft, bf(mw),
                   jax.ShapeDtypeStruct((t, 2 * LANES), F32), bf(sw), bf(sw), bf(sw)],
        scratch_shapes=[pltpu.VMEM((tm + 8, 2 * mw), F32)],
        compiler_params=pltpu.CompilerParams(
            dimension_semantics=("arbitrary",), vmem_limit_bytes=VMEM_LIMIT),
        name="inproj",
    )(x2, g_mix.reshape(1, d), wqk, wvo, wg, ws, conv_q, conv_k)


def _mlstm_kernel(q_ref, k_ref, v_ref, o_ref, gate_ref, bias_ref, gh_ref, out_ref,
                  c_ref, m_ref, *, chunk):
    L = chunk
    hd = M_HEAD_DIM
    nchunks = k_ref.shape[0] // L

    @pl.when(pl.program_id(1) == 0)
    def _():
        c_ref[...] = jnp.zeros(c_ref.shape, F32)
        m_ref[...] = jnp.zeros(m_ref.shape, F32)

    rows = lax.broadcasted_iota(jnp.int32, (L, L), 0)
    cols = lax.broadcasted_iota(jnp.int32, (L, L), 1)
    tri = (cols <= rows).astype(BF16)
    seen = rows <= cols
    ones_rows = (lax.broadcasted_iota(jnp.int32, (hd, L), 0) == 0).astype(BF16)

    def chunk_body(c, _):
        r0 = pl.multiple_of(c * L, L)
        g = gate_ref[pl.ds(r0, L), :] + bias_ref[...]
        gi = g[:, 0:LANES]
        gf = g[:, LANES:2 * LANES]
        lf = jnp.minimum(gf, 0.0) - jnp.log(1.0 + jnp.exp(-jnp.abs(gf)))
        l1, l2, l3 = _split3(lf)
        b = _dot(tri, l1) + _dot(tri, l2) + _dot(tri, l3)
        b_last = b[L - 1:L, :]
        w_end = b_last - b + gi
        m_loc = jnp.max(w_end, axis=0, keepdims=True)
        m_prev = m_ref[...]
        m_new = jnp.maximum(b_last + m_prev, m_loc)
        decay = jnp.exp(b_last + m_prev - m_new)
        scale = jnp.exp(m_loc - m_new)
        gmb = gi - b
        b_t = b.T
        e_end_t = jnp.exp(w_end - m_loc).T
        for h in range(M_HEADS):
            hs = slice(h * hd, (h + 1) * hd)
            qt = q_ref[hs, pl.ds(r0, L)]
            kh = k_ref[pl.ds(r0, L), hs]
            vext = jnp.concatenate([v_ref[hs, pl.ds(r0, L)], ones_rows], axis=0)
            b_row = b_t[h:h + 1, :]
            e = jnp.where(seen, b_row + gmb[:, h:h + 1], -jnp.inf)
            log_inter = b_row + m_prev[:, h:h + 1]
            m_t = jnp.maximum(log_inter, jnp.max(e, axis=0, keepdims=True))
            w = (jnp.exp(e - m_t) * _dot(kh, qt)).astype(BF16)
            a_int = jnp.exp(log_inter - m_t)
            cext = c_ref[h]
            num = _dot(vext, w) + a_int * _dot(cext.astype(BF16), qt)
            den = num[hd:hd + 1, :]
            hh = num[0:hd, :] / jnp.maximum(jnp.abs(den), jnp.exp(-m_t))
            hh = hh * lax.rsqrt(jnp.mean(hh * hh, axis=0, keepdims=True) + EPS) * gh_ref[hs, :]
            og = _sigmoid(o_ref[pl.ds(r0, L), hs].astype(F32))
            out_ref[pl.ds(r0, L), hs] = (og * hh.T).astype(BF16)
            ev = (vext.astype(F32) * e_end_t[h:h + 1, :]).astype(BF16)
            c_ref[h] = decay[:, h:h + 1] * cext + scale[:, h:h + 1] * _dot(ev, kh)
        m_ref[...] = m_new
        return 0

    lax.fori_loop(0, nchunks, chunk_body, 0, unroll=4)


def _mlstm(mq, mk, mv, mo, gates, b_gates, g_mhead, batch, seq_len, rows):
    t, mw = mk.shape
    h = M_HEADS
    bias = jnp.zeros((1, 2 * LANES), F32)
    bias = bias.at[0, 0:h].set(b_gates[0:h]).at[0, LANES:LANES + h].set(b_gates[h:2 * h])
    nb = seq_len // rows
    row = lambda b, i: (b * nb + i, 0)
    col = lambda b, i: (0, b * nb + i)
    const = lambda b, i: (0, 0)
    timed = pl.BlockSpec((mw, rows), col)
    rowed = pl.BlockSpec((rows, mw), row)
    return pl.pallas_call(
        functools.partial(_mlstm_kernel, chunk=MLSTM_CHUNK),
        grid=(batch, nb),
        in_specs=[timed, rowed, timed, rowed,
                  pl.BlockSpec((rows, 2 * LANES), row),
                  pl.BlockSpec((1, 2 * LANES), const),
                  pl.BlockSpec((mw, MLSTM_CHUNK), const)],
        out_specs=rowed,
        out_shape=jax.ShapeDtypeStruct((t, mw), BF16),
        scratch_shapes=[pltpu.VMEM((h, 2 * M_HEAD_DIM, M_HEAD_DIM), F32),
                        pltpu.VMEM((1, LANES), F32)],
        compiler_params=pltpu.CompilerParams(
            dimension_semantics=("arbitrary", "arbitrary"), vmem_limit_bytes=VMEM_LIMIT),
        name="mlstm",
    )(mq, mk, mv, mo, gates, bias,
      jnp.broadcast_to(g_mhead.reshape(mw, 1), (mw, MLSTM_CHUNK)))


def _sb_kernel(q_ref, k_ref, v_ref, out_ref, acc_ref, carry_ref, z_ref, sp_ref):
    blk = SB_BLOCK
    nq = q_ref.shape[0] // blk
    lane = lax.broadcasted_iota(jnp.int32, (blk, LANES), 1)
    head0 = lane < SB_HEAD_DIM
    rows = lax.broadcasted_iota(jnp.int32, (blk, blk), 0)
    cols = lax.broadcasted_iota(jnp.int32, (blk, blk), 1)
    strict = cols < rows
    neg_suffix = jnp.where(rows >= cols, -1.0, 0.0).astype(BF16)

    def split_heads(x):
        zero = jnp.zeros_like(x)
        return [jnp.where(head0, x, zero), jnp.where(head0, zero, x)]

    def rows_of(ref, j):
        return ref[pl.ds(pl.multiple_of(j * blk, blk), blk), :]

    def scores(qm_h, kb, mask):
        z = _dot_nt(qm_h, kb)
        neg_abs = lax.bitcast_convert_type(
            lax.bitcast_convert_type(z, jnp.uint32) | jnp.uint32(0x80000000), F32)
        sp = jnp.maximum(z, 0.0) + jnp.log(1.0 + jnp.exp(neg_abs))
        if mask is not None:
            sp = jnp.where(mask, sp, 0.0)
        return z, sp.astype(BF16)

    def weights(z, sp, carry, vm_h, mask):
        rc = _dot(sp, neg_suffix)
        p = jnp.exp(z + rc + carry)
        if mask is not None:
            p = jnp.where(mask, p, 0.0)
        return _dot(p.astype(BF16), vm_h), rc[:, 0:1]

    def first_half(qi, slot):
        qm = split_heads(rows_of(q_ref, qi))
        for n, (j, mask) in enumerate(((qi, strict), (jnp.maximum(qi - 1, 0), None))):
            kb = rows_of(k_ref, j)
            for h in range(2):
                z_ref[slot, 2 * n + h], sp_ref[slot, 2 * n + h] = scores(qm[h], kb, mask)

    def second_half(qi, slot, acc_ref, carry_ref):
        carries = [jnp.zeros((blk, 1), F32)] * 2
        upd = None
        for n, (j, mask, live) in enumerate(((qi, strict, None), (jnp.maximum(qi - 1, 0), None, qi > 0))):
            vm = split_heads(rows_of(v_ref, j))
            if live is not None:
                vm = [jnp.where(live, v, jnp.zeros_like(v)) for v in vm]
            new = []
            for h in range(2):
                d, total = weights(z_ref[slot, 2 * n + h], sp_ref[slot, 2 * n + h], carries[h], vm[h], mask)
                carry = carries[h] + total
                new.append(carry if live is None else jnp.where(live, carry, carries[h]))
                upd = d if upd is None else upd + d
            carries = new
        acc_ref[...] = upd
        carry_ref[0] = carries[0]
        carry_ref[1] = carries[1]
        return jnp.maximum(jnp.max(carries[0]), jnp.max(carries[1]))

    def remaining(qi, top, acc_ref, carry_ref):
        def cond(state):
            it, top = state
            return (it < qi) & (top > SB_ZERO_LOG)

        def body(state):
            it, _ = state
            j = qi - 1 - it
            qm = split_heads(rows_of(q_ref, qi))
            kb = rows_of(k_ref, j)
            vm = split_heads(rows_of(v_ref, j))
            upd = None
            tops = []
            for h in range(2):
                z, sp = scores(qm[h], kb, None)
                d, total = weights(z, sp, carry_ref[h], vm[h], None)
                carry_ref[h] = carry_ref[h] + total
                tops.append(jnp.max(carry_ref[h]))
                upd = d if upd is None else upd + d
            acc_ref[...] += upd
            return it + 1, jnp.maximum(tops[0], tops[1])

        lax.while_loop(cond, body, (jnp.int32(1), top))

    first_half(0, 0)

    group = acc_ref.shape[0]

    def query_group(g, _):
        qa = group * g
        tops = []
        for s in range(group):
            tops.append(second_half(qa + s, s % 2, acc_ref.at[s], carry_ref.at[s]))
            first_half(jnp.minimum(qa + s + 1, nq - 1), (s + 1) % 2)
        for s in range(group):
            remaining(qa + s, tops[s], acc_ref.at[s], carry_ref.at[s])
            out_ref[pl.ds(pl.multiple_of((qa + s) * blk, blk), blk), :] = acc_ref[s].astype(BF16)
        return 0

    assert group % 2 == 0 and nq % group == 0
    lax.fori_loop(0, nq // group, query_group, 0)


def _stickbreak(sq, sk, sv, batch, seq_len):
    t, sw = sq.shape
    npair = sw // LANES
    nq = seq_len // SB_BLOCK
    group = SB_GROUP if nq % SB_GROUP == 0 else 2
    seq = pl.BlockSpec((seq_len, LANES), lambda b, hp: (b, hp))
    return pl.pallas_call(
        _sb_kernel,
        grid=(batch, npair),
        in_specs=[seq, seq, seq],
        out_specs=seq,
        out_shape=jax.ShapeDtypeStruct((t, sw), BF16),
        scratch_shapes=[pltpu.VMEM((group, SB_BLOCK, LANES), F32),
                        pltpu.VMEM((group, 2, SB_BLOCK, 1), F32),
                        pltpu.VMEM((2, 4, SB_BLOCK, SB_BLOCK), F32),
                        pltpu.VMEM((2, 4, SB_BLOCK, SB_BLOCK), BF16)],
        compiler_params=pltpu.CompilerParams(
            dimension_semantics=("arbitrary", "arbitrary"), vmem_limit_bytes=VMEM_LIMIT),
        name="stickbrk",
    )(sq, sk, sv)


def _outroute_kernel(x_ref, hm_ref, hs_ref, wom_ref, wos_ref, g_ref, wr_ref, br_ref,
                     h1_ref, c_ref, route_ref, cnt_ref, run_ref, *, n_groups, per_group):
    i = pl.program_id(0)
    tm = x_ref.shape[0]

    @pl.when(i == 0)
    def _():
        run_ref[...] = jnp.zeros(run_ref.shape, F32)

    h1 = x_ref[...] + _dot(hm_ref[...], wom_ref[...]) + _dot(hs_ref[...], wos_ref[...])
    h1_ref[...] = h1
    c = _rms(h1, g_ref[...])
    c_ref[...] = c

    c1, c2, _ = _split3(c)
    pa = _dot(c1, wr_ref[...])
    pb = _dot(c2, wr_ref[...])
    logits = (pa[:, 0:LANES] + (pa[:, LANES:] + pb[:, 0:LANES]) + pb[:, LANES:]) + br_ref[...]

    lane = lax.broadcasted_iota(jnp.int32, (tm, LANES), 1).astype(F32)
    ninf = -jnp.inf
    big = float(LANES)

    def first_max(v):
        mx = jnp.max(v, axis=1, keepdims=True)
        idx = jnp.min(jnp.where(v == mx, lane, big), axis=1, keepdims=True)
        return mx, idx

    gl = jnp.where(lane < n_groups, logits, ninf)
    gmax, gsel = first_max(gl)
    p_g = 1.0 / jnp.sum(jnp.exp(gl - gmax), axis=1, keepdims=True)
    lo = LOGIT_LANE_E + per_group * gsel
    el = jnp.where((lane >= lo) & (lane < lo + per_group), logits, ninf)
    v1, i1 = first_max(el)
    v2, i2 = first_max(jnp.where(lane == i1, ninf, el))
    tt = jnp.exp(v2 - v1)
    w0 = p_g / (1.0 + tt)
    w1_ = p_g * tt / (1.0 + tt)

    oh0 = lane == i1
    oh1 = lane == i2
    ohsum = oh0.astype(F32) + oh1.astype(F32)
    rows = lax.broadcasted_iota(jnp.int32, (tm, tm), 0)
    cols = lax.broadcasted_iota(jnp.int32, (tm, tm), 1)
    before = (cols < rows).astype(BF16)
    prefix = _dot(before, ohsum.astype(BF16)) + run_ref[...]
    r0 = jnp.sum(jnp.where(oh0, prefix, 0.0), axis=1, keepdims=True)
    r1 = jnp.sum(jnp.where(oh1, prefix, 0.0), axis=1, keepdims=True)
    run = run_ref[...] + jnp.sum(ohsum, axis=0, keepdims=True)
    run_ref[...] = run
    cnt_ref[...] = jnp.broadcast_to(run, cnt_ref.shape)

    e0 = i1 - LOGIT_LANE_E
    e1 = i2 - LOGIT_LANE_E
    route = jnp.zeros((tm, LANES), F32)
    for ln, val in ((ROUTE_LANE_E, e0), (ROUTE_LANE_E + 1, e1), (ROUTE_LANE_W, w0),
                    (ROUTE_LANE_W + 1, w1_), (ROUTE_LANE_R, r0), (ROUTE_LANE_R + 1, r1)):
        route = jnp.where(lane == ln, val, route)
    route_ref[...] = route


def _tile_rows(ref, s, n, pitch=SUBLANES):
    return ref.at[pl.ds(s, n, stride=pitch), :]


def _tile_copy(src_ref, src_row, dst_ref, dst_row, sem, pitch):
    return pltpu.make_async_copy(src_ref.at[pl.ds(src_row * pitch, pitch), :],
                                 dst_ref.at[pl.ds(dst_row * pitch, pitch), :], sem)


def _lanes_to_smem(vals, vm_ref, sm_ref, sem):
    vm_ref[...] = vals.T[0:SUBLANES, :].astype(jnp.int32)
    cp = pltpu.make_async_copy(vm_ref, sm_ref, sem)
    cp.start()
    cp.wait()


def _outroute(x2, hm, hs, w_out, g_ffn, w_rg, b_rg, w_re, b_re, tm):
    t, d = x2.shape
    mw = hm.shape[1]
    sw = hs.shape[1]
    n_groups = w_rg.shape[1]
    n_exp = w_re.shape[1]
    wr = jnp.zeros((d, LANES), F32)
    wr = wr.at[:, 0:n_groups].set(w_rg).at[:, LOGIT_LANE_E:LOGIT_LANE_E + n_exp].set(w_re)
    wr_hi, wr_lo, _ = _split3(wr)
    wr2 = jnp.concatenate([wr_hi, wr_lo], axis=1)
    br = jnp.zeros((1, LANES), F32)
    br = br.at[0, 0:n_groups].set(b_rg).at[0, LOGIT_LANE_E:LOGIT_LANE_E + n_exp].set(b_re)
    row = lambda i: (i, 0)
    const = lambda i: (0, 0)
    kern = functools.partial(_outroute_kernel, n_groups=n_groups, per_group=n_exp // n_groups)
    return pl.pallas_call(
        kern,
        grid=(t // tm,),
        in_specs=[
            pl.BlockSpec((tm, d), row),
            pl.BlockSpec((tm, mw), row),
            pl.BlockSpec((tm, sw), row),
            pl.BlockSpec((mw, d), const),
            pl.BlockSpec((sw, d), const),
            pl.BlockSpec((1, d), const),
            pl.BlockSpec((d, 2 * LANES), const),
            pl.BlockSpec((1, LANES), const),
        ],
        out_specs=[
            pl.BlockSpec((tm, d), row),
            pl.BlockSpec((tm, d), row),
            pl.BlockSpec((tm, LANES), row),
            pl.BlockSpec((8, LANES), const),
        ],
        out_shape=[
            jax.ShapeDtypeStruct((t, d), F32),
            jax.ShapeDtypeStruct((t, d), F32),
            jax.ShapeDtypeStruct((t, LANES), F32),
            jax.ShapeDtypeStruct((8, LANES), F32),
        ],
        scratch_shapes=[pltpu.VMEM((1, LANES), F32)],
        compiler_params=pltpu.CompilerParams(
            dimension_semantics=("arbitrary",), vmem_limit_bytes=VMEM_LIMIT),
        name="outroute",
    )(x2, hm, hs, w_out[0:mw].astype(BF16), w_out[mw:].astype(BF16), g_ffn.reshape(1, d),
      wr2, br)


def _slotpos_kernel(route_ref, offs_ref, pos_ref):
    route = route_ref[...]
    tm = route.shape[0]
    lane = lax.broadcasted_iota(jnp.int32, (tm, LANES), 1)
    offs = offs_ref[...]
    out = jnp.zeros((tm, LANES), F32)
    for j in range(TOP_K):
        e = route[:, ROUTE_LANE_E + j:ROUTE_LANE_E + j + 1].astype(jnp.int32)
        base = jnp.sum(jnp.where(lane == e, offs, 0.0), axis=1, keepdims=True)
        out = jnp.where(lane == j, base + route[:, ROUTE_LANE_R + j:ROUTE_LANE_R + j + 1], out)
    pos_ref[...] = out.T[0:SUBLANES, :].astype(jnp.int32)


def _slotpos(route, offs_row, tm):
    t = route.shape[0]
    return pl.pallas_call(
        _slotpos_kernel,
        grid=(t // tm,),
        in_specs=[pl.BlockSpec((tm, LANES), lambda i: (i, 0)),
                  pl.BlockSpec((1, LANES), lambda i: (0, 0))],
        out_specs=pl.BlockSpec((SUBLANES, tm), lambda i: (0, i)),
        out_shape=jax.ShapeDtypeStruct((SUBLANES, t), jnp.int32),
        compiler_params=pltpu.CompilerParams(dimension_semantics=("arbitrary",)),
        name="slotpos",
    )(route, offs_row)


def _dispatch_kernel(zblk_ref, zuse_ref, pos_ref, c_ref, route_ref, xin_ref, rows_ref, zero_ref,
                     sem, zsem, *, n_tokens):
    i = pl.program_id(0)
    tm, d = c_ref.shape

    @pl.when(i == 0)
    def _():
        zero_ref[...] = jnp.zeros(zero_ref.shape, jnp.uint32)
        n = zero_ref.shape[0]

        def zero_block(k):
            return pltpu.make_async_copy(
                zero_ref, xin_ref.at[pl.ds(pl.multiple_of(zblk_ref[k] * n, n), n), :], zsem)

        for k in range(zblk_ref.shape[0]):
            pl.when(zuse_ref[k] != 0)(lambda k=k: zero_block(k).start())
        for k in range(zblk_ref.shape[0]):
            pl.when(zuse_ref[k] != 0)(lambda k=k: zero_block(k).wait())

    half = d // 2
    lane = lax.broadcasted_iota(jnp.int32, (tm, LANES), 1)
    row_id = (i * tm + lax.broadcasted_iota(jnp.int32, (tm, 1), 0)).astype(F32)
    route = route_ref[...]
    packed = _pack_bf16_pairs(c_ref[...])
    for j in range(TOP_K):
        w = route[:, ROUTE_LANE_W + j:ROUTE_LANE_W + j + 1]
        meta = jnp.where(lane == META_DEST, row_id + j * n_tokens, jnp.where(lane == META_W, w, 0.0))
        for s in range(X_SUBLANES):
            if s < half // LANES:
                sub = packed[:, s * LANES:(s + 1) * LANES]
            else:
                sub = lax.bitcast_convert_type(meta, jnp.uint32)
            _tile_rows(rows_ref.at[j], s, tm, X_SUBLANES)[...] = sub

    def issue(t, _):
        for j in range(TOP_K):
            _tile_copy(rows_ref.at[j], t, xin_ref, pos_ref[j, t], sem, X_SUBLANES).start(priority=j)
        return 0

    lax.fori_loop(0, tm, issue, 0, unroll=ROW_DMA_UNROLL)
    for j in range(TOP_K):
        pltpu.make_async_copy(rows_ref.at[j], rows_ref.at[j], sem).wait()


def _dispatch(zero_blocks, zero_use, pos, c, route, n_blocks, tm):
    t, d = c.shape
    assert d // 2 // LANES == META_SUBLANE == X_SUBLANES - 1
    rows = EXPERT_ROWS
    grid_spec = pltpu.PrefetchScalarGridSpec(
        num_scalar_prefetch=2,
        grid=(t // tm,),
        in_specs=[
            pl.BlockSpec((SUBLANES, tm), lambda i, *_: (0, i), memory_space=pltpu.SMEM),
            pl.BlockSpec((tm, d), lambda i, *_: (i, 0)),
            pl.BlockSpec((tm, LANES), lambda i, *_: (i, 0)),
        ],
        out_specs=pl.BlockSpec(memory_space=pl.ANY),
        scratch_shapes=[pltpu.VMEM((TOP_K, tm * X_SUBLANES, LANES), jnp.uint32),
                        pltpu.VMEM((rows * X_SUBLANES, LANES), jnp.uint32),
                        pltpu.SemaphoreType.DMA(()),
                        pltpu.SemaphoreType.DMA(())],
    )
    return pl.pallas_call(
        functools.partial(_dispatch_kernel, n_tokens=t),
        grid_spec=grid_spec,
        out_shape=jax.ShapeDtypeStruct((n_blocks * rows * X_SUBLANES, LANES), jnp.uint32),
        compiler_params=pltpu.CompilerParams(
            dimension_semantics=("arbitrary",), vmem_limit_bytes=VMEM_LIMIT),
        name="dispatch",
    )(zero_blocks, zero_use, pos, c, route)


def _pack_bf16_pairs(v):
    half = v.shape[1] // 2
    bits = lax.bitcast_convert_type(v.astype(BF16).astype(F32), jnp.uint32)
    return (bits[:, 0:half] >> 16) | bits[:, half:]


def _unpack_bf16_pairs(words):
    lo = [lax.bitcast_convert_type(w << 16, F32) for w in words]
    hi = [lax.bitcast_convert_type(w & jnp.uint32(0xFFFF0000), F32) for w in words]
    return jnp.concatenate(lo + hi, axis=1)


def _experts_kernel(be_ref, bn_ref, nv_ref, x_ref, wg_ref, wu_ref, wd_ref, yout_ref,
                    wgb, wub, wdb, ybuf, dest_vm, dest_sm, sems, dsem, *, dump_row):
    i = pl.program_id(0)
    s = i % 2
    rows = ybuf.shape[1] // Y_SUBLANES
    prev = be_ref[jnp.maximum(i - 1, 0)]
    active = i < nv_ref[0]

    def to_dump(slot):
        def body(r, _):
            dest_sm[slot, 0, r] = dump_row + r
            return 0
        lax.fori_loop(0, rows, body, 0)

    def send(slot, r, queue=0):
        _tile_copy(ybuf.at[slot], r, yout_ref, dest_sm[slot, 0, r], sems.at[slot],
                   Y_SUBLANES).start(priority=queue)

    def wait(slot):
        pltpu.make_async_copy(ybuf.at[slot], ybuf.at[slot], sems.at[slot]).wait()

    @pl.when(i == 0)
    def _():
        ybuf[1] = jnp.zeros(ybuf.shape[1:], jnp.uint32)
        to_dump(1)

    @pl.when(i > 0)
    def _():
        wait(s)

    @pl.when(active & ((i == 0) | (be_ref[i] != prev)))
    def _():
        wgb[...] = wg_ref[0].astype(BF16)
        wub[...] = wu_ref[0].astype(BF16)
        wdb[...] = wd_ref[0].astype(BF16)

    @pl.when(active)
    def _():
        for r in range(rows):
            send(1 - s, r, r % 2)
        x = _unpack_bf16_pairs([_tile_rows(x_ref, t, rows, X_SUBLANES)[...]
                                for t in range(META_SUBLANE)]).astype(BF16)
        meta = lax.bitcast_convert_type(_tile_rows(x_ref, META_SUBLANE, rows, X_SUBLANES)[...], F32)
        gt = _dot(x, wgb[...])
        up = _dot(x, wub[...])
        hid = (gt * _sigmoid(gt) * up).astype(BF16)
        y = _pack_bf16_pairs(_dot(hid, wdb[...]) * meta[:, META_W:META_W + 1])
        for t in range(Y_SUBLANES):
            _tile_rows(ybuf.at[s], t, rows, Y_SUBLANES)[...] = y[:, t * LANES:(t + 1) * LANES]
        row = lax.broadcasted_iota(jnp.int32, (rows, LANES), 0)
        dest = jnp.where(row < bn_ref[i], meta, (dump_row + row).astype(F32))
        _lanes_to_smem(dest, dest_vm, dest_sm.at[s], dsem)

    @pl.when(jnp.logical_not(active))
    def _():
        lax.fori_loop(0, rows, lambda r, _: send(1 - s, r) or 0, 0, unroll=ROW_DMA_UNROLL)
        to_dump(s)

    @pl.when(i == pl.num_programs(0) - 1)
    def _():
        wait(1 - s)


def _experts(blk_e, blk_n, n_valid, xin, w_gate, w_up, w_down, n_out_rows):
    rows = EXPERT_ROWS
    nblk = xin.shape[0] // (rows * X_SUBLANES)
    d, de = w_gate.shape[1], w_gate.shape[2]
    assert d == 2 * Y_SUBLANES * LANES, "an output row is 4 sublanes of packed bf16 pairs"
    grid_spec = pltpu.PrefetchScalarGridSpec(
        num_scalar_prefetch=3,
        grid=(nblk + 1,),
        in_specs=[
            pl.BlockSpec((rows * X_SUBLANES, LANES),
                         lambda i, be, bn, nv: (jnp.minimum(i, nblk - 1), 0)),
            pl.BlockSpec((1, d, de), lambda i, be, bn, nv: (be[i], 0, 0)),
            pl.BlockSpec((1, d, de), lambda i, be, bn, nv: (be[i], 0, 0)),
            pl.BlockSpec((1, de, d), lambda i, be, bn, nv: (be[i], 0, 0)),
        ],
        out_specs=pl.BlockSpec(memory_space=pl.ANY),
        scratch_shapes=[pltpu.VMEM((d, de), BF16), pltpu.VMEM((d, de), BF16),
                        pltpu.VMEM((de, d), BF16),
                        pltpu.VMEM((2, rows * Y_SUBLANES, LANES), jnp.uint32),
                        pltpu.VMEM((8, rows), jnp.int32),
                        pltpu.SMEM((2, 8, rows), jnp.int32),
                        pltpu.SemaphoreType.DMA((2,)),
                        pltpu.SemaphoreType.DMA(())],
    )
    return pl.pallas_call(
        functools.partial(_experts_kernel, dump_row=n_out_rows),
        grid_spec=grid_spec,
        out_shape=jax.ShapeDtypeStruct(((n_out_rows + rows) * Y_SUBLANES, LANES), jnp.uint32),
        compiler_params=pltpu.CompilerParams(
            dimension_semantics=("arbitrary",), vmem_limit_bytes=VMEM_LIMIT),
        name="experts",
    )(blk_e, blk_n, n_valid, xin, w_gate, w_up, w_down)


def _combine_kernel(h1_ref, y0_ref, y1_ref, p_ref, wpg_ref, wpp_ref,
                    gple_ref, gpost_ref, gfin_ref, out_ref):
    tm = h1_ref.shape[0]
    y0, y1 = (_unpack_bf16_pairs([_tile_rows(ref, s, tm, Y_SUBLANES)[...] for s in range(Y_SUBLANES)])
              for ref in (y0_ref, y1_ref))
    h2 = h1_ref[...] + (y0 + y1)
    gate = _sigmoid(_dot(_rms(h2, gple_ref[...]).astype(BF16), wpg_ref[...]))
    ple = _rms(_dot(p_ref[...].astype(BF16), wpp_ref[...]), gpost_ref[...])
    h3 = h2 + gate * ple
    out_ref[...] = _rms(h3, gfin_ref[...])


def _combine(h1, p2, y, w_pg, w_pp, g_ple, g_post, g_final, tm):
    t, d = h1.shape
    pd = p2.shape[1]
    row = lambda i: (i, 0)
    const = lambda i: (0, 0)
    return pl.pallas_call(
        _combine_kernel,
        grid=(t // tm,),
        in_specs=[
            pl.BlockSpec((tm, d), row),
            pl.BlockSpec((tm * Y_SUBLANES, LANES), lambda i: (i, 0)),
            pl.BlockSpec((tm * Y_SUBLANES, LANES), lambda i: (t // tm + i, 0)),
            pl.BlockSpec((tm, pd), row),
            pl.BlockSpec((d, d), const),
            pl.BlockSpec((pd, d), const),
            pl.BlockSpec((1, d), const),
            pl.BlockSpec((1, d), const),
            pl.BlockSpec((1, d), const),
        ],
        out_specs=pl.BlockSpec((tm, d), row),
        out_shape=jax.ShapeDtypeStruct((t, d), F32),
        compiler_params=pltpu.CompilerParams(
            dimension_semantics=("arbitrary",), vmem_limit_bytes=VMEM_LIMIT),
        name="combine",
    )(h1, y, y, p2, w_pg.astype(BF16), w_pp.astype(BF16),
      g_ple.reshape(1, d), g_post.reshape(1, d), g_final.reshape(1, d))


def _largest_tile(n, cap):
    tile = cap
    while n % tile:
        tile //= 2
    return tile


def kernel(x, p, g_mix, w_in, b_gates, conv_q, conv_k, g_mhead, w_out, g_ffn, w_router_group,
           b_router_group, w_router_expert, b_router_expert, w_exp_gate, w_exp_up, w_exp_down,
           g_ple, w_ple_gate, w_ple_proj, g_ple_post, g_final):
    batch, seq_len, d = x.shape
    t = batch * seq_len
    tm = _largest_tile(seq_len, 512)
    n_exp = w_router_expert.shape[-1]
    rows = EXPERT_ROWS
    nblk = t * TOP_K // rows + n_exp

    assert w_in.shape[0] == 1, "single-layer block"
    l = 0
    h = x.reshape(t, d)
    mq, mk, mv, mo, gates, sq, sk, sv = _inproj(
        h, g_mix[l], w_in[l], conv_q[l], conv_k[l], seq_len, _largest_tile(seq_len, 1024))
    hm = _mlstm(mq, mk, mv, mo, gates, b_gates[l], g_mhead[l], batch, seq_len,
                _largest_tile(seq_len, 1024))
    hs = _stickbreak(sq, sk, sv, batch, seq_len)
    h1, c, route, counts = _outroute(
        h, hm, hs, w_out[l], g_ffn[l], w_router_group[l], b_router_group[l],
        w_router_expert[l], b_router_expert[l], tm)

    cnt = counts[0, LOGIT_LANE_E:LOGIT_LANE_E + n_exp].astype(jnp.int32)
    nb_e = (cnt + rows - 1) // rows
    cum = jnp.cumsum(nb_e)
    offs = (cum - nb_e) * rows
    offs_row = jnp.zeros((1, LANES), F32).at[0, 0:n_exp].set(offs.astype(F32))
    n_valid = cum[-1:]
    step = jnp.arange(nblk + 1, dtype=jnp.int32)
    blk_e = jnp.minimum(jnp.sum(cum[None, :] <= step[:, None], axis=1), n_exp - 1).astype(jnp.int32)
    mine = blk_e[:, None] == jnp.arange(n_exp, dtype=jnp.int32)[None, :]
    first = jnp.sum(jnp.where(mine, (cum - nb_e)[None, :], 0), axis=1)
    blk_n = jnp.clip(jnp.sum(jnp.where(mine, cnt[None, :], 0), axis=1) - rows * (step - first), 0, rows)
    tail = n_valid + jnp.arange(n_exp, dtype=jnp.int32)
    zero_blocks = jnp.concatenate([jnp.maximum(cum - 1, 0), jnp.minimum(tail, nblk - 1)])
    zero_use = jnp.concatenate([nb_e > 0, tail < nblk]).astype(jnp.int32)

    pos = _slotpos(route, offs_row, _largest_tile(t, 4096))
    xin = _dispatch(zero_blocks, zero_use, pos, c, route, nblk, _largest_tile(seq_len, 1024))
    y = _experts(blk_e, blk_n, n_valid, xin, w_exp_gate[l], w_exp_up[l], w_exp_down[l], TOP_K * t)
    out = _combine(h1, p[l].reshape(t, -1), y, w_ple_gate[l], w_ple_proj[l],
                   g_ple[l], g_ple_post[l], g_final, _largest_tile(seq_len, 1024))
    return out.reshape(batch, seq_len, d)
```

```python
import functools

import jax
import jax.numpy as jnp
from jax import lax
from jax.experimental import pallas as pl
from jax.experimental.pallas import tpu as pltpu

F32 = jnp.float32
BF16 = jnp.bfloat16
EPS = 1e-6

M_HEADS = 4
M_HEAD_DIM = 128
SB_HEAD_DIM = 64
CONV_WIDTH = 4
TOP_K = 2
LANES = 128
VMEM_LIMIT = 56 * 1024 * 1024

MLSTM_CHUNK = 256
SB_BLOCK = 256
SB_GROUP = 8
SB_ZERO_LOG = -105.0
EXPERT_ROWS = 512
ROW_DMA_UNROLL = 8
X_SUBLANES = 5
Y_SUBLANES = 4
ROUTE_LANE_R = 0
ROUTE_LANE_W = 2
ROUTE_LANE_E = 4
SUBLANES = 8
META_SUBLANE = 4
META_DEST = 0
META_W = 1
LOGIT_LANE_E = 4


def _rms(x, g):
    return x * lax.rsqrt(jnp.mean(x * x, axis=-1, keepdims=True) + EPS) * g


def _sigmoid(x):
    return 1.0 / (1.0 + jnp.exp(-x))


def _split3(a):
    a1 = a.astype(BF16)
    r1 = a - a1.astype(F32)
    a2 = r1.astype(BF16)
    a3 = (r1 - a2.astype(F32)).astype(BF16)
    return a1, a2, a3


def _dot(a, b):
    return jnp.dot(a, b, preferred_element_type=F32)


def _dot_nt(a, b):
    return lax.dot_general(a, b, (((1,), (1,)), ((), ())), preferred_element_type=F32)


def _dot_tn(a, b):
    return lax.dot_general(a, b, (((0,), (0,)), ((), ())), preferred_element_type=F32)


def _inproj_kernel(x_ref, g_ref, wqk_ref, wvo_ref, wg_ref, ws_ref, cq_ref, ck_ref,
                   mq_ref, mk_ref, mv_ref, mo_ref, gate_ref, sq_ref, sk_ref, sv_ref,
                   ext_ref, *, tiles_per_seq, k_scale):
    i = pl.program_id(0)
    tm = x_ref.shape[0]
    mw = mk_ref.shape[1]
    sw = sq_ref.shape[1]
    a = _rms(x_ref[...], g_ref[...]).astype(BF16)

    @pl.when(i % tiles_per_seq == 0)
    def _():
        ext_ref[0:8, :] = jnp.zeros((8, 2 * mw), F32)

    ext_ref[8:8 + tm, 0:mw] = _dot_nt(a, wqk_ref[0:mw, :])
    ext_ref[8:8 + tm, mw:2 * mw] = _dot_nt(a, wqk_ref[mw:2 * mw, :])

    def conv_silu(w_ref, c0):
        acc = ext_ref[pl.ds(8 - (CONV_WIDTH - 1), tm), c0:c0 + mw] * w_ref[0:1, :]
        for j in range(1, CONV_WIDTH):
            acc = acc + ext_ref[pl.ds(8 - (CONV_WIDTH - 1) + j, tm), c0:c0 + mw] * w_ref[j:j + 1, :]
        return acc * _sigmoid(acc)

    mq_ref[...] = conv_silu(cq_ref, 0).T.astype(BF16)
    mk_ref[...] = (conv_silu(ck_ref, mw) * k_scale).astype(BF16)
    ext_ref[0:8, :] = ext_ref[tm:tm + 8, :]

    mv_ref[...] = _dot_nt(wvo_ref[0:mw, :], a).astype(BF16)
    mo_ref[...] = _dot_nt(a, wvo_ref[mw:2 * mw, :]).astype(BF16)
    gate_ref[...] = _dot_nt(a, wg_ref[...])
    sq_ref[...] = _dot_nt(a, ws_ref[0:sw, :]).astype(BF16)
    sk_ref[...] = _dot_nt(a, ws_ref[sw:2 * sw, :]).astype(BF16)
    sv_ref[...] = _dot_nt(a, ws_ref[2 * sw:3 * sw, :]).astype(BF16)


def _inproj(x2, g_mix, w_in, conv_q, conv_k, seq_len, tm):
    t, d = x2.shape
    mw = conv_q.shape[1]
    h = M_HEADS
    sw = (w_in.shape[1] - 4 * mw - 2 * h) // 3
    wt = w_in.T
    wqk = wt[0:2 * mw].astype(BF16)
    wvo = wt[2 * mw:4 * mw].astype(BF16)
    wg = jnp.zeros((2 * LANES, d), F32)
    wg = wg.at[0:h].set(wt[4 * mw:4 * mw + h])
    wg = wg.at[LANES:LANES + h].set(wt[4 * mw + h:4 * mw + 2 * h]).astype(BF16)
    ws = wt[4 * mw + 2 * h:]
    ws = jnp.concatenate([ws[0:sw] * (SB_HEAD_DIM ** -0.5), ws[sw:]], axis=0).astype(BF16)
    row = lambda i: (i, 0)
    const = lambda i: (0, 0)
    kern = functools.partial(_inproj_kernel, tiles_per_seq=seq_len // tm, k_scale=M_HEAD_DIM ** -0.5)
    bf = lambda w: jax.ShapeDtypeStruct((t, w), BF16)
    bft = jax.ShapeDtypeStruct((mw, t), BF16)
    col = lambda i: (0, i)
    return pl.pallas_call(
        kern,
        grid=(t // tm,),
        in_specs=[
            pl.BlockSpec((tm, d), row),
            pl.BlockSpec((1, d), const),
            pl.BlockSpec((2 * mw, d), const),
            pl.BlockSpec((2 * mw, d), const),
            pl.BlockSpec((2 * LANES, d), const),
            pl.BlockSpec((3 * sw, d), const),
            pl.BlockSpec((CONV_WIDTH, mw), const),
            pl.BlockSpec((CONV_WIDTH, mw), const),
        ],
        out_specs=[
            pl.BlockSpec((mw, tm), col), pl.BlockSpec((tm, mw), row),
            pl.BlockSpec((mw, tm), col), pl.BlockSpec((tm, mw), row),
            pl.BlockSpec((tm, 2 * LANES), row),
            pl.BlockSpec((tm, sw), row), pl.BlockSpec((tm, sw), row), pl.BlockSpec((tm, sw), row),
        ],
        out_shape=[bft, bf(mw), bft, bf(mw),
                   jax.ShapeDtypeStruct((t, 2 * LANES), F32), bf(sw), bf(sw), bf(sw)],
        scratch_shapes=[pltpu.VMEM((tm + 8, 2 * mw), F32)],
        compiler_params=pltpu.CompilerParams(
            dimension_semantics=("arbitrary",), vmem_limit_bytes=VMEM_LIMIT),
        name="inproj",
    )(x2, g_mix.reshape(1, d), wqk, wvo, wg, ws, conv_q, conv_k)


def _mlstm_kernel(q_ref, k_ref, v_ref, o_ref, gate_ref, bias_ref, gh_ref, out_ref,
                  c_ref, m_ref, *, chunk):
    L = chunk
    hd = M_HEAD_DIM
    nchunks = k_ref.shape[0] // L

    @pl.when(pl.program_id(1) == 0)
    def _():
        c_ref[...] = jnp.zeros(c_ref.shape, F32)
        m_ref[...] = jnp.zeros(m_ref.shape, F32)

    rows = lax.broadcasted_iota(jnp.int32, (L, L), 0)
    cols = lax.broadcasted_iota(jnp.int32, (L, L), 1)
    tri = (cols <= rows).astype(BF16)
    seen = rows <= cols
    ones_rows = (lax.broadcasted_iota(jnp.int32, (hd, L), 0) == 0).astype(BF16)

    def chunk_body(c, _):
        r0 = pl.multiple_of(c * L, L)
        g = gate_ref[pl.ds(r0, L), :] + bias_ref[...]
        gi = g[:, 0:LANES]
        gf = g[:, LANES:2 * LANES]
        lf = jnp.minimum(gf, 0.0) - jnp.log(1.0 + jnp.exp(-jnp.abs(gf)))
        l1, l2, l3 = _split3(lf)
        b = _dot(tri, l1) + _dot(tri, l2) + _dot(tri, l3)
        b_last = b[L - 1:L, :]
        w_end = b_last - b + gi
        m_loc = jnp.max(w_end, axis=0, keepdims=True)
        m_prev = m_ref[...]
        m_new = jnp.maximum(b_last + m_prev, m_loc)
        decay = jnp.exp(b_last + m_prev - m_new)
        scale = jnp.exp(m_loc - m_new)
        gmb = gi - b
        b_t = b.T
        e_end_t = jnp.exp(w_end - m_loc).T
        for h in range(M_HEADS):
            hs = slice(h * hd, (h + 1) * hd)
            qt = q_ref[hs, pl.ds(r0, L)]
            kh = k_ref[pl.ds(r0, L), hs]
            vext = jnp.concatenate([v_ref[hs, pl.ds(r0, L)], ones_rows], axis=0)
            b_row = b_t[h:h + 1, :]
            e = jnp.where(seen, b_row + gmb[:, h:h + 1], -jnp.inf)
            log_inter = b_row + m_prev[:, h:h + 1]
            m_t = jnp.maximum(log_inter, jnp.max(e, axis=0, keepdims=True))
            w = (jnp.exp(e - m_t) * _dot(kh, qt)).astype(BF16)
            a_int = jnp.exp(log_inter - m_t)
            cext = c_ref[h]
            num = _dot(vext, w) + a_int * _dot(cext.astype(BF16), qt)
            den = num[hd:hd + 1, :]
            hh = num[0:hd, :] / jnp.maximum(jnp.abs(den), jnp.exp(-m_t))
            hh = hh * lax.rsqrt(jnp.mean(hh * hh, axis=0, keepdims=True) + EPS) * gh_ref[hs, :]
            og = _sigmoid(o_ref[pl.ds(r0, L), hs].astype(F32))
            out_ref[pl.ds(r0, L), hs] = (og * hh.T).astype(BF16)
            ev = (vext.astype(F32) * e_end_t[h:h + 1, :]).astype(BF16)
            c_ref[h] = decay[:, h:h + 1] * cext + scale[:, h:h + 1] * _dot(ev, kh)
        m_ref[...] = m_new
        return 0

    lax.fori_loop(0, nchunks, chunk_body, 0, unroll=4)


def _mlstm(mq, mk, mv, mo, gates, b_gates, g_mhead, batch, seq_len, rows):
    t, mw = mk.shape
    h = M_HEADS
    bias = jnp.zeros((1, 2 * LANES), F32)
    bias = bias.at[0, 0:h].set(b_gates[0:h]).at[0, LANES:LANES + h].set(b_gates[h:2 * h])
    nb = seq_len // rows
    row = lambda b, i: (b * nb + i, 0)
    col = lambda b, i: (0, b * nb + i)
    const = lambda b, i: (0, 0)
    timed = pl.BlockSpec((mw, rows), col)
    rowed = pl.BlockSpec((rows, mw), row)
    return pl.pallas_call(
        functools.partial(_mlstm_kernel, chunk=MLSTM_CHUNK),
        grid=(batch, nb),
        in_specs=[timed, rowed, timed, rowed,
                  pl.BlockSpec((rows, 2 * LANES), row),
                  pl.BlockSpec((1, 2 * LANES), const),
                  pl.BlockSpec((mw, MLSTM_CHUNK), const)],
        out_specs=rowed,
        out_shape=jax.ShapeDtypeStruct((t, mw), BF16),
        scratch_shapes=[pltpu.VMEM((h, 2 * M_HEAD_DIM, M_HEAD_DIM), F32),
                        pltpu.VMEM((1, LANES), F32)],
        compiler_params=pltpu.CompilerParams(
            dimension_semantics=("arbitrary", "arbitrary"), vmem_limit_bytes=VMEM_LIMIT),
        name="mlstm",
    )(mq, mk, mv, mo, gates, bias,
      jnp.broadcast_to(g_mhead.reshape(mw, 1), (mw, MLSTM_CHUNK)))


def _sb_kernel(q_ref, k_ref, v_ref, out_ref, acc_ref, carry_ref, z_ref, sp_ref):
    blk = SB_BLOCK
    nq = q_ref.shape[0] // blk
    lane = lax.broadcasted_iota(jnp.int32, (blk, LANES), 1)
    head0 = lane < SB_HEAD_DIM
    rows = lax.broadcasted_iota(jnp.int32, (blk, blk), 0)
    cols = lax.broadcasted_iota(jnp.int32, (blk, blk), 1)
    strict = cols < rows
    neg_suffix = jnp.where(rows >= cols, -1.0, 0.0).astype(BF16)

    def split_heads(x):
        zero = jnp.zeros_like(x)
        return [jnp.where(head0, x, zero), jnp.where(head0, zero, x)]

    def rows_of(ref, j):
        return ref[pl.ds(pl.multiple_of(j * blk, blk), blk), :]

    def scores(qm_h, kb, mask):
        z = _dot_nt(qm_h, kb)
        neg_abs = lax.bitcast_convert_type(
            lax.bitcast_convert_type(z, jnp.uint32) | jnp.uint32(0x80000000), F32)
        sp = jnp.maximum(z, 0.0) + jnp.log(1.0 + jnp.exp(neg_abs))
        if mask is not None:
            sp = jnp.where(mask, sp, 0.0)
        return z, sp.astype(BF16)

    def weights(z, sp, carry, vm_h, mask):
        rc = _dot(sp, neg_suffix)
        p = jnp.exp(z + rc + carry)
        if mask is not None:
            p = jnp.where(mask, p, 0.0)
        return _dot(p.astype(BF16), vm_h), rc[:, 0:1]

    def first_half(qi, slot):
        qm = split_heads(rows_of(q_ref, qi))
        for n, (j, mask) in enumerate(((qi, strict), (jnp.maximum(qi - 1, 0), None))):
            kb = rows_of(k_ref, j)
            for h in range(2):
                z_ref[slot, 2 * n + h], sp_ref[slot, 2 * n + h] = scores(qm[h], kb, mask)

    def second_half(qi, slot, acc_ref, carry_ref):
        carries = [jnp.zeros((blk, 1), F32)] * 2
        upd = None
        for n, (j, mask, live) in enumerate(((qi, strict, None), (jnp.maximum(qi - 1, 0), None, qi > 0))):
            vm = split_heads(rows_of(v_ref, j))
            if live is not None:
                vm = [jnp.where(live, v, jnp.zeros_like(v)) for v in vm]
            new = []
            for h in range(2):
                d, total = weights(z_ref[slot, 2 * n + h], sp_ref[slot, 2 * n + h], carries[h], vm[h], mask)
                carry = carries[h] + total
                new.append(carry if live is None else jnp.where(live, carry, carries[h]))
                upd = d if upd is None else upd + d
            carries = new
        acc_ref[...] = upd
        carry_ref[0] = carries[0]
        carry_ref[1] = carries[1]
        return jnp.maximum(jnp.max(carries[0]), jnp.max(carries[1]))

    def remaining(qi, top, acc_ref, carry_ref):
        def cond(state):
            it, top = state
            return (it < qi) & (top > SB_ZERO_LOG)

        def body(state):
            it, _ = state
            j = qi - 1 - it
            qm = split_heads(rows_of(q_ref, qi))
            kb = rows_of(k_ref, j)
            vm = split_heads(rows_of(v_ref, j))
            upd = None
            tops = []
            for h in range(2):
                z, sp = scores(qm[h], kb, None)
                d, total = weights(z, sp, carry_ref[h], vm[h], None)
                carry_ref[h] = carry_ref[h] + total
                tops.append(jnp.max(carry_ref[h]))
                upd = d if upd is None else upd + d
            acc_ref[...] += upd
            return it + 1, jnp.maximum(tops[0], tops[1])

        lax.while_loop(cond, body, (jnp.int32(1), top))

    first_half(0, 0)

    group = acc_ref.shape[0]

    def query_group(g, _):
        qa = group * g
        tops = []
        for s in range(group):
            tops.append(second_half(qa + s, s % 2, acc_ref.at[s], carry_ref.at[s]))
            first_half(jnp.minimum(qa + s + 1, nq - 1), (s + 1) % 2)
        for s in range(group):
            remaining(qa + s, tops[s], acc_ref.at[s], carry_ref.at[s])
            out_ref[pl.ds(pl.multiple_of((qa + s) * blk, blk), blk), :] = acc_ref[s].astype(BF16)
        return 0

    assert group % 2 == 0 and nq % group == 0
    lax.fori_loop(0, nq // group, query_group, 0)


def _stickbreak(sq, sk, sv, batch, seq_len):
    t, sw = sq.shape
    npair = sw // LANES
    nq = seq_len // SB_BLOCK
    group = SB_GROUP if nq % SB_GROUP == 0 else 2
    seq = pl.BlockSpec((seq_len, LANES), lambda b, hp: (b, hp))
    return pl.pallas_call(
        _sb_kernel,
        grid=(batch, npair),
        in_specs=[seq, seq, seq],
        out_specs=seq,
        out_shape=jax.ShapeDtypeStruct((t, sw), BF16),
        scratch_shapes=[pltpu.VMEM((group, SB_BLOCK, LANES), F32),
                        pltpu.VMEM((group, 2, SB_BLOCK, 1), F32),
                        pltpu.VMEM((2, 4, SB_BLOCK, SB_BLOCK), F32),
                        pltpu.VMEM((2, 4, SB_BLOCK, SB_BLOCK), BF16)],
        compiler_params=pltpu.CompilerParams(
            dimension_semantics=("arbitrary", "arbitrary"), vmem_limit_bytes=VMEM_LIMIT),
        name="stickbrk",
    )(sq, sk, sv)


def _outroute_kernel(x_ref, hm_ref, hs_ref, wom_ref, wos_ref, g_ref, wr_ref, br_ref,
                     h1_ref, c_ref, route_ref, cnt_ref, run_ref, *, n_groups, per_group):
    i = pl.program_id(0)
    tm = x_ref.shape[0]

    @pl.when(i == 0)
    def _():
        run_ref[...] = jnp.zeros(run_ref.shape, F32)

    h1 = x_ref[...] + _dot(hm_ref[...], wom_ref[...]) + _dot(hs_ref[...], wos_ref[...])
    h1_ref[...] = h1
    c = _rms(h1, g_ref[...])
    c_ref[...] = c

    c1, c2, _ = _split3(c)
    pa = _dot(c1, wr_ref[...])
    pb = _dot(c2, wr_ref[...])
    logits = (pa[:, 0:LANES] + (pa[:, LANES:] + pb[:, 0:LANES]) + pb[:, LANES:]) + br_ref[...]

    lane = lax.broadcasted_iota(jnp.int32, (tm, LANES), 1).astype(F32)
    ninf = -jnp.inf
    big = float(LANES)

    def first_max(v):
        mx = jnp.max(v, axis=1, keepdims=True)
        idx = jnp.min(jnp.where(v == mx, lane, big), axis=1, keepdims=True)
        return mx, idx

    gl = jnp.where(lane < n_groups, logits, ninf)
    gmax, gsel = first_max(gl)
    p_g = 1.0 / jnp.sum(jnp.exp(gl - gmax), axis=1, keepdims=True)
    lo = LOGIT_LANE_E + per_group * gsel
    el = jnp.where((lane >= lo) & (lane < lo + per_group), logits, ninf)
    v1, i1 = first_max(el)
    v2, i2 = first_max(jnp.where(lane == i1, ninf, el))
    tt = jnp.exp(v2 - v1)
    w0 = p_g / (1.0 + tt)
    w1_ = p_g * tt / (1.0 + tt)

    oh0 = lane == i1
    oh1 = lane == i2
    ohsum = oh0.astype(F32) + oh1.astype(F32)
    rows = lax.broadcasted_iota(jnp.int32, (tm, tm), 0)
    cols = lax.broadcasted_iota(jnp.int32, (tm, tm), 1)
    before = (cols < rows).astype(BF16)
    prefix = _dot(before, ohsum.astype(BF16)) + run_ref[...]
    r0 = jnp.sum(jnp.where(oh0, prefix, 0.0), axis=1, keepdims=True)
    r1 = jnp.sum(jnp.where(oh1, prefix, 0.0), axis=1, keepdims=True)
    run = run_ref[...] + jnp.sum(ohsum, axis=0, keepdims=True)
    run_ref[...] = run
    cnt_ref[...] = jnp.broadcast_to(run, cnt_ref.shape)

    e0 = i1 - LOGIT_LANE_E
    e1 = i2 - LOGIT_LANE_E
    route = jnp.zeros((tm, LANES), F32)
    for ln, val in ((ROUTE_LANE_E, e0), (ROUTE_LANE_E + 1, e1), (ROUTE_LANE_W, w0),
                    (ROUTE_LANE_W + 1, w1_), (ROUTE_LANE_R, r0), (ROUTE_LANE_R + 1, r1)):
        route = jnp.where(lane == ln, val, route)
    route_ref[...] = route


def _tile_rows(ref, s, n, pitch=SUBLANES):
    return ref.at[pl.ds(s, n, stride=pitch), :]


def _tile_copy(src_ref, src_row, dst_ref, dst_row, sem, pitch):
    return pltpu.make_async_copy(src_ref.at[pl.ds(src_row * pitch, pitch), :],
                                 dst_ref.at[pl.ds(dst_row * pitch, pitch), :], sem)


def _lanes_to_smem(vals, vm_ref, sm_ref, sem):
    vm_ref[...] = vals.T[0:SUBLANES, :].astype(jnp.int32)
    cp = pltpu.make_async_copy(vm_ref, sm_ref, sem)
    cp.start()
    cp.wait()


def _outroute(x2, hm, hs, w_out, g_ffn, w_rg, b_rg, w_re, b_re, tm):
    t, d = x2.shape
    mw = hm.shape[1]
    sw = hs.shape[1]
    n_groups = w_rg.shape[1]
    n_exp = w_re.shape[1]
    wr = jnp.zeros((d, LANES), F32)
    wr = wr.at[:, 0:n_groups].set(w_rg).at[:, LOGIT_LANE_E:LOGIT_LANE_E + n_exp].set(w_re)
    wr_hi, wr_lo, _ = _split3(wr)
    wr2 = jnp.concatenate([wr_hi, wr_lo], axis=1)
    br = jnp.zeros((1, LANES), F32)
    br = br.at[0, 0:n_groups].set(b_rg).at[0, LOGIT_LANE_E:LOGIT_LANE_E + n_exp].set(b_re)
    row = lambda i: (i, 0)
    const = lambda i: (0, 0)
    kern = functools.partial(_outroute_kernel, n_groups=n_groups, per_group=n_exp // n_groups)
    return pl.pallas_call(
        kern,
        grid=(t // tm,),
        in_specs=[
            pl.BlockSpec((tm, d), row),
            pl.BlockSpec((tm, mw), row),
            pl.BlockSpec((tm, sw), row),
            pl.BlockSpec((mw, d), const),
            pl.BlockSpec((sw, d), const),
            pl.BlockSpec((1, d), const),
            pl.BlockSpec((d, 2 * LANES), const),
            pl.BlockSpec((1, LANES), const),
        ],
        out_specs=[
            pl.BlockSpec((tm, d), row),
            pl.BlockSpec((tm, d), row),
            pl.BlockSpec((tm, LANES), row),
            pl.BlockSpec((8, LANES), const),
        ],
        out_shape=[
            jax.ShapeDtypeStruct((t, d), F32),
            jax.ShapeDtypeStruct((t, d), F32),
            jax.ShapeDtypeStruct((t, LANES), F32),
            jax.ShapeDtypeStruct((8, LANES), F32),
        ],
        scratch_shapes=[pltpu.VMEM((1, LANES), F32)],
        compiler_params=pltpu.CompilerParams(
            dimension_semantics=("arbitrary",), vmem_limit_bytes=VMEM_LIMIT),
        name="outroute",
    )(x2, hm, hs, w_out[0:mw].astype(BF16), w_out[mw:].astype(BF16), g_ffn.reshape(1, d),
      wr2, br)


def _slotpos_kernel(route_ref, table_ref, pos_ref):
    route = route_ref[...]
    tm = route.shape[0]
    lane = lax.broadcasted_iota(jnp.int32, (tm, LANES), 1).astype(F32)
    onehot = jnp.concatenate(
        [jnp.where(lane == route[:, ROUTE_LANE_E + j:ROUTE_LANE_E + j + 1], 1.0, 0.0)
         for j in range(TOP_K)], axis=1).astype(BF16)
    base = float(EXPERT_ROWS) * _dot(onehot, table_ref[...])
    out = base + jnp.where(lane < TOP_K, route, 0.0)
    pos_ref[...] = out.T[0:SUBLANES, :].astype(jnp.int32)


def _slotpos(route, first_block, n_blocks, tm):
    t = route.shape[0]
    n_exp = first_block.shape[0]
    assert ROUTE_LANE_R == 0 and n_blocks <= 256, "block indices must be exact in bf16"
    table = jnp.zeros((TOP_K * LANES, LANES), F32)
    for j in range(TOP_K):
        table = table.at[j * LANES:j * LANES + n_exp, j].set(first_block.astype(F32))
    return pl.pallas_call(
        _slotpos_kernel,
        grid=(t // tm,),
        in_specs=[pl.BlockSpec((tm, LANES), lambda i: (i, 0)),
                  pl.BlockSpec((TOP_K * LANES, LANES), lambda i: (0, 0))],
        out_specs=pl.BlockSpec((SUBLANES, tm), lambda i: (0, i)),
        out_shape=jax.ShapeDtypeStruct((SUBLANES, t), jnp.int32),
        compiler_params=pltpu.CompilerParams(dimension_semantics=("arbitrary",)),
        name="slotpos",
    )(route, table.astype(BF16))


def _dispatch_kernel(zblk_ref, zuse_ref, pos_ref, c_ref, route_ref, xin_ref, rows_ref, zero_ref,
                     sem, zsem, *, n_tokens):
    i = pl.program_id(0)
    tm, d = c_ref.shape

    @pl.when(i == 0)
    def _():
        zero_ref[...] = jnp.zeros(zero_ref.shape, jnp.uint32)
        n = zero_ref.shape[0]

        def zero_block(k):
            return pltpu.make_async_copy(
                zero_ref, xin_ref.at[pl.ds(pl.multiple_of(zblk_ref[k] * n, n), n), :], zsem)

        for k in range(zblk_ref.shape[0]):
            pl.when(zuse_ref[k] != 0)(lambda k=k: zero_block(k).start())
        for k in range(zblk_ref.shape[0]):
            pl.when(zuse_ref[k] != 0)(lambda k=k: zero_block(k).wait())

    half = d // 2
    lane = lax.broadcasted_iota(jnp.int32, (tm, LANES), 1)
    row_id = (i * tm + lax.broadcasted_iota(jnp.int32, (tm, 1), 0)).astype(F32)
    route = route_ref[...]
    packed = _pack_bf16_pairs(c_ref[...])
    for j in range(TOP_K):
        w = route[:, ROUTE_LANE_W + j:ROUTE_LANE_W + j + 1]
        meta = jnp.where(lane == META_DEST, row_id + j * n_tokens, jnp.where(lane == META_W, w, 0.0))
        for s in range(X_SUBLANES):
            if s < half // LANES:
                sub = packed[:, s * LANES:(s + 1) * LANES]
            else:
                sub = lax.bitcast_convert_type(meta, jnp.uint32)
            _tile_rows(rows_ref.at[j], s, tm, X_SUBLANES)[...] = sub

    def issue(t, _):
        for j in range(TOP_K):
            _tile_copy(rows_ref.at[j], t, xin_ref, pos_ref[j, t], sem, X_SUBLANES).start(priority=j)
        return 0

    lax.fori_loop(0, tm, issue, 0, unroll=ROW_DMA_UNROLL)
    for j in range(TOP_K):
        pltpu.make_async_copy(rows_ref.at[j], rows_ref.at[j], sem).wait()


def _dispatch(zero_blocks, zero_use, pos, c, route, n_blocks, tm):
    t, d = c.shape
    assert d // 2 // LANES == META_SUBLANE == X_SUBLANES - 1
    rows = EXPERT_ROWS
    grid_spec = pltpu.PrefetchScalarGridSpec(
        num_scalar_prefetch=2,
        grid=(t // tm,),
        in_specs=[
            pl.BlockSpec((SUBLANES, tm), lambda i, *_: (0, i), memory_space=pltpu.SMEM),
            pl.BlockSpec((tm, d), lambda i, *_: (i, 0)),
            pl.BlockSpec((tm, LANES), lambda i, *_: (i, 0)),
        ],
        out_specs=pl.BlockSpec(memory_space=pl.ANY),
        scratch_shapes=[pltpu.VMEM((TOP_K, tm * X_SUBLANES, LANES), jnp.uint32),
                        pltpu.VMEM((rows * X_SUBLANES, LANES), jnp.uint32),
                        pltpu.SemaphoreType.DMA(()),
                        pltpu.SemaphoreType.DMA(())],
    )
    return pl.pallas_call(
        functools.partial(_dispatch_kernel, n_tokens=t),
        grid_spec=grid_spec,
        out_shape=jax.ShapeDtypeStruct((n_blocks * rows * X_SUBLANES, LANES), jnp.uint32),
        compiler_params=pltpu.CompilerParams(
            dimension_semantics=("arbitrary",), vmem_limit_bytes=VMEM_LIMIT),
        name="dispatch",
    )(zero_blocks, zero_use, pos, c, route)


def _pack_bf16_pairs(v):
    half = v.shape[1] // 2
    bits = lax.bitcast_convert_type(v.astype(BF16).astype(F32), jnp.uint32)
    return (bits[:, 0:half] >> 16) | bits[:, half:]


def _unpack_bf16_pairs(words):
    lo = [lax.bitcast_convert_type(w << 16, F32) for w in words]
    hi = [lax.bitcast_convert_type(w & jnp.uint32(0xFFFF0000), F32) for w in words]
    return jnp.concatenate(lo + hi, axis=1)


def _experts_kernel(be_ref, bn_ref, nv_ref, x_ref, wg_ref, wu_ref, wd_ref, yout_ref,
                    wgb, wub, wdb, ybuf, dest_vm, dest_sm, sems, dsem, *, dump_row):
    i = pl.program_id(0)
    s = i % 2
    rows = ybuf.shape[1] // Y_SUBLANES
    prev = be_ref[jnp.maximum(i - 1, 0)]
    active = i < nv_ref[0]

    def to_dump(slot):
        def body(r, _):
            dest_sm[slot, 0, r] = dump_row + r
            return 0
        lax.fori_loop(0, rows, body, 0)

    def send(slot, r, queue=0):
        _tile_copy(ybuf.at[slot], r, yout_ref, dest_sm[slot, 0, r], sems.at[slot],
                   Y_SUBLANES).start(priority=queue)

    def wait(slot):
        pltpu.make_async_copy(ybuf.at[slot], ybuf.at[slot], sems.at[slot]).wait()

    @pl.when(i == 0)
    def _():
        ybuf[1] = jnp.zeros(ybuf.shape[1:], jnp.uint32)
        to_dump(1)

    @pl.when(i > 0)
    def _():
        wait(s)

    @pl.when(active & ((i == 0) | (be_ref[i] != prev)))
    def _():
        wgb[...] = wg_ref[0].astype(BF16)
        wub[...] = wu_ref[0].astype(BF16)
        wdb[...] = wd_ref[0].astype(BF16)

    @pl.when(active)
    def _():
        for r in range(rows):
            send(1 - s, r, r % 2)
        x = _unpack_bf16_pairs([_tile_rows(x_ref, t, rows, X_SUBLANES)[...]
                                for t in range(META_SUBLANE)]).astype(BF16)
        meta = lax.bitcast_convert_type(_tile_rows(x_ref, META_SUBLANE, rows, X_SUBLANES)[...], F32)
        gt = _dot(x, wgb[...])
        up = _dot(x, wub[...])
        hid = (gt * _sigmoid(gt) * up).astype(BF16)
        y = _pack_bf16_pairs(_dot(hid, wdb[...]) * meta[:, META_W:META_W + 1])
        for t in range(Y_SUBLANES):
            _tile_rows(ybuf.at[s], t, rows, Y_SUBLANES)[...] = y[:, t * LANES:(t + 1) * LANES]
        row = lax.broadcasted_iota(jnp.int32, (rows, LANES), 0)
        dest = jnp.where(row < bn_ref[i], meta, (dump_row + row).astype(F32))
        _lanes_to_smem(dest, dest_vm, dest_sm.at[s], dsem)

    @pl.when(jnp.logical_not(active))
    def _():
        lax.fori_loop(0, rows, lambda r, _: send(1 - s, r) or 0, 0, unroll=ROW_DMA_UNROLL)
        to_dump(s)

    @pl.when(i == pl.num_programs(0) - 1)
    def _():
        wait(1 - s)


def _experts(blk_e, blk_n, n_valid, xin, w_gate, w_up, w_down, n_out_rows):
    rows = EXPERT_ROWS
    nblk = xin.shape[0] // (rows * X_SUBLANES)
    d, de = w_gate.shape[1], w_gate.shape[2]
    assert d == 2 * Y_SUBLANES * LANES, "an output row is 4 sublanes of packed bf16 pairs"
    grid_spec = pltpu.PrefetchScalarGridSpec(
        num_scalar_prefetch=3,
        grid=(nblk + 1,),
        in_specs=[
            pl.BlockSpec((rows * X_SUBLANES, LANES),
                         lambda i, be, bn, nv: (jnp.minimum(i, nblk - 1), 0)),
            pl.BlockSpec((1, d, de), lambda i, be, bn, nv: (be[i], 0, 0)),
            pl.BlockSpec((1, d, de), lambda i, be, bn, nv: (be[i], 0, 0)),
            pl.BlockSpec((1, de, d), lambda i, be, bn, nv: (be[i], 0, 0)),
        ],
        out_specs=pl.BlockSpec(memory_space=pl.ANY),
        scratch_shapes=[pltpu.VMEM((d, de), BF16), pltpu.VMEM((d, de), BF16),
                        pltpu.VMEM((de, d), BF16),
                        pltpu.VMEM((2, rows * Y_SUBLANES, LANES), jnp.uint32),
                        pltpu.VMEM((8, rows), jnp.int32),
                        pltpu.SMEM((2, 8, rows), jnp.int32),
                        pltpu.SemaphoreType.DMA((2,)),
                        pltpu.SemaphoreType.DMA(())],
    )
    return pl.pallas_call(
        functools.partial(_experts_kernel, dump_row=n_out_rows),
        grid_spec=grid_spec,
        out_shape=jax.ShapeDtypeStruct(((n_out_rows + rows) * Y_SUBLANES, LANES), jnp.uint32),
        compiler_params=pltpu.CompilerParams(
            dimension_semantics=("arbitrary",), vmem_limit_bytes=VMEM_LIMIT),
        name="experts",
    )(blk_e, blk_n, n_valid, xin, w_gate, w_up, w_down)


def _combine_kernel(h1_ref, y0_ref, y1_ref, p_ref, wpg_ref, wpp_ref,
                    gple_ref, gpost_ref, gfin_ref, out_ref):
    tm = h1_ref.shape[0]
    y0, y1 = (_unpack_bf16_pairs([_tile_rows(ref, s, tm, Y_SUBLANES)[...] for s in range(Y_SUBLANES)])
              for ref in (y0_ref, y1_ref))
    h2 = h1_ref[...] + (y0 + y1)
    gate = _sigmoid(_dot(_rms(h2, gple_ref[...]).astype(BF16), wpg_ref[...]))
    ple = _rms(_dot(p_ref[...].astype(BF16), wpp_ref[...]), gpost_ref[...])
    h3 = h2 + gate * ple
    out_ref[...] = _rms(h3, gfin_ref[...])


def _combine(h1, p2, y, w_pg, w_pp, g_ple, g_post, g_final, tm):
    t, d = h1.shape
    pd = p2.shape[1]
    row = lambda i: (i, 0)
    const = lambda i: (0, 0)
    return pl.pallas_call(
        _combine_kernel,
        grid=(t // tm,),
        in_specs=[
            pl.BlockSpec((tm, d), row),
            pl.BlockSpec((tm * Y_SUBLANES, LANES), lambda i: (i, 0)),
            pl.BlockSpec((tm * Y_SUBLANES, LANES), lambda i: (t // tm + i, 0)),
            pl.BlockSpec((tm, pd), row),
            pl.BlockSpec((d, d), const),
            pl.BlockSpec((pd, d), const),
            pl.BlockSpec((1, d), const),
            pl.BlockSpec((1, d), const),
            pl.BlockSpec((1, d), const),
        ],
        out_specs=pl.BlockSpec((tm, d), row),
        out_shape=jax.ShapeDtypeStruct((t, d), F32),
        compiler_params=pltpu.CompilerParams(
            dimension_semantics=("arbitrary",), vmem_limit_bytes=VMEM_LIMIT),
        name="combine",
    )(h1, y, y, p2, w_pg.astype(BF16), w_pp.astype(BF16),
      g_ple.reshape(1, d), g_post.reshape(1, d), g_final.reshape(1, d))


def _largest_tile(n, cap):
    tile = cap
    while n % tile:
        tile //= 2
    return tile


def kernel(x, p, g_mix, w_in, b_gates, conv_q, conv_k, g_mhead, w_out, g_ffn, w_router_group,
           b_router_group, w_router_expert, b_router_expert, w_exp_gate, w_exp_up, w_exp_down,
           g_ple, w_ple_gate, w_ple_proj, g_ple_post, g_final):
    batch, seq_len, d = x.shape
    t = batch * seq_len
    tm = _largest_tile(seq_len, 512)
    n_exp = w_router_expert.shape[-1]
    rows = EXPERT_ROWS
    nblk = t * TOP_K // rows + n_exp

    assert w_in.shape[0] == 1, "single-layer block"
    l = 0
    h = x.reshape(t, d)
    mq, mk, mv, mo, gates, sq, sk, sv = _inproj(
        h, g_mix[l], w_in[l], conv_q[l], conv_k[l], seq_len, _largest_tile(seq_len, 1024))
    hm = _mlstm(mq, mk, mv, mo, gates, b_gates[l], g_mhead[l], batch, seq_len,
                _largest_tile(seq_len, 1024))
    hs = _stickbreak(sq, sk, sv, batch, seq_len)
    h1, c, route, counts = _outroute(
        h, hm, hs, w_out[l], g_ffn[l], w_router_group[l], b_router_group[l],
        w_router_expert[l], b_router_expert[l], tm)

    cnt = counts[0, LOGIT_LANE_E:LOGIT_LANE_E + n_exp].astype(jnp.int32)
    nb_e = (cnt + rows - 1) // rows
    cum = jnp.cumsum(nb_e)
    n_valid = cum[-1:]
    step = jnp.arange(nblk + 1, dtype=jnp.int32)
    blk_e = jnp.minimum(jnp.sum(cum[None, :] <= step[:, None], axis=1), n_exp - 1).astype(jnp.int32)
    mine = blk_e[:, None] == jnp.arange(n_exp, dtype=jnp.int32)[None, :]
    first = jnp.sum(jnp.where(mine, (cum - nb_e)[None, :], 0), axis=1)
    blk_n = jnp.clip(jnp.sum(jnp.where(mine, cnt[None, :], 0), axis=1) - rows * (step - first), 0, rows)
    tail = n_valid + jnp.arange(n_exp, dtype=jnp.int32)
    zero_blocks = jnp.concatenate([jnp.maximum(cum - 1, 0), jnp.minimum(tail, nblk - 1)])
    zero_use = jnp.concatenate([nb_e > 0, tail < nblk]).astype(jnp.int32)

    pos = _slotpos(route, cum - nb_e, nblk, _largest_tile(t, 4096))
    xin = _dispatch(zero_blocks, zero_use, pos, c, route, nblk, _largest_tile(seq_len, 1024))
    y = _experts(blk_e, blk_n, n_valid, xin, w_exp_gate[l], w_exp_up[l], w_exp_down[l], TOP_K * t)
    out = _combine(h1, p[l].reshape(t, -1), y, w_ple_gate[l], w_ple_proj[l],
                   g_ple[l], g_ple_post[l], g_final, _largest_tile(seq_len, 1024))
    return out.reshape(batch, seq_len, d)
```

```python
import functools

import jax
import jax.numpy as jnp
from jax import lax
from jax.experimental import pallas as pl
from jax.experimental.pallas import tpu as pltpu

F32 = jnp.float32
BF16 = jnp.bfloat16
EPS = 1e-6

M_HEADS = 4
M_HEAD_DIM = 128
SB_HEAD_DIM = 64
CONV_WIDTH = 4
TOP_K = 2
LANES = 128
VMEM_LIMIT = 56 * 1024 * 1024

MLSTM_CHUNK = 256
SB_BLOCK = 256
SB_GROUP = 8
SB_ZERO_LOG = -105.0
EXPERT_ROWS = 512
ROW_DMA_UNROLL = 8
X_SUBLANES = 5
Y_SUBLANES = 4
COMBINE_ROWS = 256
ROUTE_LANE_R = 0
ROUTE_LANE_W = 2
ROUTE_LANE_E = 4
SUBLANES = 8
META_SUBLANE = 4
META_DEST = 0
META_W = 1
LOGIT_LANE_E = 4


def _rms(x, g):
    return x * lax.rsqrt(jnp.mean(x * x, axis=-1, keepdims=True) + EPS) * g


def _sigmoid(x):
    return 1.0 / (1.0 + jnp.exp(-x))


def _split3(a):
    a1 = a.astype(BF16)
    r1 = a - a1.astype(F32)
    a2 = r1.astype(BF16)
    a3 = (r1 - a2.astype(F32)).astype(BF16)
    return a1, a2, a3


def _dot(a, b):
    return jnp.dot(a, b, preferred_element_type=F32)


def _dot_nt(a, b):
    return lax.dot_general(a, b, (((1,), (1,)), ((), ())), preferred_element_type=F32)


def _dot_tn(a, b):
    return lax.dot_general(a, b, (((0,), (0,)), ((), ())), preferred_element_type=F32)


def _inproj_kernel(x_ref, g_ref, wqk_ref, wvo_ref, wg_ref, ws_ref, cq_ref, ck_ref,
                   mq_ref, mk_ref, mv_ref, mo_ref, gate_ref, sq_ref, sk_ref, sv_ref,
                   ext_ref, *, tiles_per_seq, k_scale):
    i = pl.program_id(0)
    tm = x_ref.shape[0]
    mw = mk_ref.shape[1]
    sw = sq_ref.shape[1]
    a = _rms(x_ref[...], g_ref[...]).astype(BF16)

    @pl.when(i % tiles_per_seq == 0)
    def _():
        ext_ref[0:8, :] = jnp.zeros((8, 2 * mw), F32)

    ext_ref[8:8 + tm, 0:mw] = _dot_nt(a, wqk_ref[0:mw, :])
    ext_ref[8:8 + tm, mw:2 * mw] = _dot_nt(a, wqk_ref[mw:2 * mw, :])

    def conv_silu(w_ref, c0):
        acc = ext_ref[pl.ds(8 - (CONV_WIDTH - 1), tm), c0:c0 + mw] * w_ref[0:1, :]
        for j in range(1, CONV_WIDTH):
            acc = acc + ext_ref[pl.ds(8 - (CONV_WIDTH - 1) + j, tm), c0:c0 + mw] * w_ref[j:j + 1, :]
        return acc * _sigmoid(acc)

    mq_ref[...] = conv_silu(cq_ref, 0).T.astype(BF16)
    mk_ref[...] = (conv_silu(ck_ref, mw) * k_scale).astype(BF16)
    ext_ref[0:8, :] = ext_ref[tm:tm + 8, :]

    mv_ref[...] = _dot_nt(wvo_ref[0:mw, :], a).astype(BF16)
    mo_ref[...] = _dot_nt(a, wvo_ref[mw:2 * mw, :]).astype(BF16)
    gate_ref[...] = _dot_nt(a, wg_ref[...])
    sq_ref[...] = _dot_nt(a, ws_ref[0:sw, :]).astype(BF16)
    sk_ref[...] = _dot_nt(a, ws_ref[sw:2 * sw, :]).astype(BF16)
    sv_ref[...] = _dot_nt(a, ws_ref[2 * sw:3 * sw, :]).astype(BF16)


def _inproj(x2, g_mix, w_in, conv_q, conv_k, seq_len, tm):
    t, d = x2.shape
    mw = conv_q.shape[1]
    h = M_HEADS
    sw = (w_in.shape[1] - 4 * mw - 2 * h) // 3
    wt = w_in.T
    wqk = wt[0:2 * mw].astype(BF16)
    wvo = wt[2 * mw:4 * mw].astype(BF16)
    wg = jnp.zeros((2 * LANES, d), F32)
    wg = wg.at[0:h].set(wt[4 * mw:4 * mw + h])
    wg = wg.at[LANES:LANES + h].set(wt[4 * mw + h:4 * mw + 2 * h]).astype(BF16)
    ws = wt[4 * mw + 2 * h:]
    ws = jnp.concatenate([ws[0:sw] * (SB_HEAD_DIM ** -0.5), ws[sw:]], axis=0).astype(BF16)
    row = lambda i: (i, 0)
    const = lambda i: (0, 0)
    kern = functools.partial(_inproj_kernel, tiles_per_seq=seq_len // tm, k_scale=M_HEAD_DIM ** -0.5)
    bf = lambda w: jax.ShapeDtypeStruct((t, w), BF16)
    bft = jax.ShapeDtypeStruct((mw, t), BF16)
    col = lambda i: (0, i)
    return pl.pallas_call(
        kern,
        grid=(t // tm,),
        in_specs=[
            pl.BlockSpec((tm, d), row),
            pl.BlockSpec((1, d), const),
            pl.BlockSpec((2 * mw, d), const),
            pl.BlockSpec((2 * mw, d), const),
            pl.BlockSpec((2 * LANES, d), const),
            pl.BlockSpec((3 * sw, d), const),
            pl.BlockSpec((CONV_WIDTH, mw), const),
            pl.BlockSpec((CONV_WIDTH, mw), const),
        ],
        out_specs=[
            pl.BlockSpec((mw, tm), col), pl.BlockSpec((tm, mw), row),
            pl.BlockSpec((mw, tm), col), pl.BlockSpec((tm, mw), row),
            pl.BlockSpec((tm, 2 * LANES), row),
            pl.BlockSpec((tm, sw), row), pl.BlockSpec((tm, sw), row), pl.BlockSpec((tm, sw), row),
        ],
        out_shape=[bft, bf(mw), bft, bf(mw),
                   jax.ShapeDtypeStruct((t, 2 * LANES), F32), bf(sw), bf(sw), bf(sw)],
        scratch_shapes=[pltpu.VMEM((tm + 8, 2 * mw), F32)],
        compiler_params=pltpu.CompilerParams(
            dimension_semantics=("arbitrary",), vmem_limit_bytes=VMEM_LIMIT),
        name="inproj",
    )(x2, g_mix.reshape(1, d), wqk, wvo, wg, ws, conv_q, conv_k)


def _mlstm_kernel(q_ref, k_ref, v_ref, o_ref, gate_ref, bias_ref, gh_ref, out_ref,
                  c_ref, m_ref, *, chunk):
    L = chunk
    hd = M_HEAD_DIM
    nchunks = k_ref.shape[0] // L

    @pl.when(pl.program_id(1) == 0)
    def _():
        c_ref[...] = jnp.zeros(c_ref.shape, F32)
        m_ref[...] = jnp.zeros(m_ref.shape, F32)

    rows = lax.broadcasted_iota(jnp.int32, (L, L), 0)
    cols = lax.broadcasted_iota(jnp.int32, (L, L), 1)
    tri = (cols <= rows).astype(BF16)
    seen = rows <= cols
    ones_rows = (lax.broadcasted_iota(jnp.int32, (hd, L), 0) == 0).astype(BF16)

    def chunk_body(c, _):
        r0 = pl.multiple_of(c * L, L)
        g = gate_ref[pl.ds(r0, L), :] + bias_ref[...]
        gi = g[:, 0:LANES]
        gf = g[:, LANES:2 * LANES]
        lf = jnp.minimum(gf, 0.0) - jnp.log(1.0 + jnp.exp(-jnp.abs(gf)))
        l1, l2, l3 = _split3(lf)
        b = _dot(tri, l1) + _dot(tri, l2) + _dot(tri, l3)
        b_last = b[L - 1:L, :]
        w_end = b_last - b + gi
        m_loc = jnp.max(w_end, axis=0, keepdims=True)
        m_prev = m_ref[...]
        m_new = jnp.maximum(b_last + m_prev, m_loc)
        decay = jnp.exp(b_last + m_prev - m_new)
        scale = jnp.exp(m_loc - m_new)
        gmb = gi - b
        b_t = b.T
        e_end_t = jnp.exp(w_end - m_loc).T
        for h in range(M_HEADS):
            hs = slice(h * hd, (h + 1) * hd)
            qt = q_ref[hs, pl.ds(r0, L)]
            kh = k_ref[pl.ds(r0, L), hs]
            vext = jnp.concatenate([v_ref[hs, pl.ds(r0, L)], ones_rows], axis=0)
            b_row = b_t[h:h + 1, :]
            e = jnp.where(seen, b_row + gmb[:, h:h + 1], -jnp.inf)
            log_inter = b_row + m_prev[:, h:h + 1]
            m_t = jnp.maximum(log_inter, jnp.max(e, axis=0, keepdims=True))
            w = (jnp.exp(e - m_t) * _dot(kh, qt)).astype(BF16)
            a_int = jnp.exp(log_inter - m_t)
            cext = c_ref[h]
            num = _dot(vext, w) + a_int * _dot(cext.astype(BF16), qt)
            den = num[hd:hd + 1, :]
            hh = num[0:hd, :] / jnp.maximum(jnp.abs(den), jnp.exp(-m_t))
            hh = hh * lax.rsqrt(jnp.mean(hh * hh, axis=0, keepdims=True) + EPS) * gh_ref[hs, :]
            og = _sigmoid(o_ref[pl.ds(r0, L), hs].astype(F32))
            out_ref[pl.ds(r0, L), hs] = (og * hh.T).astype(BF16)
            ev = (vext.astype(F32) * e_end_t[h:h + 1, :]).astype(BF16)
            c_ref[h] = decay[:, h:h + 1] * cext + scale[:, h:h + 1] * _dot(ev, kh)
        m_ref[...] = m_new
        return 0

    lax.fori_loop(0, nchunks, chunk_body, 0, unroll=4)


def _mlstm(mq, mk, mv, mo, gates, b_gates, g_mhead, batch, seq_len, rows):
    t, mw = mk.shape
    h = M_HEADS
    bias = jnp.zeros((1, 2 * LANES), F32)
    bias = bias.at[0, 0:h].set(b_gates[0:h]).at[0, LANES:LANES + h].set(b_gates[h:2 * h])
    nb = seq_len // rows
    row = lambda b, i: (b * nb + i, 0)
    col = lambda b, i: (0, b * nb + i)
    const = lambda b, i: (0, 0)
    timed = pl.BlockSpec((mw, rows), col)
    rowed = pl.BlockSpec((rows, mw), row)
    return pl.pallas_call(
        functools.partial(_mlstm_kernel, chunk=MLSTM_CHUNK),
        grid=(batch, nb),
        in_specs=[timed, rowed, timed, rowed,
                  pl.BlockSpec((rows, 2 * LANES), row),
                  pl.BlockSpec((1, 2 * LANES), const),
                  pl.BlockSpec((mw, MLSTM_CHUNK), const)],
        out_specs=rowed,
        out_shape=jax.ShapeDtypeStruct((t, mw), BF16),
        scratch_shapes=[pltpu.VMEM((h, 2 * M_HEAD_DIM, M_HEAD_DIM), F32),
                        pltpu.VMEM((1, LANES), F32)],
        compiler_params=pltpu.CompilerParams(
            dimension_semantics=("arbitrary", "arbitrary"), vmem_limit_bytes=VMEM_LIMIT),
        name="mlstm",
    )(mq, mk, mv, mo, gates, bias,
      jnp.broadcast_to(g_mhead.reshape(mw, 1), (mw, MLSTM_CHUNK)))


def _sb_kernel(q_ref, k_ref, v_ref, out_ref, acc_ref, carry_ref, z_ref, sp_ref):
    blk = SB_BLOCK
    nq = q_ref.shape[0] // blk
    lane = lax.broadcasted_iota(jnp.int32, (blk, LANES), 1)
    head0 = lane < SB_HEAD_DIM
    rows = lax.broadcasted_iota(jnp.int32, (blk, blk), 0)
    cols = lax.broadcasted_iota(jnp.int32, (blk, blk), 1)
    strict = cols < rows
    neg_suffix = jnp.where(rows >= cols, -1.0, 0.0).astype(BF16)

    def split_heads(x):
        zero = jnp.zeros_like(x)
        return [jnp.where(head0, x, zero), jnp.where(head0, zero, x)]

    def rows_of(ref, j):
        return ref[pl.ds(pl.multiple_of(j * blk, blk), blk), :]

    def scores(qm_h, kb, mask):
        z = _dot_nt(qm_h, kb)
        neg_abs = lax.bitcast_convert_type(
            lax.bitcast_convert_type(z, jnp.uint32) | jnp.uint32(0x80000000), F32)
        sp = jnp.maximum(z, 0.0) + jnp.log(1.0 + jnp.exp(neg_abs))
        if mask is not None:
            sp = jnp.where(mask, sp, 0.0)
        return z, sp.astype(BF16)

    def weights(z, sp, carry, vm_h, mask):
        rc = _dot(sp, neg_suffix)
        p = jnp.exp(z + rc + carry)
        if mask is not None:
            p = jnp.where(mask, p, 0.0)
        return _dot(p.astype(BF16), vm_h), rc[:, 0:1]

    def first_half(qi, slot):
        qm = split_heads(rows_of(q_ref, qi))
        for n, (j, mask) in enumerate(((qi, strict), (jnp.maximum(qi - 1, 0), None))):
            kb = rows_of(k_ref, j)
            for h in range(2):
                z_ref[slot, 2 * n + h], sp_ref[slot, 2 * n + h] = scores(qm[h], kb, mask)

    def second_half(qi, slot, acc_ref, carry_ref):
        carries = [jnp.zeros((blk, 1), F32)] * 2
        upd = None
        for n, (j, mask, live) in enumerate(((qi, strict, None), (jnp.maximum(qi - 1, 0), None, qi > 0))):
            vm = split_heads(rows_of(v_ref, j))
            if live is not None:
                vm = [jnp.where(live, v, jnp.zeros_like(v)) for v in vm]
            new = []
            for h in range(2):
                d, total = weights(z_ref[slot, 2 * n + h], sp_ref[slot, 2 * n + h], carries[h], vm[h], mask)
                carry = carries[h] + total
                new.append(carry if live is None else jnp.where(live, carry, carries[h]))
                upd = d if upd is None else upd + d
            carries = new
        acc_ref[...] = upd
        carry_ref[0] = carries[0]
        carry_ref[1] = carries[1]
        return jnp.maximum(jnp.max(carries[0]), jnp.max(carries[1]))

    def remaining(qi, top, acc_ref, carry_ref):
        def cond(state):
            it, top = state
            return (it < qi) & (top > SB_ZERO_LOG)

        def body(state):
            it, _ = state
            j = qi - 1 - it
            qm = split_heads(rows_of(q_ref, qi))
            kb = rows_of(k_ref, j)
            vm = split_heads(rows_of(v_ref, j))
            upd = None
            tops = []
            for h in range(2):
                z, sp = scores(qm[h], kb, None)
                d, total = weights(z, sp, carry_ref[h], vm[h], None)
                carry_ref[h] = carry_ref[h] + total
                tops.append(jnp.max(carry_ref[h]))
                upd = d if upd is None else upd + d
            acc_ref[...] += upd
            return it + 1, jnp.maximum(tops[0], tops[1])

        lax.while_loop(cond, body, (jnp.int32(1), top))

    first_half(0, 0)

    group = acc_ref.shape[0]

    def query_group(g, _):
        qa = group * g
        tops = []
        for s in range(group):
            tops.append(second_half(qa + s, s % 2, acc_ref.at[s], carry_ref.at[s]))
            first_half(jnp.minimum(qa + s + 1, nq - 1), (s + 1) % 2)
        for s in range(group):
            remaining(qa + s, tops[s], acc_ref.at[s], carry_ref.at[s])
            out_ref[pl.ds(pl.multiple_of((qa + s) * blk, blk), blk), :] = acc_ref[s].astype(BF16)
        return 0

    assert group % 2 == 0 and nq % group == 0
    lax.fori_loop(0, nq // group, query_group, 0)


def _stickbreak(sq, sk, sv, batch, seq_len):
    t, sw = sq.shape
    npair = sw // LANES
    nq = seq_len // SB_BLOCK
    group = SB_GROUP if nq % SB_GROUP == 0 else 2
    seq = pl.BlockSpec((seq_len, LANES), lambda b, hp: (b, hp))
    return pl.pallas_call(
        _sb_kernel,
        grid=(batch, npair),
        in_specs=[seq, seq, seq],
        out_specs=seq,
        out_shape=jax.ShapeDtypeStruct((t, sw), BF16),
        scratch_shapes=[pltpu.VMEM((group, SB_BLOCK, LANES), F32),
                        pltpu.VMEM((group, 2, SB_BLOCK, 1), F32),
                        pltpu.VMEM((2, 4, SB_BLOCK, SB_BLOCK), F32),
                        pltpu.VMEM((2, 4, SB_BLOCK, SB_BLOCK), BF16)],
        compiler_params=pltpu.CompilerParams(
            dimension_semantics=("arbitrary", "arbitrary"), vmem_limit_bytes=VMEM_LIMIT),
        name="stickbrk",
    )(sq, sk, sv)


def _outroute_kernel(x_ref, hm_ref, hs_ref, wom_ref, wos_ref, g_ref, wr_ref, br_ref,
                     h1_ref, c_ref, route_ref, cnt_ref, run_ref, *, n_groups, per_group):
    i = pl.program_id(0)
    tm = x_ref.shape[0]

    @pl.when(i == 0)
    def _():
        run_ref[...] = jnp.zeros(run_ref.shape, F32)

    h1 = x_ref[...] + _dot(hm_ref[...], wom_ref[...]) + _dot(hs_ref[...], wos_ref[...])
    h1_ref[...] = h1
    c = _rms(h1, g_ref[...])
    c_ref[...] = c

    c1, c2, _ = _split3(c)
    pa = _dot(c1, wr_ref[...])
    pb = _dot(c2, wr_ref[...])
    logits = (pa[:, 0:LANES] + (pa[:, LANES:] + pb[:, 0:LANES]) + pb[:, LANES:]) + br_ref[...]

    lane = lax.broadcasted_iota(jnp.int32, (tm, LANES), 1).astype(F32)
    ninf = -jnp.inf
    big = float(LANES)

    def first_max(v):
        mx = jnp.max(v, axis=1, keepdims=True)
        idx = jnp.min(jnp.where(v == mx, lane, big), axis=1, keepdims=True)
        return mx, idx

    gl = jnp.where(lane < n_groups, logits, ninf)
    gmax, gsel = first_max(gl)
    p_g = 1.0 / jnp.sum(jnp.exp(gl - gmax), axis=1, keepdims=True)
    lo = LOGIT_LANE_E + per_group * gsel
    el = jnp.where((lane >= lo) & (lane < lo + per_group), logits, ninf)
    v1, i1 = first_max(el)
    v2, i2 = first_max(jnp.where(lane == i1, ninf, el))
    tt = jnp.exp(v2 - v1)
    w0 = p_g / (1.0 + tt)
    w1_ = p_g * tt / (1.0 + tt)

    oh0 = lane == i1
    oh1 = lane == i2
    ohsum = oh0.astype(F32) + oh1.astype(F32)
    rows = lax.broadcasted_iota(jnp.int32, (tm, tm), 0)
    cols = lax.broadcasted_iota(jnp.int32, (tm, tm), 1)
    before = (cols < rows).astype(BF16)
    prefix = _dot(before, ohsum.astype(BF16)) + run_ref[...]
    r0 = jnp.sum(jnp.where(oh0, prefix, 0.0), axis=1, keepdims=True)
    r1 = jnp.sum(jnp.where(oh1, prefix, 0.0), axis=1, keepdims=True)
    run = run_ref[...] + jnp.sum(ohsum, axis=0, keepdims=True)
    run_ref[...] = run
    cnt_ref[...] = jnp.broadcast_to(run, cnt_ref.shape)

    e0 = i1 - LOGIT_LANE_E
    e1 = i2 - LOGIT_LANE_E
    route = jnp.zeros((tm, LANES), F32)
    for ln, val in ((ROUTE_LANE_E, e0), (ROUTE_LANE_E + 1, e1), (ROUTE_LANE_W, w0),
                    (ROUTE_LANE_W + 1, w1_), (ROUTE_LANE_R, r0), (ROUTE_LANE_R + 1, r1)):
        route = jnp.where(lane == ln, val, route)
    route_ref[...] = route


def _tile_rows(ref, s, n, pitch=SUBLANES):
    return ref.at[pl.ds(s, n, stride=pitch), :]


def _tile_copy(src_ref, src_row, dst_ref, dst_row, sem, pitch):
    return pltpu.make_async_copy(src_ref.at[pl.ds(src_row * pitch, pitch), :],
                                 dst_ref.at[pl.ds(dst_row * pitch, pitch), :], sem)


def _lanes_to_smem(vals, vm_ref, sm_ref, sem):
    vm_ref[...] = vals.T[0:SUBLANES, :].astype(jnp.int32)
    cp = pltpu.make_async_copy(vm_ref, sm_ref, sem)
    cp.start()
    cp.wait()


def _outroute(x2, hm, hs, w_out, g_ffn, w_rg, b_rg, w_re, b_re, tm):
    t, d = x2.shape
    mw = hm.shape[1]
    sw = hs.shape[1]
    n_groups = w_rg.shape[1]
    n_exp = w_re.shape[1]
    wr = jnp.zeros((d, LANES), F32)
    wr = wr.at[:, 0:n_groups].set(w_rg).at[:, LOGIT_LANE_E:LOGIT_LANE_E + n_exp].set(w_re)
    wr_hi, wr_lo, _ = _split3(wr)
    wr2 = jnp.concatenate([wr_hi, wr_lo], axis=1)
    br = jnp.zeros((1, LANES), F32)
    br = br.at[0, 0:n_groups].set(b_rg).at[0, LOGIT_LANE_E:LOGIT_LANE_E + n_exp].set(b_re)
    row = lambda i: (i, 0)
    const = lambda i: (0, 0)
    kern = functools.partial(_outroute_kernel, n_groups=n_groups, per_group=n_exp // n_groups)
    return pl.pallas_call(
        kern,
        grid=(t // tm,),
        in_specs=[
            pl.BlockSpec((tm, d), row),
            pl.BlockSpec((tm, mw), row),
            pl.BlockSpec((tm, sw), row),
            pl.BlockSpec((mw, d), const),
            pl.BlockSpec((sw, d), const),
            pl.BlockSpec((1, d), const),
            pl.BlockSpec((d, 2 * LANES), const),
            pl.BlockSpec((1, LANES), const),
        ],
        out_specs=[
            pl.BlockSpec((tm, d), row),
            pl.BlockSpec((tm, d), row),
            pl.BlockSpec((tm, LANES), row),
            pl.BlockSpec((8, LANES), const),
        ],
        out_shape=[
            jax.ShapeDtypeStruct((t, d), F32),
            jax.ShapeDtypeStruct((t, d), F32),
            jax.ShapeDtypeStruct((t, LANES), F32),
            jax.ShapeDtypeStruct((8, LANES), F32),
        ],
        scratch_shapes=[pltpu.VMEM((1, LANES), F32)],
        compiler_params=pltpu.CompilerParams(
            dimension_semantics=("arbitrary",), vmem_limit_bytes=VMEM_LIMIT),
        name="outroute",
    )(x2, hm, hs, w_out[0:mw].astype(BF16), w_out[mw:].astype(BF16), g_ffn.reshape(1, d),
      wr2, br)


def _slotpos_kernel(route_ref, table_ref, pos_ref):
    route = route_ref[...]
    tm = route.shape[0]
    lane = lax.broadcasted_iota(jnp.int32, (tm, LANES), 1).astype(F32)
    onehot = jnp.concatenate(
        [jnp.where(lane == route[:, ROUTE_LANE_E + j:ROUTE_LANE_E + j + 1], 1.0, 0.0)
         for j in range(TOP_K)], axis=1).astype(BF16)
    base = float(EXPERT_ROWS) * _dot(onehot, table_ref[...])
    out = base + jnp.where(lane < TOP_K, route, 0.0)
    pos_ref[...] = out.T[0:SUBLANES, :].astype(jnp.int32)


def _slotpos(route, first_block, n_blocks, tm):
    t = route.shape[0]
    n_exp = first_block.shape[0]
    assert ROUTE_LANE_R == 0 and n_blocks <= 256, "block indices must be exact in bf16"
    table = jnp.zeros((TOP_K * LANES, LANES), F32)
    for j in range(TOP_K):
        table = table.at[j * LANES:j * LANES + n_exp, j].set(first_block.astype(F32))
    return pl.pallas_call(
        _slotpos_kernel,
        grid=(t // tm,),
        in_specs=[pl.BlockSpec((tm, LANES), lambda i: (i, 0)),
                  pl.BlockSpec((TOP_K * LANES, LANES), lambda i: (0, 0))],
        out_specs=pl.BlockSpec((SUBLANES, tm), lambda i: (0, i)),
        out_shape=jax.ShapeDtypeStruct((SUBLANES, t), jnp.int32),
        compiler_params=pltpu.CompilerParams(dimension_semantics=("arbitrary",)),
        name="slotpos",
    )(route, table.astype(BF16))


def _dispatch_kernel(zblk_ref, zuse_ref, pos_ref, c_ref, route_ref, xin_ref, rows_ref, zero_ref,
                     sem, zsem, *, n_tokens):
    i = pl.program_id(0)
    tm, d = c_ref.shape

    @pl.when(i == 0)
    def _():
        zero_ref[...] = jnp.zeros(zero_ref.shape, jnp.uint32)
        n = zero_ref.shape[0]

        def zero_block(k):
            return pltpu.make_async_copy(
                zero_ref, xin_ref.at[pl.ds(pl.multiple_of(zblk_ref[k] * n, n), n), :], zsem)

        for k in range(zblk_ref.shape[0]):
            pl.when(zuse_ref[k] != 0)(lambda k=k: zero_block(k).start())
        for k in range(zblk_ref.shape[0]):
            pl.when(zuse_ref[k] != 0)(lambda k=k: zero_block(k).wait())

    half = d // 2
    lane = lax.broadcasted_iota(jnp.int32, (tm, LANES), 1)
    row_id = (i * tm + lax.broadcasted_iota(jnp.int32, (tm, 1), 0)).astype(F32)
    route = route_ref[...]
    packed = _pack_bf16_pairs(c_ref[...])
    for j in range(TOP_K):
        w = route[:, ROUTE_LANE_W + j:ROUTE_LANE_W + j + 1]
        meta = jnp.where(lane == META_DEST, row_id + j * n_tokens, jnp.where(lane == META_W, w, 0.0))
        for s in range(X_SUBLANES):
            if s < half // LANES:
                sub = packed[:, s * LANES:(s + 1) * LANES]
            else:
                sub = lax.bitcast_convert_type(meta, jnp.uint32)
            _tile_rows(rows_ref.at[j], s, tm, X_SUBLANES)[...] = sub

    def issue(t, _):
        for j in range(TOP_K):
            _tile_copy(rows_ref.at[j], t, xin_ref, pos_ref[j, t], sem, X_SUBLANES).start(priority=j)
        return 0

    lax.fori_loop(0, tm, issue, 0, unroll=ROW_DMA_UNROLL)
    for j in range(TOP_K):
        pltpu.make_async_copy(rows_ref.at[j], rows_ref.at[j], sem).wait()


def _dispatch(zero_blocks, zero_use, pos, c, route, n_blocks, tm):
    t, d = c.shape
    assert d // 2 // LANES == META_SUBLANE == X_SUBLANES - 1
    rows = EXPERT_ROWS
    grid_spec = pltpu.PrefetchScalarGridSpec(
        num_scalar_prefetch=2,
        grid=(t // tm,),
        in_specs=[
            pl.BlockSpec((SUBLANES, tm), lambda i, *_: (0, i), memory_space=pltpu.SMEM),
            pl.BlockSpec((tm, d), lambda i, *_: (i, 0)),
            pl.BlockSpec((tm, LANES), lambda i, *_: (i, 0)),
        ],
        out_specs=pl.BlockSpec(memory_space=pl.ANY),
        scratch_shapes=[pltpu.VMEM((TOP_K, tm * X_SUBLANES, LANES), jnp.uint32),
                        pltpu.VMEM((rows * X_SUBLANES, LANES), jnp.uint32),
                        pltpu.SemaphoreType.DMA(()),
                        pltpu.SemaphoreType.DMA(())],
    )
    return pl.pallas_call(
        functools.partial(_dispatch_kernel, n_tokens=t),
        grid_spec=grid_spec,
        out_shape=jax.ShapeDtypeStruct((n_blocks * rows * X_SUBLANES, LANES), jnp.uint32),
        compiler_params=pltpu.CompilerParams(
            dimension_semantics=("arbitrary",), vmem_limit_bytes=VMEM_LIMIT),
        name="dispatch",
    )(zero_blocks, zero_use, pos, c, route)


def _pack_bf16_pairs(v):
    half = v.shape[1] // 2
    bits = lax.bitcast_convert_type(v.astype(BF16).astype(F32), jnp.uint32)
    return (bits[:, 0:half] >> 16) | bits[:, half:]


def _unpack_bf16_pairs(words):
    lo = [lax.bitcast_convert_type(w << 16, F32) for w in words]
    hi = [lax.bitcast_convert_type(w & jnp.uint32(0xFFFF0000), F32) for w in words]
    return jnp.concatenate(lo + hi, axis=1)


def _experts_kernel(be_ref, bn_ref, nv_ref, x_ref, wg_ref, wu_ref, wd_ref, yout_ref,
                    wgb, wub, wdb, ybuf, dest_vm, dest_sm, sems, dsem, *, dump_row):
    i = pl.program_id(0)
    s = i % 2
    rows = ybuf.shape[1] // Y_SUBLANES
    prev = be_ref[jnp.maximum(i - 1, 0)]
    active = i < nv_ref[0]

    def to_dump(slot):
        def body(r, _):
            dest_sm[slot, 0, r] = dump_row + r
            return 0
        lax.fori_loop(0, rows, body, 0)

    def send(slot, r, queue=0):
        _tile_copy(ybuf.at[slot], r, yout_ref, dest_sm[slot, 0, r], sems.at[slot],
                   Y_SUBLANES).start(priority=queue)

    def wait(slot):
        pltpu.make_async_copy(ybuf.at[slot], ybuf.at[slot], sems.at[slot]).wait()

    @pl.when(i == 0)
    def _():
        ybuf[1] = jnp.zeros(ybuf.shape[1:], jnp.uint32)
        to_dump(1)

    @pl.when(i > 0)
    def _():
        wait(s)

    @pl.when(active & ((i == 0) | (be_ref[i] != prev)))
    def _():
        wgb[...] = wg_ref[0].astype(BF16)
        wub[...] = wu_ref[0].astype(BF16)
        wdb[...] = wd_ref[0].astype(BF16)

    @pl.when(active)
    def _():
        for r in range(rows):
            send(1 - s, r, r % 2)
        x = _unpack_bf16_pairs([_tile_rows(x_ref, t, rows, X_SUBLANES)[...]
                                for t in range(META_SUBLANE)]).astype(BF16)
        meta = lax.bitcast_convert_type(_tile_rows(x_ref, META_SUBLANE, rows, X_SUBLANES)[...], F32)
        gt = _dot(x, wgb[...])
        up = _dot(x, wub[...])
        hid = (gt * _sigmoid(gt) * up).astype(BF16)
        y = _pack_bf16_pairs(_dot(hid, wdb[...]) * meta[:, META_W:META_W + 1])
        for t in range(Y_SUBLANES):
            _tile_rows(ybuf.at[s], t, rows, Y_SUBLANES)[...] = y[:, t * LANES:(t + 1) * LANES]
        row = lax.broadcasted_iota(jnp.int32, (rows, LANES), 0)
        dest = jnp.where(row < bn_ref[i], meta, (dump_row + row).astype(F32))
        _lanes_to_smem(dest, dest_vm, dest_sm.at[s], dsem)

    @pl.when(jnp.logical_not(active))
    def _():
        lax.fori_loop(0, rows, lambda r, _: send(1 - s, r) or 0, 0, unroll=ROW_DMA_UNROLL)
        to_dump(s)

    @pl.when(i == pl.num_programs(0) - 1)
    def _():
        wait(1 - s)


def _experts(blk_e, blk_n, n_valid, xin, w_gate, w_up, w_down, n_out_rows):
    rows = EXPERT_ROWS
    nblk = xin.shape[0] // (rows * X_SUBLANES)
    d, de = w_gate.shape[1], w_gate.shape[2]
    assert d == 2 * Y_SUBLANES * LANES, "an output row is 4 sublanes of packed bf16 pairs"
    grid_spec = pltpu.PrefetchScalarGridSpec(
        num_scalar_prefetch=3,
        grid=(nblk + 1,),
        in_specs=[
            pl.BlockSpec((rows * X_SUBLANES, LANES),
                         lambda i, be, bn, nv: (jnp.minimum(i, nblk - 1), 0)),
            pl.BlockSpec((1, d, de), lambda i, be, bn, nv: (be[i], 0, 0)),
            pl.BlockSpec((1, d, de), lambda i, be, bn, nv: (be[i], 0, 0)),
            pl.BlockSpec((1, de, d), lambda i, be, bn, nv: (be[i], 0, 0)),
        ],
        out_specs=pl.BlockSpec(memory_space=pl.ANY),
        scratch_shapes=[pltpu.VMEM((d, de), BF16), pltpu.VMEM((d, de), BF16),
                        pltpu.VMEM((de, d), BF16),
                        pltpu.VMEM((2, rows * Y_SUBLANES, LANES), jnp.uint32),
                        pltpu.VMEM((8, rows), jnp.int32),
                        pltpu.SMEM((2, 8, rows), jnp.int32),
                        pltpu.SemaphoreType.DMA((2,)),
                        pltpu.SemaphoreType.DMA(())],
    )
    return pl.pallas_call(
        functools.partial(_experts_kernel, dump_row=n_out_rows),
        grid_spec=grid_spec,
        out_shape=jax.ShapeDtypeStruct(((n_out_rows + rows) * Y_SUBLANES, LANES), jnp.uint32),
        compiler_params=pltpu.CompilerParams(
            dimension_semantics=("arbitrary",), vmem_limit_bytes=VMEM_LIMIT),
        name="experts",
    )(blk_e, blk_n, n_valid, xin, w_gate, w_up, w_down)


def _combine_kernel(h1_ref, y0_ref, y1_ref, p_ref, wpg_ref, wpp_ref,
                    gple_ref, gpost_ref, gfin_ref, out_ref):
    tm = h1_ref.shape[0]
    sub = min(tm, COMBINE_ROWS)

    def chunk(ci, carry):
        r0 = pl.multiple_of(ci * sub, sub)
        rows = pl.ds(r0, sub)
        y0, y1 = (_unpack_bf16_pairs([ref[pl.ds(r0 * Y_SUBLANES + s, sub, stride=Y_SUBLANES), :]
                                      for s in range(Y_SUBLANES)])
                  for ref in (y0_ref, y1_ref))
        h2 = h1_ref[rows, :] + (y0 + y1)
        gate = _sigmoid(_dot(_rms(h2, gple_ref[...]).astype(BF16), wpg_ref[...]))
        ple = _rms(_dot(p_ref[rows, :].astype(BF16), wpp_ref[...]), gpost_ref[...])
        h3 = h2 + gate * ple
        out_ref[rows, :] = _rms(h3, gfin_ref[...])
        return carry

    lax.fori_loop(0, tm // sub, chunk, 0, unroll=True)


def _combine(h1, p2, y, w_pg, w_pp, g_ple, g_post, g_final, tm):
    t, d = h1.shape
    pd = p2.shape[1]
    row = lambda i: (i, 0)
    const = lambda i: (0, 0)
    return pl.pallas_call(
        _combine_kernel,
        grid=(t // tm,),
        in_specs=[
            pl.BlockSpec((tm, d), row),
            pl.BlockSpec((tm * Y_SUBLANES, LANES), lambda i: (i, 0)),
            pl.BlockSpec((tm * Y_SUBLANES, LANES), lambda i: (t // tm + i, 0)),
            pl.BlockSpec((tm, pd), row),
            pl.BlockSpec((d, d), const),
            pl.BlockSpec((pd, d), const),
            pl.BlockSpec((1, d), const),
            pl.BlockSpec((1, d), const),
            pl.BlockSpec((1, d), const),
        ],
        out_specs=pl.BlockSpec((tm, d), row),
        out_shape=jax.ShapeDtypeStruct((t, d), F32),
        compiler_params=pltpu.CompilerParams(
            dimension_semantics=("arbitrary",), vmem_limit_bytes=VMEM_LIMIT),
        name="combine",
    )(h1, y, y, p2, w_pg.astype(BF16), w_pp.astype(BF16),
      g_ple.reshape(1, d), g_post.reshape(1, d), g_final.reshape(1, d))


def _largest_tile(n, cap):
    tile = cap
    while n % tile:
        tile //= 2
    return tile


def kernel(x, p, g_mix, w_in, b_gates, conv_q, conv_k, g_mhead, w_out, g_ffn, w_router_group,
           b_router_group, w_router_expert, b_router_expert, w_exp_gate, w_exp_up, w_exp_down,
           g_ple, w_ple_gate, w_ple_proj, g_ple_post, g_final):
    batch, seq_len, d = x.shape
    t = batch * seq_len
    tm = _largest_tile(seq_len, 512)
    n_exp = w_router_expert.shape[-1]
    rows = EXPERT_ROWS
    nblk = t * TOP_K // rows + n_exp

    assert w_in.shape[0] == 1, "single-layer block"
    l = 0
    h = x.reshape(t, d)
    mq, mk, mv, mo, gates, sq, sk, sv = _inproj(
        h, g_mix[l], w_in[l], conv_q[l], conv_k[l], seq_len, _largest_tile(seq_len, 1024))
    hm = _mlstm(mq, mk, mv, mo, gates, b_gates[l], g_mhead[l], batch, seq_len,
                _largest_tile(seq_len, 1024))
    hs = _stickbreak(sq, sk, sv, batch, seq_len)
    h1, c, route, counts = _outroute(
        h, hm, hs, w_out[l], g_ffn[l], w_router_group[l], b_router_group[l],
        w_router_expert[l], b_router_expert[l], tm)

    cnt = counts[0, LOGIT_LANE_E:LOGIT_LANE_E + n_exp].astype(jnp.int32)
    nb_e = (cnt + rows - 1) // rows
    cum = jnp.cumsum(nb_e)
    n_valid = cum[-1:]
    step = jnp.arange(nblk + 1, dtype=jnp.int32)
    blk_e = jnp.minimum(jnp.sum(cum[None, :] <= step[:, None], axis=1), n_exp - 1).astype(jnp.int32)
    mine = blk_e[:, None] == jnp.arange(n_exp, dtype=jnp.int32)[None, :]
    first = jnp.sum(jnp.where(mine, (cum - nb_e)[None, :], 0), axis=1)
    blk_n = jnp.clip(jnp.sum(jnp.where(mine, cnt[None, :], 0), axis=1) - rows * (step - first), 0, rows)
    tail = n_valid + jnp.arange(n_exp, dtype=jnp.int32)
    zero_blocks = jnp.concatenate([jnp.maximum(cum - 1, 0), jnp.minimum(tail, nblk - 1)])
    zero_use = jnp.concatenate([nb_e > 0, tail < nblk]).astype(jnp.int32)

    pos = _slotpos(route, cum - nb_e, nblk, _largest_tile(t, 4096))
    xin = _dispatch(zero_blocks, zero_use, pos, c, route, nblk, _largest_tile(seq_len, 1024))
    y = _experts(blk_e, blk_n, n_valid, xin, w_exp_gate[l], w_exp_up[l], w_exp_down[l], TOP_K * t)
    out = _combine(h1, p[l].reshape(t, -1), y, w_ple_gate[l], w_ple_proj[l],
                   g_ple[l], g_ple_post[l], g_final, _largest_tile(seq_len, 1024))
    return out.reshape(batch, seq_len, d)
```
